```python
import math
import jax, jax.numpy as jnp
from jax import lax
import numpy as np


D_MODEL = 2048
BATCH = 2
SEQ = 8192
DEPTH = 1

HEAD_DIM = 128
N_MOBA_HEADS = D_MODEL // (2 * HEAD_DIM)
N_FOX_HEADS = D_MODEL // (2 * HEAD_DIM)
MOBA_WIDTH = N_MOBA_HEADS * HEAD_DIM
FOX_WIDTH = N_FOX_HEADS * HEAD_DIM
MOBA_BLOCK = 256
MOBA_TOP_K = 3
MOBA_Q_CHUNK = 64
FOX_Q_BLOCK = 128
N_MEM = 256
N_CROSS_HEADS = 4
CROSS_WIDTH = N_CROSS_HEADS * HEAD_DIM
D_FF = 4 * D_MODEL
NUM_BUCKETS = 32
MAX_DISTANCE = 1024
RMS_EPS = 1e-6
NEG_INF = -1e30
IN_SPLIT_WIDTHS = (MOBA_WIDTH, MOBA_WIDTH, MOBA_WIDTH, FOX_WIDTH, FOX_WIDTH, FOX_WIDTH, N_FOX_HEADS, D_MODEL, D_MODEL)
IN_WIDTH = sum(IN_SPLIT_WIDTHS)
IN_SPLIT_POINTS = tuple(int(v) for v in np.cumsum(IN_SPLIT_WIDTHS)[:-1])

kernel_name = 'hybrid_moba_fox_gated_block'


def rmsnorm(x, g):
    xf = x.astype(jnp.float32)
    y = xf * lax.rsqrt(jnp.mean(xf * xf, axis=-1, keepdims=True) + RMS_EPS)
    return (y * g.astype(jnp.float32)).astype(x.dtype)


def split_heads(t, n_heads):
    b, s, _ = t.shape
    return t.reshape(b, s, n_heads, -1).transpose(0, 2, 1, 3)


def merge_heads(t):
    b, h, s, d = t.shape
    return t.transpose(0, 2, 1, 3).reshape(b, s, h * d)


def t5_bucket(dist):
    n = jnp.maximum(dist, 0)
    max_exact = NUM_BUCKETS // 2
    nf = jnp.maximum(n, 1).astype(jnp.float32)
    large = max_exact + (jnp.log(nf / max_exact) / math.log(MAX_DISTANCE / max_exact)
                         * (NUM_BUCKETS - max_exact)).astype(jnp.int32)
    large = jnp.minimum(large, NUM_BUCKETS - 1)
    return jnp.where(n < max_exact, n, large)


def moba_attention(q, k, v, rel_bias):
    b, h, s, dh = q.shape
    nb = -(-s // MOBA_BLOCK)
    pad = nb * MOBA_BLOCK - s
    kb = jnp.pad(k, ((0, 0), (0, 0), (0, pad), (0, 0))).reshape(b, h, nb, MOBA_BLOCK, dh)
    vb = jnp.pad(v, ((0, 0), (0, 0), (0, pad), (0, 0))).reshape(b, h, nb, MOBA_BLOCK, dh)
    kbar = jnp.mean(kb.astype(jnp.float32), axis=3)
    top_k = min(MOBA_TOP_K, nb)
    scale = dh ** -0.5
    block_ids = jnp.arange(nb)
    key_off = jnp.arange(MOBA_BLOCK)
    head_idx = jnp.arange(h)[:, None, None, None]
    bias_tab = rel_bias.astype(jnp.float32)
    gather_blocks = jax.vmap(jax.vmap(lambda blocks, idx: blocks[idx]))

    def chunk(ci):
        start = ci * MOBA_Q_CHUNK
        qc = lax.dynamic_slice_in_dim(q, start, MOBA_Q_CHUNK, axis=2)
        t = start + jnp.arange(MOBA_Q_CHUNK)
        own = start // MOBA_BLOCK
        gate = jnp.einsum('bhqd,bhnd->bhqn', qc.astype(jnp.float32), kbar)
        gate = jnp.where(block_ids < own, gate, NEG_INF)
        _, sel = lax.top_k(gate, top_k)
        valid = sel < own
        ks = gather_blocks(kb, sel)
        vs = gather_blocks(vb, sel)
        s_sel = jnp.einsum('bhqd,bhqkld->bhqkl', qc, ks).astype(jnp.float32) * scale
        pos_sel = sel[..., None] * MOBA_BLOCK + key_off
        s_sel = s_sel + bias_tab[head_idx, t5_bucket(t[:, None, None] - pos_sel)]
        s_sel = jnp.where(valid[..., None], s_sel, NEG_INF)
        k_own = lax.dynamic_index_in_dim(kb, own, axis=2, keepdims=False)
        v_own = lax.dynamic_index_in_dim(vb, own, axis=2, keepdims=False)
        pos_own = own * MOBA_BLOCK + key_off
        s_own = jnp.einsum('bhqd,bhld->bhql', qc, k_own).astype(jnp.float32) * scale
        s_own = s_own + bias_tab[:, t5_bucket(t[:, None] - pos_own[None, :])]
        s_own = jnp.where(pos_own[None, :] <= t[:, None], s_own, NEG_INF)
        logits = jnp.concatenate([s_sel.reshape(b, h, MOBA_Q_CHUNK, top_k * MOBA_BLOCK), s_own], axis=-1)
        p = jax.nn.softmax(logits, axis=-1).astype(v.dtype)
        p_sel = p[..., :top_k * MOBA_BLOCK].reshape(b, h, MOBA_Q_CHUNK, top_k, MOBA_BLOCK)
        p_own = p[..., top_k * MOBA_BLOCK:]
        return (jnp.einsum('bhqkl,bhqkld->bhqd', p_sel, vs)
                + jnp.einsum('bhql,bhld->bhqd', p_own, v_own))

    out = lax.map(chunk, jnp.arange(s // MOBA_Q_CHUNK))
    return jnp.moveaxis(out, 0, 2).reshape(b, h, s, dh)


def forgetting_attention(q, k, v, log_f):
    b, h, s, dh = q.shape
    scale = dh ** -0.5
    c = jnp.cumsum(log_f, axis=-1)
    outs = []
    for i in range(s // FOX_Q_BLOCK):
        s0, s1 = i * FOX_Q_BLOCK, (i + 1) * FOX_Q_BLOCK
        logits = jnp.einsum('bhqd,bhkd->bhqk', q[:, :, s0:s1], k[:, :, :s1]).astype(jnp.float32) * scale
        logits = logits + c[:, :, s0:s1, None] - c[:, :, None, :s1]
        mask = jnp.arange(s0, s1)[:, None] >= jnp.arange(s1)[None, :]
        p = jax.nn.softmax(jnp.where(mask, logits, NEG_INF), axis=-1).astype(v.dtype)
        outs.append(jnp.einsum('bhqk,bhkd->bhqd', p, v[:, :, :s1]))
    return jnp.concatenate(outs, axis=2)


def memory_cross_attention(c, m, w_cq, w_ck, w_cv, w_co):
    q = split_heads(c @ w_cq, N_CROSS_HEADS)
    k = split_heads(m @ w_ck, N_CROSS_HEADS)
    v = split_heads(m @ w_cv, N_CROSS_HEADS)
    logits = jnp.einsum('bhqd,bhmd->bhqm', q, k).astype(jnp.float32) * HEAD_DIM ** -0.5
    p = jax.nn.softmax(logits, axis=-1).astype(v.dtype)
    return merge_heads(jnp.einsum('bhqm,bhmd->bhqd', p, v)) @ w_co


def setup_inputs(seed: int = 0) -> dict:
    key = jax.random.key(seed)
    ks = jax.random.split(key, 24)
    f32 = jnp.float32

    def dense(k, shape):
        return jax.random.normal(k, shape, f32) * shape[-2] ** -0.5

    def gain(k, shape):
        return 1.0 + 0.02 * jax.random.normal(k, shape, f32)

    return {
        'x': jax.random.normal(ks[0], (BATCH, SEQ, D_MODEL), f32),
        'mem': jax.random.normal(ks[1], (BATCH, N_MEM, D_MODEL), f32),
        'g_mix': gain(ks[2], (DEPTH, D_MODEL)),
        'w_in': dense(ks[3], (DEPTH, D_MODEL, IN_WIDTH)),
        'b_forget': 3.0 + 0.5 * jax.random.normal(ks[4], (DEPTH, N_FOX_HEADS), f32),
        'w_branch_moba': dense(ks[5], (DEPTH, MOBA_WIDTH, D_MODEL)),
        'w_branch_fox': dense(ks[6], (DEPTH, FOX_WIDTH, D_MODEL)),
        'w_mix_out': dense(ks[7], (DEPTH, D_MODEL, D_MODEL)),
        'rel_bias': 0.5 * jax.random.normal(ks[8], (N_MOBA_HEADS, NUM_BUCKETS), f32),
        'g_cross': gain(ks[9], (DEPTH, D_MODEL)),
        'g_mem': gain(ks[10], (DEPTH, D_MODEL)),
        'w_cq': dense(ks[11], (DEPTH, D_MODEL, CROSS_WIDTH)),
        'w_ck': dense(ks[12], (DEPTH, D_MODEL, CROSS_WIDTH)),
        'w_cv': dense(ks[13], (DEPTH, D_MODEL, CROSS_WIDTH)),
        'w_co': dense(ks[14], (DEPTH, CROSS_WIDTH, D_MODEL)),
        'g_mlp': gain(ks[15], (DEPTH, D_MODEL)),
        'w_ff1': dense(ks[16], (DEPTH, D_MODEL, D_FF)),
        'w_ff2': dense(ks[17], (DEPTH, D_FF, D_MODEL)),
        'g_final': gain(ks[18], (D_MODEL,)),
    }


def reference(x, mem, g_mix, w_in, b_forget, w_branch_moba, w_branch_fox, w_mix_out, rel_bias,
              g_cross, g_mem, w_cq, w_ck, w_cv, w_co, g_mlp, w_ff1, w_ff2, g_final):
    h = x
    for l in range(DEPTH):
        a = rmsnorm(h, g_mix[l])
        z = a @ w_in[l]
        q_a, k_a, v_a, q_f, k_f, v_f, f_logit, gate_a, gate_f = jnp.split(z, IN_SPLIT_POINTS, axis=-1)
        o_a = merge_heads(moba_attention(split_heads(q_a, N_MOBA_HEADS), split_heads(k_a, N_MOBA_HEADS),
                                         split_heads(v_a, N_MOBA_HEADS), rel_bias))
        log_f = jax.nn.log_sigmoid((f_logit + b_forget[l]).astype(jnp.float32)).transpose(0, 2, 1)
        o_f = merge_heads(forgetting_attention(split_heads(q_f, N_FOX_HEADS), split_heads(k_f, N_FOX_HEADS),
                                               split_heads(v_f, N_FOX_HEADS), log_f))
        merged = (jax.nn.sigmoid(gate_a) * (o_a @ w_branch_moba[l])
                  + jax.nn.sigmoid(gate_f) * (o_f @ w_branch_fox[l]))
        h = h + merged @ w_mix_out[l]
        h = h + memory_cross_attention(rmsnorm(h, g_cross[l]), rmsnorm(mem, g_mem[l]),
                                       w_cq[l], w_ck[l], w_cv[l], w_co[l])
        u = rmsnorm(h, g_mlp[l]) @ w_ff1[l]
        h = h + jnp.square(jax.nn.relu(u)) @ w_ff2[l]
    return rmsnorm(h, g_final)
```

```python
import functools
import math

import jax
import jax.numpy as jnp
from jax import lax
from jax.experimental import pallas as pl
from jax.experimental.pallas import tpu as pltpu

F32 = jnp.float32
BF16 = jnp.bfloat16

HEAD_DIM = 128
MOBA_BLOCK = 256
MOBA_TOP_K = 3
NUM_BUCKETS = 32
MAX_DISTANCE = 1024
N_CROSS_HEADS = 4
RMS_EPS = 1e-6
NEG_INF = -1e30
LANES = 128
ATT_TILE = 256
NEAR_BLOCKS = 5
VMEM_LIMIT = 48 * 1024 * 1024


def _bucket_thresholds():
    max_exact = NUM_BUCKETS // 2
    thr = list(range(1, max_exact + 1))
    for k in range(max_exact + 1, NUM_BUCKETS):
        v = max_exact * (MAX_DISTANCE / max_exact) ** ((k - max_exact) / (NUM_BUCKETS - max_exact))
        n = int(math.floor(v))
        while max_exact + int(math.log(n / max_exact) / math.log(MAX_DISTANCE / max_exact)
                              * (NUM_BUCKETS - max_exact)) < k:
            n += 1
        thr.append(n)
    return tuple(thr)


BUCKET_THRESHOLDS = _bucket_thresholds()
assert (NEAR_BLOCKS - 1) * MOBA_BLOCK + 1 >= BUCKET_THRESHOLDS[-1]


def _params(*sem):
    return pltpu.CompilerParams(dimension_semantics=sem, vmem_limit_bytes=VMEM_LIMIT)


def _rms_proj_kernel(x_ref, g_ref, w_ref, o_ref, a_ref, *, mode, act, scale, n_scaled):
    j = pl.program_id(1)

    @pl.when(j == 0)
    def _():
        x = x_ref[...]
        ms = jnp.mean(x * x, axis=-1, keepdims=True)
        a_ref[...] = (x * lax.rsqrt(ms + RMS_EPS) * g_ref[...]).astype(BF16)

    acc = jnp.dot(a_ref[...], w_ref[...], preferred_element_type=F32)
    if scale is not None:
        acc = acc * jnp.where(j < n_scaled, F32(scale), F32(1.0))
    tm, tn = acc.shape
    if mode == "rowmajor":
        if act == "sigmoid":
            acc = jax.nn.sigmoid(acc)
        elif act == "relu2":
            acc = jnp.square(jnp.maximum(acc, 0.0))
        o_ref[...] = acc.astype(o_ref.dtype)
    elif mode == "headmajor":
        for c in range(tn // LANES):
            o_ref[c] = acc[:, c * LANES:(c + 1) * LANES].astype(o_ref.dtype)
    else:
        for c in range(tn // LANES):
            for r in range(tm // ATT_TILE):
                blk = acc[r * ATT_TILE:(r + 1) * ATT_TILE, c * LANES:(c + 1) * LANES]
                o_ref[c, r] = blk.T.astype(o_ref.dtype)


def rms_proj(x, g, w, *, mode="rowmajor", act=None, scale=None, n_scaled=0,
             out_dtype=BF16, tm=512, tn=1024):
    m, d = x.shape
    n = w.shape[1]
    tm, tn = min(tm, m), min(tn, n)
    if mode == "rowmajor":
        out_shape = jax.ShapeDtypeStruct((m, n), out_dtype)
        out_spec = pl.BlockSpec((tm, tn), lambda i, j: (i, j))
    elif mode == "headmajor":
        out_shape = jax.ShapeDtypeStruct((n // LANES, m, LANES), out_dtype)
        out_spec = pl.BlockSpec((tn // LANES, tm, LANES), lambda i, j: (j, i, 0))
    else:
        out_shape = jax.ShapeDtypeStruct((n // LANES, m // ATT_TILE, LANES, ATT_TILE), out_dtype)
        out_spec = pl.BlockSpec((tn // LANES, tm // ATT_TILE, LANES, ATT_TILE),
                                lambda i, j: (j, i, 0, 0))
    return pl.pallas_call(
        functools.partial(_rms_proj_kernel, mode=mode, act=act, scale=scale, n_scaled=n_scaled),
        out_shape=out_shape,
        grid=(m // tm, n // tn),
        in_specs=[pl.BlockSpec((tm, d), lambda i, j: (i, 0)),
                  pl.BlockSpec((1, d), lambda i, j: (0, 0)),
                  pl.BlockSpec((d, tn), lambda i, j: (0, j))],
        out_specs=out_spec,
        scratch_shapes=[pltpu.VMEM((tm, d), BF16)],
        compiler_params=_params("parallel", "arbitrary"),
        name="rms_proj_" + mode,
    )(x, g.reshape(1, d), w)


def _forget_cumsum_kernel(x_ref, g_ref, w_ref, b_ref, o_ref, carry_ref, *, n_heads):
    t = pl.program_id(1)

    @pl.when(t == 0)
    def _():
        carry_ref[...] = jnp.zeros_like(carry_ref)

    x = x_ref[...]
    tm = x.shape[0]
    ms = jnp.mean(x * x, axis=-1, keepdims=True)
    a = x * lax.rsqrt(ms + RMS_EPS) * g_ref[...]
    f = jnp.dot(a, w_ref[...], precision=lax.Precision.HIGHEST,
                preferred_element_type=F32) + b_ref[...]
    logf = jnp.minimum(f, 0.0) - jnp.log1p(jnp.exp(-jnp.abs(f)))
    row = lax.broadcasted_iota(jnp.int32, (tm, tm), 0)
    col = lax.broadcasted_iota(jnp.int32, (tm, tm), 1)
    tri = (col <= row).astype(F32)
    for h in range(n_heads):
        lf = jnp.broadcast_to(logf[:, h:h + 1], (tm, LANES))
        c = jnp.dot(tri, lf, precision=lax.Precision.HIGHEST,
                    preferred_element_type=F32) + carry_ref[h, 0:1, :]
        o_ref[h] = c
        carry_ref[h] = jnp.broadcast_to(c[tm - 1:tm, :], carry_ref.shape[1:])


def forget_cumsum(x, g, w_f, b_f, *, batch, n_heads, tm=256):
    m, d = x.shape
    s = m // batch
    nt = s // tm
    w_pad = jnp.zeros((d, LANES), F32).at[:, :n_heads].set(w_f)
    b_pad = jnp.zeros((1, LANES), F32).at[0, :n_heads].set(b_f)
    return pl.pallas_call(
        functools.partial(_forget_cumsum_kernel, n_heads=n_heads),
        out_shape=jax.ShapeDtypeStruct((n_heads, m, LANES), F32),
        grid=(batch, nt),
        in_specs=[pl.BlockSpec((tm, d), lambda b, t: (b * nt + t, 0)),
                  pl.BlockSpec((1, d), lambda b, t: (0, 0)),
                  pl.BlockSpec((d, LANES), lambda b, t: (0, 0)),
                  pl.BlockSpec((1, LANES), lambda b, t: (0, 0))],
        out_specs=pl.BlockSpec((n_heads, tm, LANES), lambda b, t: (0, b * nt + t, 0)),
        scratch_shapes=[pltpu.VMEM((n_heads, 8, LANES), F32)],
        compiler_params=_params("parallel", "arbitrary"),
        name="forget_cumsum",
    )(x, g.reshape(1, d), w_pad, b_pad)


def _moba_bias_kernel(tab_ref, o_ref):
    h = pl.program_id(0)
    key = lax.broadcasted_iota(jnp.int32, (ATT_TILE, ATT_TILE), 0)
    qry = lax.broadcasted_iota(jnp.int32, (ATT_TILE, ATT_TILE), 1)
    for delta in range(NEAR_BLOCKS):
        dist = delta * MOBA_BLOCK + qry - key
        val = jnp.full((ATT_TILE, ATT_TILE), tab_ref[h, 0], F32)
        for k in range(1, NUM_BUCKETS):
            val = jnp.where(dist >= BUCKET_THRESHOLDS[k - 1], tab_ref[h, k], val)
        if delta == 0:
            val = jnp.where(dist >= 0, val, NEG_INF)
        o_ref[0, delta] = val


def moba_bias(rel_bias):
    h = rel_bias.shape[0]
    return pl.pallas_call(
        _moba_bias_kernel,
        out_shape=jax.ShapeDtypeStruct((h, NEAR_BLOCKS, ATT_TILE, ATT_TILE), F32),
        grid=(h,),
        in_specs=[pl.BlockSpec(memory_space=pltpu.SMEM)],
        out_specs=pl.BlockSpec((1, NEAR_BLOCKS, ATT_TILE, ATT_TILE), lambda i: (i, 0, 0, 0)),
        compiler_params=_params("parallel"),
        name="moba_bias",
    )(rel_bias.astype(F32))


def _softmax_first(s, vt):
    m = jnp.max(s, axis=0, keepdims=True)
    p = jnp.exp(s - m)
    l = jnp.sum(p, axis=0, keepdims=True)
    acc = jnp.dot(vt, p.astype(BF16), preferred_element_type=F32)
    return m, l, acc


def _softmax_next(s, vt, m, l, acc):
    m_new = jnp.maximum(m, jnp.max(s, axis=0, keepdims=True))
    alpha = jnp.exp(m - m_new)
    p = jnp.exp(s - m_new)
    l = alpha * l + jnp.sum(p, axis=0, keepdims=True)
    acc = alpha * acc + jnp.dot(vt, p.astype(BF16), preferred_element_type=F32)
    return m_new, l, acc


def _moba_attn_kernel(tab_ref, q_ref, k_ref, v_ref, bias_ref, o_ref, kbar_ref, kb3_ref, selb_ref,
                      *, n_blocks):
    h = pl.program_id(1)
    i = pl.program_id(2)

    @pl.when(i == 0)
    def _():
        for n in range(n_blocks):
            kbar_ref[n:n + 1, :] = jnp.sum(k_ref[0, n].astype(F32), axis=0, keepdims=True)
        kbar = kbar_ref[...] * (1.0 / MOBA_BLOCK)
        hi = kbar.astype(BF16)
        r1 = kbar - hi.astype(F32)
        mid = r1.astype(BF16)
        lo = (r1 - mid.astype(F32)).astype(BF16)
        kb3_ref[0:n_blocks, :] = hi
        kb3_ref[n_blocks:2 * n_blocks, :] = mid
        kb3_ref[2 * n_blocks:3 * n_blocks, :] = lo

    qt = q_ref[0, 0]
    g3 = jnp.dot(kb3_ref[...], qt, preferred_element_type=F32)
    gate = g3[0:n_blocks] + g3[n_blocks:2 * n_blocks] + g3[2 * n_blocks:3 * n_blocks]
    blk = lax.broadcasted_iota(jnp.int32, gate.shape, 0)
    eligible = blk < i
    gate = jnp.where(eligible, gate, NEG_INF)
    rank = jnp.zeros(gate.shape, jnp.int32)
    for mth in range(n_blocks):
        other = gate[mth:mth + 1, :]
        beats = jnp.where(other > gate, 1, jnp.where((other == gate) & (blk > mth), 1, 0))
        rank = rank + beats
    selected = (rank < MOBA_TOP_K) & eligible
    selb_ref[...] = jnp.where(selected, 0.0, NEG_INF).astype(F32)

    s = jnp.dot(k_ref[0, i], qt, preferred_element_type=F32) + bias_ref[0, 0]
    carry = _softmax_first(s, v_ref[0, i])

    far_bias = tab_ref[h, NUM_BUCKETS - 1]
    n_far = jnp.maximum(i - (NEAR_BLOCKS - 1), 0)

    def far_body(n, c):
        s = jnp.dot(k_ref[0, n], qt, preferred_element_type=F32)
        s = s + (selb_ref[pl.ds(n, 1), :] + far_bias)
        return _softmax_next(s, v_ref[0, n], *c)

    def near_body(n, c):
        s = jnp.dot(k_ref[0, n], qt, preferred_element_type=F32)
        s = s + bias_ref[0, i - n] + selb_ref[pl.ds(n, 1), :]
        return _softmax_next(s, v_ref[0, n], *c)

    carry = lax.fori_loop(0, n_far, far_body, carry)
    m, l, acc = lax.fori_loop(n_far, i, near_body, carry)
    o_ref[...] = (acc * (1.0 / l)).T.astype(o_ref.dtype)


def moba_attn(qv_t, k, bias, rel_bias, *, batch, n_heads, q_off, k_off, v_off):
    mb = k.shape[1]
    nb = mb // batch
    m = mb * ATT_TILE
    return pl.pallas_call(
        functools.partial(_moba_attn_kernel, n_blocks=nb),
        out_shape=jax.ShapeDtypeStruct((m, n_heads * HEAD_DIM), BF16),
        grid=(batch, n_heads, nb),
        in_specs=[pl.BlockSpec(memory_space=pltpu.SMEM),
                  pl.BlockSpec((1, 1, HEAD_DIM, ATT_TILE), lambda b, h, i: (q_off + h, b * nb + i, 0, 0)),
                  pl.BlockSpec((1, nb, ATT_TILE, HEAD_DIM), lambda b, h, i: (k_off + h, b, 0, 0)),
                  pl.BlockSpec((1, nb, HEAD_DIM, ATT_TILE), lambda b, h, i: (v_off + h, b, 0, 0)),
                  pl.BlockSpec((1, NEAR_BLOCKS, ATT_TILE, ATT_TILE), lambda b, h, i: (h, 0, 0, 0))],
        out_specs=pl.BlockSpec((ATT_TILE, HEAD_DIM), lambda b, h, i: (b * nb + i, h)),
        scratch_shapes=[pltpu.VMEM((nb, HEAD_DIM), F32),
                        pltpu.VMEM((3 * nb, HEAD_DIM), BF16),
                        pltpu.VMEM((nb, ATT_TILE), F32)],
        compiler_params=_params("parallel", "parallel", "arbitrary"),
        name="moba_attn",
    )(rel_bias.astype(F32), qv_t, k, qv_t, bias)


def _fox_attn_kernel(q_ref, k_ref, v_ref, c_ref, o_ref):
    i = pl.program_id(2)
    qt = q_ref[0, 0]

    def logits(n):
        c = c_ref[0, n]
        return jnp.dot(k_ref[0, n], qt, preferred_element_type=F32) - jnp.concatenate(
            [c] * (ATT_TILE // LANES), axis=1)

    key = lax.broadcasted_iota(jnp.int32, (ATT_TILE, ATT_TILE), 0)
    qry = lax.broadcasted_iota(jnp.int32, (ATT_TILE, ATT_TILE), 1)
    s = jnp.where(key <= qry, logits(i), NEG_INF)
    carry = _softmax_first(s, v_ref[0, i])

    def body(n, c):
        return _softmax_next(logits(n), v_ref[0, n], *c)

    m, l, acc = lax.fori_loop(0, i, body, carry)
    o_ref[...] = (acc * (1.0 / l)).T.astype(o_ref.dtype)


def fox_attn(qv_t, k, c_rep, *, batch, n_heads, q_off, k_off, v_off):
    mb = k.shape[1]
    nb = mb // batch
    m = mb * ATT_TILE
    return pl.pallas_call(
        _fox_attn_kernel,
        out_shape=jax.ShapeDtypeStruct((m, n_heads * HEAD_DIM), BF16),
        grid=(batch, n_heads, nb),
        in_specs=[pl.BlockSpec((1, 1, HEAD_DIM, ATT_TILE), lambda b, h, i: (q_off + h, b * nb + i, 0, 0)),
                  pl.BlockSpec((1, nb, ATT_TILE, HEAD_DIM), lambda b, h, i: (k_off + h, b, 0, 0)),
                  pl.BlockSpec((1, nb, HEAD_DIM, ATT_TILE), lambda b, h, i: (v_off + h, b, 0, 0)),
                  pl.BlockSpec((1, nb, ATT_TILE, LANES), lambda b, h, i: (h, b, 0, 0))],
        out_specs=pl.BlockSpec((ATT_TILE, HEAD_DIM), lambda b, h, i: (b * nb + i, h)),
        compiler_params=_params("parallel", "parallel", "arbitrary"),
        name="fox_attn",
    )(qv_t, k, qv_t, c_rep)


def _gated_merge_kernel(oa_ref, of_ref, wa_ref, wf_ref, ga_ref, gf_ref, o_ref):
    ua = jnp.dot(oa_ref[...], wa_ref[...], preferred_element_type=F32)
    uf = jnp.dot(of_ref[...], wf_ref[...], preferred_element_type=F32)
    o_ref[...] = (ga_ref[...].astype(F32) * ua + gf_ref[...].astype(F32) * uf).astype(o_ref.dtype)


def gated_merge(o_a, o_f, w_a, w_f, gates, *, tm=512, tn=1024):
    m, ka = o_a.shape
    kf = o_f.shape[1]
    n = w_a.shape[1]
    nj = n // tn
    return pl.pallas_call(
        _gated_merge_kernel,
        out_shape=jax.ShapeDtypeStruct((m, n), BF16),
        grid=(m // tm, nj),
        in_specs=[pl.BlockSpec((tm, ka), lambda i, j: (i, 0)),
                  pl.BlockSpec((tm, kf), lambda i, j: (i, 0)),
                  pl.BlockSpec((ka, tn), lambda i, j: (0, j)),
                  pl.BlockSpec((kf, tn), lambda i, j: (0, j)),
                  pl.BlockSpec((tm, tn), lambda i, j: (i, j)),
                  pl.BlockSpec((tm, tn), lambda i, j: (i, nj + j))],
        out_specs=pl.BlockSpec((tm, tn), lambda i, j: (i, j)),
        compiler_params=_params("parallel", "parallel"),
        name="gated_merge",
    )(o_a, o_f, w_a, w_f, gates, gates)


def _mm_res_kernel(lhs_ref, w_ref, res_ref, g_ref, o_ref, acc_ref, *, final_norm):
    kk = pl.program_id(1)

    @pl.when(kk == 0)
    def _():
        acc_ref[...] = res_ref[...]

    acc_ref[...] += jnp.dot(lhs_ref[...], w_ref[...], preferred_element_type=F32)

    @pl.when(kk == pl.num_programs(1) - 1)
    def _():
        hres = acc_ref[...]
        if final_norm:
            ms = jnp.mean(hres * hres, axis=-1, keepdims=True)
            hres = hres * lax.rsqrt(ms + RMS_EPS) * g_ref[...]
        o_ref[...] = hres


def mm_res(lhs, w, res, g=None, *, tm=512, tk=1024):
    m, k = lhs.shape
    n = w.shape[1]
    tk = min(tk, k)
    final_norm = g is not None
    if g is None:
        g = jnp.ones((n,), F32)
    return pl.pallas_call(
        functools.partial(_mm_res_kernel, final_norm=final_norm),
        out_shape=jax.ShapeDtypeStruct((m, n), F32),
        grid=(m // tm, k // tk),
        in_specs=[pl.BlockSpec((tm, tk), lambda i, kk: (i, kk)),
                  pl.BlockSpec((tk, n), lambda i, kk: (kk, 0)),
                  pl.BlockSpec((tm, n), lambda i, kk: (i, 0)),
                  pl.BlockSpec((1, n), lambda i, kk: (0, 0))],
        out_specs=pl.BlockSpec((tm, n), lambda i, kk: (i, 0)),
        scratch_shapes=[pltpu.VMEM((tm, n), F32)],
        compiler_params=_params("parallel", "arbitrary"),
        name="mm_res",
    )(lhs, w, res, g.reshape(1, n))


def _cross_attn_kernel(q_ref, kv_ref, w_ref, res_ref, o_ref, *, n_heads):
    width = n_heads * HEAD_DIM
    heads = []
    for h in range(n_heads):
        q = q_ref[:, h * HEAD_DIM:(h + 1) * HEAD_DIM]
        k = kv_ref[0, :, h * HEAD_DIM:(h + 1) * HEAD_DIM]
        v = kv_ref[0, :, width + h * HEAD_DIM:width + (h + 1) * HEAD_DIM]
        s = lax.dot_general(q, k, (((1,), (1,)), ((), ())), preferred_element_type=F32)
        m = jnp.max(s, axis=-1, keepdims=True)
        p = jnp.exp(s - m)
        l = jnp.sum(p, axis=-1, keepdims=True)
        o = jnp.dot(p.astype(BF16), v, preferred_element_type=F32) * (1.0 / l)
        heads.append(o.astype(BF16))
    o_all = jnp.concatenate(heads, axis=1)
    o_ref[...] = res_ref[...] + jnp.dot(o_all, w_ref[...], preferred_element_type=F32)


def cross_attn(q, kv, w_o, res, *, batch, n_heads, tm=512):
    m, width = q.shape
    d = w_o.shape[1]
    n_mem = kv.shape[0] // batch
    tiles_per_batch = (m // batch) // tm
    kv3 = kv.reshape(batch, n_mem, 2 * width)
    return pl.pallas_call(
        functools.partial(_cross_attn_kernel, n_heads=n_heads),
        out_shape=jax.ShapeDtypeStruct((m, d), F32),
        grid=(m // tm,),
        in_specs=[pl.BlockSpec((tm, width), lambda i: (i, 0)),
                  pl.BlockSpec((1, n_mem, 2 * width), lambda i: (i // tiles_per_batch, 0, 0)),
                  pl.BlockSpec((width, d), lambda i: (0, 0)),
                  pl.BlockSpec((tm, d), lambda i: (i, 0))],
        out_specs=pl.BlockSpec((tm, d), lambda i: (i, 0)),
        compiler_params=_params("parallel"),
        name="cross_attn",
    )(q, kv3, w_o, res)


def kernel(x, mem, g_mix, w_in, b_forget, w_branch_moba, w_branch_fox, w_mix_out, rel_bias,
           g_cross, g_mem, w_cq, w_ck, w_cv, w_co, g_mlp, w_ff1, w_ff2, g_final):
    batch, seq, d = x.shape
    depth = w_in.shape[0]
    n_heads = rel_bias.shape[0]
    n_fox = b_forget.shape[1]
    wm = n_heads * HEAD_DIM
    wf = n_fox * HEAD_DIM
    m = batch * seq
    scale = HEAD_DIM ** -0.5
    mem2 = mem.reshape(-1, d)

    bias = moba_bias(rel_bias)
    h = x.reshape(m, d)
    for l in range(depth):
        wi = w_in[l]
        o0 = 0
        parts = []
        for width in (wm, wm, wm, wf, wf, wf, n_fox, d, d):
            parts.append(wi[:, o0:o0 + width])
            o0 += width
        wq_m, wk_m, wv_m, wq_f, wk_f, wv_f, w_fl, wg_a, wg_f = parts
        w_qv = jnp.concatenate([wq_m, wq_f, wv_m, wv_f], axis=1).astype(BF16)
        w_k = jnp.concatenate([wk_m, wk_f], axis=1).astype(BF16)
        w_g = jnp.concatenate([wg_a, wg_f], axis=1).astype(BF16)

        n_q_tiles = (wm + wf) // 1024
        qv_t = rms_proj(h, g_mix[l], w_qv, mode="transposed", scale=scale, n_scaled=n_q_tiles)
        k_hm = rms_proj(h, g_mix[l], w_k, mode="headmajor")
        k_hm = k_hm.reshape(k_hm.shape[0], m // ATT_TILE, ATT_TILE, HEAD_DIM)
        gates = rms_proj(h, g_mix[l], w_g, act="sigmoid")
        c_rep = forget_cumsum(h, g_mix[l], w_fl, b_forget[l], batch=batch, n_heads=n_fox)
        c_rep = c_rep.reshape(n_fox, m // ATT_TILE, ATT_TILE, LANES)

        o_a = moba_attn(qv_t, k_hm, bias, rel_bias, batch=batch, n_heads=n_heads,
                        q_off=0, k_off=0, v_off=n_heads + n_fox)
        o_f = fox_attn(qv_t, k_hm, c_rep, batch=batch, n_heads=n_fox,
                       q_off=n_heads, k_off=n_heads, v_off=2 * n_heads + n_fox)
        merged = gated_merge(o_a, o_f, w_branch_moba[l].astype(BF16), w_branch_fox[l].astype(BF16), gates)
        h = mm_res(merged, w_mix_out[l].astype(BF16), h)

        cw = w_cq.shape[2]
        qc = rms_proj(h, g_cross[l], w_cq[l].astype(BF16), scale=scale, n_scaled=1, tn=cw)
        w_kv = jnp.concatenate([w_ck[l], w_cv[l]], axis=1).astype(BF16)
        kv = rms_proj(mem2, g_mem[l], w_kv, tn=2 * cw)
        h = cross_attn(qc, kv, w_co[l].astype(BF16), h, batch=batch, n_heads=cw // HEAD_DIM)

        u = rms_proj(h, g_mlp[l], w_ff1[l].astype(BF16), act="relu2")
        h = mm_res(u, w_ff2[l].astype(BF16), h, g_final if l == depth - 1 else None)
    return h.reshape(batch, seq, d)
```

```python
import functools
import math

import jax
import jax.numpy as jnp
from jax import lax
from jax.experimental import pallas as pl
from jax.experimental.pallas import tpu as pltpu

F32 = jnp.float32
BF16 = jnp.bfloat16

HEAD_DIM = 128
MOBA_BLOCK = 256
MOBA_TOP_K = 3
NUM_BUCKETS = 32
MAX_DISTANCE = 1024
N_CROSS_HEADS = 4
RMS_EPS = 1e-6
LOG2E = math.log2(math.e)
NEG_INF = -1e30
LANES = 128
ATT_TILE = 256
NEAR_BLOCKS = 5
VMEM_LIMIT = 48 * 1024 * 1024


def _bucket_thresholds():
    max_exact = NUM_BUCKETS // 2
    thr = list(range(1, max_exact + 1))
    for k in range(max_exact + 1, NUM_BUCKETS):
        v = max_exact * (MAX_DISTANCE / max_exact) ** ((k - max_exact) / (NUM_BUCKETS - max_exact))
        n = int(math.floor(v))
        while max_exact + int(math.log(n / max_exact) / math.log(MAX_DISTANCE / max_exact)
                              * (NUM_BUCKETS - max_exact)) < k:
            n += 1
        thr.append(n)
    return tuple(thr)


BUCKET_THRESHOLDS = _bucket_thresholds()
assert (NEAR_BLOCKS - 1) * MOBA_BLOCK + 1 >= BUCKET_THRESHOLDS[-1]


def _params(*sem):
    return pltpu.CompilerParams(dimension_semantics=sem, vmem_limit_bytes=VMEM_LIMIT)


def _rms_proj_kernel(x_ref, g_ref, w_ref, o_ref, a_ref, *, mode, act, scale, n_scaled):
    j = pl.program_id(1)

    @pl.when(j == 0)
    def _():
        x = x_ref[...]
        ms = jnp.mean(x * x, axis=-1, keepdims=True)
        a_ref[...] = (x * lax.rsqrt(ms + RMS_EPS) * g_ref[...]).astype(BF16)

    acc = jnp.dot(a_ref[...], w_ref[...], preferred_element_type=F32)
    if scale is not None:
        acc = acc * jnp.where(j < n_scaled, F32(scale), F32(1.0))
    tm, tn = acc.shape
    if mode == "rowmajor":
        if act == "sigmoid":
            acc = jax.nn.sigmoid(acc)
        elif act == "relu2":
            acc = jnp.square(jnp.maximum(acc, 0.0))
        o_ref[...] = acc.astype(o_ref.dtype)
    elif mode == "headmajor":
        for c in range(tn // LANES):
            o_ref[c] = acc[:, c * LANES:(c + 1) * LANES].astype(o_ref.dtype)
    else:
        for c in range(tn // LANES):
            for r in range(tm // ATT_TILE):
                blk = acc[r * ATT_TILE:(r + 1) * ATT_TILE, c * LANES:(c + 1) * LANES]
                o_ref[c, r] = blk.T.astype(o_ref.dtype)


def rms_proj(x, g, w, *, mode="rowmajor", act=None, scale=None, n_scaled=0,
             out_dtype=BF16, tm=512, tn=1024):
    m, d = x.shape
    n = w.shape[1]
    tm, tn = min(tm, m), min(tn, n)
    if mode == "rowmajor":
        out_shape = jax.ShapeDtypeStruct((m, n), out_dtype)
        out_spec = pl.BlockSpec((tm, tn), lambda i, j: (i, j))
    elif mode == "headmajor":
        out_shape = jax.ShapeDtypeStruct((n // LANES, m, LANES), out_dtype)
        out_spec = pl.BlockSpec((tn // LANES, tm, LANES), lambda i, j: (j, i, 0))
    else:
        out_shape = jax.ShapeDtypeStruct((n // LANES, m // ATT_TILE, LANES, ATT_TILE), out_dtype)
        out_spec = pl.BlockSpec((tn // LANES, tm // ATT_TILE, LANES, ATT_TILE),
                                lambda i, j: (j, i, 0, 0))
    return pl.pallas_call(
        functools.partial(_rms_proj_kernel, mode=mode, act=act, scale=scale, n_scaled=n_scaled),
        out_shape=out_shape,
        grid=(m // tm, n // tn),
        in_specs=[pl.BlockSpec((tm, d), lambda i, j: (i, 0)),
                  pl.BlockSpec((1, d), lambda i, j: (0, 0)),
                  pl.BlockSpec((d, tn), lambda i, j: (0, j))],
        out_specs=out_spec,
        scratch_shapes=[pltpu.VMEM((tm, d), BF16)],
        compiler_params=_params("parallel", "arbitrary"),
        name="rms_proj_" + mode,
    )(x, g.reshape(1, d), w)


def _forget_cumsum_kernel(x_ref, g_ref, w_ref, b_ref, o_ref, carry_ref, *, n_heads):
    t = pl.program_id(1)

    @pl.when(t == 0)
    def _():
        carry_ref[...] = jnp.zeros_like(carry_ref)

    x = x_ref[...]
    tm = x.shape[0]
    ms = jnp.mean(x * x, axis=-1, keepdims=True)
    a = x * lax.rsqrt(ms + RMS_EPS) * g_ref[...]
    f = jnp.dot(a, w_ref[...], precision=lax.Precision.HIGHEST,
                preferred_element_type=F32) + b_ref[...]
    logf = jnp.minimum(f, 0.0) - jnp.log1p(jnp.exp(-jnp.abs(f)))
    logf = logf * LOG2E
    row = lax.broadcasted_iota(jnp.int32, (tm, tm), 0)
    col = lax.broadcasted_iota(jnp.int32, (tm, tm), 1)
    tri = (col <= row).astype(F32)
    for h in range(n_heads):
        lf = jnp.broadcast_to(logf[:, h:h + 1], (tm, LANES))
        c = jnp.dot(tri, lf, precision=lax.Precision.HIGHEST,
                    preferred_element_type=F32) + carry_ref[h, 0:1, :]
        o_ref[h] = c
        carry_ref[h] = jnp.broadcast_to(c[tm - 1:tm, :], carry_ref.shape[1:])


def forget_cumsum(x, g, w_f, b_f, *, batch, n_heads, tm=256):
    m, d = x.shape
    s = m // batch
    nt = s // tm
    w_pad = jnp.zeros((d, LANES), F32).at[:, :n_heads].set(w_f)
    b_pad = jnp.zeros((1, LANES), F32).at[0, :n_heads].set(b_f)
    return pl.pallas_call(
        functools.partial(_forget_cumsum_kernel, n_heads=n_heads),
        out_shape=jax.ShapeDtypeStruct((n_heads, m, LANES), F32),
        grid=(batch, nt),
        in_specs=[pl.BlockSpec((tm, d), lambda b, t: (b * nt + t, 0)),
                  pl.BlockSpec((1, d), lambda b, t: (0, 0)),
                  pl.BlockSpec((d, LANES), lambda b, t: (0, 0)),
                  pl.BlockSpec((1, LANES), lambda b, t: (0, 0))],
        out_specs=pl.BlockSpec((n_heads, tm, LANES), lambda b, t: (0, b * nt + t, 0)),
        scratch_shapes=[pltpu.VMEM((n_heads, 8, LANES), F32)],
        compiler_params=_params("parallel", "arbitrary"),
        name="forget_cumsum",
    )(x, g.reshape(1, d), w_pad, b_pad)


def _moba_bias_kernel(tab_ref, o_ref):
    h = pl.program_id(0)
    key = lax.broadcasted_iota(jnp.int32, (ATT_TILE, ATT_TILE), 0)
    qry = lax.broadcasted_iota(jnp.int32, (ATT_TILE, ATT_TILE), 1)
    for delta in range(NEAR_BLOCKS):
        dist = delta * MOBA_BLOCK + qry - key
        val = jnp.full((ATT_TILE, ATT_TILE), tab_ref[h, 0], F32)
        for k in range(1, NUM_BUCKETS):
            val = jnp.where(dist >= BUCKET_THRESHOLDS[k - 1], tab_ref[h, k], val)
        val = val * LOG2E
        if delta == 0:
            val = jnp.where(dist >= 0, val, NEG_INF)
        o_ref[0, delta] = val


def moba_bias(rel_bias):
    h = rel_bias.shape[0]
    return pl.pallas_call(
        _moba_bias_kernel,
        out_shape=jax.ShapeDtypeStruct((h, NEAR_BLOCKS, ATT_TILE, ATT_TILE), F32),
        grid=(h,),
        in_specs=[pl.BlockSpec(memory_space=pltpu.SMEM)],
        out_specs=pl.BlockSpec((1, NEAR_BLOCKS, ATT_TILE, ATT_TILE), lambda i: (i, 0, 0, 0)),
        compiler_params=_params("parallel"),
        name="moba_bias",
    )(rel_bias.astype(F32))


def _softmax_first(s, vt):
    m = jnp.max(s, axis=0, keepdims=True)
    p = jnp.exp2(s - m)
    l = jnp.sum(p, axis=0, keepdims=True)
    acc = jnp.dot(vt, p.astype(BF16), preferred_element_type=F32)
    return m, l, acc


def _softmax_next(s, vt, m, l, acc):
    m_new = jnp.maximum(m, jnp.max(s, axis=0, keepdims=True))
    alpha = jnp.exp2(m - m_new)
    p = jnp.exp2(s - m_new)
    l = alpha * l + jnp.sum(p, axis=0, keepdims=True)
    acc = alpha * acc + jnp.dot(vt, p.astype(BF16), preferred_element_type=F32)
    return m_new, l, acc


def _moba_attn_kernel(tab_ref, q_ref, k_ref, v_ref, bias_ref, o_ref, kbar_ref, kb3_ref, selb_ref,
                      *, n_blocks):
    h = pl.program_id(1)
    i = pl.program_id(2)

    @pl.when(i == 0)
    def _():
        for n in range(n_blocks):
            kbar_ref[n:n + 1, :] = jnp.sum(k_ref[0, n].astype(F32), axis=0, keepdims=True)
        kbar = kbar_ref[...] * (1.0 / MOBA_BLOCK)
        hi = kbar.astype(BF16)
        r1 = kbar - hi.astype(F32)
        mid = r1.astype(BF16)
        lo = (r1 - mid.astype(F32)).astype(BF16)
        kb3_ref[0:n_blocks, :] = hi
        kb3_ref[n_blocks:2 * n_blocks, :] = mid
        kb3_ref[2 * n_blocks:3 * n_blocks, :] = lo

    qt = q_ref[0, 0]
    g3 = jnp.dot(kb3_ref[...], qt, preferred_element_type=F32)
    gate = g3[0:n_blocks] + g3[n_blocks:2 * n_blocks] + g3[2 * n_blocks:3 * n_blocks]
    blk = lax.broadcasted_iota(jnp.int32, gate.shape, 0)
    eligible = blk < i
    gate = jnp.where(eligible, gate, NEG_INF)
    rank = jnp.zeros(gate.shape, jnp.int32)
    for mth in range(n_blocks):
        other = gate[mth:mth + 1, :]
        beats = jnp.where(other > gate, 1, jnp.where((other == gate) & (blk > mth), 1, 0))
        rank = rank + beats
    selected = (rank < MOBA_TOP_K) & eligible
    selb_ref[...] = jnp.where(selected, 0.0, NEG_INF).astype(F32)

    s = jnp.dot(k_ref[0, i], qt, preferred_element_type=F32) + bias_ref[0, 0]
    carry = _softmax_first(s, v_ref[0, i])

    far_bias = tab_ref[h, NUM_BUCKETS - 1] * LOG2E
    n_far = jnp.maximum(i - (NEAR_BLOCKS - 1), 0)

    def far_body(n, c):
        s = jnp.dot(k_ref[0, n], qt, preferred_element_type=F32)
        s = s + (selb_ref[pl.ds(n, 1), :] + far_bias)
        return _softmax_next(s, v_ref[0, n], *c)

    def near_body(n, c):
        s = jnp.dot(k_ref[0, n], qt, preferred_element_type=F32)
        s = s + bias_ref[0, i - n] + selb_ref[pl.ds(n, 1), :]
        return _softmax_next(s, v_ref[0, n], *c)

    carry = lax.fori_loop(0, n_far, far_body, carry)
    m, l, acc = lax.fori_loop(n_far, i, near_body, carry)
    o_ref[...] = (acc * (1.0 / l)).T.astype(o_ref.dtype)


def moba_attn(qv_t, k, bias, rel_bias, *, batch, n_heads, q_off, k_off, v_off):
    mb = k.shape[1]
    nb = mb // batch
    m = mb * ATT_TILE
    return pl.pallas_call(
        functools.partial(_moba_attn_kernel, n_blocks=nb),
        out_shape=jax.ShapeDtypeStruct((m, n_heads * HEAD_DIM), BF16),
        grid=(batch, n_heads, nb),
        in_specs=[pl.BlockSpec(memory_space=pltpu.SMEM),
                  pl.BlockSpec((1, 1, HEAD_DIM, ATT_TILE), lambda b, h, i: (q_off + h, b * nb + i, 0, 0)),
                  pl.BlockSpec((1, nb, ATT_TILE, HEAD_DIM), lambda b, h, i: (k_off + h, b, 0, 0)),
                  pl.BlockSpec((1, nb, HEAD_DIM, ATT_TILE), lambda b, h, i: (v_off + h, b, 0, 0)),
                  pl.BlockSpec((1, NEAR_BLOCKS, ATT_TILE, ATT_TILE), lambda b, h, i: (h, 0, 0, 0))],
        out_specs=pl.BlockSpec((ATT_TILE, HEAD_DIM), lambda b, h, i: (b * nb + i, h)),
        scratch_shapes=[pltpu.VMEM((nb, HEAD_DIM), F32),
                        pltpu.VMEM((3 * nb, HEAD_DIM), BF16),
                        pltpu.VMEM((nb, ATT_TILE), F32)],
        compiler_params=_params("parallel", "parallel", "arbitrary"),
        name="moba_attn",
    )(rel_bias.astype(F32), qv_t, k, qv_t, bias)


def _fox_attn_kernel(q_ref, k_ref, v_ref, c_ref, o_ref, s_ref, m_ref, l_ref, acc_ref, *, sub):
    i = pl.program_id(2)
    t = sub * ATT_TILE
    qt = jnp.concatenate([q_ref[0, r] for r in range(sub)], axis=1)

    def logits(n):
        k = k_ref[0, pl.ds(n * sub, sub)].reshape(t, HEAD_DIM)
        c = c_ref[0, pl.ds(n * sub, sub)].reshape(t, LANES)
        return jnp.dot(k, qt, preferred_element_type=F32) - jnp.concatenate([c] * (t // LANES), axis=1)

    def values(n):
        return jnp.concatenate([v_ref[0, n * sub + r] for r in range(sub)], axis=1)

    def step(cur, n, prefetch, diagonal):
        if prefetch is not None:
            s_ref[1 - cur] = logits(prefetch)
        s = s_ref[cur]
        if diagonal:
            key = lax.broadcasted_iota(jnp.int32, (t, t), 0)
            qry = lax.broadcasted_iota(jnp.int32, (t, t), 1)
            s = jnp.where(key <= qry, s, NEG_INF)
        m, l, acc = _softmax_next(s, values(n), m_ref[...], l_ref[...], acc_ref[...])
        m_ref[...] = m
        l_ref[...] = l
        acc_ref[...] = acc

    m_ref[...] = jnp.full(m_ref.shape, NEG_INF, F32)
    l_ref[...] = jnp.zeros(l_ref.shape, F32)
    acc_ref[...] = jnp.zeros(acc_ref.shape, F32)
    s_ref[0] = logits(0)

    def pair_body(j, carry):
        step(0, 2 * j, 2 * j + 1, False)
        step(1, 2 * j + 1, 2 * j + 2, False)
        return carry

    lax.fori_loop(0, i // 2, pair_body, 0)

    @pl.when(i % 2 == 1)
    def _():
        step(0, i - 1, i, False)
        step(1, i, None, True)

    @pl.when(i % 2 == 0)
    def _():
        step(0, i, None, True)

    o_ref[...] = (acc_ref[...] * (1.0 / l_ref[...])).T.astype(o_ref.dtype)


def fox_attn(qv_t, k, c_rep, *, batch, n_heads, q_off, k_off, v_off, sub=2):
    mb = k.shape[1]
    nb = mb // batch
    m = mb * ATT_TILE
    t = sub * ATT_TILE
    return pl.pallas_call(
        functools.partial(_fox_attn_kernel, sub=sub),
        out_shape=jax.ShapeDtypeStruct((m, n_heads * HEAD_DIM), BF16),
        grid=(batch, n_heads, nb // sub),
        in_specs=[pl.BlockSpec((1, sub, HEAD_DIM, ATT_TILE),
                               lambda b, h, i: (q_off + h, b * (nb // sub) + i, 0, 0)),
                  pl.BlockSpec((1, nb, ATT_TILE, HEAD_DIM), lambda b, h, i: (k_off + h, b, 0, 0)),
                  pl.BlockSpec((1, nb, HEAD_DIM, ATT_TILE), lambda b, h, i: (v_off + h, b, 0, 0)),
                  pl.BlockSpec((1, nb, ATT_TILE, LANES), lambda b, h, i: (h, b, 0, 0))],
        out_specs=pl.BlockSpec((t, HEAD_DIM), lambda b, h, i: (b * (nb // sub) + i, h)),
        scratch_shapes=[pltpu.VMEM((2, t, t), F32),
                        pltpu.VMEM((1, t), F32),
                        pltpu.VMEM((1, t), F32),
                        pltpu.VMEM((HEAD_DIM, t), F32)],
        compiler_params=_params("parallel", "parallel", "arbitrary"),
        name="fox_attn",
    )(qv_t, k, qv_t, c_rep)


def _gated_merge_kernel(oa_ref, of_ref, wa_ref, wf_ref, ga_ref, gf_ref, o_ref):
    ua = jnp.dot(oa_ref[...], wa_ref[...], preferred_element_type=F32)
    uf = jnp.dot(of_ref[...], wf_ref[...], preferred_element_type=F32)
    o_ref[...] = (ga_ref[...].astype(F32) * ua + gf_ref[...].astype(F32) * uf).astype(o_ref.dtype)


def gated_merge(o_a, o_f, w_a, w_f, gates, *, tm=512, tn=1024):
    m, ka = o_a.shape
    kf = o_f.shape[1]
    n = w_a.shape[1]
    nj = n // tn
    return pl.pallas_call(
        _gated_merge_kernel,
        out_shape=jax.ShapeDtypeStruct((m, n), BF16),
        grid=(m // tm, nj),
        in_specs=[pl.BlockSpec((tm, ka), lambda i, j: (i, 0)),
                  pl.BlockSpec((tm, kf), lambda i, j: (i, 0)),
                  pl.BlockSpec((ka, tn), lambda i, j: (0, j)),
                  pl.BlockSpec((kf, tn), lambda i, j: (0, j)),
                  pl.BlockSpec((tm, tn), lambda i, j: (i, j)),
                  pl.BlockSpec((tm, tn), lambda i, j: (i, nj + j))],
        out_specs=pl.BlockSpec((tm, tn), lambda i, j: (i, j)),
        compiler_params=_params("parallel", "parallel"),
        name="gated_merge",
    )(o_a, o_f, w_a, w_f, gates, gates)


def _mm_res_kernel(lhs_ref, w_ref, res_ref, g_ref, o_ref, acc_ref, *, final_norm):
    kk = pl.program_id(1)

    @pl.when(kk == 0)
    def _():
        acc_ref[...] = res_ref[...]

    acc_ref[...] += jnp.dot(lhs_ref[...], w_ref[...], preferred_element_type=F32)

    @pl.when(kk == pl.num_programs(1) - 1)
    def _():
        hres = acc_ref[...]
        if final_norm:
            ms = jnp.mean(hres * hres, axis=-1, keepdims=True)
            hres = hres * lax.rsqrt(ms + RMS_EPS) * g_ref[...]
        o_ref[...] = hres


def mm_res(lhs, w, res, g=None, *, tm=512, tk=1024):
    m, k = lhs.shape
    n = w.shape[1]
    tk = min(tk, k)
    final_norm = g is not None
    if g is None:
        g = jnp.ones((n,), F32)
    return pl.pallas_call(
        functools.partial(_mm_res_kernel, final_norm=final_norm),
        out_shape=jax.ShapeDtypeStruct((m, n), F32),
        grid=(m // tm, k // tk),
        in_specs=[pl.BlockSpec((tm, tk), lambda i, kk: (i, kk)),
                  pl.BlockSpec((tk, n), lambda i, kk: (kk, 0)),
                  pl.BlockSpec((tm, n), lambda i, kk: (i, 0)),
                  pl.BlockSpec((1, n), lambda i, kk: (0, 0))],
        out_specs=pl.BlockSpec((tm, n), lambda i, kk: (i, 0)),
        scratch_shapes=[pltpu.VMEM((tm, n), F32)],
        compiler_params=_params("parallel", "arbitrary"),
        name="mm_res",
    )(lhs, w, res, g.reshape(1, n))


def _cross_attn_kernel(q_ref, kv_ref, w_ref, res_ref, o_ref, *, n_heads):
    width = n_heads * HEAD_DIM
    heads = []
    for h in range(n_heads):
        q = q_ref[:, h * HEAD_DIM:(h + 1) * HEAD_DIM]
        k = kv_ref[0, :, h * HEAD_DIM:(h + 1) * HEAD_DIM]
        v = kv_ref[0, :, width + h * HEAD_DIM:width + (h + 1) * HEAD_DIM]
        s = lax.dot_general(q, k, (((1,), (1,)), ((), ())), preferred_element_type=F32)
        m = jnp.max(s, axis=-1, keepdims=True)
        p = jnp.exp(s - m)
        l = jnp.sum(p, axis=-1, keepdims=True)
        o = jnp.dot(p.astype(BF16), v, preferred_element_type=F32) * (1.0 / l)
        heads.append(o.astype(BF16))
    o_all = jnp.concatenate(heads, axis=1)
    o_ref[...] = res_ref[...] + jnp.dot(o_all, w_ref[...], preferred_element_type=F32)


def cross_attn(q, kv, w_o, res, *, batch, n_heads, tm=512):
    m, width = q.shape
    d = w_o.shape[1]
    n_mem = kv.shape[0] // batch
    tiles_per_batch = (m // batch) // tm
    kv3 = kv.reshape(batch, n_mem, 2 * width)
    return pl.pallas_call(
        functools.partial(_cross_attn_kernel, n_heads=n_heads),
        out_shape=jax.ShapeDtypeStruct((m, d), F32),
        grid=(m // tm,),
        in_specs=[pl.BlockSpec((tm, width), lambda i: (i, 0)),
                  pl.BlockSpec((1, n_mem, 2 * width), lambda i: (i // tiles_per_batch, 0, 0)),
                  pl.BlockSpec((width, d), lambda i: (0, 0)),
                  pl.BlockSpec((tm, d), lambda i: (i, 0))],
        out_specs=pl.BlockSpec((tm, d), lambda i: (i, 0)),
        compiler_params=_params("parallel"),
        name="cross_attn",
    )(q, kv3, w_o, res)


def kernel(x, mem, g_mix, w_in, b_forget, w_branch_moba, w_branch_fox, w_mix_out, rel_bias,
           g_cross, g_mem, w_cq, w_ck, w_cv, w_co, g_mlp, w_ff1, w_ff2, g_final):
    batch, seq, d = x.shape
    depth = w_in.shape[0]
    n_heads = rel_bias.shape[0]
    n_fox = b_forget.shape[1]
    wm = n_heads * HEAD_DIM
    wf = n_fox * HEAD_DIM
    m = batch * seq
    scale = HEAD_DIM ** -0.5
    mem2 = mem.reshape(-1, d)

    bias = moba_bias(rel_bias)
    h = x.reshape(m, d)
    for l in range(depth):
        wi = w_in[l]
        o0 = 0
        parts = []
        for width in (wm, wm, wm, wf, wf, wf, n_fox, d, d):
            parts.append(wi[:, o0:o0 + width])
            o0 += width
        wq_m, wk_m, wv_m, wq_f, wk_f, wv_f, w_fl, wg_a, wg_f = parts
        w_qv = jnp.concatenate([wq_m, wq_f, wv_m, wv_f], axis=1).astype(BF16)
        w_k = jnp.concatenate([wk_m, wk_f], axis=1).astype(BF16)
        w_g = jnp.concatenate([wg_a, wg_f], axis=1).astype(BF16)

        n_q_tiles = (wm + wf) // 1024
        qv_t = rms_proj(h, g_mix[l], w_qv, mode="transposed", scale=scale * LOG2E, n_scaled=n_q_tiles)
        k_hm = rms_proj(h, g_mix[l], w_k, mode="headmajor")
        k_hm = k_hm.reshape(k_hm.shape[0], m // ATT_TILE, ATT_TILE, HEAD_DIM)
        gates = rms_proj(h, g_mix[l], w_g, act="sigmoid")
        c_rep = forget_cumsum(h, g_mix[l], w_fl, b_forget[l], batch=batch, n_heads=n_fox)
        c_rep = c_rep.reshape(n_fox, m // ATT_TILE, ATT_TILE, LANES)

        o_a = moba_attn(qv_t, k_hm, bias, rel_bias, batch=batch, n_heads=n_heads,
                        q_off=0, k_off=0, v_off=n_heads + n_fox)
        o_f = fox_attn(qv_t, k_hm, c_rep, batch=batch, n_heads=n_fox,
                       q_off=n_heads, k_off=n_heads, v_off=2 * n_heads + n_fox)
        merged = gated_merge(o_a, o_f, w_branch_moba[l].astype(BF16), w_branch_fox[l].astype(BF16), gates)
        h = mm_res(merged, w_mix_out[l].astype(BF16), h)

        cw = w_cq.shape[2]
        qc = rms_proj(h, g_cross[l], w_cq[l].astype(BF16), scale=scale, n_scaled=1, tn=cw)
        w_kv = jnp.concatenate([w_ck[l], w_cv[l]], axis=1).astype(BF16)
        kv = rms_proj(mem2, g_mem[l], w_kv, tn=2 * cw)
        h = cross_attn(qc, kv, w_co[l].astype(BF16), h, batch=batch, n_heads=cw // HEAD_DIM)

        u = rms_proj(h, g_mlp[l], w_ff1[l].astype(BF16), act="relu2")
        h = mm_res(u, w_ff2[l].astype(BF16), h, g_final if l == depth - 1 else None)
    return h.reshape(batch, seq, d)
```

```python
import functools
import math

import jax
import jax.numpy as jnp
from jax import lax
from jax.experimental import pallas as pl
from jax.experimental.pallas import tpu as pltpu

F32 = jnp.float32
BF16 = jnp.bfloat16

HEAD_DIM = 128
MOBA_BLOCK = 256
MOBA_TOP_K = 3
NUM_BUCKETS = 32
MAX_DISTANCE = 1024
N_CROSS_HEADS = 4
RMS_EPS = 1e-6
LOG2E = math.log2(math.e)
NEG_INF = -1e30
LANES = 128
ATT_TILE = 256
TILE = 2 * ATT_TILE
AUG = 256
NEAR_BLOCKS = 5
VMEM_LIMIT = 56 * 1024 * 1024
ROW_TILE = 1024
COL_TILE = 512


def _bucket_thresholds():
    max_exact = NUM_BUCKETS // 2
    thr = list(range(1, max_exact + 1))
    for k in range(max_exact + 1, NUM_BUCKETS):
        v = max_exact * (MAX_DISTANCE / max_exact) ** ((k - max_exact) / (NUM_BUCKETS - max_exact))
        n = int(math.floor(v))
        while max_exact + int(math.log(n / max_exact) / math.log(MAX_DISTANCE / max_exact)
                              * (NUM_BUCKETS - max_exact)) < k:
            n += 1
        thr.append(n)
    return tuple(thr)


BUCKET_THRESHOLDS = _bucket_thresholds()
assert (NEAR_BLOCKS - 1) * MOBA_BLOCK + 1 >= BUCKET_THRESHOLDS[-1]


def _params(*sem):
    return pltpu.CompilerParams(dimension_semantics=sem, vmem_limit_bytes=VMEM_LIMIT)


def _rms_proj_kernel(x_ref, g_ref, w_ref, o_ref, a_ref, *, mode, act, scale, n_scaled):
    j = pl.program_id(1)

    @pl.when(j == 0)
    def _():
        x = x_ref[...]
        ms = jnp.mean(x * x, axis=-1, keepdims=True)
        a_ref[...] = (x * lax.rsqrt(ms + RMS_EPS) * g_ref[...]).astype(BF16)

    acc = jnp.dot(a_ref[...], w_ref[...], preferred_element_type=F32)
    if scale is not None:
        acc = acc * jnp.where(j < n_scaled, F32(scale), F32(1.0))
    tm, tn = acc.shape
    if mode == "rowmajor":
        if act == "sigmoid":
            acc = jax.nn.sigmoid(acc)
        elif act == "relu2":
            acc = jnp.square(jnp.maximum(acc, 0.0))
        o_ref[...] = acc.astype(o_ref.dtype)
    elif mode == "headmajor":
        for c in range(tn // LANES):
            o_ref[c] = acc[:, c * LANES:(c + 1) * LANES].astype(o_ref.dtype)
    else:
        for c in range(tn // LANES):
            for r in range(tm // ATT_TILE):
                blk = acc[r * ATT_TILE:(r + 1) * ATT_TILE, c * LANES:(c + 1) * LANES]
                o_ref[c, r] = blk.T.astype(o_ref.dtype)


def rms_proj(x, g, w, *, mode="rowmajor", act=None, scale=None, n_scaled=0,
             out_dtype=BF16, tm=ROW_TILE, tn=COL_TILE):
    m, d = x.shape
    n = w.shape[1]
    tm, tn = min(tm, m), min(tn, n)
    if mode == "rowmajor":
        out_shape = jax.ShapeDtypeStruct((m, n), out_dtype)
        out_spec = pl.BlockSpec((tm, tn), lambda i, j: (i, j))
    elif mode == "headmajor":
        out_shape = jax.ShapeDtypeStruct((n // LANES, m, LANES), out_dtype)
        out_spec = pl.BlockSpec((tn // LANES, tm, LANES), lambda i, j: (j, i, 0))
    else:
        out_shape = jax.ShapeDtypeStruct((n // LANES, m // ATT_TILE, LANES, ATT_TILE), out_dtype)
        out_spec = pl.BlockSpec((tn // LANES, tm // ATT_TILE, LANES, ATT_TILE),
                                lambda i, j: (j, i, 0, 0))
    return pl.pallas_call(
        functools.partial(_rms_proj_kernel, mode=mode, act=act, scale=scale, n_scaled=n_scaled),
        out_shape=out_shape,
        grid=(m // tm, n // tn),
        in_specs=[pl.BlockSpec((tm, d), lambda i, j: (i, 0)),
                  pl.BlockSpec((1, d), lambda i, j: (0, 0)),
                  pl.BlockSpec((d, tn), lambda i, j: (0, j))],
        out_specs=out_spec,
        scratch_shapes=[pltpu.VMEM((tm, d), BF16)],
        compiler_params=_params("parallel", "arbitrary"),
        name="rms_proj_" + mode,
    )(x, g.reshape(1, d), w)


def _forget_cumsum_kernel(x_ref, g_ref, w_ref, b_ref, o_ref, carry_ref):
    t = pl.program_id(1)

    @pl.when(t == 0)
    def _():
        carry_ref[...] = jnp.zeros_like(carry_ref)

    def bf16_dot(a, b):
        return jnp.dot(a.astype(BF16), b.astype(BF16), preferred_element_type=F32)

    x = x_ref[...]
    tm = x.shape[0]
    ms = jnp.mean(x * x, axis=-1, keepdims=True)
    a = x * lax.rsqrt(ms + RMS_EPS) * g_ref[...]
    a_hi, a_lo, _ = _split3(a)
    w_hi, w_lo, _ = _split3(w_ref[...])
    f = bf16_dot(a_hi, w_hi) + (bf16_dot(a_lo, w_hi) + bf16_dot(a_hi, w_lo)) + b_ref[...]
    logf = jnp.minimum(f, 0.0) - jnp.log1p(jnp.exp(-jnp.abs(f)))
    logf = logf * LOG2E
    row = lax.broadcasted_iota(jnp.int32, (tm, tm), 0)
    col = lax.broadcasted_iota(jnp.int32, (tm, tm), 1)
    tri = jnp.where(col <= row, 1.0, 0.0).astype(BF16)
    hi, mid, lo = _split3(logf)
    c = (bf16_dot(tri, hi) + (bf16_dot(tri, mid) + bf16_dot(tri, lo))) + carry_ref[0:1, :]
    o_ref[...] = c
    carry_ref[...] = jnp.broadcast_to(c[tm - 1:tm, :], carry_ref.shape)


def forget_cumsum(x, g, w_f, b_f, *, batch, n_heads, tm=512):
    m, d = x.shape
    s = m // batch
    nt = s // tm
    w_pad = jnp.zeros((d, LANES), F32).at[:, :n_heads].set(w_f)
    b_pad = jnp.zeros((1, LANES), F32).at[0, :n_heads].set(b_f)
    return pl.pallas_call(
        _forget_cumsum_kernel,
        out_shape=jax.ShapeDtypeStruct((m, LANES), F32),
        grid=(batch, nt),
        in_specs=[pl.BlockSpec((tm, d), lambda b, t: (b * nt + t, 0)),
                  pl.BlockSpec((1, d), lambda b, t: (0, 0)),
                  pl.BlockSpec((d, LANES), lambda b, t: (0, 0)),
                  pl.BlockSpec((1, LANES), lambda b, t: (0, 0))],
        out_specs=pl.BlockSpec((tm, LANES), lambda b, t: (b * nt + t, 0)),
        scratch_shapes=[pltpu.VMEM((8, LANES), F32)],
        compiler_params=_params("parallel", "arbitrary"),
        name="forget_cumsum",
    )(x, g.reshape(1, d), w_pad, b_pad)


def _moba_bias_kernel(tab_ref, o_ref):
    h = pl.program_id(0)
    key = lax.broadcasted_iota(jnp.int32, (ATT_TILE, ATT_TILE), 0)
    qry = lax.broadcasted_iota(jnp.int32, (ATT_TILE, ATT_TILE), 1)
    far = tab_ref[h, NUM_BUCKETS - 1] * LOG2E

    def block(delta):
        dist = delta * MOBA_BLOCK + qry - key
        val = jnp.full((ATT_TILE, ATT_TILE), tab_ref[h, 0], F32)
        for k in range(1, NUM_BUCKETS):
            val = jnp.where(dist >= BUCKET_THRESHOLDS[k - 1], tab_ref[h, k], val)
        val = val * LOG2E
        if delta == 0:
            val = jnp.where(dist >= 0, val, NEG_INF)
        return val

    t = [block(delta) for delta in range(NEAR_BLOCKS + 1)]
    a = ATT_TILE
    o_ref[0, 0, 0:a, 0:a] = t[0]
    o_ref[0, 0, 0:a, a:2 * a] = t[1]
    o_ref[0, 0, a:2 * a, 0:a] = jnp.full((a, a), NEG_INF, F32)
    o_ref[0, 0, a:2 * a, a:2 * a] = t[0]
    for d in (1, 2):
        o_ref[0, d, 0:a, 0:a] = t[2 * d] - far
        o_ref[0, d, 0:a, a:2 * a] = t[2 * d + 1] - far
        o_ref[0, d, a:2 * a, 0:a] = t[2 * d - 1] - far
        o_ref[0, d, a:2 * a, a:2 * a] = t[2 * d] - far


def moba_bias(rel_bias):
    h = rel_bias.shape[0]
    return pl.pallas_call(
        _moba_bias_kernel,
        out_shape=jax.ShapeDtypeStruct((h, 3, TILE, TILE), F32),
        grid=(h,),
        in_specs=[pl.BlockSpec(memory_space=pltpu.SMEM)],
        out_specs=pl.BlockSpec((1, 3, TILE, TILE), lambda i: (i, 0, 0, 0)),
        compiler_params=_params("parallel"),
        name="moba_bias",
    )(rel_bias.astype(F32))


class _Softmax:
    def __init__(self, s_ref, m_ref, l_ref, acc_ref, load_values):
        self.s_ref, self.m_ref, self.l_ref, self.acc_ref = s_ref, m_ref, l_ref, acc_ref
        self.load_values = load_values

    def reset(self):
        self.m_ref[...] = jnp.full(self.m_ref.shape, NEG_INF, F32)
        self.l_ref[...] = jnp.zeros(self.l_ref.shape, F32)
        self.acc_ref[...] = jnp.zeros(self.acc_ref.shape, F32)

    def fold(self, slot, tile, mask=None):
        s = self.s_ref[slot]
        if mask is not None:
            s = jnp.where(mask, s, NEG_INF)
        m = self.m_ref[...]
        m_new = jnp.maximum(m, jnp.max(s, axis=0, keepdims=True))
        alpha = jnp.exp2(m - m_new)
        p = jnp.exp2(s - m_new)
        self.l_ref[...] = alpha * self.l_ref[...] + jnp.sum(p, axis=0, keepdims=True)
        self.m_ref[...] = m_new
        self.acc_ref[...] = alpha * self.acc_ref[...] + jnp.dot(
            self.load_values(tile), p.astype(BF16), preferred_element_type=F32)

    def result(self):
        return self.acc_ref[...] * (1.0 / self.l_ref[...])


def _softmax_scratch(t):
    return [pltpu.VMEM((2, t, t), F32),
            pltpu.VMEM((1, t), F32),
            pltpu.VMEM((1, t), F32),
            pltpu.VMEM((HEAD_DIM, t), F32)]


def _split3(x):
    hi = x.astype(BF16).astype(F32)
    mid = (x - hi).astype(BF16).astype(F32)
    lo = (x - hi - mid).astype(BF16).astype(F32)
    return hi, mid, lo


def _moba_attn_kernel(tab_ref, q_ref, k_ref, v_ref, bias_ref, o_ref,
                      kaug_ref, kbar_ref, kb3_ref, s_ref, m_ref, l_ref, acc_ref, *, n_blocks):
    h = pl.program_id(1)
    i = pl.program_id(2)
    n_tiles = n_blocks // 2

    @pl.when(i == 0)
    def _():
        for n in range(n_blocks):
            kbar_ref[n:n + 1, :] = jnp.sum(k_ref[0, n].astype(F32), axis=0, keepdims=True)
        hi, mid, lo = _split3(kbar_ref[...] * (1.0 / MOBA_BLOCK))
        kb3_ref[0:n_blocks, :] = hi.astype(BF16)
        kb3_ref[n_blocks:2 * n_blocks, :] = mid.astype(BF16)
        kb3_ref[2 * n_blocks:3 * n_blocks, :] = lo.astype(BF16)
        lane = lax.broadcasted_iota(jnp.int32, (TILE, AUG - HEAD_DIM), 1)
        row = lax.broadcasted_iota(jnp.int32, (TILE, AUG - HEAD_DIM), 0)
        ones_lane = jnp.where(lane == n_blocks, 1.0, jnp.where(lane == n_blocks + 1, 1.0, 0.0))
        for j in range(n_tiles):
            blk_of_row = jnp.where(row < ATT_TILE, 2 * j, 2 * j + 1)
            right = jnp.where(lane == blk_of_row, 1.0, ones_lane).astype(BF16)
            kaug_ref[j] = jnp.concatenate(
                [k_ref[0, 2 * j:2 * j + 2].reshape(TILE, HEAD_DIM), right], axis=1)

    qt = jnp.concatenate([q_ref[0, 0], q_ref[0, 1]], axis=1)
    g3 = jnp.dot(kb3_ref[...], qt, preferred_element_type=F32)
    gate = g3[0:n_blocks] + g3[n_blocks:2 * n_blocks] + g3[2 * n_blocks:3 * n_blocks]
    blk = lax.broadcasted_iota(jnp.int32, gate.shape, 0)
    qlane = lax.broadcasted_iota(jnp.int32, gate.shape, 1)
    own = 2 * i + jnp.where(qlane >= ATT_TILE, 1, 0)
    eligible = blk < own
    gate = jnp.where(eligible, gate, NEG_INF)
    rank = jnp.zeros(gate.shape, jnp.int32)
    for mth in range(n_blocks):
        other = gate[mth:mth + 1, :]
        beats = jnp.where(other > gate, 1, jnp.where((other == gate) & (blk > mth), 1, 0))
        rank = rank + beats
    selb = jnp.where((rank < MOBA_TOP_K) & eligible, 0.0, NEG_INF).astype(F32)

    far = jnp.full((16, TILE), tab_ref[h, NUM_BUCKETS - 1] * LOG2E, F32)
    far_hi = far.astype(BF16).astype(F32)
    r16 = lax.broadcasted_iota(jnp.int32, (16, TILE), 0)
    far_rows = jnp.where(r16 == 0, far_hi, jnp.where(r16 == 1, far - far_hi, 0.0))
    q_aug = jnp.concatenate(
        [qt, selb.astype(BF16), far_rows.astype(BF16),
         jnp.zeros((AUG - HEAD_DIM - n_blocks - 16, TILE), BF16)], axis=0)

    def values(j):
        return jnp.concatenate([v_ref[0, 2 * j], v_ref[0, 2 * j + 1]], axis=1)

    def far_logits(j):
        return jnp.dot(kaug_ref[j], q_aug, preferred_element_type=F32)

    sm = _Softmax(s_ref, m_ref, l_ref, acc_ref, values)

    sel_own = jnp.sum(jnp.where(blk == 2 * i, selb, 0.0), axis=0, keepdims=True)
    sel_own = jnp.where(qlane[0:1] >= ATT_TILE, sel_own, 0.0)
    kd = k_ref[0, pl.ds(2 * i, 2)].reshape(TILE, HEAD_DIM)
    sd = jnp.dot(kd, qt, preferred_element_type=F32) + bias_ref[0, 0]
    s_ref[0, 0:ATT_TILE, :] = sd[0:ATT_TILE] + sel_own
    s_ref[0, ATT_TILE:TILE, :] = sd[ATT_TILE:TILE]
    sm.reset()

    j1 = jnp.where(i >= 1, i - 1, i + 1)
    j2 = jnp.where(i >= 2, i - 2, i + 1)
    n_far = jnp.maximum(i - 2, 0)

    def clamp(j):
        return jnp.minimum(j, n_tiles - 1)

    s_ref[1] = far_logits(j1) + bias_ref[0, 1]
    sm.fold(0, i)
    s_ref[0] = far_logits(j2) + bias_ref[0, 2]
    sm.fold(1, j1)
    s_ref[1] = far_logits(0)
    sm.fold(0, j2)

    def pair_body(p, carry):
        s_ref[0] = far_logits(clamp(2 * p + 1))
        sm.fold(1, 2 * p)
        s_ref[1] = far_logits(clamp(2 * p + 2))
        sm.fold(0, 2 * p + 1)
        return carry

    lax.fori_loop(0, n_far // 2, pair_body, 0)

    @pl.when(n_far % 2 == 1)
    def _():
        sm.fold(1, n_far - 1)

    o_ref[...] = sm.result().T.astype(o_ref.dtype)


def moba_attn(qv_t, k, bias, rel_bias, *, batch, n_heads, q_off, k_off, v_off):
    mb = k.shape[1]
    nb = mb // batch
    nt = nb // 2
    m = mb * ATT_TILE
    assert nb + 16 <= AUG - HEAD_DIM and nt >= 4
    return pl.pallas_call(
        functools.partial(_moba_attn_kernel, n_blocks=nb),
        out_shape=jax.ShapeDtypeStruct((m, n_heads * HEAD_DIM), BF16),
        grid=(batch, n_heads, nt),
        in_specs=[pl.BlockSpec(memory_space=pltpu.SMEM),
                  pl.BlockSpec((1, 2, HEAD_DIM, ATT_TILE), lambda b, h, i: (q_off + h, b * nt + i, 0, 0)),
                  pl.BlockSpec((1, nb, ATT_TILE, HEAD_DIM), lambda b, h, i: (k_off + h, b, 0, 0)),
                  pl.BlockSpec((1, nb, HEAD_DIM, ATT_TILE), lambda b, h, i: (v_off + h, b, 0, 0)),
                  pl.BlockSpec((1, 3, TILE, TILE), lambda b, h, i: (h, 0, 0, 0))],
        out_specs=pl.BlockSpec((TILE, HEAD_DIM), lambda b, h, i: (b * nt + i, h)),
        scratch_shapes=[pltpu.VMEM((nt, TILE, AUG), BF16),
                        pltpu.VMEM((nb, HEAD_DIM), F32),
                        pltpu.VMEM((3 * nb, HEAD_DIM), BF16)] + _softmax_scratch(TILE),
        compiler_params=_params("parallel", "parallel", "arbitrary"),
        name="moba_attn",
    )(rel_bias.astype(F32), qv_t, k, qv_t, bias)


def _fox_attn_kernel(q_ref, k_ref, v_ref, c_ref, o_ref, kaug_ref, s_ref, m_ref, l_ref, acc_ref,
                     *, n_tiles):
    i = pl.program_id(2)
    t = TILE

    @pl.when(i == 0)
    def _():
        lane = lax.broadcasted_iota(jnp.int32, (t, AUG - HEAD_DIM), 1)
        head = pl.program_id(1)
        for j in range(n_tiles):
            c_all = c_ref[j * t:(j + 1) * t, :]
            c = jnp.sum(jnp.where(lane == head, c_all, 0.0), axis=1, keepdims=True)
            hi, mid, lo = _split3(-c)
            right = jnp.where(lane == 0, hi, jnp.where(lane == 1, mid, jnp.where(lane == 2, lo, 0.0)))
            kaug_ref[j] = jnp.concatenate(
                [k_ref[0, 2 * j:2 * j + 2].reshape(t, HEAD_DIM), right.astype(BF16)], axis=1)

    qt = jnp.concatenate([q_ref[0, 0], q_ref[0, 1]], axis=1)
    r_aug = lax.broadcasted_iota(jnp.int32, (AUG - HEAD_DIM, t), 0)
    q_aug = jnp.concatenate([qt, jnp.where(r_aug < 3, 1.0, 0.0).astype(BF16)], axis=0)

    def logits(n):
        return jnp.dot(kaug_ref[n], q_aug, preferred_element_type=F32)

    def values(n):
        return jnp.concatenate([v_ref[0, 2 * n], v_ref[0, 2 * n + 1]], axis=1)

    sm = _Softmax(s_ref, m_ref, l_ref, acc_ref, values)

    def clamp(n):
        return jnp.minimum(n, n_tiles - 1)

    s_ref[0] = logits(i)
    sm.reset()
    key = lax.broadcasted_iota(jnp.int32, (t, t), 0)
    qry = lax.broadcasted_iota(jnp.int32, (t, t), 1)
    s_ref[1] = logits(0)
    sm.fold(0, i, mask=key <= qry)

    def pair_body(p, carry):
        s_ref[0] = logits(clamp(2 * p + 1))
        sm.fold(1, 2 * p)
        s_ref[1] = logits(clamp(2 * p + 2))
        sm.fold(0, 2 * p + 1)
        return carry

    lax.fori_loop(0, i // 2, pair_body, 0)

    @pl.when(i % 2 == 1)
    def _():
        sm.fold(1, i - 1)

    o_ref[...] = sm.result().T.astype(o_ref.dtype)


def fox_attn(qv_t, k, c_rep, *, batch, n_heads, q_off, k_off, v_off):
    mb = k.shape[1]
    nb = mb // batch
    nt = nb // 2
    m = mb * ATT_TILE
    t = TILE
    return pl.pallas_call(
        functools.partial(_fox_attn_kernel, n_tiles=nt),
        out_shape=jax.ShapeDtypeStruct((m, n_heads * HEAD_DIM), BF16),
        grid=(batch, n_heads, nt),
        in_specs=[pl.BlockSpec((1, 2, HEAD_DIM, ATT_TILE), lambda b, h, i: (q_off + h, b * nt + i, 0, 0)),
                  pl.BlockSpec((1, nb, ATT_TILE, HEAD_DIM), lambda b, h, i: (k_off + h, b, 0, 0)),
                  pl.BlockSpec((1, nb, HEAD_DIM, ATT_TILE), lambda b, h, i: (v_off + h, b, 0, 0)),
                  pl.BlockSpec((nb * ATT_TILE, LANES), lambda b, h, i: (b, 0))],
        out_specs=pl.BlockSpec((t, HEAD_DIM), lambda b, h, i: (b * nt + i, h)),
        scratch_shapes=[pltpu.VMEM((nt, t, AUG), BF16)] + _softmax_scratch(t),
        compiler_params=_params("parallel", "parallel", "arbitrary"),
        name="fox_attn",
    )(qv_t, k, qv_t, c_rep)


def _gated_merge_kernel(oa_ref, of_ref, wa_ref, wf_ref, ga_ref, gf_ref, o_ref):
    ua = jnp.dot(oa_ref[...], wa_ref[...], preferred_element_type=F32)
    uf = jnp.dot(of_ref[...], wf_ref[...], preferred_element_type=F32)
    o_ref[...] = (ga_ref[...].astype(F32) * ua + gf_ref[...].astype(F32) * uf).astype(o_ref.dtype)


def gated_merge(o_a, o_f, w_a, w_f, gates, *, tm=ROW_TILE, tn=COL_TILE):
    m, ka = o_a.shape
    kf = o_f.shape[1]
    n = w_a.shape[1]
    nj = n // tn
    return pl.pallas_call(
        _gated_merge_kernel,
        out_shape=jax.ShapeDtypeStruct((m, n), BF16),
        grid=(m // tm, nj),
        in_specs=[pl.BlockSpec((tm, ka), lambda i, j: (i, 0)),
                  pl.BlockSpec((tm, kf), lambda i, j: (i, 0)),
                  pl.BlockSpec((ka, tn), lambda i, j: (0, j)),
                  pl.BlockSpec((kf, tn), lambda i, j: (0, j)),
                  pl.BlockSpec((tm, tn), lambda i, j: (i, j)),
                  pl.BlockSpec((tm, tn), lambda i, j: (i, nj + j))],
        out_specs=pl.BlockSpec((tm, tn), lambda i, j: (i, j)),
        compiler_params=_params("parallel", "parallel"),
        name="gated_merge",
    )(o_a, o_f, w_a, w_f, gates, gates)


def _mm_res_kernel(lhs_ref, w_ref, res_ref, g_ref, o_ref, *, final_norm):
    kk = pl.program_id(1)

    @pl.when(kk == 0)
    def _():
        o_ref[...] = res_ref[...]

    o_ref[...] += jnp.dot(lhs_ref[...], w_ref[...], preferred_element_type=F32)

    if final_norm:
        @pl.when(kk == pl.num_programs(1) - 1)
        def _():
            hres = o_ref[...]
            ms = jnp.mean(hres * hres, axis=-1, keepdims=True)
            o_ref[...] = hres * lax.rsqrt(ms + RMS_EPS) * g_ref[...]


def mm_res(lhs, w, res, g=None, *, tm=ROW_TILE, tk=512):
    m, k = lhs.shape
    n = w.shape[1]
    tk = min(tk, k)
    final_norm = g is not None
    if g is None:
        g = jnp.ones((n,), F32)
    return pl.pallas_call(
        functools.partial(_mm_res_kernel, final_norm=final_norm),
        out_shape=jax.ShapeDtypeStruct((m, n), F32),
        grid=(m // tm, k // tk),
        in_specs=[pl.BlockSpec((tm, tk), lambda i, kk: (i, kk)),
                  pl.BlockSpec((tk, n), lambda i, kk: (kk, 0)),
                  pl.BlockSpec((tm, n), lambda i, kk: (i, 0)),
                  pl.BlockSpec((1, n), lambda i, kk: (0, 0))],
        out_specs=pl.BlockSpec((tm, n), lambda i, kk: (i, 0)),
        compiler_params=_params("parallel", "arbitrary"),
        name="mm_res",
    )(lhs, w, res, g.reshape(1, n))


def _cross_attn_kernel(q_ref, kv_ref, w_ref, res_ref, o_ref, *, n_heads):
    width = n_heads * HEAD_DIM
    heads = []
    for h in range(n_heads):
        q = q_ref[:, h * HEAD_DIM:(h + 1) * HEAD_DIM]
        k = kv_ref[0, :, h * HEAD_DIM:(h + 1) * HEAD_DIM]
        v = kv_ref[0, :, width + h * HEAD_DIM:width + (h + 1) * HEAD_DIM]
        s = lax.dot_general(q, k, (((1,), (1,)), ((), ())), preferred_element_type=F32)
        m = jnp.max(s, axis=-1, keepdims=True)
        p = jnp.exp(s - m)
        l = jnp.sum(p, axis=-1, keepdims=True)
        o = jnp.dot(p.astype(BF16), v, preferred_element_type=F32) * (1.0 / l)
        heads.append(o.astype(BF16))
    o_all = jnp.concatenate(heads, axis=1)
    o_ref[...] = res_ref[...] + jnp.dot(o_all, w_ref[...], preferred_element_type=F32)


def cross_attn(q, kv, w_o, res, *, batch, n_heads, tm=512):
    m, width = q.shape
    d = w_o.shape[1]
    n_mem = kv.shape[0] // batch
    tiles_per_batch = (m // batch) // tm
    kv3 = kv.reshape(batch, n_mem, 2 * width)
    return pl.pallas_call(
        functools.partial(_cross_attn_kernel, n_heads=n_heads),
        out_shape=jax.ShapeDtypeStruct((m, d), F32),
        grid=(m // tm,),
        in_specs=[pl.BlockSpec((tm, width), lambda i: (i, 0)),
                  pl.BlockSpec((1, n_mem, 2 * width), lambda i: (i // tiles_per_batch, 0, 0)),
                  pl.BlockSpec((width, d), lambda i: (0, 0)),
                  pl.BlockSpec((tm, d), lambda i: (i, 0))],
        out_specs=pl.BlockSpec((tm, d), lambda i: (i, 0)),
        compiler_params=_params("parallel"),
        name="cross_attn",
    )(q, kv3, w_o, res)


def kernel(x, mem, g_mix, w_in, b_forget, w_branch_moba, w_branch_fox, w_mix_out, rel_bias,
           g_cross, g_mem, w_cq, w_ck, w_cv, w_co, g_mlp, w_ff1, w_ff2, g_final):
    batch, seq, d = x.shape
    depth = w_in.shape[0]
    n_heads = rel_bias.shape[0]
    n_fox = b_forget.shape[1]
    wm = n_heads * HEAD_DIM
    wf = n_fox * HEAD_DIM
    m = batch * seq
    scale = HEAD_DIM ** -0.5
    mem2 = mem.reshape(-1, d)

    bias = moba_bias(rel_bias)
    h = x.reshape(m, d)
    for l in range(depth):
        wi = w_in[l]
        o0 = 0
        parts = []
        for width in (wm, wm, wm, wf, wf, wf, n_fox, d, d):
            parts.append(wi[:, o0:o0 + width])
            o0 += width
        wq_m, wk_m, wv_m, wq_f, wk_f, wv_f, w_fl, wg_a, wg_f = parts
        w_qv = jnp.concatenate([wq_m, wq_f, wv_m, wv_f], axis=1).astype(BF16)
        w_k = jnp.concatenate([wk_m, wk_f], axis=1).astype(BF16)
        w_g = jnp.concatenate([wg_a, wg_f], axis=1).astype(BF16)

        n_q_tiles = (wm + wf) // COL_TILE
        qv_t = rms_proj(h, g_mix[l], w_qv, mode="transposed", scale=scale * LOG2E, n_scaled=n_q_tiles)
        k_hm = rms_proj(h, g_mix[l], w_k, mode="headmajor")
        k_hm = k_hm.reshape(k_hm.shape[0], m // ATT_TILE, ATT_TILE, HEAD_DIM)
        gates = rms_proj(h, g_mix[l], w_g, act="sigmoid")
        c_rep = forget_cumsum(h, g_mix[l], w_fl, b_forget[l], batch=batch, n_heads=n_fox)

        o_a = moba_attn(qv_t, k_hm, bias, rel_bias, batch=batch, n_heads=n_heads,
                        q_off=0, k_off=0, v_off=n_heads + n_fox)
        o_f = fox_attn(qv_t, k_hm, c_rep, batch=batch, n_heads=n_fox,
                       q_off=n_heads, k_off=n_heads, v_off=2 * n_heads + n_fox)
        merged = gated_merge(o_a, o_f, w_branch_moba[l].astype(BF16), w_branch_fox[l].astype(BF16), gates)
        h = mm_res(merged, w_mix_out[l].astype(BF16), h)

        cw = w_cq.shape[2]
        qc = rms_proj(h, g_cross[l], w_cq[l].astype(BF16), scale=scale, n_scaled=1, tn=cw)
        w_kv = jnp.concatenate([w_ck[l], w_cv[l]], axis=1).astype(BF16)
        kv = rms_proj(mem2, g_mem[l], w_kv, tn=2 * cw)
        h = cross_attn(qc, kv, w_co[l].astype(BF16), h, batch=batch, n_heads=cw // HEAD_DIM)

        u = rms_proj(h, g_mlp[l], w_ff1[l].astype(BF16), act="relu2")
        h = mm_res(u, w_ff2[l].astype(BF16), h, g_final if l == depth - 1 else None)
    return h.reshape(batch, seq, d)
```

```python
import functools
import math

import jax
import jax.numpy as jnp
from jax import lax
from jax.experimental import pallas as pl
from jax.experimental.pallas import tpu as pltpu

F32 = jnp.float32
BF16 = jnp.bfloat16

HEAD_DIM = 128
MOBA_BLOCK = 256
MOBA_TOP_K = 3
NUM_BUCKETS = 32
MAX_DISTANCE = 1024
N_CROSS_HEADS = 4
RMS_EPS = 1e-6
LOG2E = math.log2(math.e)
NEG_INF = -1e30
LANES = 128
ATT_TILE = 256
TILE = 2 * ATT_TILE
AUG = 256
HEADS_PER_STEP = 2
UNDERFLOW_LOG2 = 160.0
NEAR_BLOCKS = 5
VMEM_LIMIT = 56 * 1024 * 1024
ROW_TILE = 1024
COL_TILE = 512


def _bucket_thresholds():
    max_exact = NUM_BUCKETS // 2
    thr = list(range(1, max_exact + 1))
    for k in range(max_exact + 1, NUM_BUCKETS):
        v = max_exact * (MAX_DISTANCE / max_exact) ** ((k - max_exact) / (NUM_BUCKETS - max_exact))
        n = int(math.floor(v))
        while max_exact + int(math.log(n / max_exact) / math.log(MAX_DISTANCE / max_exact)
                              * (NUM_BUCKETS - max_exact)) < k:
            n += 1
        thr.append(n)
    return tuple(thr)


BUCKET_THRESHOLDS = _bucket_thresholds()
assert (NEAR_BLOCKS - 1) * MOBA_BLOCK + 1 >= BUCKET_THRESHOLDS[-1]


def _params(*sem):
    return pltpu.CompilerParams(dimension_semantics=sem, vmem_limit_bytes=VMEM_LIMIT)


def _rms_proj_kernel(x_ref, g_ref, w_ref, o_ref, a_ref, *, mode, act, scale, n_scaled):
    j = pl.program_id(1)

    @pl.when(j == 0)
    def _():
        x = x_ref[...]
        ms = jnp.mean(x * x, axis=-1, keepdims=True)
        a_ref[...] = (x * lax.rsqrt(ms + RMS_EPS) * g_ref[...]).astype(BF16)

    acc = jnp.dot(a_ref[...], w_ref[...], preferred_element_type=F32)
    if scale is not None:
        acc = acc * jnp.where(j < n_scaled, F32(scale), F32(1.0))
    tm, tn = acc.shape
    if mode == "rowmajor":
        if act == "sigmoid":
            acc = jax.nn.sigmoid(acc)
        elif act == "relu2":
            acc = jnp.square(jnp.maximum(acc, 0.0))
        o_ref[...] = acc.astype(o_ref.dtype)
    elif mode == "headmajor":
        for c in range(tn // LANES):
            o_ref[c] = acc[:, c * LANES:(c + 1) * LANES].astype(o_ref.dtype)
    else:
        for c in range(tn // LANES):
            for r in range(tm // ATT_TILE):
                blk = acc[r * ATT_TILE:(r + 1) * ATT_TILE, c * LANES:(c + 1) * LANES]
                o_ref[c, r] = blk.T.astype(o_ref.dtype)


def rms_proj(x, g, w, *, mode="rowmajor", act=None, scale=None, n_scaled=0,
             out_dtype=BF16, tm=ROW_TILE, tn=COL_TILE):
    m, d = x.shape
    n = w.shape[1]
    tm, tn = min(tm, m), min(tn, n)
    if mode == "rowmajor":
        out_shape = jax.ShapeDtypeStruct((m, n), out_dtype)
        out_spec = pl.BlockSpec((tm, tn), lambda i, j: (i, j))
    elif mode == "headmajor":
        out_shape = jax.ShapeDtypeStruct((n // LANES, m, LANES), out_dtype)
        out_spec = pl.BlockSpec((tn // LANES, tm, LANES), lambda i, j: (j, i, 0))
    else:
        out_shape = jax.ShapeDtypeStruct((n // LANES, m // ATT_TILE, LANES, ATT_TILE), out_dtype)
        out_spec = pl.BlockSpec((tn // LANES, tm // ATT_TILE, LANES, ATT_TILE),
                                lambda i, j: (j, i, 0, 0))
    return pl.pallas_call(
        functools.partial(_rms_proj_kernel, mode=mode, act=act, scale=scale, n_scaled=n_scaled),
        out_shape=out_shape,
        grid=(m // tm, n // tn),
        in_specs=[pl.BlockSpec((tm, d), lambda i, j: (i, 0)),
                  pl.BlockSpec((1, d), lambda i, j: (0, 0)),
                  pl.BlockSpec((d, tn), lambda i, j: (0, j))],
        out_specs=out_spec,
        scratch_shapes=[pltpu.VMEM((tm, d), BF16)],
        compiler_params=_params("parallel", "arbitrary"),
        name="rms_proj_" + mode,
    )(x, g.reshape(1, d), w)


def _forget_cumsum_kernel(x_ref, g_ref, w_ref, b_ref, o_ref, carry_ref):
    t = pl.program_id(1)

    @pl.when(t == 0)
    def _():
        carry_ref[...] = jnp.zeros_like(carry_ref)

    def bf16_dot(a, b):
        return jnp.dot(a.astype(BF16), b.astype(BF16), preferred_element_type=F32)

    x = x_ref[...]
    tm = x.shape[0]
    ms = jnp.mean(x * x, axis=-1, keepdims=True)
    a = x * lax.rsqrt(ms + RMS_EPS) * g_ref[...]
    a_hi, a_lo, _ = _split3(a)
    w_hi, w_lo, _ = _split3(w_ref[...])
    f = bf16_dot(a_hi, w_hi) + (bf16_dot(a_lo, w_hi) + bf16_dot(a_hi, w_lo)) + b_ref[...]
    logf = jnp.minimum(f, 0.0) - jnp.log1p(jnp.exp(-jnp.abs(f)))
    logf = logf * LOG2E
    row = lax.broadcasted_iota(jnp.int32, (tm, tm), 0)
    col = lax.broadcasted_iota(jnp.int32, (tm, tm), 1)
    tri = jnp.where(col <= row, 1.0, 0.0).astype(BF16)
    hi, mid, lo = _split3(logf)
    c = (bf16_dot(tri, hi) + (bf16_dot(tri, mid) + bf16_dot(tri, lo))) + carry_ref[0:1, :]
    o_ref[...] = c
    carry_ref[...] = jnp.broadcast_to(c[tm - 1:tm, :], carry_ref.shape)


def forget_cumsum(x, g, w_f, b_f, *, batch, n_heads, tm=512):
    m, d = x.shape
    s = m // batch
    nt = s // tm
    w_pad = jnp.zeros((d, LANES), F32).at[:, :n_heads].set(w_f)
    b_pad = jnp.zeros((1, LANES), F32).at[0, :n_heads].set(b_f)
    return pl.pallas_call(
        _forget_cumsum_kernel,
        out_shape=jax.ShapeDtypeStruct((m, LANES), F32),
        grid=(batch, nt),
        in_specs=[pl.BlockSpec((tm, d), lambda b, t: (b * nt + t, 0)),
                  pl.BlockSpec((1, d), lambda b, t: (0, 0)),
                  pl.BlockSpec((d, LANES), lambda b, t: (0, 0)),
                  pl.BlockSpec((1, LANES), lambda b, t: (0, 0))],
        out_specs=pl.BlockSpec((tm, LANES), lambda b, t: (b * nt + t, 0)),
        scratch_shapes=[pltpu.VMEM((8, LANES), F32)],
        compiler_params=_params("parallel", "arbitrary"),
        name="forget_cumsum",
    )(x, g.reshape(1, d), w_pad, b_pad)


def _moba_bias_kernel(tab_ref, o_ref):
    h = pl.program_id(0)
    key = lax.broadcasted_iota(jnp.int32, (ATT_TILE, ATT_TILE), 0)
    qry = lax.broadcasted_iota(jnp.int32, (ATT_TILE, ATT_TILE), 1)
    far = tab_ref[h, NUM_BUCKETS - 1] * LOG2E

    def block(delta):
        dist = delta * MOBA_BLOCK + qry - key
        val = jnp.full((ATT_TILE, ATT_TILE), tab_ref[h, 0], F32)
        for k in range(1, NUM_BUCKETS):
            val = jnp.where(dist >= BUCKET_THRESHOLDS[k - 1], tab_ref[h, k], val)
        val = val * LOG2E
        if delta == 0:
            val = jnp.where(dist >= 0, val, NEG_INF)
        return val

    t = [block(delta) for delta in range(NEAR_BLOCKS + 1)]
    a = ATT_TILE
    o_ref[0, 0, 0:a, 0:a] = t[0]
    o_ref[0, 0, 0:a, a:2 * a] = t[1]
    o_ref[0, 0, a:2 * a, 0:a] = jnp.full((a, a), NEG_INF, F32)
    o_ref[0, 0, a:2 * a, a:2 * a] = t[0]
    for d in (1, 2):
        o_ref[0, d, 0:a, 0:a] = t[2 * d] - far
        o_ref[0, d, 0:a, a:2 * a] = t[2 * d + 1] - far
        o_ref[0, d, a:2 * a, 0:a] = t[2 * d - 1] - far
        o_ref[0, d, a:2 * a, a:2 * a] = t[2 * d] - far


def moba_bias(rel_bias):
    h = rel_bias.shape[0]
    return pl.pallas_call(
        _moba_bias_kernel,
        out_shape=jax.ShapeDtypeStruct((h, 3, TILE, TILE), F32),
        grid=(h,),
        in_specs=[pl.BlockSpec(memory_space=pltpu.SMEM)],
        out_specs=pl.BlockSpec((1, 3, TILE, TILE), lambda i: (i, 0, 0, 0)),
        compiler_params=_params("parallel"),
        name="moba_bias",
    )(rel_bias.astype(F32))


class _Softmax:
    def __init__(self, hh, s_ref, m_ref, l_ref, acc_ref, load_values):
        self.s_ref, self.m_ref, self.l_ref, self.acc_ref = s_ref.at[hh], m_ref.at[hh], l_ref.at[hh], acc_ref.at[hh]
        self.load_values = load_values

    def reset(self):
        self.m_ref[...] = jnp.full(self.m_ref.shape, NEG_INF, F32)
        self.l_ref[...] = jnp.zeros(self.l_ref.shape, F32)
        self.acc_ref[...] = jnp.zeros(self.acc_ref.shape, F32)

    def fold(self, slot, tile, mask=None):
        s = self.s_ref[slot]
        if mask is not None:
            s = jnp.where(mask, s, NEG_INF)
        m = self.m_ref[...]
        m_new = jnp.maximum(m, jnp.max(s, axis=0, keepdims=True))
        alpha = jnp.exp2(m - m_new)
        p = jnp.exp2(s - m_new)
        self.l_ref[...] = alpha * self.l_ref[...] + jnp.sum(p, axis=0, keepdims=True)
        self.m_ref[...] = m_new
        self.acc_ref[...] = alpha * self.acc_ref[...] + jnp.dot(
            self.load_values(tile), p.astype(BF16), preferred_element_type=F32)

    def result(self):
        return self.acc_ref[...] * (1.0 / self.l_ref[...])


def _softmax_scratch(t):
    nh = HEADS_PER_STEP
    return [pltpu.VMEM((nh, 2, t, t), F32),
            pltpu.VMEM((nh, 1, t), F32),
            pltpu.VMEM((nh, 1, t), F32),
            pltpu.VMEM((nh, HEAD_DIM, t), F32)]


def _top_k_bias(gate, eligible, blk):
    lowest = float(jnp.finfo(F32).min)
    blk_f = blk.astype(F32)
    g = jnp.where(eligible, gate, NEG_INF)
    bias = jnp.full(gate.shape, NEG_INF, F32)
    for _ in range(MOBA_TOP_K):
        best = jnp.max(g, axis=0, keepdims=True)
        first = jnp.min(jnp.where(g == best, blk_f, float(gate.shape[0])), axis=0, keepdims=True)
        hit = blk_f == first
        bias = jnp.where(hit, 0.0, bias)
        g = jnp.where(hit, lowest, g)
    return jnp.where(eligible, bias, NEG_INF)


def _split3(x):
    hi = x.astype(BF16).astype(F32)
    mid = (x - hi).astype(BF16).astype(F32)
    lo = (x - hi - mid).astype(BF16).astype(F32)
    return hi, mid, lo


def _moba_attn_kernel(tab_ref, q_ref, k_ref, v_ref, bias_ref, o_ref,
                      kaug_ref, kbar_ref, kb3_ref, s_ref, m_ref, l_ref, acc_ref, *, n_blocks):
    hp = pl.program_id(1)
    i = pl.program_id(2)
    n_tiles = n_blocks // 2
    heads = range(HEADS_PER_STEP)

    @pl.when(i == 0)
    def _():
        lane = lax.broadcasted_iota(jnp.int32, (TILE, AUG - HEAD_DIM), 1)
        row = lax.broadcasted_iota(jnp.int32, (TILE, AUG - HEAD_DIM), 0)
        ones_lane = jnp.where(lane == n_blocks, 1.0, jnp.where(lane == n_blocks + 1, 1.0, 0.0))
        for hh in heads:
            for n in range(n_blocks):
                kbar_ref[hh, n:n + 1, :] = jnp.sum(k_ref[hh, n].astype(F32), axis=0, keepdims=True)
            hi, mid, lo = _split3(kbar_ref[hh] * (1.0 / MOBA_BLOCK))
            kb3_ref[hh, 0:n_blocks, :] = hi.astype(BF16)
            kb3_ref[hh, n_blocks:2 * n_blocks, :] = mid.astype(BF16)
            kb3_ref[hh, 2 * n_blocks:3 * n_blocks, :] = lo.astype(BF16)
            for j in range(n_tiles):
                blk_of_row = jnp.where(row < ATT_TILE, 2 * j, 2 * j + 1)
                right = jnp.where(lane == blk_of_row, 1.0, ones_lane).astype(BF16)
                kaug_ref[hh, j] = jnp.concatenate(
                    [k_ref[hh, 2 * j:2 * j + 2].reshape(TILE, HEAD_DIM), right], axis=1)

    blk = lax.broadcasted_iota(jnp.int32, (n_blocks, TILE), 0)
    qlane = lax.broadcasted_iota(jnp.int32, (n_blocks, TILE), 1)
    own = 2 * i + jnp.where(qlane >= ATT_TILE, 1, 0)
    eligible = blk < own
    r16 = lax.broadcasted_iota(jnp.int32, (16, TILE), 0)

    j1 = jnp.where(i >= 1, i - 1, i + 1)
    j2 = jnp.where(i >= 2, i - 2, i + 1)
    n_far = jnp.maximum(i - 2, 0)

    def clamp(j):
        return jnp.minimum(j, n_tiles - 1)

    far_logits, fold = [], []
    for hh in heads:
        qt = jnp.concatenate([q_ref[hh, 0], q_ref[hh, 1]], axis=1)
        g3 = jnp.dot(kb3_ref[hh], qt, preferred_element_type=F32)
        gate = g3[0:n_blocks] + g3[n_blocks:2 * n_blocks] + g3[2 * n_blocks:3 * n_blocks]
        selb = _top_k_bias(gate, eligible, blk)

        far = jnp.full((16, TILE), tab_ref[hp * HEADS_PER_STEP + hh, NUM_BUCKETS - 1] * LOG2E, F32)
        far_hi = far.astype(BF16).astype(F32)
        far_rows = jnp.where(r16 == 0, far_hi, jnp.where(r16 == 1, far - far_hi, 0.0))
        q_aug = jnp.concatenate(
            [qt, selb.astype(BF16), far_rows.astype(BF16),
             jnp.zeros((AUG - HEAD_DIM - n_blocks - 16, TILE), BF16)], axis=0)

        def values(j, hh=hh):
            return jnp.concatenate([v_ref[hh, 2 * j], v_ref[hh, 2 * j + 1]], axis=1)

        def head_far_logits(j, hh=hh, q_aug=q_aug):
            return jnp.dot(kaug_ref[hh, j], q_aug, preferred_element_type=F32)

        sm = _Softmax(hh, s_ref, m_ref, l_ref, acc_ref, values)
        far_logits.append(head_far_logits)
        fold.append(sm)

        sel_own = jnp.sum(jnp.where(blk == 2 * i, selb, 0.0), axis=0, keepdims=True)
        sel_own = jnp.where(qlane[0:1] >= ATT_TILE, sel_own, 0.0)
        kd = k_ref[hh, pl.ds(2 * i, 2)].reshape(TILE, HEAD_DIM)
        sd = jnp.dot(kd, qt, preferred_element_type=F32) + bias_ref[hh, 0]
        s_ref[hh, 0, 0:ATT_TILE, :] = sd[0:ATT_TILE] + sel_own
        s_ref[hh, 0, ATT_TILE:TILE, :] = sd[ATT_TILE:TILE]
        sm.reset()

    for hh in heads:
        s_ref[hh, 1] = far_logits[hh](j1) + bias_ref[hh, 1]
        fold[hh].fold(0, i)
    for hh in heads:
        s_ref[hh, 0] = far_logits[hh](j2) + bias_ref[hh, 2]
        fold[hh].fold(1, j1)
    for hh in heads:
        s_ref[hh, 1] = far_logits[hh](0)
        fold[hh].fold(0, j2)

    def pair_body(p, carry):
        for hh in heads:
            s_ref[hh, 0] = far_logits[hh](clamp(2 * p + 1))
            fold[hh].fold(1, 2 * p)
        for hh in heads:
            s_ref[hh, 1] = far_logits[hh](clamp(2 * p + 2))
            fold[hh].fold(0, 2 * p + 1)
        return carry

    lax.fori_loop(0, n_far // 2, pair_body, 0)

    @pl.when(n_far % 2 == 1)
    def _():
        for hh in heads:
            fold[hh].fold(1, n_far - 1)

    for hh in heads:
        o_ref[:, hh * HEAD_DIM:(hh + 1) * HEAD_DIM] = fold[hh].result().T.astype(o_ref.dtype)


def moba_attn(qv_t, k, bias, rel_bias, *, batch, n_heads, q_off, k_off, v_off):
    mb = k.shape[1]
    nb = mb // batch
    nt = nb // 2
    m = mb * ATT_TILE
    hb = HEADS_PER_STEP
    assert nb + 16 <= AUG - HEAD_DIM and nt >= 4
    assert n_heads % hb == 0 and q_off % hb == 0 and k_off % hb == 0 and v_off % hb == 0
    return pl.pallas_call(
        functools.partial(_moba_attn_kernel, n_blocks=nb),
        out_shape=jax.ShapeDtypeStruct((m, n_heads * HEAD_DIM), BF16),
        grid=(batch, n_heads // hb, nt),
        in_specs=[pl.BlockSpec(memory_space=pltpu.SMEM),
                  pl.BlockSpec((hb, 2, HEAD_DIM, ATT_TILE),
                               lambda b, h, i: (q_off // hb + h, b * nt + i, 0, 0)),
                  pl.BlockSpec((hb, nb, ATT_TILE, HEAD_DIM), lambda b, h, i: (k_off // hb + h, b, 0, 0)),
                  pl.BlockSpec((hb, nb, HEAD_DIM, ATT_TILE), lambda b, h, i: (v_off // hb + h, b, 0, 0)),
                  pl.BlockSpec((hb, 3, TILE, TILE), lambda b, h, i: (h, 0, 0, 0))],
        out_specs=pl.BlockSpec((TILE, hb * HEAD_DIM), lambda b, h, i: (b * nt + i, h)),
        scratch_shapes=[pltpu.VMEM((hb, nt, TILE, AUG), BF16),
                        pltpu.VMEM((hb, nb, HEAD_DIM), F32),
                        pltpu.VMEM((hb, 3 * nb, HEAD_DIM), BF16)] + _softmax_scratch(TILE),
        compiler_params=_params("parallel", "parallel", "arbitrary"),
        name="moba_attn",
    )(rel_bias.astype(F32), qv_t, k, qv_t, bias)


def _fox_attn_kernel(q_ref, k_ref, v_ref, c_ref, o_ref, kaug_ref, bound_ref, s_ref, m_ref, l_ref,
                     acc_ref, *, n_tiles):
    hp = pl.program_id(1)
    i = pl.program_id(2)
    t = TILE
    heads = range(HEADS_PER_STEP)

    lane1 = lax.broadcasted_iota(jnp.int32, (1, LANES), 1)

    @pl.when(i == 0)
    def _():
        lane = lax.broadcasted_iota(jnp.int32, (t, AUG - HEAD_DIM), 1)
        for hh in heads:
            c_first = jnp.zeros((1, LANES), F32)
            c_last = jnp.zeros((1, LANES), F32)
            k_norm2 = jnp.zeros((t, 1), F32)
            for j in range(n_tiles):
                c_all = c_ref[j * t:(j + 1) * t, :]
                c = jnp.sum(jnp.where(lane == hp * HEADS_PER_STEP + hh, c_all, 0.0),
                            axis=1, keepdims=True)
                hi, mid, lo = _split3(-c)
                right = jnp.where(lane == 0, hi, jnp.where(lane == 1, mid, jnp.where(lane == 2, lo, 0.0)))
                k = k_ref[hh, 2 * j:2 * j + 2].reshape(t, HEAD_DIM)
                kaug_ref[hh, j] = jnp.concatenate([k, right.astype(BF16)], axis=1)
                kf = k.astype(F32)
                k_norm2 = jnp.maximum(k_norm2, jnp.sum(kf * kf, axis=1, keepdims=True))
                c_first = jnp.where(lane1 == j, c[0:1, :], c_first)
                c_last = jnp.where(lane1 == j, c[t - 1:t, :], c_last)
            bound_ref[hh, 0:1, :] = c_first
            bound_ref[hh, 1:2, :] = c_last
            bound_ref[hh, 2:3, :] = jnp.broadcast_to(jnp.max(k_norm2, axis=0, keepdims=True), (1, LANES))

    r_aug = lax.broadcasted_iota(jnp.int32, (AUG - HEAD_DIM, t), 0)
    ones_rows = jnp.where(r_aug < 3, 1.0, 0.0).astype(BF16)

    def clamp(n):
        return jnp.minimum(n, n_tiles - 1)

    logits, fold, skippable = [], [], []
    for hh in heads:
        qt = jnp.concatenate([q_ref[hh, 0], q_ref[hh, 1]], axis=1)
        q_aug = jnp.concatenate([qt, ones_rows], axis=0)

        qf = qt.astype(F32)
        q_norm2 = jnp.max(jnp.sum(qf * qf, axis=0, keepdims=True), axis=1, keepdims=True)
        c_here = jnp.sum(jnp.where(lane1 == i, bound_ref[hh, 0:1, :], 0.0), axis=1, keepdims=True)
        gap = (bound_ref[hh, 1:2, :] - c_here) - UNDERFLOW_LOG2
        dead = (lane1 < i) & (gap > 0.0) & (gap * gap > 4.0 * q_norm2 * bound_ref[hh, 2:3, :])
        skippable.append(jnp.sum(jnp.where(dead, 1.0, 0.0), axis=1, keepdims=True))

        def head_logits(n, hh=hh, q_aug=q_aug):
            return jnp.dot(kaug_ref[hh, n], q_aug, preferred_element_type=F32)

        def values(n, hh=hh):
            return jnp.concatenate([v_ref[hh, 2 * n], v_ref[hh, 2 * n + 1]], axis=1)

        logits.append(head_logits)
        fold.append(_Softmax(hh, s_ref, m_ref, l_ref, acc_ref, values))

    key = lax.broadcasted_iota(jnp.int32, (t, t), 0)
    qry = lax.broadcasted_iota(jnp.int32, (t, t), 1)
    causal = key <= qry
    j0 = functools.reduce(jnp.minimum, skippable)[0, 0].astype(jnp.int32)
    n_past = i - j0
    for hh in heads:
        s_ref[hh, 0] = logits[hh](i)
        fold[hh].reset()
    for hh in heads:
        s_ref[hh, 1] = logits[hh](j0)
        fold[hh].fold(0, i, mask=causal)

    def pair_body(p, carry):
        for hh in heads:
            s_ref[hh, 0] = logits[hh](clamp(j0 + 2 * p + 1))
            fold[hh].fold(1, j0 + 2 * p)
        for hh in heads:
            s_ref[hh, 1] = logits[hh](clamp(j0 + 2 * p + 2))
            fold[hh].fold(0, j0 + 2 * p + 1)
        return carry

    lax.fori_loop(0, n_past // 2, pair_body, 0)

    @pl.when(n_past % 2 == 1)
    def _():
        for hh in heads:
            fold[hh].fold(1, i - 1)

    for hh in heads:
        o_ref[:, hh * HEAD_DIM:(hh + 1) * HEAD_DIM] = fold[hh].result().T.astype(o_ref.dtype)


def fox_attn(qv_t, k, c_rep, *, batch, n_heads, q_off, k_off, v_off):
    mb = k.shape[1]
    nb = mb // batch
    nt = nb // 2
    m = mb * ATT_TILE
    t = TILE
    hb = HEADS_PER_STEP
    assert n_heads % hb == 0 and q_off % hb == 0 and k_off % hb == 0 and v_off % hb == 0
    assert nt <= LANES and n_heads <= LANES
    return pl.pallas_call(
        functools.partial(_fox_attn_kernel, n_tiles=nt),
        out_shape=jax.ShapeDtypeStruct((m, n_heads * HEAD_DIM), BF16),
        grid=(batch, n_heads // hb, nt),
        in_specs=[pl.BlockSpec((hb, 2, HEAD_DIM, ATT_TILE),
                               lambda b, h, i: (q_off // hb + h, b * nt + i, 0, 0)),
                  pl.BlockSpec((hb, nb, ATT_TILE, HEAD_DIM), lambda b, h, i: (k_off // hb + h, b, 0, 0)),
                  pl.BlockSpec((hb, nb, HEAD_DIM, ATT_TILE), lambda b, h, i: (v_off // hb + h, b, 0, 0)),
                  pl.BlockSpec((nb * ATT_TILE, LANES), lambda b, h, i: (b, 0))],
        out_specs=pl.BlockSpec((t, hb * HEAD_DIM), lambda b, h, i: (b * nt + i, h)),
        scratch_shapes=[pltpu.VMEM((hb, nt, t, AUG), BF16),
                        pltpu.VMEM((hb, 8, LANES), F32)] + _softmax_scratch(t),
        compiler_params=_params("parallel", "parallel", "arbitrary"),
        name="fox_attn",
    )(qv_t, k, qv_t, c_rep)


def _gated_merge_kernel(oa_ref, of_ref, wa_ref, wf_ref, ga_ref, gf_ref, o_ref):
    ua = jnp.dot(oa_ref[...], wa_ref[...], preferred_element_type=F32)
    uf = jnp.dot(of_ref[...], wf_ref[...], preferred_element_type=F32)
    o_ref[...] = (ga_ref[...].astype(F32) * ua + gf_ref[...].astype(F32) * uf).astype(o_ref.dtype)


def gated_merge(o_a, o_f, w_a, w_f, gates, *, tm=ROW_TILE, tn=COL_TILE):
    m, ka = o_a.shape
    kf = o_f.shape[1]
    n = w_a.shape[1]
    nj = n // tn
    return pl.pallas_call(
        _gated_merge_kernel,
        out_shape=jax.ShapeDtypeStruct((m, n), BF16),
        grid=(m // tm, nj),
        in_specs=[pl.BlockSpec((tm, ka), lambda i, j: (i, 0)),
                  pl.BlockSpec((tm, kf), lambda i, j: (i, 0)),
                  pl.BlockSpec((ka, tn), lambda i, j: (0, j)),
                  pl.BlockSpec((kf, tn), lambda i, j: (0, j)),
                  pl.BlockSpec((tm, tn), lambda i, j: (i, j)),
                  pl.BlockSpec((tm, tn), lambda i, j: (i, nj + j))],
        out_specs=pl.BlockSpec((tm, tn), lambda i, j: (i, j)),
        compiler_params=_params("parallel", "parallel"),
        name="gated_merge",
    )(o_a, o_f, w_a, w_f, gates, gates)


def _mm_res_kernel(lhs_ref, w_ref, res_ref, g_ref, o_ref, *, final_norm):
    kk = pl.program_id(1)

    @pl.when(kk == 0)
    def _():
        o_ref[...] = res_ref[...]

    o_ref[...] += jnp.dot(lhs_ref[...], w_ref[...], preferred_element_type=F32)

    if final_norm:
        @pl.when(kk == pl.num_programs(1) - 1)
        def _():
            hres = o_ref[...]
            ms = jnp.mean(hres * hres, axis=-1, keepdims=True)
            o_ref[...] = hres * lax.rsqrt(ms + RMS_EPS) * g_ref[...]


def mm_res(lhs, w, res, g=None, *, tm=ROW_TILE, tk=512):
    m, k = lhs.shape
    n = w.shape[1]
    tk = min(tk, k)
    final_norm = g is not None
    if g is None:
        g = jnp.ones((n,), F32)
    return pl.pallas_call(
        functools.partial(_mm_res_kernel, final_norm=final_norm),
        out_shape=jax.ShapeDtypeStruct((m, n), F32),
        grid=(m // tm, k // tk),
        in_specs=[pl.BlockSpec((tm, tk), lambda i, kk: (i, kk)),
                  pl.BlockSpec((tk, n), lambda i, kk: (kk, 0)),
                  pl.BlockSpec((tm, n), lambda i, kk: (i, 0)),
                  pl.BlockSpec((1, n), lambda i, kk: (0, 0))],
        out_specs=pl.BlockSpec((tm, n), lambda i, kk: (i, 0)),
        compiler_params=_params("parallel", "arbitrary"),
        name="mm_res",
    )(lhs, w, res, g.reshape(1, n))


def _cross_attn_kernel(q_ref, kv_ref, w_ref, res_ref, o_ref, *, n_heads):
    width = n_heads * HEAD_DIM
    heads = []
    for h in range(n_heads):
        q = q_ref[:, h * HEAD_DIM:(h + 1) * HEAD_DIM]
        k = kv_ref[0, :, h * HEAD_DIM:(h + 1) * HEAD_DIM]
        v = kv_ref[0, :, width + h * HEAD_DIM:width + (h + 1) * HEAD_DIM]
        s = lax.dot_general(q, k, (((1,), (1,)), ((), ())), preferred_element_type=F32)
        m = jnp.max(s, axis=-1, keepdims=True)
        p = jnp.exp(s - m)
        l = jnp.sum(p, axis=-1, keepdims=True)
        o = jnp.dot(p.astype(BF16), v, preferred_element_type=F32) * (1.0 / l)
        heads.append(o.astype(BF16))
    o_all = jnp.concatenate(heads, axis=1)
    o_ref[...] = res_ref[...] + jnp.dot(o_all, w_ref[...], preferred_element_type=F32)


def cross_attn(q, kv, w_o, res, *, batch, n_heads, tm=512):
    m, width = q.shape
    d = w_o.shape[1]
    n_mem = kv.shape[0] // batch
    tiles_per_batch = (m // batch) // tm
    kv3 = kv.reshape(batch, n_mem, 2 * width)
    return pl.pallas_call(
        functools.partial(_cross_attn_kernel, n_heads=n_heads),
        out_shape=jax.ShapeDtypeStruct((m, d), F32),
        grid=(m // tm,),
        in_specs=[pl.BlockSpec((tm, width), lambda i: (i, 0)),
                  pl.BlockSpec((1, n_mem, 2 * width), lambda i: (i // tiles_per_batch, 0, 0)),
                  pl.BlockSpec((width, d), lambda i: (0, 0)),
                  pl.BlockSpec((tm, d), lambda i: (i, 0))],
        out_specs=pl.BlockSpec((tm, d), lambda i: (i, 0)),
        compiler_params=_params("parallel"),
        name="cross_attn",
    )(q, kv3, w_o, res)


def kernel(x, mem, g_mix, w_in, b_forget, w_branch_moba, w_branch_fox, w_mix_out, rel_bias,
           g_cross, g_mem, w_cq, w_ck, w_cv, w_co, g_mlp, w_ff1, w_ff2, g_final):
    batch, seq, d = x.shape
    depth = w_in.shape[0]
    n_heads = rel_bias.shape[0]
    n_fox = b_forget.shape[1]
    wm = n_heads * HEAD_DIM
    wf = n_fox * HEAD_DIM
    m = batch * seq
    scale = HEAD_DIM ** -0.5
    mem2 = mem.reshape(-1, d)

    bias = moba_bias(rel_bias)
    h = x.reshape(m, d)
    for l in range(depth):
        wi = w_in[l]
        o0 = 0
        parts = []
        for width in (wm, wm, wm, wf, wf, wf, n_fox, d, d):
            parts.append(wi[:, o0:o0 + width])
            o0 += width
        wq_m, wk_m, wv_m, wq_f, wk_f, wv_f, w_fl, wg_a, wg_f = parts
        w_qv = jnp.concatenate([wq_m, wq_f, wv_m, wv_f], axis=1).astype(BF16)
        w_k = jnp.concatenate([wk_m, wk_f], axis=1).astype(BF16)
        w_g = jnp.concatenate([wg_a, wg_f], axis=1).astype(BF16)

        n_q_tiles = (wm + wf) // COL_TILE
        qv_t = rms_proj(h, g_mix[l], w_qv, mode="transposed", scale=scale * LOG2E, n_scaled=n_q_tiles)
        k_hm = rms_proj(h, g_mix[l], w_k, mode="headmajor")
        k_hm = k_hm.reshape(k_hm.shape[0], m // ATT_TILE, ATT_TILE, HEAD_DIM)
        gates = rms_proj(h, g_mix[l], w_g, act="sigmoid")
        c_rep = forget_cumsum(h, g_mix[l], w_fl, b_forget[l], batch=batch, n_heads=n_fox)

        o_a = moba_attn(qv_t, k_hm, bias, rel_bias, batch=batch, n_heads=n_heads,
                        q_off=0, k_off=0, v_off=n_heads + n_fox)
        o_f = fox_attn(qv_t, k_hm, c_rep, batch=batch, n_heads=n_fox,
                       q_off=n_heads, k_off=n_heads, v_off=2 * n_heads + n_fox)
        merged = gated_merge(o_a, o_f, w_branch_moba[l].astype(BF16), w_branch_fox[l].astype(BF16), gates)
        h = mm_res(merged, w_mix_out[l].astype(BF16), h)

        cw = w_cq.shape[2]
        qc = rms_proj(h, g_cross[l], w_cq[l].astype(BF16), scale=scale, n_scaled=1, tn=cw)
        w_kv = jnp.concatenate([w_ck[l], w_cv[l]], axis=1).astype(BF16)
        kv = rms_proj(mem2, g_mem[l], w_kv, tn=2 * cw)
        h = cross_attn(qc, kv, w_co[l].astype(BF16), h, batch=batch, n_heads=cw // HEAD_DIM)

        u = rms_proj(h, g_mlp[l], w_ff1[l].astype(BF16), act="relu2")
        h = mm_res(u, w_ff2[l].astype(BF16), h, g_final if l == depth - 1 else None)
    return h.reshape(batch, seq, d)
```

```python
import functools
import math

import jax
import jax.numpy as jnp
from jax import lax
from jax.experimental import pallas as pl
from jax.experimental.pallas import tpu as pltpu

F32 = jnp.float32
BF16 = jnp.bfloat16

HEAD_DIM = 128
MOBA_BLOCK = 256
MOBA_TOP_K = 3
NUM_BUCKETS = 32
MAX_DISTANCE = 1024
N_CROSS_HEADS = 4
RMS_EPS = 1e-6
LOG2E = math.log2(math.e)
NEG_INF = -1e30
LANES = 128
ATT_TILE = 256
TILE = 2 * ATT_TILE
AUG = 256
HEADS_PER_STEP = 2
UNDERFLOW_LOG2 = 160.0
NEAR_BLOCKS = 5
VMEM_LIMIT = 56 * 1024 * 1024
ROW_TILE = 1024
COL_TILE = 1024


def _bucket_thresholds():
    max_exact = NUM_BUCKETS // 2
    thr = list(range(1, max_exact + 1))
    for k in range(max_exact + 1, NUM_BUCKETS):
        v = max_exact * (MAX_DISTANCE / max_exact) ** ((k - max_exact) / (NUM_BUCKETS - max_exact))
        n = int(math.floor(v))
        while max_exact + int(math.log(n / max_exact) / math.log(MAX_DISTANCE / max_exact)
                              * (NUM_BUCKETS - max_exact)) < k:
            n += 1
        thr.append(n)
    return tuple(thr)


BUCKET_THRESHOLDS = _bucket_thresholds()
assert (NEAR_BLOCKS - 1) * MOBA_BLOCK + 1 >= BUCKET_THRESHOLDS[-1]


def _params(*sem):
    return pltpu.CompilerParams(dimension_semantics=sem, vmem_limit_bytes=VMEM_LIMIT)


def _bf16_dot(a, b):
    return jnp.dot(a.astype(BF16), b.astype(BF16), preferred_element_type=F32)


def _rms_proj_kernel(x_ref, g_ref, w_ref, *rest, mode, act, scale, n_scaled, has_aux):
    if has_aux:
        aux_w_ref, o_ref, aux_o_ref, a_ref = rest
    else:
        o_ref, a_ref = rest
    j = pl.program_id(1)

    @pl.when(j == 0)
    def _():
        x = x_ref[...]
        ms = jnp.mean(x * x, axis=-1, keepdims=True)
        a = x * lax.rsqrt(ms + RMS_EPS) * g_ref[...]
        a_ref[...] = a.astype(BF16)
        if has_aux:
            a_hi = a_ref[...]
            a_lo = a - a_hi.astype(F32)
            w_hi, w_lo, _ = _split3(aux_w_ref[...])
            aux_o_ref[...] = _bf16_dot(a_hi, w_hi) + (_bf16_dot(a_lo, w_hi) + _bf16_dot(a_hi, w_lo))

    acc = jnp.dot(a_ref[...], w_ref[...], preferred_element_type=F32)
    if scale is not None:
        acc = acc * jnp.where(j < n_scaled, F32(scale), F32(1.0))
    tm, tn = acc.shape
    if mode == "rowmajor":
        if act == "sigmoid":
            acc = jax.nn.sigmoid(acc)
        elif act == "relu2":
            acc = jnp.square(jnp.maximum(acc, 0.0))
        o_ref[...] = acc.astype(o_ref.dtype)
    elif mode == "headmajor":
        for c in range(tn // LANES):
            o_ref[c] = acc[:, c * LANES:(c + 1) * LANES].astype(o_ref.dtype)
    else:
        for c in range(tn // LANES):
            for r in range(tm // ATT_TILE):
                blk = acc[r * ATT_TILE:(r + 1) * ATT_TILE, c * LANES:(c + 1) * LANES]
                o_ref[c, r] = blk.T.astype(o_ref.dtype)


def rms_proj(x, g, w, *, mode="rowmajor", act=None, scale=None, n_scaled=0, col_tiles=None,
             aux_w=None, out_dtype=BF16, tm=ROW_TILE, tn=COL_TILE):
    m, d = x.shape
    tm, tn = min(tm, m), min(tn, w.shape[1])
    if col_tiles is None:
        col_tiles = tuple(range(w.shape[1] // tn))
    n = len(col_tiles) * tn

    def w_tile(j):
        idx = col_tiles[-1]
        for t, src in reversed(list(enumerate(col_tiles[:-1]))):
            idx = jnp.where(j == t, src, idx)
        return idx

    if mode == "rowmajor":
        out_shape = jax.ShapeDtypeStruct((m, n), out_dtype)
        out_spec = pl.BlockSpec((tm, tn), lambda i, j: (i, j))
    elif mode == "headmajor":
        out_shape = jax.ShapeDtypeStruct((n // LANES, m, LANES), out_dtype)
        out_spec = pl.BlockSpec((tn // LANES, tm, LANES), lambda i, j: (j, i, 0))
    else:
        out_shape = jax.ShapeDtypeStruct((n // LANES, m // ATT_TILE, LANES, ATT_TILE), out_dtype)
        out_spec = pl.BlockSpec((tn // LANES, tm // ATT_TILE, LANES, ATT_TILE),
                                lambda i, j: (j, i, 0, 0))
    in_specs = [pl.BlockSpec((tm, d), lambda i, j: (i, 0)),
                pl.BlockSpec((1, d), lambda i, j: (0, 0)),
                pl.BlockSpec((d, tn), lambda i, j: (0, w_tile(j)))]
    operands = [x, g.reshape(1, d), w]
    if aux_w is not None:
        in_specs.append(pl.BlockSpec((d, LANES), lambda i, j: (0, 0)))
        operands.append(aux_w)
        out_shape = (out_shape, jax.ShapeDtypeStruct((m, LANES), F32))
        out_spec = (out_spec, pl.BlockSpec((tm, LANES), lambda i, j: (i, 0)))
    return pl.pallas_call(
        functools.partial(_rms_proj_kernel, mode=mode, act=act, scale=scale, n_scaled=n_scaled,
                          has_aux=aux_w is not None),
        out_shape=out_shape,
        grid=(m // tm, n // tn),
        in_specs=in_specs,
        out_specs=out_spec,
        scratch_shapes=[pltpu.VMEM((tm, d), BF16)],
        compiler_params=_params("parallel", "arbitrary"),
        name="rms_proj_" + mode,
    )(*operands)


def _forget_cumsum_kernel(f_ref, b_ref, o_ref, carry_ref):
    t = pl.program_id(1)

    @pl.when(t == 0)
    def _():
        carry_ref[...] = jnp.zeros_like(carry_ref)

    f = f_ref[...] + b_ref[...]
    tm = f.shape[0]
    logf = jnp.minimum(f, 0.0) - jnp.log1p(jnp.exp(-jnp.abs(f)))
    logf = logf * LOG2E
    row = lax.broadcasted_iota(jnp.int32, (tm, tm), 0)
    col = lax.broadcasted_iota(jnp.int32, (tm, tm), 1)
    tri = jnp.where(col <= row, 1.0, 0.0).astype(BF16)
    hi, mid, lo = _split3(logf)
    c = (_bf16_dot(tri, hi) + (_bf16_dot(tri, mid) + _bf16_dot(tri, lo))) + carry_ref[0:1, :]
    o_ref[...] = c
    carry_ref[...] = jnp.broadcast_to(c[tm - 1:tm, :], carry_ref.shape)


def forget_cumsum(f, b_f, *, batch, tm=512):
    m = f.shape[0]
    nt = (m // batch) // tm
    b_pad = jnp.zeros((1, LANES), F32).at[0, :b_f.shape[0]].set(b_f)
    return pl.pallas_call(
        _forget_cumsum_kernel,
        out_shape=jax.ShapeDtypeStruct((m, LANES), F32),
        grid=(batch, nt),
        in_specs=[pl.BlockSpec((tm, LANES), lambda b, t: (b * nt + t, 0)),
                  pl.BlockSpec((1, LANES), lambda b, t: (0, 0))],
        out_specs=pl.BlockSpec((tm, LANES), lambda b, t: (b * nt + t, 0)),
        scratch_shapes=[pltpu.VMEM((8, LANES), F32)],
        compiler_params=_params("parallel", "arbitrary"),
        name="forget_cumsum",
    )(f, b_pad)


def _moba_bias_kernel(tab_ref, o_ref):
    h = pl.program_id(0)
    key = lax.broadcasted_iota(jnp.int32, (ATT_TILE, ATT_TILE), 0)
    qry = lax.broadcasted_iota(jnp.int32, (ATT_TILE, ATT_TILE), 1)
    far = tab_ref[h, NUM_BUCKETS - 1] * LOG2E

    def block(delta):
        dist = delta * MOBA_BLOCK + qry - key
        val = jnp.full((ATT_TILE, ATT_TILE), tab_ref[h, 0], F32)
        for k in range(1, NUM_BUCKETS):
            val = jnp.where(dist >= BUCKET_THRESHOLDS[k - 1], tab_ref[h, k], val)
        val = val * LOG2E
        if delta == 0:
            val = jnp.where(dist >= 0, val, NEG_INF)
        return val

    t = [block(delta) for delta in range(NEAR_BLOCKS + 1)]
    a = ATT_TILE
    o_ref[0, 0, 0:a, 0:a] = t[0]
    o_ref[0, 0, 0:a, a:2 * a] = t[1]
    o_ref[0, 0, a:2 * a, 0:a] = jnp.full((a, a), NEG_INF, F32)
    o_ref[0, 0, a:2 * a, a:2 * a] = t[0]
    for d in (1, 2):
        o_ref[0, d, 0:a, 0:a] = t[2 * d] - far
        o_ref[0, d, 0:a, a:2 * a] = t[2 * d + 1] - far
        o_ref[0, d, a:2 * a, 0:a] = t[2 * d - 1] - far
        o_ref[0, d, a:2 * a, a:2 * a] = t[2 * d] - far


def moba_bias(rel_bias):
    h = rel_bias.shape[0]
    return pl.pallas_call(
        _moba_bias_kernel,
        out_shape=jax.ShapeDtypeStruct((h, 3, TILE, TILE), F32),
        grid=(h,),
        in_specs=[pl.BlockSpec(memory_space=pltpu.SMEM)],
        out_specs=pl.BlockSpec((1, 3, TILE, TILE), lambda i: (i, 0, 0, 0)),
        compiler_params=_params("parallel"),
        name="moba_bias",
    )(rel_bias.astype(F32))


class _Softmax:
    def __init__(self, hh, s_ref, m_ref, l_ref, acc_ref, load_values):
        self.s_ref, self.m_ref, self.l_ref, self.acc_ref = s_ref.at[hh], m_ref.at[hh], l_ref.at[hh], acc_ref.at[hh]
        self.load_values = load_values

    def reset(self):
        self.m_ref[...] = jnp.full(self.m_ref.shape, NEG_INF, F32)
        self.l_ref[...] = jnp.zeros(self.l_ref.shape, F32)
        self.acc_ref[...] = jnp.zeros(self.acc_ref.shape, F32)

    def fold(self, slot, tile, mask=None):
        s = self.s_ref[slot]
        if mask is not None:
            s = jnp.where(mask, s, NEG_INF)
        m = self.m_ref[...]
        m_new = jnp.maximum(m, jnp.max(s, axis=0, keepdims=True))
        alpha = jnp.exp2(m - m_new)
        p = jnp.exp2(s - m_new)
        self.l_ref[...] = alpha * self.l_ref[...] + jnp.sum(p, axis=0, keepdims=True)
        self.m_ref[...] = m_new
        self.acc_ref[...] = alpha * self.acc_ref[...] + jnp.dot(
            self.load_values(tile), p.astype(BF16), preferred_element_type=F32)

    def result(self):
        return self.acc_ref[...] * (1.0 / self.l_ref[...])


def _softmax_scratch(t):
    nh = HEADS_PER_STEP
    return [pltpu.VMEM((nh, 2, t, t), F32),
            pltpu.VMEM((nh, 1, t), F32),
            pltpu.VMEM((nh, 1, t), F32),
            pltpu.VMEM((nh, HEAD_DIM, t), F32)]


def _top_k_bias(gate, eligible, blk):
    lowest = float(jnp.finfo(F32).min)
    blk_f = blk.astype(F32)
    g = jnp.where(eligible, gate, NEG_INF)
    bias = jnp.full(gate.shape, NEG_INF, F32)
    for _ in range(MOBA_TOP_K):
        best = jnp.max(g, axis=0, keepdims=True)
        first = jnp.min(jnp.where(g == best, blk_f, float(gate.shape[0])), axis=0, keepdims=True)
        hit = blk_f == first
        bias = jnp.where(hit, 0.0, bias)
        g = jnp.where(hit, lowest, g)
    return jnp.where(eligible, bias, NEG_INF)


def _split3(x):
    hi = x.astype(BF16).astype(F32)
    mid = (x - hi).astype(BF16).astype(F32)
    lo = (x - hi - mid).astype(BF16).astype(F32)
    return hi, mid, lo


def _moba_attn_kernel(tab_ref, q_ref, k_ref, v_ref, bias_ref, o_ref,
                      kaug_ref, kbar_ref, kb3_ref, s_ref, m_ref, l_ref, acc_ref, *, n_blocks):
    hp = pl.program_id(1)
    i = pl.program_id(2)
    n_tiles = n_blocks // 2
    heads = range(HEADS_PER_STEP)

    @pl.when(i == 0)
    def _():
        lane = lax.broadcasted_iota(jnp.int32, (TILE, AUG - HEAD_DIM), 1)
        row = lax.broadcasted_iota(jnp.int32, (TILE, AUG - HEAD_DIM), 0)
        ones_lane = jnp.where(lane == n_blocks, 1.0, jnp.where(lane == n_blocks + 1, 1.0, 0.0))
        for hh in heads:
            for n in range(n_blocks):
                kbar_ref[hh, n:n + 1, :] = jnp.sum(k_ref[hh, n].astype(F32), axis=0, keepdims=True)
            hi, mid, lo = _split3(kbar_ref[hh] * (1.0 / MOBA_BLOCK))
            kb3_ref[hh, 0:n_blocks, :] = hi.astype(BF16)
            kb3_ref[hh, n_blocks:2 * n_blocks, :] = mid.astype(BF16)
            kb3_ref[hh, 2 * n_blocks:3 * n_blocks, :] = lo.astype(BF16)
            for j in range(n_tiles):
                blk_of_row = jnp.where(row < ATT_TILE, 2 * j, 2 * j + 1)
                right = jnp.where(lane == blk_of_row, 1.0, ones_lane).astype(BF16)
                kaug_ref[hh, j] = jnp.concatenate(
                    [k_ref[hh, 2 * j:2 * j + 2].reshape(TILE, HEAD_DIM), right], axis=1)

    blk = lax.broadcasted_iota(jnp.int32, (n_blocks, TILE), 0)
    qlane = lax.broadcasted_iota(jnp.int32, (n_blocks, TILE), 1)
    own = 2 * i + jnp.where(qlane >= ATT_TILE, 1, 0)
    eligible = blk < own
    r16 = lax.broadcasted_iota(jnp.int32, (16, TILE), 0)

    j1 = jnp.where(i >= 1, i - 1, i + 1)
    j2 = jnp.where(i >= 2, i - 2, i + 1)
    n_far = jnp.maximum(i - 2, 0)

    def clamp(j):
        return jnp.minimum(j, n_tiles - 1)

    far_logits, fold = [], []
    for hh in heads:
        qt = jnp.concatenate([q_ref[hh, 0], q_ref[hh, 1]], axis=1)
        g3 = jnp.dot(kb3_ref[hh], qt, preferred_element_type=F32)
        gate = g3[0:n_blocks] + g3[n_blocks:2 * n_blocks] + g3[2 * n_blocks:3 * n_blocks]
        selb = _top_k_bias(gate, eligible, blk)

        far = jnp.full((16, TILE), tab_ref[hp * HEADS_PER_STEP + hh, NUM_BUCKETS - 1] * LOG2E, F32)
        far_hi = far.astype(BF16).astype(F32)
        far_rows = jnp.where(r16 == 0, far_hi, jnp.where(r16 == 1, far - far_hi, 0.0))
        q_aug = jnp.concatenate(
            [qt, selb.astype(BF16), far_rows.astype(BF16),
             jnp.zeros((AUG - HEAD_DIM - n_blocks - 16, TILE), BF16)], axis=0)

        def values(j, hh=hh):
            return jnp.concatenate([v_ref[hh, 2 * j], v_ref[hh, 2 * j + 1]], axis=1)

        def head_far_logits(j, hh=hh, q_aug=q_aug):
            return jnp.dot(kaug_ref[hh, j], q_aug, preferred_element_type=F32)

        sm = _Softmax(hh, s_ref, m_ref, l_ref, acc_ref, values)
        far_logits.append(head_far_logits)
        fold.append(sm)

        sel_own = jnp.sum(jnp.where(blk == 2 * i, selb, 0.0), axis=0, keepdims=True)
        sel_own = jnp.where(qlane[0:1] >= ATT_TILE, sel_own, 0.0)
        kd = k_ref[hh, pl.ds(2 * i, 2)].reshape(TILE, HEAD_DIM)
        sd = jnp.dot(kd, qt, preferred_element_type=F32) + bias_ref[hh, 0]
        s_ref[hh, 0, 0:ATT_TILE, :] = sd[0:ATT_TILE] + sel_own
        s_ref[hh, 0, ATT_TILE:TILE, :] = sd[ATT_TILE:TILE]
        sm.reset()

    for hh in heads:
        s_ref[hh, 1] = far_logits[hh](j1) + bias_ref[hh, 1]
        fold[hh].fold(0, i)
    for hh in heads:
        s_ref[hh, 0] = far_logits[hh](j2) + bias_ref[hh, 2]
        fold[hh].fold(1, j1)
    for hh in heads:
        s_ref[hh, 1] = far_logits[hh](0)
        fold[hh].fold(0, j2)

    def pair_body(p, carry):
        for hh in heads:
            s_ref[hh, 0] = far_logits[hh](clamp(2 * p + 1))
            fold[hh].fold(1, 2 * p)
        for hh in heads:
            s_ref[hh, 1] = far_logits[hh](clamp(2 * p + 2))
            fold[hh].fold(0, 2 * p + 1)
        return carry

    lax.fori_loop(0, n_far // 2, pair_body, 0)

    @pl.when(n_far % 2 == 1)
    def _():
        for hh in heads:
            fold[hh].fold(1, n_far - 1)

    for hh in heads:
        o_ref[:, hh * HEAD_DIM:(hh + 1) * HEAD_DIM] = fold[hh].result().T.astype(o_ref.dtype)


def moba_attn(qv_t, k, bias, rel_bias, *, batch, n_heads, q_off, k_off, v_off):
    mb = k.shape[1]
    nb = mb // batch
    nt = nb // 2
    m = mb * ATT_TILE
    hb = HEADS_PER_STEP
    assert nb + 16 <= AUG - HEAD_DIM and nt >= 4
    assert n_heads % hb == 0 and q_off % hb == 0 and k_off % hb == 0 and v_off % hb == 0
    return pl.pallas_call(
        functools.partial(_moba_attn_kernel, n_blocks=nb),
        out_shape=jax.ShapeDtypeStruct((m, n_heads * HEAD_DIM), BF16),
        grid=(batch, n_heads // hb, nt),
        in_specs=[pl.BlockSpec(memory_space=pltpu.SMEM),
                  pl.BlockSpec((hb, 2, HEAD_DIM, ATT_TILE),
                               lambda b, h, i: (q_off // hb + h, b * nt + i, 0, 0)),
                  pl.BlockSpec((hb, nb, ATT_TILE, HEAD_DIM), lambda b, h, i: (k_off // hb + h, b, 0, 0)),
                  pl.BlockSpec((hb, nb, HEAD_DIM, ATT_TILE), lambda b, h, i: (v_off // hb + h, b, 0, 0)),
                  pl.BlockSpec((hb, 3, TILE, TILE), lambda b, h, i: (h, 0, 0, 0))],
        out_specs=pl.BlockSpec((TILE, hb * HEAD_DIM), lambda b, h, i: (b * nt + i, h)),
        scratch_shapes=[pltpu.VMEM((hb, nt, TILE, AUG), BF16),
                        pltpu.VMEM((hb, nb, HEAD_DIM), F32),
                        pltpu.VMEM((hb, 3 * nb, HEAD_DIM), BF16)] + _softmax_scratch(TILE),
        compiler_params=_params("parallel", "parallel", "arbitrary"),
        name="moba_attn",
    )(rel_bias.astype(F32), qv_t, k, qv_t, bias)


def _fox_attn_kernel(q_ref, k_ref, v_ref, c_ref, o_ref, kaug_ref, bound_ref, s_ref, m_ref, l_ref,
                     acc_ref, *, n_tiles):
    hp = pl.program_id(1)
    i = pl.program_id(2)
    t = TILE
    heads = range(HEADS_PER_STEP)

    lane1 = lax.broadcasted_iota(jnp.int32, (1, LANES), 1)

    @pl.when(i == 0)
    def _():
        lane = lax.broadcasted_iota(jnp.int32, (t, AUG - HEAD_DIM), 1)
        for hh in heads:
            c_first = jnp.zeros((1, LANES), F32)
            c_last = jnp.zeros((1, LANES), F32)
            k_norm2 = jnp.zeros((t, 1), F32)
            for j in range(n_tiles):
                c_all = c_ref[j * t:(j + 1) * t, :]
                c = jnp.sum(jnp.where(lane == hp * HEADS_PER_STEP + hh, c_all, 0.0),
                            axis=1, keepdims=True)
                hi, mid, lo = _split3(-c)
                right = jnp.where(lane == 0, hi, jnp.where(lane == 1, mid, jnp.where(lane == 2, lo, 0.0)))
                k = k_ref[hh, 2 * j:2 * j + 2].reshape(t, HEAD_DIM)
                kaug_ref[hh, j] = jnp.concatenate([k, right.astype(BF16)], axis=1)
                kf = k.astype(F32)
                k_norm2 = jnp.maximum(k_norm2, jnp.sum(kf * kf, axis=1, keepdims=True))
                c_first = jnp.where(lane1 == j, c[0:1, :], c_first)
                c_last = jnp.where(lane1 == j, c[t - 1:t, :], c_last)
            bound_ref[hh, 0:1, :] = c_first
            bound_ref[hh, 1:2, :] = c_last
            bound_ref[hh, 2:3, :] = jnp.broadcast_to(jnp.max(k_norm2, axis=0, keepdims=True), (1, LANES))

    r_aug = lax.broadcasted_iota(jnp.int32, (AUG - HEAD_DIM, t), 0)
    ones_rows = jnp.where(r_aug < 3, 1.0, 0.0).astype(BF16)

    def clamp(n):
        return jnp.minimum(n, n_tiles - 1)

    logits, fold, skippable = [], [], []
    for hh in heads:
        qt = jnp.concatenate([q_ref[hh, 0], q_ref[hh, 1]], axis=1)
        q_aug = jnp.concatenate([qt, ones_rows], axis=0)

        qf = qt.astype(F32)
        q_norm2 = jnp.max(jnp.sum(qf * qf, axis=0, keepdims=True), axis=1, keepdims=True)
        c_here = jnp.sum(jnp.where(lane1 == i, bound_ref[hh, 0:1, :], 0.0), axis=1, keepdims=True)
        gap = (bound_ref[hh, 1:2, :] - c_here) - UNDERFLOW_LOG2
        dead = (lane1 < i) & (gap > 0.0) & (gap * gap > 4.0 * q_norm2 * bound_ref[hh, 2:3, :])
        skippable.append(jnp.sum(jnp.where(dead, 1.0, 0.0), axis=1, keepdims=True))

        def head_logits(n, hh=hh, q_aug=q_aug):
            return jnp.dot(kaug_ref[hh, n], q_aug, preferred_element_type=F32)

        def values(n, hh=hh):
            return jnp.concatenate([v_ref[hh, 2 * n], v_ref[hh, 2 * n + 1]], axis=1)

        logits.append(head_logits)
        fold.append(_Softmax(hh, s_ref, m_ref, l_ref, acc_ref, values))

    key = lax.broadcasted_iota(jnp.int32, (t, t), 0)
    qry = lax.broadcasted_iota(jnp.int32, (t, t), 1)
    causal = key <= qry
    j0 = functools.reduce(jnp.minimum, skippable)[0, 0].astype(jnp.int32)
    n_past = i - j0
    for hh in heads:
        s_ref[hh, 0] = logits[hh](i)
        fold[hh].reset()
    for hh in heads:
        s_ref[hh, 1] = logits[hh](j0)
        fold[hh].fold(0, i, mask=causal)

    def pair_body(p, carry):
        for hh in heads:
            s_ref[hh, 0] = logits[hh](clamp(j0 + 2 * p + 1))
            fold[hh].fold(1, j0 + 2 * p)
        for hh in heads:
            s_ref[hh, 1] = logits[hh](clamp(j0 + 2 * p + 2))
            fold[hh].fold(0, j0 + 2 * p + 1)
        return carry

    lax.fori_loop(0, n_past // 2, pair_body, 0)

    @pl.when(n_past % 2 == 1)
    def _():
        for hh in heads:
            fold[hh].fold(1, i - 1)

    for hh in heads:
        o_ref[:, hh * HEAD_DIM:(hh + 1) * HEAD_DIM] = fold[hh].result().T.astype(o_ref.dtype)


def fox_attn(qv_t, k, c_rep, *, batch, n_heads, q_off, k_off, v_off):
    mb = k.shape[1]
    nb = mb // batch
    nt = nb // 2
    m = mb * ATT_TILE
    t = TILE
    hb = HEADS_PER_STEP
    assert n_heads % hb == 0 and q_off % hb == 0 and k_off % hb == 0 and v_off % hb == 0
    assert nt <= LANES and n_heads <= LANES
    return pl.pallas_call(
        functools.partial(_fox_attn_kernel, n_tiles=nt),
        out_shape=jax.ShapeDtypeStruct((m, n_heads * HEAD_DIM), BF16),
        grid=(batch, n_heads // hb, nt),
        in_specs=[pl.BlockSpec((hb, 2, HEAD_DIM, ATT_TILE),
                               lambda b, h, i: (q_off // hb + h, b * nt + i, 0, 0)),
                  pl.BlockSpec((hb, nb, ATT_TILE, HEAD_DIM), lambda b, h, i: (k_off // hb + h, b, 0, 0)),
                  pl.BlockSpec((hb, nb, HEAD_DIM, ATT_TILE), lambda b, h, i: (v_off // hb + h, b, 0, 0)),
                  pl.BlockSpec((nb * ATT_TILE, LANES), lambda b, h, i: (b, 0))],
        out_specs=pl.BlockSpec((t, hb * HEAD_DIM), lambda b, h, i: (b * nt + i, h)),
        scratch_shapes=[pltpu.VMEM((hb, nt, t, AUG), BF16),
                        pltpu.VMEM((hb, 8, LANES), F32)] + _softmax_scratch(t),
        compiler_params=_params("parallel", "parallel", "arbitrary"),
        name="fox_attn",
    )(qv_t, k, qv_t, c_rep)


def _gated_merge_kernel(oa_ref, of_ref, wa_ref, wf_ref, ga_ref, gf_ref, o_ref):
    ua = jnp.dot(oa_ref[...], wa_ref[...], preferred_element_type=F32)
    uf = jnp.dot(of_ref[...], wf_ref[...], preferred_element_type=F32)
    o_ref[...] = (ga_ref[...].astype(F32) * ua + gf_ref[...].astype(F32) * uf).astype(o_ref.dtype)


def gated_merge(o_a, o_f, w_a, w_f, gates, *, tm=ROW_TILE, tn=COL_TILE):
    m, ka = o_a.shape
    kf = o_f.shape[1]
    n = w_a.shape[1]
    nj = n // tn
    return pl.pallas_call(
        _gated_merge_kernel,
        out_shape=jax.ShapeDtypeStruct((m, n), BF16),
        grid=(m // tm, nj),
        in_specs=[pl.BlockSpec((tm, ka), lambda i, j: (i, 0)),
                  pl.BlockSpec((tm, kf), lambda i, j: (i, 0)),
                  pl.BlockSpec((ka, tn), lambda i, j: (0, j)),
                  pl.BlockSpec((kf, tn), lambda i, j: (0, j)),
                  pl.BlockSpec((tm, tn), lambda i, j: (i, j)),
                  pl.BlockSpec((tm, tn), lambda i, j: (i, nj + j))],
        out_specs=pl.BlockSpec((tm, tn), lambda i, j: (i, j)),
        compiler_params=_params("parallel", "parallel"),
        name="gated_merge",
    )(o_a, o_f, w_a, w_f, gates, gates)


def _mm_res_kernel(lhs_ref, w_ref, res_ref, g_ref, o_ref, *, final_norm):
    kk = pl.program_id(1)

    @pl.when(kk == 0)
    def _():
        o_ref[...] = res_ref[...]

    o_ref[...] += jnp.dot(lhs_ref[...], w_ref[...], preferred_element_type=F32)

    if final_norm:
        @pl.when(kk == pl.num_programs(1) - 1)
        def _():
            hres = o_ref[...]
            ms = jnp.mean(hres * hres, axis=-1, keepdims=True)
            o_ref[...] = hres * lax.rsqrt(ms + RMS_EPS) * g_ref[...]


def mm_res(lhs, w, res, g=None, *, tm=ROW_TILE, tk=1024):
    m, k = lhs.shape
    n = w.shape[1]
    tk = min(tk, k)
    final_norm = g is not None
    if g is None:
        g = jnp.ones((n,), F32)
    return pl.pallas_call(
        functools.partial(_mm_res_kernel, final_norm=final_norm),
        out_shape=jax.ShapeDtypeStruct((m, n), F32),
        grid=(m // tm, k // tk),
        in_specs=[pl.BlockSpec((tm, tk), lambda i, kk: (i, kk)),
                  pl.BlockSpec((tk, n), lambda i, kk: (kk, 0)),
                  pl.BlockSpec((tm, n), lambda i, kk: (i, 0)),
                  pl.BlockSpec((1, n), lambda i, kk: (0, 0))],
        out_specs=pl.BlockSpec((tm, n), lambda i, kk: (i, 0)),
        compiler_params=_params("parallel", "arbitrary"),
        name="mm_res",
    )(lhs, w, res, g.reshape(1, n))


def _cross_attn_kernel(q_ref, kv_ref, w_ref, res_ref, o_ref, *, n_heads):
    width = n_heads * HEAD_DIM
    heads = []
    for h in range(n_heads):
        q = q_ref[:, h * HEAD_DIM:(h + 1) * HEAD_DIM]
        k = kv_ref[0, :, h * HEAD_DIM:(h + 1) * HEAD_DIM]
        v = kv_ref[0, :, width + h * HEAD_DIM:width + (h + 1) * HEAD_DIM]
        s = lax.dot_general(q, k, (((1,), (1,)), ((), ())), preferred_element_type=F32)
        m = jnp.max(s, axis=-1, keepdims=True)
        p = jnp.exp(s - m)
        l = jnp.sum(p, axis=-1, keepdims=True)
        o = jnp.dot(p.astype(BF16), v, preferred_element_type=F32) * (1.0 / l)
        heads.append(o.astype(BF16))
    o_all = jnp.concatenate(heads, axis=1)
    o_ref[...] = res_ref[...] + jnp.dot(o_all, w_ref[...], preferred_element_type=F32)


def cross_attn(q, kv, w_o, res, *, batch, n_heads, tm=512):
    m, width = q.shape
    d = w_o.shape[1]
    n_mem = kv.shape[0] // batch
    tiles_per_batch = (m // batch) // tm
    kv3 = kv.reshape(batch, n_mem, 2 * width)
    return pl.pallas_call(
        functools.partial(_cross_attn_kernel, n_heads=n_heads),
        out_shape=jax.ShapeDtypeStruct((m, d), F32),
        grid=(m // tm,),
        in_specs=[pl.BlockSpec((tm, width), lambda i: (i, 0)),
                  pl.BlockSpec((1, n_mem, 2 * width), lambda i: (i // tiles_per_batch, 0, 0)),
                  pl.BlockSpec((width, d), lambda i: (0, 0)),
                  pl.BlockSpec((tm, d), lambda i: (i, 0))],
        out_specs=pl.BlockSpec((tm, d), lambda i: (i, 0)),
        compiler_params=_params("parallel"),
        name="cross_attn",
    )(q, kv3, w_o, res)


def kernel(x, mem, g_mix, w_in, b_forget, w_branch_moba, w_branch_fox, w_mix_out, rel_bias,
           g_cross, g_mem, w_cq, w_ck, w_cv, w_co, g_mlp, w_ff1, w_ff2, g_final):
    batch, seq, d = x.shape
    depth = w_in.shape[0]
    n_heads = rel_bias.shape[0]
    n_fox = b_forget.shape[1]
    wm = n_heads * HEAD_DIM
    wf = n_fox * HEAD_DIM
    m = batch * seq
    assert wm == wf and wm % COL_TILE == 0
    scale = HEAD_DIM ** -0.5
    mem2 = mem.reshape(-1, d)

    bias = moba_bias(rel_bias)
    h = x.reshape(m, d)
    for l in range(depth):
        wi = w_in[l].astype(BF16)
        qkv_w = 3 * (wm + wf)
        w_fl = jnp.zeros((d, LANES), F32).at[:, :n_fox].set(w_in[l][:, qkv_w:qkv_w + n_fox])
        w_g = wi[:, qkv_w + n_fox:]
        per = wm // COL_TILE

        def part_tiles(*parts):
            return tuple(p * per + t for p in parts for t in range(per))

        qv_t = rms_proj(h, g_mix[l], wi, mode="transposed", col_tiles=part_tiles(0, 3, 2, 5),
                        scale=scale * LOG2E, n_scaled=2 * per)
        k_hm, f_logit = rms_proj(h, g_mix[l], wi, mode="headmajor", col_tiles=part_tiles(1, 4),
                                 aux_w=w_fl)
        k_hm = k_hm.reshape(k_hm.shape[0], m // ATT_TILE, ATT_TILE, HEAD_DIM)
        gates = rms_proj(h, g_mix[l], w_g, act="sigmoid")
        c_rep = forget_cumsum(f_logit, b_forget[l], batch=batch)

        o_a = moba_attn(qv_t, k_hm, bias, rel_bias, batch=batch, n_heads=n_heads,
                        q_off=0, k_off=0, v_off=n_heads + n_fox)
        o_f = fox_attn(qv_t, k_hm, c_rep, batch=batch, n_heads=n_fox,
                       q_off=n_heads, k_off=n_heads, v_off=2 * n_heads + n_fox)
        merged = gated_merge(o_a, o_f, w_branch_moba[l].astype(BF16), w_branch_fox[l].astype(BF16), gates)
        h = mm_res(merged, w_mix_out[l].astype(BF16), h)

        cw = w_cq.shape[2]
        qc = rms_proj(h, g_cross[l], w_cq[l].astype(BF16), scale=scale, n_scaled=1, tn=cw)
        w_kv = jnp.concatenate([w_ck[l], w_cv[l]], axis=1).astype(BF16)
        kv = rms_proj(mem2, g_mem[l], w_kv, tn=2 * cw)
        h = cross_attn(qc, kv, w_co[l].astype(BF16), h, batch=batch, n_heads=cw // HEAD_DIM)

        u = rms_proj(h, g_mlp[l], w_ff1[l].astype(BF16), act="relu2")
        h = mm_res(u, w_ff2[l].astype(BF16), h, g_final if l == depth - 1 else None)
    return h.reshape(batch, seq, d)
```

```python
import functools
import math

import jax
import jax.numpy as jnp
from jax import lax
from jax.experimental import pallas as pl
from jax.experimental.pallas import tpu as pltpu

F32 = jnp.float32
BF16 = jnp.bfloat16

HEAD_DIM = 128
MOBA_BLOCK = 256
MOBA_TOP_K = 3
NUM_BUCKETS = 32
MAX_DISTANCE = 1024
N_CROSS_HEADS = 4
RMS_EPS = 1e-6
LOG2E = math.log2(math.e)
NEG_INF = -1e30
LANES = 128
ATT_TILE = 256
TILE = 2 * ATT_TILE
AUG = 256
HEADS_PER_STEP = 2
UNDERFLOW_LOG2 = 160.0
NEAR_BLOCKS = 5
VMEM_LIMIT = 56 * 1024 * 1024
ROW_TILE = 1024
COL_TILE = 1024


def _bucket_thresholds():
    max_exact = NUM_BUCKETS // 2
    thr = list(range(1, max_exact + 1))
    for k in range(max_exact + 1, NUM_BUCKETS):
        v = max_exact * (MAX_DISTANCE / max_exact) ** ((k - max_exact) / (NUM_BUCKETS - max_exact))
        n = int(math.floor(v))
        while max_exact + int(math.log(n / max_exact) / math.log(MAX_DISTANCE / max_exact)
                              * (NUM_BUCKETS - max_exact)) < k:
            n += 1
        thr.append(n)
    return tuple(thr)


BUCKET_THRESHOLDS = _bucket_thresholds()
assert (NEAR_BLOCKS - 1) * MOBA_BLOCK + 1 >= BUCKET_THRESHOLDS[-1]


def _params(*sem):
    return pltpu.CompilerParams(dimension_semantics=sem, vmem_limit_bytes=VMEM_LIMIT)


def _bf16_dot(a, b):
    return jnp.dot(a.astype(BF16), b.astype(BF16), preferred_element_type=F32)


def _rms_proj_kernel(x_ref, g_ref, w_ref, *rest, mode, act, scale, n_scaled, has_aux, w_rows):
    if has_aux:
        aux_w_ref, o_ref, aux_o_ref, a_ref = rest
    else:
        o_ref, a_ref = rest
    j = pl.program_id(1)

    @pl.when(j == 0)
    def _():
        x = x_ref[...]
        ms = jnp.mean(x * x, axis=-1, keepdims=True)
        a = x * lax.rsqrt(ms + RMS_EPS) * g_ref[...]
        a_ref[...] = a.astype(BF16)
        if has_aux:
            a_hi = a_ref[...]
            a_lo = a - a_hi.astype(F32)
            w_hi, w_lo, _ = _split3(aux_w_ref[...])
            aux_o_ref[...] = _bf16_dot(a_hi, w_hi) + (_bf16_dot(a_lo, w_hi) + _bf16_dot(a_hi, w_lo))

    if w_rows:
        acc = lax.dot_general(a_ref[...], w_ref[...], (((1,), (1,)), ((), ())),
                              preferred_element_type=F32)
    else:
        acc = jnp.dot(a_ref[...], w_ref[...], preferred_element_type=F32)
    if scale is not None:
        acc = acc * jnp.where(j < n_scaled, F32(scale), F32(1.0))
    tm, tn = acc.shape
    if mode == "rowmajor":
        if act == "sigmoid":
            acc = jax.nn.sigmoid(acc)
        elif act == "relu2":
            acc = jnp.square(jnp.maximum(acc, 0.0))
        o_ref[...] = acc.astype(o_ref.dtype)
    elif mode == "headmajor":
        for c in range(tn // LANES):
            o_ref[c] = acc[:, c * LANES:(c + 1) * LANES].astype(o_ref.dtype)
    else:
        for c in range(tn // LANES):
            for r in range(tm // ATT_TILE):
                blk = acc[r * ATT_TILE:(r + 1) * ATT_TILE, c * LANES:(c + 1) * LANES]
                o_ref[c, r] = blk.T.astype(o_ref.dtype)


def rms_proj(x, g, w, *, mode="rowmajor", act=None, scale=None, n_scaled=0, col_tiles=None,
             aux_w=None, w_rows=False, out_dtype=BF16, tm=ROW_TILE, tn=COL_TILE):
    m, d = x.shape
    n_w = w.shape[0] if w_rows else w.shape[1]
    tm, tn = min(tm, m), min(tn, n_w)
    if col_tiles is None:
        col_tiles = tuple(range(n_w // tn))
    n = len(col_tiles) * tn

    def w_tile(j):
        idx = col_tiles[-1]
        for t, src in reversed(list(enumerate(col_tiles[:-1]))):
            idx = jnp.where(j == t, src, idx)
        return idx

    if mode == "rowmajor":
        out_shape = jax.ShapeDtypeStruct((m, n), out_dtype)
        out_spec = pl.BlockSpec((tm, tn), lambda i, j: (i, j))
    elif mode == "headmajor":
        out_shape = jax.ShapeDtypeStruct((n // LANES, m, LANES), out_dtype)
        out_spec = pl.BlockSpec((tn // LANES, tm, LANES), lambda i, j: (j, i, 0))
    else:
        out_shape = jax.ShapeDtypeStruct((n // LANES, m // ATT_TILE, LANES, ATT_TILE), out_dtype)
        out_spec = pl.BlockSpec((tn // LANES, tm // ATT_TILE, LANES, ATT_TILE),
                                lambda i, j: (j, i, 0, 0))
    in_specs = [pl.BlockSpec((tm, d), lambda i, j: (i, 0)),
                pl.BlockSpec((1, d), lambda i, j: (0, 0)),
                pl.BlockSpec((tn, d), lambda i, j: (w_tile(j), 0)) if w_rows else
                pl.BlockSpec((d, tn), lambda i, j: (0, w_tile(j)))]
    operands = [x, g.reshape(1, d), w]
    if aux_w is not None:
        in_specs.append(pl.BlockSpec((d, LANES), lambda i, j: (0, 0)))
        operands.append(aux_w)
        out_shape = (out_shape, jax.ShapeDtypeStruct((m, LANES), F32))
        out_spec = (out_spec, pl.BlockSpec((tm, LANES), lambda i, j: (i, 0)))
    return pl.pallas_call(
        functools.partial(_rms_proj_kernel, mode=mode, act=act, scale=scale, n_scaled=n_scaled,
                          has_aux=aux_w is not None, w_rows=w_rows),
        out_shape=out_shape,
        grid=(m // tm, n // tn),
        in_specs=in_specs,
        out_specs=out_spec,
        scratch_shapes=[pltpu.VMEM((tm, d), BF16)],
        compiler_params=_params("parallel", "arbitrary"),
        name="rms_proj_" + mode,
    )(*operands)


def _forget_cumsum_kernel(f_ref, b_ref, o_ref, carry_ref):
    t = pl.program_id(1)

    @pl.when(t == 0)
    def _():
        carry_ref[...] = jnp.zeros_like(carry_ref)

    f = f_ref[...] + b_ref[...]
    tm = f.shape[0]
    logf = jnp.minimum(f, 0.0) - jnp.log1p(jnp.exp(-jnp.abs(f)))
    logf = logf * LOG2E
    row = lax.broadcasted_iota(jnp.int32, (tm, tm), 0)
    col = lax.broadcasted_iota(jnp.int32, (tm, tm), 1)
    tri = jnp.where(col <= row, 1.0, 0.0).astype(BF16)
    hi, mid, lo = _split3(logf)
    c = (_bf16_dot(tri, hi) + (_bf16_dot(tri, mid) + _bf16_dot(tri, lo))) + carry_ref[0:1, :]
    o_ref[...] = c
    carry_ref[...] = jnp.broadcast_to(c[tm - 1:tm, :], carry_ref.shape)


def forget_cumsum(f, b_f, *, batch, tm=512):
    m = f.shape[0]
    nt = (m // batch) // tm
    b_pad = jnp.zeros((1, LANES), F32).at[0, :b_f.shape[0]].set(b_f)
    return pl.pallas_call(
        _forget_cumsum_kernel,
        out_shape=jax.ShapeDtypeStruct((m, LANES), F32),
        grid=(batch, nt),
        in_specs=[pl.BlockSpec((tm, LANES), lambda b, t: (b * nt + t, 0)),
                  pl.BlockSpec((1, LANES), lambda b, t: (0, 0))],
        out_specs=pl.BlockSpec((tm, LANES), lambda b, t: (b * nt + t, 0)),
        scratch_shapes=[pltpu.VMEM((8, LANES), F32)],
        compiler_params=_params("parallel", "arbitrary"),
        name="forget_cumsum",
    )(f, b_pad)


def _moba_bias_kernel(tab_ref, o_ref):
    h = pl.program_id(0)
    key = lax.broadcasted_iota(jnp.int32, (ATT_TILE, ATT_TILE), 0)
    qry = lax.broadcasted_iota(jnp.int32, (ATT_TILE, ATT_TILE), 1)
    far = tab_ref[h, NUM_BUCKETS - 1] * LOG2E

    def block(delta):
        dist = delta * MOBA_BLOCK + qry - key
        val = jnp.full((ATT_TILE, ATT_TILE), tab_ref[h, 0], F32)
        for k in range(1, NUM_BUCKETS):
            val = jnp.where(dist >= BUCKET_THRESHOLDS[k - 1], tab_ref[h, k], val)
        val = val * LOG2E
        if delta == 0:
            val = jnp.where(dist >= 0, val, NEG_INF)
        return val

    t = [block(delta) for delta in range(NEAR_BLOCKS + 1)]
    a = ATT_TILE
    o_ref[0, 0, 0:a, 0:a] = t[0]
    o_ref[0, 0, 0:a, a:2 * a] = t[1]
    o_ref[0, 0, a:2 * a, 0:a] = jnp.full((a, a), NEG_INF, F32)
    o_ref[0, 0, a:2 * a, a:2 * a] = t[0]
    for d in (1, 2):
        o_ref[0, d, 0:a, 0:a] = t[2 * d] - far
        o_ref[0, d, 0:a, a:2 * a] = t[2 * d + 1] - far
        o_ref[0, d, a:2 * a, 0:a] = t[2 * d - 1] - far
        o_ref[0, d, a:2 * a, a:2 * a] = t[2 * d] - far


def moba_bias(rel_bias):
    h = rel_bias.shape[0]
    return pl.pallas_call(
        _moba_bias_kernel,
        out_shape=jax.ShapeDtypeStruct((h, 3, TILE, TILE), F32),
        grid=(h,),
        in_specs=[pl.BlockSpec(memory_space=pltpu.SMEM)],
        out_specs=pl.BlockSpec((1, 3, TILE, TILE), lambda i: (i, 0, 0, 0)),
        compiler_params=_params("parallel"),
        name="moba_bias",
    )(rel_bias.astype(F32))


class _Softmax:
    def __init__(self, hh, s_ref, m_ref, l_ref, acc_ref, load_values):
        self.s_ref, self.m_ref, self.l_ref, self.acc_ref = s_ref.at[hh], m_ref.at[hh], l_ref.at[hh], acc_ref.at[hh]
        self.load_values = load_values

    def reset(self):
        self.m_ref[...] = jnp.full(self.m_ref.shape, NEG_INF, F32)
        self.l_ref[...] = jnp.zeros(self.l_ref.shape, F32)
        self.acc_ref[...] = jnp.zeros(self.acc_ref.shape, F32)

    def fold(self, slot, tile, mask=None):
        s = self.s_ref[slot]
        if mask is not None:
            s = jnp.where(mask, s, NEG_INF)
        m = self.m_ref[...]
        m_new = jnp.maximum(m, jnp.max(s, axis=0, keepdims=True))
        alpha = jnp.exp2(m - m_new)
        p = jnp.exp2(s - m_new)
        self.l_ref[...] = alpha * self.l_ref[...] + jnp.sum(p, axis=0, keepdims=True)
        self.m_ref[...] = m_new
        self.acc_ref[...] = alpha * self.acc_ref[...] + jnp.dot(
            self.load_values(tile), p.astype(BF16), preferred_element_type=F32)

    def result(self):
        return self.acc_ref[...] * (1.0 / self.l_ref[...])


def _softmax_scratch(t):
    nh = HEADS_PER_STEP
    return [pltpu.VMEM((nh, 2, t, t), F32),
            pltpu.VMEM((nh, 1, t), F32),
            pltpu.VMEM((nh, 1, t), F32),
            pltpu.VMEM((nh, HEAD_DIM, t), F32)]


def _top_k_bias(gate, eligible, blk):
    lowest = float(jnp.finfo(F32).min)
    blk_f = blk.astype(F32)
    g = jnp.where(eligible, gate, NEG_INF)
    bias = jnp.full(gate.shape, NEG_INF, F32)
    for _ in range(MOBA_TOP_K):
        best = jnp.max(g, axis=0, keepdims=True)
        first = jnp.min(jnp.where(g == best, blk_f, float(gate.shape[0])), axis=0, keepdims=True)
        hit = blk_f == first
        bias = jnp.where(hit, 0.0, bias)
        g = jnp.where(hit, lowest, g)
    return jnp.where(eligible, bias, NEG_INF)


def _split3(x):
    hi = x.astype(BF16).astype(F32)
    mid = (x - hi).astype(BF16).astype(F32)
    lo = (x - hi - mid).astype(BF16).astype(F32)
    return hi, mid, lo


def _moba_attn_kernel(tab_ref, q_ref, k_ref, v_ref, bias_ref, o_ref,
                      kaug_ref, kbar_ref, kb3_ref, s_ref, m_ref, l_ref, acc_ref, *, n_blocks):
    hp = pl.program_id(1)
    i = pl.program_id(2)
    n_tiles = n_blocks // 2
    heads = range(HEADS_PER_STEP)

    @pl.when(i == 0)
    def _():
        lane = lax.broadcasted_iota(jnp.int32, (TILE, AUG - HEAD_DIM), 1)
        row = lax.broadcasted_iota(jnp.int32, (TILE, AUG - HEAD_DIM), 0)
        ones_lane = jnp.where(lane == n_blocks, 1.0, jnp.where(lane == n_blocks + 1, 1.0, 0.0))
        for hh in heads:
            for n in range(n_blocks):
                kbar_ref[hh, n:n + 1, :] = jnp.sum(k_ref[hh, n].astype(F32), axis=0, keepdims=True)
            hi, mid, lo = _split3(kbar_ref[hh] * (1.0 / MOBA_BLOCK))
            kb3_ref[hh, 0:n_blocks, :] = hi.astype(BF16)
            kb3_ref[hh, n_blocks:2 * n_blocks, :] = mid.astype(BF16)
            kb3_ref[hh, 2 * n_blocks:3 * n_blocks, :] = lo.astype(BF16)
            for j in range(n_tiles):
                blk_of_row = jnp.where(row < ATT_TILE, 2 * j, 2 * j + 1)
                right = jnp.where(lane == blk_of_row, 1.0, ones_lane).astype(BF16)
                kaug_ref[hh, j] = jnp.concatenate(
                    [k_ref[hh, 2 * j:2 * j + 2].reshape(TILE, HEAD_DIM), right], axis=1)

    blk = lax.broadcasted_iota(jnp.int32, (n_blocks, TILE), 0)
    qlane = lax.broadcasted_iota(jnp.int32, (n_blocks, TILE), 1)
    own = 2 * i + jnp.where(qlane >= ATT_TILE, 1, 0)
    eligible = blk < own
    r16 = lax.broadcasted_iota(jnp.int32, (16, TILE), 0)

    j1 = jnp.where(i >= 1, i - 1, i + 1)
    j2 = jnp.where(i >= 2, i - 2, i + 1)
    n_far = jnp.maximum(i - 2, 0)

    def clamp(j):
        return jnp.minimum(j, n_tiles - 1)

    far_logits, fold = [], []
    for hh in heads:
        qt = jnp.concatenate([q_ref[hh, 0], q_ref[hh, 1]], axis=1)
        g3 = jnp.dot(kb3_ref[hh], qt, preferred_element_type=F32)
        gate = g3[0:n_blocks] + g3[n_blocks:2 * n_blocks] + g3[2 * n_blocks:3 * n_blocks]
        selb = _top_k_bias(gate, eligible, blk)

        far = jnp.full((16, TILE), tab_ref[hp * HEADS_PER_STEP + hh, NUM_BUCKETS - 1] * LOG2E, F32)
        far_hi = far.astype(BF16).astype(F32)
        far_rows = jnp.where(r16 == 0, far_hi, jnp.where(r16 == 1, far - far_hi, 0.0))
        q_aug = jnp.concatenate(
            [qt, selb.astype(BF16), far_rows.astype(BF16),
             jnp.zeros((AUG - HEAD_DIM - n_blocks - 16, TILE), BF16)], axis=0)

        def values(j, hh=hh):
            return jnp.concatenate([v_ref[hh, 2 * j], v_ref[hh, 2 * j + 1]], axis=1)

        def head_far_logits(j, hh=hh, q_aug=q_aug):
            return jnp.dot(kaug_ref[hh, j], q_aug, preferred_element_type=F32)

        sm = _Softmax(hh, s_ref, m_ref, l_ref, acc_ref, values)
        far_logits.append(head_far_logits)
        fold.append(sm)

        sel_own = jnp.sum(jnp.where(blk == 2 * i, selb, 0.0), axis=0, keepdims=True)
        sel_own = jnp.where(qlane[0:1] >= ATT_TILE, sel_own, 0.0)
        kd = k_ref[hh, pl.ds(2 * i, 2)].reshape(TILE, HEAD_DIM)
        sd = jnp.dot(kd, qt, preferred_element_type=F32) + bias_ref[hh, 0]
        s_ref[hh, 0, 0:ATT_TILE, :] = sd[0:ATT_TILE] + sel_own
        s_ref[hh, 0, ATT_TILE:TILE, :] = sd[ATT_TILE:TILE]
        sm.reset()

    for hh in heads:
        s_ref[hh, 1] = far_logits[hh](j1) + bias_ref[hh, 1]
        fold[hh].fold(0, i)
    for hh in heads:
        s_ref[hh, 0] = far_logits[hh](j2) + bias_ref[hh, 2]
        fold[hh].fold(1, j1)
    for hh in heads:
        s_ref[hh, 1] = far_logits[hh](0)
        fold[hh].fold(0, j2)

    def pair_body(p, carry):
        for hh in heads:
            s_ref[hh, 0] = far_logits[hh](clamp(2 * p + 1))
            fold[hh].fold(1, 2 * p)
        for hh in heads:
            s_ref[hh, 1] = far_logits[hh](clamp(2 * p + 2))
            fold[hh].fold(0, 2 * p + 1)
        return carry

    lax.fori_loop(0, n_far // 2, pair_body, 0)

    @pl.when(n_far % 2 == 1)
    def _():
        for hh in heads:
            fold[hh].fold(1, n_far - 1)

    for hh in heads:
        o_ref[:, hh * HEAD_DIM:(hh + 1) * HEAD_DIM] = fold[hh].result().T.astype(o_ref.dtype)


def moba_attn(qv_t, k, bias, rel_bias, *, batch, n_heads, q_off, k_off, v_off):
    mb = k.shape[1]
    nb = mb // batch
    nt = nb // 2
    m = mb * ATT_TILE
    hb = HEADS_PER_STEP
    assert nb + 16 <= AUG - HEAD_DIM and nt >= 4
    assert n_heads % hb == 0 and q_off % hb == 0 and k_off % hb == 0 and v_off % hb == 0
    return pl.pallas_call(
        functools.partial(_moba_attn_kernel, n_blocks=nb),
        out_shape=jax.ShapeDtypeStruct((m, n_heads * HEAD_DIM), BF16),
        grid=(batch, n_heads // hb, nt),
        in_specs=[pl.BlockSpec(memory_space=pltpu.SMEM),
                  pl.BlockSpec((hb, 2, HEAD_DIM, ATT_TILE),
                               lambda b, h, i: (q_off // hb + h, b * nt + i, 0, 0)),
                  pl.BlockSpec((hb, nb, ATT_TILE, HEAD_DIM), lambda b, h, i: (k_off // hb + h, b, 0, 0)),
                  pl.BlockSpec((hb, nb, HEAD_DIM, ATT_TILE), lambda b, h, i: (v_off // hb + h, b, 0, 0)),
                  pl.BlockSpec((hb, 3, TILE, TILE), lambda b, h, i: (h, 0, 0, 0))],
        out_specs=pl.BlockSpec((TILE, hb * HEAD_DIM), lambda b, h, i: (b * nt + i, h)),
        scratch_shapes=[pltpu.VMEM((hb, nt, TILE, AUG), BF16),
                        pltpu.VMEM((hb, nb, HEAD_DIM), F32),
                        pltpu.VMEM((hb, 3 * nb, HEAD_DIM), BF16)] + _softmax_scratch(TILE),
        compiler_params=_params("parallel", "parallel", "arbitrary"),
        name="moba_attn",
    )(rel_bias.astype(F32), qv_t, k, qv_t, bias)


def _fox_attn_kernel(q_ref, k_ref, v_ref, c_ref, o_ref, kaug_ref, bound_ref, s_ref, m_ref, l_ref,
                     acc_ref, *, n_tiles):
    hp = pl.program_id(1)
    i = pl.program_id(2)
    t = TILE
    heads = range(HEADS_PER_STEP)

    lane1 = lax.broadcasted_iota(jnp.int32, (1, LANES), 1)

    @pl.when(i == 0)
    def _():
        lane = lax.broadcasted_iota(jnp.int32, (t, AUG - HEAD_DIM), 1)
        for hh in heads:
            c_first = jnp.zeros((1, LANES), F32)
            c_last = jnp.zeros((1, LANES), F32)
            k_norm2 = jnp.zeros((t, 1), F32)
            for j in range(n_tiles):
                c_all = c_ref[j * t:(j + 1) * t, :]
                c = jnp.sum(jnp.where(lane == hp * HEADS_PER_STEP + hh, c_all, 0.0),
                            axis=1, keepdims=True)
                hi, mid, lo = _split3(-c)
                right = jnp.where(lane == 0, hi, jnp.where(lane == 1, mid, jnp.where(lane == 2, lo, 0.0)))
                k = k_ref[hh, 2 * j:2 * j + 2].reshape(t, HEAD_DIM)
                kaug_ref[hh, j] = jnp.concatenate([k, right.astype(BF16)], axis=1)
                kf = k.astype(F32)
                k_norm2 = jnp.maximum(k_norm2, jnp.sum(kf * kf, axis=1, keepdims=True))
                c_first = jnp.where(lane1 == j, c[0:1, :], c_first)
                c_last = jnp.where(lane1 == j, c[t - 1:t, :], c_last)
            bound_ref[hh, 0:1, :] = c_first
            bound_ref[hh, 1:2, :] = c_last
            bound_ref[hh, 2:3, :] = jnp.broadcast_to(jnp.max(k_norm2, axis=0, keepdims=True), (1, LANES))

    r_aug = lax.broadcasted_iota(jnp.int32, (AUG - HEAD_DIM, t), 0)
    ones_rows = jnp.where(r_aug < 3, 1.0, 0.0).astype(BF16)

    def clamp(n):
        return jnp.minimum(n, n_tiles - 1)

    logits, fold, skippable = [], [], []
    for hh in heads:
        qt = jnp.concatenate([q_ref[hh, 0], q_ref[hh, 1]], axis=1)
        q_aug = jnp.concatenate([qt, ones_rows], axis=0)

        qf = qt.astype(F32)
        q_norm2 = jnp.max(jnp.sum(qf * qf, axis=0, keepdims=True), axis=1, keepdims=True)
        c_here = jnp.sum(jnp.where(lane1 == i, bound_ref[hh, 0:1, :], 0.0), axis=1, keepdims=True)
        gap = (bound_ref[hh, 1:2, :] - c_here) - UNDERFLOW_LOG2
        dead = (lane1 < i) & (gap > 0.0) & (gap * gap > 4.0 * q_norm2 * bound_ref[hh, 2:3, :])
        skippable.append(jnp.sum(jnp.where(dead, 1.0, 0.0), axis=1, keepdims=True))

        def head_logits(n, hh=hh, q_aug=q_aug):
            return jnp.dot(kaug_ref[hh, n], q_aug, preferred_element_type=F32)

        def values(n, hh=hh):
            return jnp.concatenate([v_ref[hh, 2 * n], v_ref[hh, 2 * n + 1]], axis=1)

        logits.append(head_logits)
        fold.append(_Softmax(hh, s_ref, m_ref, l_ref, acc_ref, values))

    key = lax.broadcasted_iota(jnp.int32, (t, t), 0)
    qry = lax.broadcasted_iota(jnp.int32, (t, t), 1)
    causal = key <= qry
    j0 = functools.reduce(jnp.minimum, skippable)[0, 0].astype(jnp.int32)
    n_past = i - j0
    for hh in heads:
        s_ref[hh, 0] = logits[hh](i)
        fold[hh].reset()
    for hh in heads:
        s_ref[hh, 1] = logits[hh](j0)
        fold[hh].fold(0, i, mask=causal)

    def pair_body(p, carry):
        for hh in heads:
            s_ref[hh, 0] = logits[hh](clamp(j0 + 2 * p + 1))
            fold[hh].fold(1, j0 + 2 * p)
        for hh in heads:
            s_ref[hh, 1] = logits[hh](clamp(j0 + 2 * p + 2))
            fold[hh].fold(0, j0 + 2 * p + 1)
        return carry

    lax.fori_loop(0, n_past // 2, pair_body, 0)

    @pl.when(n_past % 2 == 1)
    def _():
        for hh in heads:
            fold[hh].fold(1, i - 1)

    for hh in heads:
        o_ref[:, hh * HEAD_DIM:(hh + 1) * HEAD_DIM] = fold[hh].result().T.astype(o_ref.dtype)


def fox_attn(qv_t, k, c_rep, *, batch, n_heads, q_off, k_off, v_off):
    mb = k.shape[1]
    nb = mb // batch
    nt = nb // 2
    m = mb * ATT_TILE
    t = TILE
    hb = HEADS_PER_STEP
    assert n_heads % hb == 0 and q_off % hb == 0 and k_off % hb == 0 and v_off % hb == 0
    assert nt <= LANES and n_heads <= LANES
    return pl.pallas_call(
        functools.partial(_fox_attn_kernel, n_tiles=nt),
        out_shape=jax.ShapeDtypeStruct((m, n_heads * HEAD_DIM), BF16),
        grid=(batch, n_heads // hb, nt),
        in_specs=[pl.BlockSpec((hb, 2, HEAD_DIM, ATT_TILE),
                               lambda b, h, i: (q_off // hb + h, b * nt + i, 0, 0)),
                  pl.BlockSpec((hb, nb, ATT_TILE, HEAD_DIM), lambda b, h, i: (k_off // hb + h, b, 0, 0)),
                  pl.BlockSpec((hb, nb, HEAD_DIM, ATT_TILE), lambda b, h, i: (v_off // hb + h, b, 0, 0)),
                  pl.BlockSpec((nb * ATT_TILE, LANES), lambda b, h, i: (b, 0))],
        out_specs=pl.BlockSpec((t, hb * HEAD_DIM), lambda b, h, i: (b * nt + i, h)),
        scratch_shapes=[pltpu.VMEM((hb, nt, t, AUG), BF16),
                        pltpu.VMEM((hb, 8, LANES), F32)] + _softmax_scratch(t),
        compiler_params=_params("parallel", "parallel", "arbitrary"),
        name="fox_attn",
    )(qv_t, k, qv_t, c_rep)


def _gated_merge_kernel(oa_ref, of_ref, wa_ref, wf_ref, ga_ref, gf_ref, o_ref):
    ua = jnp.dot(oa_ref[...], wa_ref[...], preferred_element_type=F32)
    uf = jnp.dot(of_ref[...], wf_ref[...], preferred_element_type=F32)
    o_ref[...] = (ga_ref[...].astype(F32) * ua + gf_ref[...].astype(F32) * uf).astype(o_ref.dtype)


def gated_merge(o_a, o_f, w_a, w_f, gates, *, tm=ROW_TILE, tn=COL_TILE):
    m, ka = o_a.shape
    kf = o_f.shape[1]
    n = w_a.shape[1]
    nj = n // tn
    return pl.pallas_call(
        _gated_merge_kernel,
        out_shape=jax.ShapeDtypeStruct((m, n), BF16),
        grid=(m // tm, nj),
        in_specs=[pl.BlockSpec((tm, ka), lambda i, j: (i, 0)),
                  pl.BlockSpec((tm, kf), lambda i, j: (i, 0)),
                  pl.BlockSpec((ka, tn), lambda i, j: (0, j)),
                  pl.BlockSpec((kf, tn), lambda i, j: (0, j)),
                  pl.BlockSpec((tm, tn), lambda i, j: (i, j)),
                  pl.BlockSpec((tm, tn), lambda i, j: (i, nj + j))],
        out_specs=pl.BlockSpec((tm, tn), lambda i, j: (i, j)),
        compiler_params=_params("parallel", "parallel"),
        name="gated_merge",
    )(o_a, o_f, w_a, w_f, gates, gates)


def _mm_res_kernel(lhs_ref, w_ref, res_ref, g_ref, o_ref, *, final_norm):
    kk = pl.program_id(1)

    @pl.when(kk == 0)
    def _():
        o_ref[...] = res_ref[...]

    o_ref[...] += jnp.dot(lhs_ref[...], w_ref[...], preferred_element_type=F32)

    if final_norm:
        @pl.when(kk == pl.num_programs(1) - 1)
        def _():
            hres = o_ref[...]
            ms = jnp.mean(hres * hres, axis=-1, keepdims=True)
            o_ref[...] = hres * lax.rsqrt(ms + RMS_EPS) * g_ref[...]


def mm_res(lhs, w, res, g=None, *, tm=ROW_TILE, tk=1024):
    m, k = lhs.shape
    n = w.shape[1]
    tk = min(tk, k)
    final_norm = g is not None
    if g is None:
        g = jnp.ones((n,), F32)
    return pl.pallas_call(
        functools.partial(_mm_res_kernel, final_norm=final_norm),
        out_shape=jax.ShapeDtypeStruct((m, n), F32),
        grid=(m // tm, k // tk),
        in_specs=[pl.BlockSpec((tm, tk), lambda i, kk: (i, kk)),
                  pl.BlockSpec((tk, n), lambda i, kk: (kk, 0)),
                  pl.BlockSpec((tm, n), lambda i, kk: (i, 0)),
                  pl.BlockSpec((1, n), lambda i, kk: (0, 0))],
        out_specs=pl.BlockSpec((tm, n), lambda i, kk: (i, 0)),
        compiler_params=_params("parallel", "arbitrary"),
        name="mm_res",
    )(lhs, w, res, g.reshape(1, n))


def _cross_attn_kernel(h_ref, g_ref, wq_ref, kv_ref, wo_ref, o_ref, *, n_heads):
    width = n_heads * HEAD_DIM
    hres = h_ref[...]
    ms = jnp.mean(hres * hres, axis=-1, keepdims=True)
    c = (hres * lax.rsqrt(ms + RMS_EPS) * g_ref[...]).astype(BF16)
    q_all = (jnp.dot(c, wq_ref[...], preferred_element_type=F32)
             * (HEAD_DIM ** -0.5 * LOG2E)).astype(BF16)
    heads = []
    for h in range(n_heads):
        q = q_all[:, h * HEAD_DIM:(h + 1) * HEAD_DIM]
        k = kv_ref[0, :, h * HEAD_DIM:(h + 1) * HEAD_DIM]
        v = kv_ref[0, :, width + h * HEAD_DIM:width + (h + 1) * HEAD_DIM]
        s = lax.dot_general(q, k, (((1,), (1,)), ((), ())), preferred_element_type=F32)
        m = jnp.max(s, axis=-1, keepdims=True)
        p = jnp.exp2(s - m)
        l = jnp.sum(p, axis=-1, keepdims=True)
        o = jnp.dot(p.astype(BF16), v, preferred_element_type=F32) * (1.0 / l)
        heads.append(o.astype(BF16))
    o_all = jnp.concatenate(heads, axis=1)
    o_ref[...] = hres + jnp.dot(o_all, wo_ref[...], preferred_element_type=F32)


def cross_attn(h, g, w_q, kv, w_o, *, batch, n_heads, tm=512):
    m, d = h.shape
    width = w_q.shape[1]
    n_mem = kv.shape[0] // batch
    tiles_per_batch = (m // batch) // tm
    kv3 = kv.reshape(batch, n_mem, 2 * width)
    return pl.pallas_call(
        functools.partial(_cross_attn_kernel, n_heads=n_heads),
        out_shape=jax.ShapeDtypeStruct((m, d), F32),
        grid=(m // tm,),
        in_specs=[pl.BlockSpec((tm, d), lambda i: (i, 0)),
                  pl.BlockSpec((1, d), lambda i: (0, 0)),
                  pl.BlockSpec((d, width), lambda i: (0, 0)),
                  pl.BlockSpec((1, n_mem, 2 * width), lambda i: (i // tiles_per_batch, 0, 0)),
                  pl.BlockSpec((width, d), lambda i: (0, 0))],
        out_specs=pl.BlockSpec((tm, d), lambda i: (i, 0)),
        compiler_params=_params("parallel"),
        name="cross_attn",
    )(h, g.reshape(1, d), w_q, kv3, w_o)


def kernel(x, mem, g_mix, w_in, b_forget, w_branch_moba, w_branch_fox, w_mix_out, rel_bias,
           g_cross, g_mem, w_cq, w_ck, w_cv, w_co, g_mlp, w_ff1, w_ff2, g_final):
    batch, seq, d = x.shape
    depth = w_in.shape[0]
    n_heads = rel_bias.shape[0]
    n_fox = b_forget.shape[1]
    wm = n_heads * HEAD_DIM
    wf = n_fox * HEAD_DIM
    m = batch * seq
    assert wm == wf and wm % COL_TILE == 0
    scale = HEAD_DIM ** -0.5
    mem2 = mem.reshape(-1, d)

    bias = moba_bias(rel_bias)
    h = x.reshape(m, d)
    for l in range(depth):
        wi = jnp.swapaxes(w_in[l], 0, 1).astype(BF16)
        qkv_w = 3 * (wm + wf)
        w_fl = jnp.zeros((d, LANES), F32).at[:, :n_fox].set(w_in[l][:, qkv_w:qkv_w + n_fox])
        w_g = wi[qkv_w + n_fox:]
        per = wm // COL_TILE

        def part_tiles(*parts):
            return tuple(p * per + t for p in parts for t in range(per))

        qv_t = rms_proj(h, g_mix[l], wi, mode="transposed", col_tiles=part_tiles(0, 3, 2, 5),
                        w_rows=True, scale=scale * LOG2E, n_scaled=2 * per)
        k_hm, f_logit = rms_proj(h, g_mix[l], wi, mode="headmajor", col_tiles=part_tiles(1, 4),
                                 w_rows=True, aux_w=w_fl)
        k_hm = k_hm.reshape(k_hm.shape[0], m // ATT_TILE, ATT_TILE, HEAD_DIM)
        gates = rms_proj(h, g_mix[l], w_g, w_rows=True, act="sigmoid")
        c_rep = forget_cumsum(f_logit, b_forget[l], batch=batch)

        o_a = moba_attn(qv_t, k_hm, bias, rel_bias, batch=batch, n_heads=n_heads,
                        q_off=0, k_off=0, v_off=n_heads + n_fox)
        o_f = fox_attn(qv_t, k_hm, c_rep, batch=batch, n_heads=n_fox,
                       q_off=n_heads, k_off=n_heads, v_off=2 * n_heads + n_fox)
        merged = gated_merge(o_a, o_f, w_branch_moba[l].astype(BF16), w_branch_fox[l].astype(BF16), gates)
        h = mm_res(merged, w_mix_out[l].astype(BF16), h)

        cw = w_cq.shape[2]
        w_kv = jnp.concatenate([w_ck[l], w_cv[l]], axis=1).astype(BF16)
        kv = rms_proj(mem2, g_mem[l], w_kv, tn=2 * cw)
        h = cross_attn(h, g_cross[l], w_cq[l].astype(BF16), kv, w_co[l].astype(BF16),
                       batch=batch, n_heads=cw // HEAD_DIM)

        u = rms_proj(h, g_mlp[l], w_ff1[l].astype(BF16), act="relu2")
        h = mm_res(u, w_ff2[l].astype(BF16), h, g_final if l == depth - 1 else None)
    return h.reshape(batch, seq, d)
```

```python
import functools
import math

import jax
import jax.numpy as jnp
from jax import lax
from jax.experimental import pallas as pl
from jax.experimental.pallas import tpu as pltpu

F32 = jnp.float32
BF16 = jnp.bfloat16

HEAD_DIM = 128
MOBA_BLOCK = 256
MOBA_TOP_K = 3
NUM_BUCKETS = 32
MAX_DISTANCE = 1024
N_CROSS_HEADS = 4
RMS_EPS = 1e-6
LOG2E = math.log2(math.e)
NEG_INF = -1e30
LANES = 128
ATT_TILE = 256
TILE = 2 * ATT_TILE
AUG = 256
HEADS_PER_STEP = 2
UNDERFLOW_LOG2 = 160.0
NEAR_BLOCKS = 5
VMEM_LIMIT = 56 * 1024 * 1024
ROW_TILE = 1024
COL_TILE = 1024


def _bucket_thresholds():
    max_exact = NUM_BUCKETS // 2
    thr = list(range(1, max_exact + 1))
    for k in range(max_exact + 1, NUM_BUCKETS):
        v = max_exact * (MAX_DISTANCE / max_exact) ** ((k - max_exact) / (NUM_BUCKETS - max_exact))
        n = int(math.floor(v))
        while max_exact + int(math.log(n / max_exact) / math.log(MAX_DISTANCE / max_exact)
                              * (NUM_BUCKETS - max_exact)) < k:
            n += 1
        thr.append(n)
    return tuple(thr)


BUCKET_THRESHOLDS = _bucket_thresholds()
assert (NEAR_BLOCKS - 1) * MOBA_BLOCK + 1 >= BUCKET_THRESHOLDS[-1]


def _params(*sem):
    return pltpu.CompilerParams(dimension_semantics=sem, vmem_limit_bytes=VMEM_LIMIT)


def _bf16_dot(a, b):
    return jnp.dot(a.astype(BF16), b.astype(BF16), preferred_element_type=F32)


def _rms_proj_kernel(x_ref, g_ref, w_ref, *rest, mode, act, scale, n_scaled, has_aux, w_rows):
    if has_aux:
        aux_w_ref, o_ref, aux_o_ref, a_ref = rest
    else:
        o_ref, a_ref = rest
    j = pl.program_id(1)

    @pl.when(j == 0)
    def _():
        x = x_ref[...]
        ms = jnp.mean(x * x, axis=-1, keepdims=True)
        a = x * lax.rsqrt(ms + RMS_EPS) * g_ref[...]
        a_ref[...] = a.astype(BF16)
        if has_aux:
            w_hi, w_lo, _ = _split3(aux_w_ref[...])
            both = _bf16_dot(a_ref[...], jnp.concatenate([w_hi, w_lo], axis=1))
            aux_o_ref[...] = both[:, :LANES] + both[:, LANES:]

    if w_rows:
        acc = lax.dot_general(a_ref[...], w_ref[...], (((1,), (1,)), ((), ())),
                              preferred_element_type=F32)
    else:
        acc = jnp.dot(a_ref[...], w_ref[...], preferred_element_type=F32)
    if scale is not None:
        acc = acc * jnp.where(j < n_scaled, F32(scale), F32(1.0))
    tm, tn = acc.shape
    if mode == "rowmajor":
        if act == "sigmoid":
            acc = jax.nn.sigmoid(acc)
        elif act == "relu2":
            acc = jnp.square(jnp.maximum(acc, 0.0))
        o_ref[...] = acc.astype(o_ref.dtype)
    elif mode == "headmajor":
        for c in range(tn // LANES):
            o_ref[c] = acc[:, c * LANES:(c + 1) * LANES].astype(o_ref.dtype)
    else:
        for c in range(tn // LANES):
            for r in range(tm // ATT_TILE):
                blk = acc[r * ATT_TILE:(r + 1) * ATT_TILE, c * LANES:(c + 1) * LANES]
                o_ref[c, r] = blk.T.astype(o_ref.dtype)


def rms_proj(x, g, w, *, mode="rowmajor", act=None, scale=None, n_scaled=0, col_tiles=None,
             aux_w=None, w_rows=False, out_dtype=BF16, tm=ROW_TILE, tn=COL_TILE):
    m, d = x.shape
    n_w = w.shape[0] if w_rows else w.shape[1]
    tm, tn = min(tm, m), min(tn, n_w)
    if col_tiles is None:
        col_tiles = tuple(range(n_w // tn))
    n = len(col_tiles) * tn

    def w_tile(j):
        idx = col_tiles[-1]
        for t, src in reversed(list(enumerate(col_tiles[:-1]))):
            idx = jnp.where(j == t, src, idx)
        return idx

    if mode == "rowmajor":
        out_shape = jax.ShapeDtypeStruct((m, n), out_dtype)
        out_spec = pl.BlockSpec((tm, tn), lambda i, j: (i, j))
    elif mode == "headmajor":
        out_shape = jax.ShapeDtypeStruct((n // LANES, m, LANES), out_dtype)
        out_spec = pl.BlockSpec((tn // LANES, tm, LANES), lambda i, j: (j, i, 0))
    else:
        out_shape = jax.ShapeDtypeStruct((n // LANES, m // ATT_TILE, LANES, ATT_TILE), out_dtype)
        out_spec = pl.BlockSpec((tn // LANES, tm // ATT_TILE, LANES, ATT_TILE),
                                lambda i, j: (j, i, 0, 0))
    in_specs = [pl.BlockSpec((tm, d), lambda i, j: (i, 0)),
                pl.BlockSpec((1, d), lambda i, j: (0, 0)),
                pl.BlockSpec((tn, d), lambda i, j: (w_tile(j), 0)) if w_rows else
                pl.BlockSpec((d, tn), lambda i, j: (0, w_tile(j)))]
    operands = [x, g.reshape(1, d), w]
    if aux_w is not None:
        in_specs.append(pl.BlockSpec((d, LANES), lambda i, j: (0, 0)))
        operands.append(aux_w)
        out_shape = (out_shape, jax.ShapeDtypeStruct((m, LANES), F32))
        out_spec = (out_spec, pl.BlockSpec((tm, LANES), lambda i, j: (i, 0)))
    return pl.pallas_call(
        functools.partial(_rms_proj_kernel, mode=mode, act=act, scale=scale, n_scaled=n_scaled,
                          has_aux=aux_w is not None, w_rows=w_rows),
        out_shape=out_shape,
        grid=(m // tm, n // tn),
        in_specs=in_specs,
        out_specs=out_spec,
        scratch_shapes=[pltpu.VMEM((tm, d), BF16)],
        compiler_params=_params("parallel", "arbitrary"),
        name="rms_proj_" + mode,
    )(*operands)


def _forget_cumsum_kernel(f_ref, b_ref, o_ref, carry_ref):
    t = pl.program_id(1)

    @pl.when(t == 0)
    def _():
        carry_ref[...] = jnp.zeros_like(carry_ref)

    f = f_ref[...] + b_ref[...]
    tm = f.shape[0]
    logf = jnp.minimum(f, 0.0) - jnp.log1p(jnp.exp(-jnp.abs(f)))
    logf = logf * LOG2E
    row = lax.broadcasted_iota(jnp.int32, (tm, tm), 0)
    col = lax.broadcasted_iota(jnp.int32, (tm, tm), 1)
    tri = jnp.where(col <= row, 1.0, 0.0).astype(BF16)
    hi, mid, lo = _split3(logf)
    c = (_bf16_dot(tri, hi) + (_bf16_dot(tri, mid) + _bf16_dot(tri, lo))) + carry_ref[0:1, :]
    o_ref[...] = c
    carry_ref[...] = jnp.broadcast_to(c[tm - 1:tm, :], carry_ref.shape)


def forget_cumsum(f, b_f, *, batch, tm=512):
    m = f.shape[0]
    nt = (m // batch) // tm
    b_pad = jnp.zeros((1, LANES), F32).at[0, :b_f.shape[0]].set(b_f)
    return pl.pallas_call(
        _forget_cumsum_kernel,
        out_shape=jax.ShapeDtypeStruct((m, LANES), F32),
        grid=(batch, nt),
        in_specs=[pl.BlockSpec((tm, LANES), lambda b, t: (b * nt + t, 0)),
                  pl.BlockSpec((1, LANES), lambda b, t: (0, 0))],
        out_specs=pl.BlockSpec((tm, LANES), lambda b, t: (b * nt + t, 0)),
        scratch_shapes=[pltpu.VMEM((8, LANES), F32)],
        compiler_params=_params("parallel", "arbitrary"),
        name="forget_cumsum",
    )(f, b_pad)


def _moba_bias_kernel(tab_ref, o_ref):
    h = pl.program_id(0)
    key = lax.broadcasted_iota(jnp.int32, (ATT_TILE, ATT_TILE), 0)
    qry = lax.broadcasted_iota(jnp.int32, (ATT_TILE, ATT_TILE), 1)
    far = tab_ref[h, NUM_BUCKETS - 1] * LOG2E

    def block(delta):
        dist = delta * MOBA_BLOCK + qry - key
        val = jnp.full((ATT_TILE, ATT_TILE), tab_ref[h, 0], F32)
        for k in range(1, NUM_BUCKETS):
            val = jnp.where(dist >= BUCKET_THRESHOLDS[k - 1], tab_ref[h, k], val)
        val = val * LOG2E
        if delta == 0:
            val = jnp.where(dist >= 0, val, NEG_INF)
        return val

    t = [block(delta) for delta in range(NEAR_BLOCKS + 1)]
    a = ATT_TILE
    o_ref[0, 0, 0:a, 0:a] = t[0]
    o_ref[0, 0, 0:a, a:2 * a] = t[1]
    o_ref[0, 0, a:2 * a, 0:a] = jnp.full((a, a), NEG_INF, F32)
    o_ref[0, 0, a:2 * a, a:2 * a] = t[0]
    for d in (1, 2):
        o_ref[0, d, 0:a, 0:a] = t[2 * d] - far
        o_ref[0, d, 0:a, a:2 * a] = t[2 * d + 1] - far
        o_ref[0, d, a:2 * a, 0:a] = t[2 * d - 1] - far
        o_ref[0, d, a:2 * a, a:2 * a] = t[2 * d] - far


def moba_bias(rel_bias):
    h = rel_bias.shape[0]
    return pl.pallas_call(
        _moba_bias_kernel,
        out_shape=jax.ShapeDtypeStruct((h, 3, TILE, TILE), F32),
        grid=(h,),
        in_specs=[pl.BlockSpec(memory_space=pltpu.SMEM)],
        out_specs=pl.BlockSpec((1, 3, TILE, TILE), lambda i: (i, 0, 0, 0)),
        compiler_params=_params("parallel"),
        name="moba_bias",
    )(rel_bias.astype(F32))


class _Softmax:
    def __init__(self, hh, s_ref, m_ref, l_ref, acc_ref, load_values):
        self.s_ref, self.m_ref, self.l_ref, self.acc_ref = s_ref.at[hh], m_ref.at[hh], l_ref.at[hh], acc_ref.at[hh]
        self.load_values = load_values

    def reset(self):
        self.m_ref[...] = jnp.full(self.m_ref.shape, NEG_INF, F32)
        self.l_ref[...] = jnp.zeros(self.l_ref.shape, F32)
        self.acc_ref[...] = jnp.zeros(self.acc_ref.shape, F32)

    def fold(self, slot, tile, mask=None):
        s = self.s_ref[slot]
        if mask is not None:
            s = jnp.where(mask, s, NEG_INF)
        m = self.m_ref[...]
        m_new = jnp.maximum(m, jnp.max(s, axis=0, keepdims=True))
        alpha = jnp.exp2(m - m_new)
        p = jnp.exp2(s - m_new)
        self.l_ref[...] = alpha * self.l_ref[...] + jnp.sum(p, axis=0, keepdims=True)
        self.m_ref[...] = m_new
        self.acc_ref[...] = alpha * self.acc_ref[...] + jnp.dot(
            self.load_values(tile), p.astype(BF16), preferred_element_type=F32)

    def result(self):
        return self.acc_ref[...] * (1.0 / self.l_ref[...])


def _softmax_scratch(t):
    nh = HEADS_PER_STEP
    return [pltpu.VMEM((nh, 2, t, t), F32),
            pltpu.VMEM((nh, 1, t), F32),
            pltpu.VMEM((nh, 1, t), F32),
            pltpu.VMEM((nh, HEAD_DIM, t), F32)]


def _top_k_bias(gate, eligible, blk):
    lowest = float(jnp.finfo(F32).min)
    blk_f = blk.astype(F32)
    g = jnp.where(eligible, gate, NEG_INF)
    bias = jnp.full(gate.shape, NEG_INF, F32)
    for _ in range(MOBA_TOP_K):
        best = jnp.max(g, axis=0, keepdims=True)
        first = jnp.min(jnp.where(g == best, blk_f, float(gate.shape[0])), axis=0, keepdims=True)
        hit = blk_f == first
        bias = jnp.where(hit, 0.0, bias)
        g = jnp.where(hit, lowest, g)
    return jnp.where(eligible, bias, NEG_INF)


def _split3(x):
    hi = x.astype(BF16).astype(F32)
    mid = (x - hi).astype(BF16).astype(F32)
    lo = (x - hi - mid).astype(BF16).astype(F32)
    return hi, mid, lo


def _slab_cast_specs(weights, n_steps, step_index):
    in_specs, out_specs, out_shapes = [], [], []
    for w in weights:
        rows = w.shape[0] // n_steps
        assert rows * n_steps == w.shape[0] and rows % 16 == 0
        in_specs.append(pl.BlockSpec((rows, w.shape[1]), lambda *g: (step_index(*g), 0)))
        out_specs.append(pl.BlockSpec((rows, w.shape[1]), lambda *g: (step_index(*g), 0)))
        out_shapes.append(jax.ShapeDtypeStruct(w.shape, BF16))
    return in_specs, out_specs, out_shapes


def _moba_attn_kernel(*refs, n_blocks, n_cast):
    tab_ref, q_ref, k_ref, v_ref, bias_ref = refs[:5]
    o_ref = refs[5 + n_cast]
    kaug_ref, kbar_ref, kb3_ref, s_ref, m_ref, l_ref, acc_ref = refs[6 + 2 * n_cast:]
    for src, dst in zip(refs[5:5 + n_cast], refs[6 + n_cast:6 + 2 * n_cast]):
        dst[...] = src[...].astype(dst.dtype)
    _moba_attn_body(tab_ref, q_ref, k_ref, v_ref, bias_ref, o_ref,
                    kaug_ref, kbar_ref, kb3_ref, s_ref, m_ref, l_ref, acc_ref, n_blocks=n_blocks)


def _moba_attn_body(tab_ref, q_ref, k_ref, v_ref, bias_ref, o_ref,
                    kaug_ref, kbar_ref, kb3_ref, s_ref, m_ref, l_ref, acc_ref, *, n_blocks):
    hp = pl.program_id(1)
    i = pl.program_id(2)
    n_tiles = n_blocks // 2
    heads = range(HEADS_PER_STEP)

    @pl.when(i == 0)
    def _():
        lane = lax.broadcasted_iota(jnp.int32, (TILE, AUG - HEAD_DIM), 1)
        row = lax.broadcasted_iota(jnp.int32, (TILE, AUG - HEAD_DIM), 0)
        ones_lane = jnp.where(lane == n_blocks, 1.0, jnp.where(lane == n_blocks + 1, 1.0, 0.0))
        for hh in heads:
            for n in range(n_blocks):
                kbar_ref[hh, n:n + 1, :] = jnp.sum(k_ref[hh, n].astype(F32), axis=0, keepdims=True)
            hi, mid, lo = _split3(kbar_ref[hh] * (1.0 / MOBA_BLOCK))
            kb3_ref[hh, 0:n_blocks, :] = hi.astype(BF16)
            kb3_ref[hh, n_blocks:2 * n_blocks, :] = mid.astype(BF16)
            kb3_ref[hh, 2 * n_blocks:3 * n_blocks, :] = lo.astype(BF16)
            for j in range(n_tiles):
                blk_of_row = jnp.where(row < ATT_TILE, 2 * j, 2 * j + 1)
                right = jnp.where(lane == blk_of_row, 1.0, ones_lane).astype(BF16)
                kaug_ref[hh, j] = jnp.concatenate(
                    [k_ref[hh, 2 * j:2 * j + 2].reshape(TILE, HEAD_DIM), right], axis=1)

    blk = lax.broadcasted_iota(jnp.int32, (n_blocks, TILE), 0)
    qlane = lax.broadcasted_iota(jnp.int32, (n_blocks, TILE), 1)
    own = 2 * i + jnp.where(qlane >= ATT_TILE, 1, 0)
    eligible = blk < own
    r16 = lax.broadcasted_iota(jnp.int32, (16, TILE), 0)

    j1 = jnp.where(i >= 1, i - 1, i + 1)
    j2 = jnp.where(i >= 2, i - 2, i + 1)
    n_far = jnp.maximum(i - 2, 0)

    def clamp(j):
        return jnp.minimum(j, n_tiles - 1)

    far_logits, fold = [], []
    for hh in heads:
        qt = jnp.concatenate([q_ref[hh, 0], q_ref[hh, 1]], axis=1)
        g3 = jnp.dot(kb3_ref[hh], qt, preferred_element_type=F32)
        gate = g3[0:n_blocks] + g3[n_blocks:2 * n_blocks] + g3[2 * n_blocks:3 * n_blocks]
        selb = _top_k_bias(gate, eligible, blk)

        far = jnp.full((16, TILE), tab_ref[hp * HEADS_PER_STEP + hh, NUM_BUCKETS - 1] * LOG2E, F32)
        far_hi = far.astype(BF16).astype(F32)
        far_rows = jnp.where(r16 == 0, far_hi, jnp.where(r16 == 1, far - far_hi, 0.0))
        q_aug = jnp.concatenate(
            [qt, selb.astype(BF16), far_rows.astype(BF16),
             jnp.zeros((AUG - HEAD_DIM - n_blocks - 16, TILE), BF16)], axis=0)

        def values(j, hh=hh):
            return jnp.concatenate([v_ref[hh, 2 * j], v_ref[hh, 2 * j + 1]], axis=1)

        def head_far_logits(j, hh=hh, q_aug=q_aug):
            return jnp.dot(kaug_ref[hh, j], q_aug, preferred_element_type=F32)

        sm = _Softmax(hh, s_ref, m_ref, l_ref, acc_ref, values)
        far_logits.append(head_far_logits)
        fold.append(sm)

        sel_own = jnp.sum(jnp.where(blk == 2 * i, selb, 0.0), axis=0, keepdims=True)
        sel_own = jnp.where(qlane[0:1] >= ATT_TILE, sel_own, 0.0)
        kd = k_ref[hh, pl.ds(2 * i, 2)].reshape(TILE, HEAD_DIM)
        sd = jnp.dot(kd, qt, preferred_element_type=F32) + bias_ref[hh, 0]
        s_ref[hh, 0, 0:ATT_TILE, :] = sd[0:ATT_TILE] + sel_own
        s_ref[hh, 0, ATT_TILE:TILE, :] = sd[ATT_TILE:TILE]
        sm.reset()

    for hh in heads:
        s_ref[hh, 1] = far_logits[hh](j1) + bias_ref[hh, 1]
        fold[hh].fold(0, i)
    for hh in heads:
        s_ref[hh, 0] = far_logits[hh](j2) + bias_ref[hh, 2]
        fold[hh].fold(1, j1)
    for hh in heads:
        s_ref[hh, 1] = far_logits[hh](0)
        fold[hh].fold(0, j2)

    def pair_body(p, carry):
        for hh in heads:
            s_ref[hh, 0] = far_logits[hh](clamp(2 * p + 1))
            fold[hh].fold(1, 2 * p)
        for hh in heads:
            s_ref[hh, 1] = far_logits[hh](clamp(2 * p + 2))
            fold[hh].fold(0, 2 * p + 1)
        return carry

    lax.fori_loop(0, n_far // 2, pair_body, 0)

    @pl.when(n_far % 2 == 1)
    def _():
        for hh in heads:
            fold[hh].fold(1, n_far - 1)

    for hh in heads:
        o_ref[:, hh * HEAD_DIM:(hh + 1) * HEAD_DIM] = fold[hh].result().T.astype(o_ref.dtype)


def moba_attn(qv_t, k, bias, rel_bias, *, batch, n_heads, q_off, k_off, v_off, cast=()):
    mb = k.shape[1]
    nb = mb // batch
    nt = nb // 2
    m = mb * ATT_TILE
    hb = HEADS_PER_STEP
    nhp = n_heads // hb
    assert nb + 16 <= AUG - HEAD_DIM and nt >= 4
    assert n_heads % hb == 0 and q_off % hb == 0 and k_off % hb == 0 and v_off % hb == 0
    c_in, c_out, c_shape = _slab_cast_specs(cast, batch * nhp * nt, lambda b, h, i: (b * nhp + h) * nt + i)
    return pl.pallas_call(
        functools.partial(_moba_attn_kernel, n_blocks=nb, n_cast=len(cast)),
        out_shape=[jax.ShapeDtypeStruct((m, n_heads * HEAD_DIM), BF16)] + c_shape,
        grid=(batch, nhp, nt),
        in_specs=[pl.BlockSpec(memory_space=pltpu.SMEM),
                  pl.BlockSpec((hb, 2, HEAD_DIM, ATT_TILE),
                               lambda b, h, i: (q_off // hb + h, b * nt + i, 0, 0)),
                  pl.BlockSpec((hb, nb, ATT_TILE, HEAD_DIM), lambda b, h, i: (k_off // hb + h, b, 0, 0)),
                  pl.BlockSpec((hb, nb, HEAD_DIM, ATT_TILE), lambda b, h, i: (v_off // hb + h, b, 0, 0)),
                  pl.BlockSpec((hb, 3, TILE, TILE), lambda b, h, i: (h, 0, 0, 0))] + c_in,
        out_specs=[pl.BlockSpec((TILE, hb * HEAD_DIM), lambda b, h, i: (b * nt + i, h))] + c_out,
        scratch_shapes=[pltpu.VMEM((hb, nt, TILE, AUG), BF16),
                        pltpu.VMEM((hb, nb, HEAD_DIM), F32),
                        pltpu.VMEM((hb, 3 * nb, HEAD_DIM), BF16)] + _softmax_scratch(TILE),
        compiler_params=_params("parallel", "parallel", "arbitrary"),
        name="moba_attn",
    )(rel_bias.astype(F32), qv_t, k, qv_t, bias, *cast)


def _fox_attn_kernel(*refs, n_tiles, n_cast):
    q_ref, k_ref, v_ref, c_ref = refs[:4]
    o_ref = refs[4 + n_cast]
    kaug_ref, bound_ref, s_ref, m_ref, l_ref, acc_ref = refs[5 + 2 * n_cast:]
    for src, dst in zip(refs[4:4 + n_cast], refs[5 + n_cast:5 + 2 * n_cast]):
        dst[...] = src[...].astype(dst.dtype)
    _fox_attn_body(q_ref, k_ref, v_ref, c_ref, o_ref, kaug_ref, bound_ref, s_ref, m_ref, l_ref,
                   acc_ref, n_tiles=n_tiles)


def _fox_attn_body(q_ref, k_ref, v_ref, c_ref, o_ref, kaug_ref, bound_ref, s_ref, m_ref, l_ref,
                   acc_ref, *, n_tiles):
    hp = pl.program_id(1)
    i = pl.program_id(2)
    t = TILE
    heads = range(HEADS_PER_STEP)

    lane1 = lax.broadcasted_iota(jnp.int32, (1, LANES), 1)

    @pl.when(i == 0)
    def _():
        lane = lax.broadcasted_iota(jnp.int32, (t, AUG - HEAD_DIM), 1)
        for hh in heads:
            c_first = jnp.zeros((1, LANES), F32)
            c_last = jnp.zeros((1, LANES), F32)
            k_norm2 = jnp.zeros((t, 1), F32)
            for j in range(n_tiles):
                c_all = c_ref[j * t:(j + 1) * t, :]
                c = jnp.sum(jnp.where(lane == hp * HEADS_PER_STEP + hh, c_all, 0.0),
                            axis=1, keepdims=True)
                hi, mid, lo = _split3(-c)
                right = jnp.where(lane == 0, hi, jnp.where(lane == 1, mid, jnp.where(lane == 2, lo, 0.0)))
                k = k_ref[hh, 2 * j:2 * j + 2].reshape(t, HEAD_DIM)
                kaug_ref[hh, j] = jnp.concatenate([k, right.astype(BF16)], axis=1)
                kf = k.astype(F32)
                k_norm2 = jnp.maximum(k_norm2, jnp.sum(kf * kf, axis=1, keepdims=True))
                c_first = jnp.where(lane1 == j, c[0:1, :], c_first)
                c_last = jnp.where(lane1 == j, c[t - 1:t, :], c_last)
            bound_ref[hh, 0:1, :] = c_first
            bound_ref[hh, 1:2, :] = c_last
            bound_ref[hh, 2:3, :] = jnp.broadcast_to(jnp.max(k_norm2, axis=0, keepdims=True), (1, LANES))

    r_aug = lax.broadcasted_iota(jnp.int32, (AUG - HEAD_DIM, t), 0)
    ones_rows = jnp.where(r_aug < 3, 1.0, 0.0).astype(BF16)

    def clamp(n):
        return jnp.minimum(n, n_tiles - 1)

    logits, fold, skippable = [], [], []
    for hh in heads:
        qt = jnp.concatenate([q_ref[hh, 0], q_ref[hh, 1]], axis=1)
        q_aug = jnp.concatenate([qt, ones_rows], axis=0)

        qf = qt.astype(F32)
        q_norm2 = jnp.max(jnp.sum(qf * qf, axis=0, keepdims=True), axis=1, keepdims=True)
        c_here = jnp.sum(jnp.where(lane1 == i, bound_ref[hh, 0:1, :], 0.0), axis=1, keepdims=True)
        gap = (bound_ref[hh, 1:2, :] - c_here) - UNDERFLOW_LOG2
        dead = (lane1 < i) & (gap > 0.0) & (gap * gap > 4.0 * q_norm2 * bound_ref[hh, 2:3, :])
        skippable.append(jnp.sum(jnp.where(dead, 1.0, 0.0), axis=1, keepdims=True))

        def head_logits(n, hh=hh, q_aug=q_aug):
            return jnp.dot(kaug_ref[hh, n], q_aug, preferred_element_type=F32)

        def values(n, hh=hh):
            return jnp.concatenate([v_ref[hh, 2 * n], v_ref[hh, 2 * n + 1]], axis=1)

        logits.append(head_logits)
        fold.append(_Softmax(hh, s_ref, m_ref, l_ref, acc_ref, values))

    key = lax.broadcasted_iota(jnp.int32, (t, t), 0)
    qry = lax.broadcasted_iota(jnp.int32, (t, t), 1)
    causal = key <= qry
    j0 = functools.reduce(jnp.minimum, skippable)[0, 0].astype(jnp.int32)
    n_past = i - j0
    for hh in heads:
        s_ref[hh, 0] = logits[hh](i)
        fold[hh].reset()
    for hh in heads:
        s_ref[hh, 1] = logits[hh](j0)
        fold[hh].fold(0, i, mask=causal)

    def pair_body(p, carry):
        for hh in heads:
            s_ref[hh, 0] = logits[hh](clamp(j0 + 2 * p + 1))
            fold[hh].fold(1, j0 + 2 * p)
        for hh in heads:
            s_ref[hh, 1] = logits[hh](clamp(j0 + 2 * p + 2))
            fold[hh].fold(0, j0 + 2 * p + 1)
        return carry

    lax.fori_loop(0, n_past // 2, pair_body, 0)

    @pl.when(n_past % 2 == 1)
    def _():
        for hh in heads:
            fold[hh].fold(1, i - 1)

    for hh in heads:
        o_ref[:, hh * HEAD_DIM:(hh + 1) * HEAD_DIM] = fold[hh].result().T.astype(o_ref.dtype)


def fox_attn(qv_t, k, c_rep, *, batch, n_heads, q_off, k_off, v_off, cast=()):
    mb = k.shape[1]
    nb = mb // batch
    nt = nb // 2
    m = mb * ATT_TILE
    t = TILE
    hb = HEADS_PER_STEP
    nhp = n_heads // hb
    assert n_heads % hb == 0 and q_off % hb == 0 and k_off % hb == 0 and v_off % hb == 0
    assert nt <= LANES and n_heads <= LANES
    c_in, c_out, c_shape = _slab_cast_specs(cast, batch * nhp * nt, lambda b, h, i: (b * nhp + h) * nt + i)
    return pl.pallas_call(
        functools.partial(_fox_attn_kernel, n_tiles=nt, n_cast=len(cast)),
        out_shape=[jax.ShapeDtypeStruct((m, n_heads * HEAD_DIM), BF16)] + c_shape,
        grid=(batch, nhp, nt),
        in_specs=[pl.BlockSpec((hb, 2, HEAD_DIM, ATT_TILE),
                               lambda b, h, i: (q_off // hb + h, b * nt + i, 0, 0)),
                  pl.BlockSpec((hb, nb, ATT_TILE, HEAD_DIM), lambda b, h, i: (k_off // hb + h, b, 0, 0)),
                  pl.BlockSpec((hb, nb, HEAD_DIM, ATT_TILE), lambda b, h, i: (v_off // hb + h, b, 0, 0)),
                  pl.BlockSpec((nb * ATT_TILE, LANES), lambda b, h, i: (b, 0))] + c_in,
        out_specs=[pl.BlockSpec((t, hb * HEAD_DIM), lambda b, h, i: (b * nt + i, h))] + c_out,
        scratch_shapes=[pltpu.VMEM((hb, nt, t, AUG), BF16),
                        pltpu.VMEM((hb, 8, LANES), F32)] + _softmax_scratch(t),
        compiler_params=_params("parallel", "parallel", "arbitrary"),
        name="fox_attn",
    )(qv_t, k, qv_t, c_rep, *cast)


def _gated_merge_kernel(oa_ref, of_ref, wa_ref, wf_ref, ga_ref, gf_ref, o_ref):
    ua = jnp.dot(oa_ref[...], wa_ref[...], preferred_element_type=F32)
    uf = jnp.dot(of_ref[...], wf_ref[...], preferred_element_type=F32)
    o_ref[...] = (ga_ref[...].astype(F32) * ua + gf_ref[...].astype(F32) * uf).astype(o_ref.dtype)


def gated_merge(o_a, o_f, w_a, w_f, gates, *, tm=ROW_TILE, tn=COL_TILE):
    m, ka = o_a.shape
    kf = o_f.shape[1]
    n = w_a.shape[1]
    nj = n // tn
    return pl.pallas_call(
        _gated_merge_kernel,
        out_shape=jax.ShapeDtypeStruct((m, n), BF16),
        grid=(m // tm, nj),
        in_specs=[pl.BlockSpec((tm, ka), lambda i, j: (i, 0)),
                  pl.BlockSpec((tm, kf), lambda i, j: (i, 0)),
                  pl.BlockSpec((ka, tn), lambda i, j: (0, j)),
                  pl.BlockSpec((kf, tn), lambda i, j: (0, j)),
                  pl.BlockSpec((tm, tn), lambda i, j: (i, j)),
                  pl.BlockSpec((tm, tn), lambda i, j: (i, nj + j))],
        out_specs=pl.BlockSpec((tm, tn), lambda i, j: (i, j)),
        compiler_params=_params("parallel", "parallel"),
        name="gated_merge",
    )(o_a, o_f, w_a, w_f, gates, gates)


def _mm_res_kernel(lhs_ref, w_ref, res_ref, g_ref, o_ref, *, final_norm):
    kk = pl.program_id(1)

    @pl.when(kk == 0)
    def _():
        o_ref[...] = res_ref[...]

    o_ref[...] += jnp.dot(lhs_ref[...], w_ref[...], preferred_element_type=F32)

    if final_norm:
        @pl.when(kk == pl.num_programs(1) - 1)
        def _():
            hres = o_ref[...]
            ms = jnp.mean(hres * hres, axis=-1, keepdims=True)
            o_ref[...] = hres * lax.rsqrt(ms + RMS_EPS) * g_ref[...]


def mm_res(lhs, w, res, g=None, *, tm=ROW_TILE, tk=1024):
    m, k = lhs.shape
    n = w.shape[1]
    tk = min(tk, k)
    final_norm = g is not None
    if g is None:
        g = jnp.ones((n,), F32)
    return pl.pallas_call(
        functools.partial(_mm_res_kernel, final_norm=final_norm),
        out_shape=jax.ShapeDtypeStruct((m, n), F32),
        grid=(m // tm, k // tk),
        in_specs=[pl.BlockSpec((tm, tk), lambda i, kk: (i, kk)),
                  pl.BlockSpec((tk, n), lambda i, kk: (kk, 0)),
                  pl.BlockSpec((tm, n), lambda i, kk: (i, 0)),
                  pl.BlockSpec((1, n), lambda i, kk: (0, 0))],
        out_specs=pl.BlockSpec((tm, n), lambda i, kk: (i, 0)),
        compiler_params=_params("parallel", "arbitrary"),
        name="mm_res",
    )(lhs, w, res, g.reshape(1, n))


def _cross_attn_kernel(h_ref, g_ref, wq_ref, kv_ref, wo_ref, o_ref, *, n_heads):
    width = n_heads * HEAD_DIM
    hres = h_ref[...]
    ms = jnp.mean(hres * hres, axis=-1, keepdims=True)
    c = (hres * lax.rsqrt(ms + RMS_EPS) * g_ref[...]).astype(BF16)
    q_all = (jnp.dot(c, wq_ref[...], preferred_element_type=F32)
             * (HEAD_DIM ** -0.5 * LOG2E)).astype(BF16)
    heads = []
    for h in range(n_heads):
        q = q_all[:, h * HEAD_DIM:(h + 1) * HEAD_DIM]
        k = kv_ref[0, :, h * HEAD_DIM:(h + 1) * HEAD_DIM]
        v = kv_ref[0, :, width + h * HEAD_DIM:width + (h + 1) * HEAD_DIM]
        s = lax.dot_general(q, k, (((1,), (1,)), ((), ())), preferred_element_type=F32)
        m = jnp.max(s, axis=-1, keepdims=True)
        p = jnp.exp2(s - m)
        l = jnp.sum(p, axis=-1, keepdims=True)
        o = jnp.dot(p.astype(BF16), v, preferred_element_type=F32) * (1.0 / l)
        heads.append(o.astype(BF16))
    o_all = jnp.concatenate(heads, axis=1)
    o_ref[...] = hres + jnp.dot(o_all, wo_ref[...], preferred_element_type=F32)


def cross_attn(h, g, w_q, kv, w_o, *, batch, n_heads, tm=512):
    m, d = h.shape
    width = w_q.shape[1]
    n_mem = kv.shape[0] // batch
    tiles_per_batch = (m // batch) // tm
    kv3 = kv.reshape(batch, n_mem, 2 * width)
    return pl.pallas_call(
        functools.partial(_cross_attn_kernel, n_heads=n_heads),
        out_shape=jax.ShapeDtypeStruct((m, d), F32),
        grid=(m // tm,),
        in_specs=[pl.BlockSpec((tm, d), lambda i: (i, 0)),
                  pl.BlockSpec((1, d), lambda i: (0, 0)),
                  pl.BlockSpec((d, width), lambda i: (0, 0)),
                  pl.BlockSpec((1, n_mem, 2 * width), lambda i: (i // tiles_per_batch, 0, 0)),
                  pl.BlockSpec((width, d), lambda i: (0, 0))],
        out_specs=pl.BlockSpec((tm, d), lambda i: (i, 0)),
        compiler_params=_params("parallel"),
        name="cross_attn",
    )(h, g.reshape(1, d), w_q, kv3, w_o)


def kernel(x, mem, g_mix, w_in, b_forget, w_branch_moba, w_branch_fox, w_mix_out, rel_bias,
           g_cross, g_mem, w_cq, w_ck, w_cv, w_co, g_mlp, w_ff1, w_ff2, g_final):
    batch, seq, d = x.shape
    depth = w_in.shape[0]
    n_heads = rel_bias.shape[0]
    n_fox = b_forget.shape[1]
    wm = n_heads * HEAD_DIM
    wf = n_fox * HEAD_DIM
    m = batch * seq
    assert wm == wf and wm % COL_TILE == 0
    scale = HEAD_DIM ** -0.5
    mem2 = mem.reshape(-1, d)

    bias = moba_bias(rel_bias)
    h = x.reshape(m, d)
    for l in range(depth):
        wi = jnp.swapaxes(w_in[l], 0, 1).astype(BF16)
        qkv_w = 3 * (wm + wf)
        w_fl = jnp.zeros((d, LANES), F32).at[:, :n_fox].set(w_in[l][:, qkv_w:qkv_w + n_fox])
        w_g = wi[qkv_w + n_fox:]
        per = wm // COL_TILE

        def part_tiles(*parts):
            return tuple(p * per + t for p in parts for t in range(per))

        qv_t = rms_proj(h, g_mix[l], wi, mode="transposed", col_tiles=part_tiles(0, 3, 2, 5),
                        w_rows=True, scale=scale * LOG2E, n_scaled=2 * per)
        k_hm, f_logit = rms_proj(h, g_mix[l], wi, mode="headmajor", col_tiles=part_tiles(1, 4),
                                 w_rows=True, aux_w=w_fl)
        k_hm = k_hm.reshape(k_hm.shape[0], m // ATT_TILE, ATT_TILE, HEAD_DIM)
        gates = rms_proj(h, g_mix[l], w_g, w_rows=True, act="sigmoid")
        c_rep = forget_cumsum(f_logit, b_forget[l], batch=batch)

        o_a, w_ff1_16, w_mix_16 = moba_attn(qv_t, k_hm, bias, rel_bias, batch=batch, n_heads=n_heads,
                                            q_off=0, k_off=0, v_off=n_heads + n_fox,
                                            cast=(w_ff1[l], w_mix_out[l]))
        o_f, w_ff2_16 = fox_attn(qv_t, k_hm, c_rep, batch=batch, n_heads=n_fox,
                                 q_off=n_heads, k_off=n_heads, v_off=2 * n_heads + n_fox,
                                 cast=(w_ff2[l],))
        merged = gated_merge(o_a, o_f, w_branch_moba[l].astype(BF16), w_branch_fox[l].astype(BF16), gates)
        h = mm_res(merged, w_mix_16, h)

        cw = w_cq.shape[2]
        w_kv = jnp.concatenate([w_ck[l], w_cv[l]], axis=1).astype(BF16)
        kv = rms_proj(mem2, g_mem[l], w_kv, tn=2 * cw)
        h = cross_attn(h, g_cross[l], w_cq[l].astype(BF16), kv, w_co[l].astype(BF16),
                       batch=batch, n_heads=cw // HEAD_DIM)

        u = rms_proj(h, g_mlp[l], w_ff1_16, act="relu2")
        h = mm_res(u, w_ff2_16, h, g_final if l == depth - 1 else None)
    return h.reshape(batch, seq, d)
```

```python
import functools
import math

import jax
import jax.numpy as jnp
from jax import lax
from jax.experimental import pallas as pl
from jax.experimental.pallas import tpu as pltpu

F32 = jnp.float32
BF16 = jnp.bfloat16

HEAD_DIM = 128
MOBA_BLOCK = 256
MOBA_TOP_K = 3
NUM_BUCKETS = 32
MAX_DISTANCE = 1024
N_CROSS_HEADS = 4
RMS_EPS = 1e-6
LOG2E = math.log2(math.e)
NEG_INF = -1e30
LANES = 128
ATT_TILE = 256
TILE = 2 * ATT_TILE
AUG = 256
HEADS_PER_STEP = 2
LOOP_UNROLL = 4
UNDERFLOW_LOG2 = 160.0
NEAR_BLOCKS = 5
VMEM_LIMIT = 56 * 1024 * 1024
ROW_TILE = 1024
COL_TILE = 1024


def _bucket_thresholds():
    max_exact = NUM_BUCKETS // 2
    thr = list(range(1, max_exact + 1))
    for k in range(max_exact + 1, NUM_BUCKETS):
        v = max_exact * (MAX_DISTANCE / max_exact) ** ((k - max_exact) / (NUM_BUCKETS - max_exact))
        n = int(math.floor(v))
        while max_exact + int(math.log(n / max_exact) / math.log(MAX_DISTANCE / max_exact)
                              * (NUM_BUCKETS - max_exact)) < k:
            n += 1
        thr.append(n)
    return tuple(thr)


BUCKET_THRESHOLDS = _bucket_thresholds()
assert (NEAR_BLOCKS - 1) * MOBA_BLOCK + 1 >= BUCKET_THRESHOLDS[-1]


def _params(*sem):
    return pltpu.CompilerParams(dimension_semantics=sem, vmem_limit_bytes=VMEM_LIMIT)


def _bf16_dot(a, b):
    return jnp.dot(a.astype(BF16), b.astype(BF16), preferred_element_type=F32)


def _rms_proj_kernel(x_ref, g_ref, w_ref, *rest, mode, act, scale, n_scaled, has_aux, w_rows):
    if has_aux:
        aux_w_ref, o_ref, aux_o_ref, a_ref = rest
    else:
        o_ref, a_ref = rest
    j = pl.program_id(1)

    @pl.when(j == 0)
    def _():
        x = x_ref[...]
        ms = jnp.mean(x * x, axis=-1, keepdims=True)
        a = x * lax.rsqrt(ms + RMS_EPS) * g_ref[...]
        a_ref[...] = a.astype(BF16)
        if has_aux:
            w_hi, w_lo, _ = _split3(aux_w_ref[...])
            both = _bf16_dot(a_ref[...], jnp.concatenate([w_hi, w_lo], axis=1))
            aux_o_ref[...] = both[:, :LANES] + both[:, LANES:]

    if w_rows:
        acc = lax.dot_general(a_ref[...], w_ref[...], (((1,), (1,)), ((), ())),
                              preferred_element_type=F32)
    else:
        acc = jnp.dot(a_ref[...], w_ref[...], preferred_element_type=F32)
    if scale is not None:
        acc = acc * jnp.where(j < n_scaled, F32(scale), F32(1.0))
    tm, tn = acc.shape
    if mode == "rowmajor":
        if act == "sigmoid":
            acc = jax.nn.sigmoid(acc)
        elif act == "relu2":
            acc = jnp.square(jnp.maximum(acc, 0.0))
        o_ref[...] = acc.astype(o_ref.dtype)
    elif mode == "headmajor":
        for c in range(tn // LANES):
            o_ref[c] = acc[:, c * LANES:(c + 1) * LANES].astype(o_ref.dtype)
    else:
        for c in range(tn // LANES):
            for r in range(tm // ATT_TILE):
                blk = acc[r * ATT_TILE:(r + 1) * ATT_TILE, c * LANES:(c + 1) * LANES]
                o_ref[c, r] = blk.T.astype(o_ref.dtype)


def rms_proj(x, g, w, *, mode="rowmajor", act=None, scale=None, n_scaled=0, col_tiles=None,
             aux_w=None, w_rows=False, out_dtype=BF16, tm=ROW_TILE, tn=COL_TILE):
    m, d = x.shape
    n_w = w.shape[0] if w_rows else w.shape[1]
    tm, tn = min(tm, m), min(tn, n_w)
    if col_tiles is None:
        col_tiles = tuple(range(n_w // tn))
    n = len(col_tiles) * tn

    def w_tile(j):
        idx = col_tiles[-1]
        for t, src in reversed(list(enumerate(col_tiles[:-1]))):
            idx = jnp.where(j == t, src, idx)
        return idx

    if mode == "rowmajor":
        out_shape = jax.ShapeDtypeStruct((m, n), out_dtype)
        out_spec = pl.BlockSpec((tm, tn), lambda i, j: (i, j))
    elif mode == "headmajor":
        out_shape = jax.ShapeDtypeStruct((n // LANES, m, LANES), out_dtype)
        out_spec = pl.BlockSpec((tn // LANES, tm, LANES), lambda i, j: (j, i, 0))
    else:
        out_shape = jax.ShapeDtypeStruct((n // LANES, m // ATT_TILE, LANES, ATT_TILE), out_dtype)
        out_spec = pl.BlockSpec((tn // LANES, tm // ATT_TILE, LANES, ATT_TILE),
                                lambda i, j: (j, i, 0, 0))
    in_specs = [pl.BlockSpec((tm, d), lambda i, j: (i, 0)),
                pl.BlockSpec((1, d), lambda i, j: (0, 0)),
                pl.BlockSpec((tn, d), lambda i, j: (w_tile(j), 0)) if w_rows else
                pl.BlockSpec((d, tn), lambda i, j: (0, w_tile(j)))]
    operands = [x, g.reshape(1, d), w]
    if aux_w is not None:
        in_specs.append(pl.BlockSpec((d, LANES), lambda i, j: (0, 0)))
        operands.append(aux_w)
        out_shape = (out_shape, jax.ShapeDtypeStruct((m, LANES), F32))
        out_spec = (out_spec, pl.BlockSpec((tm, LANES), lambda i, j: (i, 0)))
    return pl.pallas_call(
        functools.partial(_rms_proj_kernel, mode=mode, act=act, scale=scale, n_scaled=n_scaled,
                          has_aux=aux_w is not None, w_rows=w_rows),
        out_shape=out_shape,
        grid=(m // tm, n // tn),
        in_specs=in_specs,
        out_specs=out_spec,
        scratch_shapes=[pltpu.VMEM((tm, d), BF16)],
        compiler_params=_params("parallel", "arbitrary"),
        name="rms_proj_" + mode,
    )(*operands)


def _forget_cumsum_kernel(f_ref, b_ref, o_ref, carry_ref):
    t = pl.program_id(1)

    @pl.when(t == 0)
    def _():
        carry_ref[...] = jnp.zeros_like(carry_ref)

    f = f_ref[...] + b_ref[...]
    tm = f.shape[0]
    logf = jnp.minimum(f, 0.0) - jnp.log1p(jnp.exp(-jnp.abs(f)))
    logf = logf * LOG2E
    row = lax.broadcasted_iota(jnp.int32, (tm, tm), 0)
    col = lax.broadcasted_iota(jnp.int32, (tm, tm), 1)
    tri = jnp.where(col <= row, 1.0, 0.0).astype(BF16)
    hi, mid, lo = _split3(logf)
    c = (_bf16_dot(tri, hi) + (_bf16_dot(tri, mid) + _bf16_dot(tri, lo))) + carry_ref[0:1, :]
    o_ref[...] = c
    carry_ref[...] = jnp.broadcast_to(c[tm - 1:tm, :], carry_ref.shape)


def forget_cumsum(f, b_f, *, batch, tm=512):
    m = f.shape[0]
    nt = (m // batch) // tm
    b_pad = jnp.zeros((1, LANES), F32).at[0, :b_f.shape[0]].set(b_f)
    return pl.pallas_call(
        _forget_cumsum_kernel,
        out_shape=jax.ShapeDtypeStruct((m, LANES), F32),
        grid=(batch, nt),
        in_specs=[pl.BlockSpec((tm, LANES), lambda b, t: (b * nt + t, 0)),
                  pl.BlockSpec((1, LANES), lambda b, t: (0, 0))],
        out_specs=pl.BlockSpec((tm, LANES), lambda b, t: (b * nt + t, 0)),
        scratch_shapes=[pltpu.VMEM((8, LANES), F32)],
        compiler_params=_params("parallel", "arbitrary"),
        name="forget_cumsum",
    )(f, b_pad)


def _moba_bias_kernel(tab_ref, o_ref):
    h = pl.program_id(0)
    key = lax.broadcasted_iota(jnp.int32, (ATT_TILE, ATT_TILE), 0)
    qry = lax.broadcasted_iota(jnp.int32, (ATT_TILE, ATT_TILE), 1)
    far = tab_ref[h, NUM_BUCKETS - 1] * LOG2E

    def block(delta):
        dist = delta * MOBA_BLOCK + qry - key
        val = jnp.full((ATT_TILE, ATT_TILE), tab_ref[h, 0], F32)
        for k in range(1, NUM_BUCKETS):
            val = jnp.where(dist >= BUCKET_THRESHOLDS[k - 1], tab_ref[h, k], val)
        val = val * LOG2E
        if delta == 0:
            val = jnp.where(dist >= 0, val, NEG_INF)
        return val

    t = [block(delta) for delta in range(NEAR_BLOCKS + 1)]
    a = ATT_TILE
    o_ref[0, 0, 0:a, 0:a] = t[0]
    o_ref[0, 0, 0:a, a:2 * a] = t[1]
    o_ref[0, 0, a:2 * a, 0:a] = jnp.full((a, a), NEG_INF, F32)
    o_ref[0, 0, a:2 * a, a:2 * a] = t[0]
    for d in (1, 2):
        o_ref[0, d, 0:a, 0:a] = t[2 * d] - far
        o_ref[0, d, 0:a, a:2 * a] = t[2 * d + 1] - far
        o_ref[0, d, a:2 * a, 0:a] = t[2 * d - 1] - far
        o_ref[0, d, a:2 * a, a:2 * a] = t[2 * d] - far


def moba_bias(rel_bias):
    h = rel_bias.shape[0]
    return pl.pallas_call(
        _moba_bias_kernel,
        out_shape=jax.ShapeDtypeStruct((h, 3, TILE, TILE), F32),
        grid=(h,),
        in_specs=[pl.BlockSpec(memory_space=pltpu.SMEM)],
        out_specs=pl.BlockSpec((1, 3, TILE, TILE), lambda i: (i, 0, 0, 0)),
        compiler_params=_params("parallel"),
        name="moba_bias",
    )(rel_bias.astype(F32))


class _Softmax:
    def __init__(self, hh, s_ref, m_ref, l_ref, acc_ref, load_values):
        self.s_ref, self.m_ref, self.l_ref, self.acc_ref = s_ref.at[hh], m_ref.at[hh], l_ref.at[hh], acc_ref.at[hh]
        self.load_values = load_values

    def reset(self):
        self.m_ref[...] = jnp.full(self.m_ref.shape, NEG_INF, F32)
        self.l_ref[...] = jnp.zeros(self.l_ref.shape, F32)
        self.acc_ref[...] = jnp.zeros(self.acc_ref.shape, F32)

    def fold(self, slot, tile, mask=None):
        s = self.s_ref[slot]
        if mask is not None:
            s = jnp.where(mask, s, NEG_INF)
        m = self.m_ref[...]
        m_new = jnp.maximum(m, jnp.max(s, axis=0, keepdims=True))
        alpha = jnp.exp2(m - m_new)
        p = jnp.exp2(s - m_new)
        self.l_ref[...] = alpha * self.l_ref[...] + jnp.sum(p, axis=0, keepdims=True)
        self.m_ref[...] = m_new
        self.acc_ref[...] = alpha * self.acc_ref[...] + jnp.dot(
            self.load_values(tile), p.astype(BF16), preferred_element_type=F32)

    def result(self):
        return self.acc_ref[...] * (1.0 / self.l_ref[...])


def _softmax_scratch(t):
    nh = HEADS_PER_STEP
    return [pltpu.VMEM((nh, 2, t, t), F32),
            pltpu.VMEM((nh, 1, t), F32),
            pltpu.VMEM((nh, 1, t), F32),
            pltpu.VMEM((nh, HEAD_DIM, t), F32)]


def _fold_tile_run(heads, s_ref, logits, fold, base, count, last_tile):
    def step(r, slot, look_ahead=True):
        for hh in heads:
            if look_ahead:
                s_ref[hh, 1 - slot] = logits[hh](jnp.minimum(base + r + 1, last_tile))
            fold[hh].fold(slot, base + r)

    def unrolled_body(p, carry):
        for u in range(LOOP_UNROLL):
            step(LOOP_UNROLL * p + u, (u + 1) % 2)
        return carry

    lax.fori_loop(0, count // LOOP_UNROLL, unrolled_body, 0)
    rem = count % LOOP_UNROLL
    done = count - rem

    @pl.when(rem >= 2)
    def _():
        step(done, 1)
        step(done + 1, 0)

    @pl.when(rem % 2 == 1)
    def _():
        step(count - 1, 1, look_ahead=False)


def _top_k_bias(gate, eligible, blk):
    lowest = float(jnp.finfo(F32).min)
    blk_f = blk.astype(F32)
    g = jnp.where(eligible, gate, NEG_INF)
    bias = jnp.full(gate.shape, NEG_INF, F32)
    for _ in range(MOBA_TOP_K):
        best = jnp.max(g, axis=0, keepdims=True)
        first = jnp.min(jnp.where(g == best, blk_f, float(gate.shape[0])), axis=0, keepdims=True)
        hit = blk_f == first
        bias = jnp.where(hit, 0.0, bias)
        g = jnp.where(hit, lowest, g)
    return jnp.where(eligible, bias, NEG_INF)


def _split3(x):
    hi = x.astype(BF16).astype(F32)
    mid = (x - hi).astype(BF16).astype(F32)
    lo = (x - hi - mid).astype(BF16).astype(F32)
    return hi, mid, lo


def _slab_cast_specs(weights, n_steps, step_index):
    in_specs, out_specs, out_shapes = [], [], []
    for w in weights:
        rows = w.shape[0] // n_steps
        assert rows * n_steps == w.shape[0] and rows % 16 == 0
        in_specs.append(pl.BlockSpec((rows, w.shape[1]), lambda *g: (step_index(*g), 0)))
        out_specs.append(pl.BlockSpec((rows, w.shape[1]), lambda *g: (step_index(*g), 0)))
        out_shapes.append(jax.ShapeDtypeStruct(w.shape, BF16))
    return in_specs, out_specs, out_shapes


def _moba_attn_kernel(*refs, n_blocks, n_cast):
    tab_ref, q_ref, k_ref, v_ref, bias_ref = refs[:5]
    o_ref = refs[5 + n_cast]
    kaug_ref, kbar_ref, kb3_ref, s_ref, m_ref, l_ref, acc_ref = refs[6 + 2 * n_cast:]
    for src, dst in zip(refs[5:5 + n_cast], refs[6 + n_cast:6 + 2 * n_cast]):
        dst[...] = src[...].astype(dst.dtype)
    _moba_attn_body(tab_ref, q_ref, k_ref, v_ref, bias_ref, o_ref,
                    kaug_ref, kbar_ref, kb3_ref, s_ref, m_ref, l_ref, acc_ref, n_blocks=n_blocks)


def _moba_attn_body(tab_ref, q_ref, k_ref, v_ref, bias_ref, o_ref,
                    kaug_ref, kbar_ref, kb3_ref, s_ref, m_ref, l_ref, acc_ref, *, n_blocks):
    hp = pl.program_id(1)
    i = pl.program_id(2)
    n_tiles = n_blocks // 2
    heads = range(HEADS_PER_STEP)

    @pl.when(i == 0)
    def _():
        lane = lax.broadcasted_iota(jnp.int32, (TILE, AUG - HEAD_DIM), 1)
        row = lax.broadcasted_iota(jnp.int32, (TILE, AUG - HEAD_DIM), 0)
        ones_lane = jnp.where(lane == n_blocks, 1.0, jnp.where(lane == n_blocks + 1, 1.0, 0.0))
        for hh in heads:
            for n in range(n_blocks):
                kbar_ref[hh, n:n + 1, :] = jnp.sum(k_ref[hh, n].astype(F32), axis=0, keepdims=True)
            hi, mid, lo = _split3(kbar_ref[hh] * (1.0 / MOBA_BLOCK))
            kb3_ref[hh, 0:n_blocks, :] = hi.astype(BF16)
            kb3_ref[hh, n_blocks:2 * n_blocks, :] = mid.astype(BF16)
            kb3_ref[hh, 2 * n_blocks:3 * n_blocks, :] = lo.astype(BF16)
            for j in range(n_tiles):
                blk_of_row = jnp.where(row < ATT_TILE, 2 * j, 2 * j + 1)
                right = jnp.where(lane == blk_of_row, 1.0, ones_lane).astype(BF16)
                kaug_ref[hh, j] = jnp.concatenate(
                    [k_ref[hh, 2 * j:2 * j + 2].reshape(TILE, HEAD_DIM), right], axis=1)

    blk = lax.broadcasted_iota(jnp.int32, (n_blocks, TILE), 0)
    qlane = lax.broadcasted_iota(jnp.int32, (n_blocks, TILE), 1)
    own = 2 * i + jnp.where(qlane >= ATT_TILE, 1, 0)
    eligible = blk < own
    r16 = lax.broadcasted_iota(jnp.int32, (16, TILE), 0)

    j1 = jnp.where(i >= 1, i - 1, i + 1)
    j2 = jnp.where(i >= 2, i - 2, i + 1)
    n_far = jnp.maximum(i - 2, 0)

    far_logits, fold = [], []
    for hh in heads:
        qt = jnp.concatenate([q_ref[hh, 0], q_ref[hh, 1]], axis=1)
        g3 = jnp.dot(kb3_ref[hh], qt, preferred_element_type=F32)
        gate = g3[0:n_blocks] + g3[n_blocks:2 * n_blocks] + g3[2 * n_blocks:3 * n_blocks]
        selb = _top_k_bias(gate, eligible, blk)

        far = jnp.full((16, TILE), tab_ref[hp * HEADS_PER_STEP + hh, NUM_BUCKETS - 1] * LOG2E, F32)
        far_hi = far.astype(BF16).astype(F32)
        far_rows = jnp.where(r16 == 0, far_hi, jnp.where(r16 == 1, far - far_hi, 0.0))
        q_aug = jnp.concatenate(
            [qt, selb.astype(BF16), far_rows.astype(BF16),
             jnp.zeros((AUG - HEAD_DIM - n_blocks - 16, TILE), BF16)], axis=0)

        def values(j, hh=hh):
            return jnp.concatenate([v_ref[hh, 2 * j], v_ref[hh, 2 * j + 1]], axis=1)

        def head_far_logits(j, hh=hh, q_aug=q_aug):
            return jnp.dot(kaug_ref[hh, j], q_aug, preferred_element_type=F32)

        sm = _Softmax(hh, s_ref, m_ref, l_ref, acc_ref, values)
        far_logits.append(head_far_logits)
        fold.append(sm)

        sel_own = jnp.sum(jnp.where(blk == 2 * i, selb, 0.0), axis=0, keepdims=True)
        sel_own = jnp.where(qlane[0:1] >= ATT_TILE, sel_own, 0.0)
        kd = k_ref[hh, pl.ds(2 * i, 2)].reshape(TILE, HEAD_DIM)
        sd = jnp.dot(kd, qt, preferred_element_type=F32) + bias_ref[hh, 0]
        s_ref[hh, 0, 0:ATT_TILE, :] = sd[0:ATT_TILE] + sel_own
        s_ref[hh, 0, ATT_TILE:TILE, :] = sd[ATT_TILE:TILE]
        sm.reset()

    for hh in heads:
        s_ref[hh, 1] = far_logits[hh](j1) + bias_ref[hh, 1]
        fold[hh].fold(0, i)
    for hh in heads:
        s_ref[hh, 0] = far_logits[hh](j2) + bias_ref[hh, 2]
        fold[hh].fold(1, j1)
    for hh in heads:
        s_ref[hh, 1] = far_logits[hh](0)
        fold[hh].fold(0, j2)

    _fold_tile_run(heads, s_ref, far_logits, fold, 0, n_far, n_tiles - 1)

    for hh in heads:
        o_ref[:, hh * HEAD_DIM:(hh + 1) * HEAD_DIM] = fold[hh].result().T.astype(o_ref.dtype)


def moba_attn(qv_t, k, bias, rel_bias, *, batch, n_heads, q_off, k_off, v_off, cast=()):
    mb = k.shape[1]
    nb = mb // batch
    nt = nb // 2
    m = mb * ATT_TILE
    hb = HEADS_PER_STEP
    nhp = n_heads // hb
    assert nb + 16 <= AUG - HEAD_DIM and nt >= 4
    assert n_heads % hb == 0 and q_off % hb == 0 and k_off % hb == 0 and v_off % hb == 0
    c_in, c_out, c_shape = _slab_cast_specs(cast, batch * nhp * nt, lambda b, h, i: (b * nhp + h) * nt + i)
    return pl.pallas_call(
        functools.partial(_moba_attn_kernel, n_blocks=nb, n_cast=len(cast)),
        out_shape=[jax.ShapeDtypeStruct((m, n_heads * HEAD_DIM), BF16)] + c_shape,
        grid=(batch, nhp, nt),
        in_specs=[pl.BlockSpec(memory_space=pltpu.SMEM),
                  pl.BlockSpec((hb, 2, HEAD_DIM, ATT_TILE),
                               lambda b, h, i: (q_off // hb + h, b * nt + i, 0, 0)),
                  pl.BlockSpec((hb, nb, ATT_TILE, HEAD_DIM), lambda b, h, i: (k_off // hb + h, b, 0, 0)),
                  pl.BlockSpec((hb, nb, HEAD_DIM, ATT_TILE), lambda b, h, i: (v_off // hb + h, b, 0, 0)),
                  pl.BlockSpec((hb, 3, TILE, TILE), lambda b, h, i: (h, 0, 0, 0))] + c_in,
        out_specs=[pl.BlockSpec((TILE, hb * HEAD_DIM), lambda b, h, i: (b * nt + i, h))] + c_out,
        scratch_shapes=[pltpu.VMEM((hb, nt, TILE, AUG), BF16),
                        pltpu.VMEM((hb, nb, HEAD_DIM), F32),
                        pltpu.VMEM((hb, 3 * nb, HEAD_DIM), BF16)] + _softmax_scratch(TILE),
        compiler_params=_params("parallel", "parallel", "arbitrary"),
        name="moba_attn",
    )(rel_bias.astype(F32), qv_t, k, qv_t, bias, *cast)


def _fox_attn_kernel(*refs, n_tiles, n_cast):
    q_ref, k_ref, v_ref, c_ref = refs[:4]
    o_ref = refs[4 + n_cast]
    kaug_ref, bound_ref, s_ref, m_ref, l_ref, acc_ref = refs[5 + 2 * n_cast:]
    for src, dst in zip(refs[4:4 + n_cast], refs[5 + n_cast:5 + 2 * n_cast]):
        dst[...] = src[...].astype(dst.dtype)
    _fox_attn_body(q_ref, k_ref, v_ref, c_ref, o_ref, kaug_ref, bound_ref, s_ref, m_ref, l_ref,
                   acc_ref, n_tiles=n_tiles)


def _fox_attn_body(q_ref, k_ref, v_ref, c_ref, o_ref, kaug_ref, bound_ref, s_ref, m_ref, l_ref,
                   acc_ref, *, n_tiles):
    hp = pl.program_id(1)
    i = pl.program_id(2)
    t = TILE
    heads = range(HEADS_PER_STEP)

    lane1 = lax.broadcasted_iota(jnp.int32, (1, LANES), 1)

    @pl.when(i == 0)
    def _():
        lane = lax.broadcasted_iota(jnp.int32, (t, AUG - HEAD_DIM), 1)
        for hh in heads:
            c_first = jnp.zeros((1, LANES), F32)
            c_last = jnp.zeros((1, LANES), F32)
            k_norm2 = jnp.zeros((t, 1), F32)
            for j in range(n_tiles):
                c_all = c_ref[j * t:(j + 1) * t, :]
                c = jnp.sum(jnp.where(lane == hp * HEADS_PER_STEP + hh, c_all, 0.0),
                            axis=1, keepdims=True)
                hi, mid, lo = _split3(-c)
                right = jnp.where(lane == 0, hi, jnp.where(lane == 1, mid, jnp.where(lane == 2, lo, 0.0)))
                k = k_ref[hh, 2 * j:2 * j + 2].reshape(t, HEAD_DIM)
                kaug_ref[hh, j] = jnp.concatenate([k, right.astype(BF16)], axis=1)
                kf = k.astype(F32)
                k_norm2 = jnp.maximum(k_norm2, jnp.sum(kf * kf, axis=1, keepdims=True))
                c_first = jnp.where(lane1 == j, c[0:1, :], c_first)
                c_last = jnp.where(lane1 == j, c[t - 1:t, :], c_last)
            bound_ref[hh, 0:1, :] = c_first
            bound_ref[hh, 1:2, :] = c_last
            bound_ref[hh, 2:3, :] = jnp.broadcast_to(jnp.max(k_norm2, axis=0, keepdims=True), (1, LANES))

    r_aug = lax.broadcasted_iota(jnp.int32, (AUG - HEAD_DIM, t), 0)
    ones_rows = jnp.where(r_aug < 3, 1.0, 0.0).astype(BF16)

    logits, fold, skippable = [], [], []
    for hh in heads:
        qt = jnp.concatenate([q_ref[hh, 0], q_ref[hh, 1]], axis=1)
        q_aug = jnp.concatenate([qt, ones_rows], axis=0)

        qf = qt.astype(F32)
        q_norm2 = jnp.max(jnp.sum(qf * qf, axis=0, keepdims=True), axis=1, keepdims=True)
        c_here = jnp.sum(jnp.where(lane1 == i, bound_ref[hh, 0:1, :], 0.0), axis=1, keepdims=True)
        gap = (bound_ref[hh, 1:2, :] - c_here) - UNDERFLOW_LOG2
        dead = (lane1 < i) & (gap > 0.0) & (gap * gap > 4.0 * q_norm2 * bound_ref[hh, 2:3, :])
        skippable.append(jnp.sum(jnp.where(dead, 1.0, 0.0), axis=1, keepdims=True))

        def head_logits(n, hh=hh, q_aug=q_aug):
            return jnp.dot(kaug_ref[hh, n], q_aug, preferred_element_type=F32)

        def values(n, hh=hh):
            return jnp.concatenate([v_ref[hh, 2 * n], v_ref[hh, 2 * n + 1]], axis=1)

        logits.append(head_logits)
        fold.append(_Softmax(hh, s_ref, m_ref, l_ref, acc_ref, values))

    key = lax.broadcasted_iota(jnp.int32, (t, t), 0)
    qry = lax.broadcasted_iota(jnp.int32, (t, t), 1)
    causal = key <= qry
    j0 = functools.reduce(jnp.minimum, skippable)[0, 0].astype(jnp.int32)
    n_past = i - j0
    for hh in heads:
        s_ref[hh, 0] = logits[hh](i)
        fold[hh].reset()
    for hh in heads:
        s_ref[hh, 1] = logits[hh](j0)
        fold[hh].fold(0, i, mask=causal)

    _fold_tile_run(heads, s_ref, logits, fold, j0, n_past, n_tiles - 1)

    for hh in heads:
        o_ref[:, hh * HEAD_DIM:(hh + 1) * HEAD_DIM] = fold[hh].result().T.astype(o_ref.dtype)


def fox_attn(qv_t, k, c_rep, *, batch, n_heads, q_off, k_off, v_off, cast=()):
    mb = k.shape[1]
    nb = mb // batch
    nt = nb // 2
    m = mb * ATT_TILE
    t = TILE
    hb = HEADS_PER_STEP
    nhp = n_heads // hb
    assert n_heads % hb == 0 and q_off % hb == 0 and k_off % hb == 0 and v_off % hb == 0
    assert nt <= LANES and n_heads <= LANES
    c_in, c_out, c_shape = _slab_cast_specs(cast, batch * nhp * nt, lambda b, h, i: (b * nhp + h) * nt + i)
    return pl.pallas_call(
        functools.partial(_fox_attn_kernel, n_tiles=nt, n_cast=len(cast)),
        out_shape=[jax.ShapeDtypeStruct((m, n_heads * HEAD_DIM), BF16)] + c_shape,
        grid=(batch, nhp, nt),
        in_specs=[pl.BlockSpec((hb, 2, HEAD_DIM, ATT_TILE),
                               lambda b, h, i: (q_off // hb + h, b * nt + i, 0, 0)),
                  pl.BlockSpec((hb, nb, ATT_TILE, HEAD_DIM), lambda b, h, i: (k_off // hb + h, b, 0, 0)),
                  pl.BlockSpec((hb, nb, HEAD_DIM, ATT_TILE), lambda b, h, i: (v_off // hb + h, b, 0, 0)),
                  pl.BlockSpec((nb * ATT_TILE, LANES), lambda b, h, i: (b, 0))] + c_in,
        out_specs=[pl.BlockSpec((t, hb * HEAD_DIM), lambda b, h, i: (b * nt + i, h))] + c_out,
        scratch_shapes=[pltpu.VMEM((hb, nt, t, AUG), BF16),
                        pltpu.VMEM((hb, 8, LANES), F32)] + _softmax_scratch(t),
        compiler_params=_params("parallel", "parallel", "arbitrary"),
        name="fox_attn",
    )(qv_t, k, qv_t, c_rep, *cast)


def _gated_merge_kernel(oa_ref, of_ref, wa_ref, wf_ref, ga_ref, gf_ref, o_ref):
    ua = jnp.dot(oa_ref[...], wa_ref[...], preferred_element_type=F32)
    uf = jnp.dot(of_ref[...], wf_ref[...], preferred_element_type=F32)
    o_ref[...] = (ga_ref[...].astype(F32) * ua + gf_ref[...].astype(F32) * uf).astype(o_ref.dtype)


def gated_merge(o_a, o_f, w_a, w_f, gates, *, tm=ROW_TILE, tn=COL_TILE):
    m, ka = o_a.shape
    kf = o_f.shape[1]
    n = w_a.shape[1]
    nj = n // tn
    return pl.pallas_call(
        _gated_merge_kernel,
        out_shape=jax.ShapeDtypeStruct((m, n), BF16),
        grid=(m // tm, nj),
        in_specs=[pl.BlockSpec((tm, ka), lambda i, j: (i, 0)),
                  pl.BlockSpec((tm, kf), lambda i, j: (i, 0)),
                  pl.BlockSpec((ka, tn), lambda i, j: (0, j)),
                  pl.BlockSpec((kf, tn), lambda i, j: (0, j)),
                  pl.BlockSpec((tm, tn), lambda i, j: (i, j)),
                  pl.BlockSpec((tm, tn), lambda i, j: (i, nj + j))],
        out_specs=pl.BlockSpec((tm, tn), lambda i, j: (i, j)),
        compiler_params=_params("parallel", "parallel"),
        name="gated_merge",
    )(o_a, o_f, w_a, w_f, gates, gates)


def _mm_res_kernel(lhs_ref, w_ref, res_ref, g_ref, o_ref, *, final_norm):
    kk = pl.program_id(1)

    @pl.when(kk == 0)
    def _():
        o_ref[...] = res_ref[...]

    o_ref[...] += jnp.dot(lhs_ref[...], w_ref[...], preferred_element_type=F32)

    if final_norm:
        @pl.when(kk == pl.num_programs(1) - 1)
        def _():
            hres = o_ref[...]
            ms = jnp.mean(hres * hres, axis=-1, keepdims=True)
            o_ref[...] = hres * lax.rsqrt(ms + RMS_EPS) * g_ref[...]


def mm_res(lhs, w, res, g=None, *, tm=ROW_TILE, tk=1024):
    m, k = lhs.shape
    n = w.shape[1]
    tk = min(tk, k)
    final_norm = g is not None
    if g is None:
        g = jnp.ones((n,), F32)
    return pl.pallas_call(
        functools.partial(_mm_res_kernel, final_norm=final_norm),
        out_shape=jax.ShapeDtypeStruct((m, n), F32),
        grid=(m // tm, k // tk),
        in_specs=[pl.BlockSpec((tm, tk), lambda i, kk: (i, kk)),
                  pl.BlockSpec((tk, n), lambda i, kk: (kk, 0)),
                  pl.BlockSpec((tm, n), lambda i, kk: (i, 0)),
                  pl.BlockSpec((1, n), lambda i, kk: (0, 0))],
        out_specs=pl.BlockSpec((tm, n), lambda i, kk: (i, 0)),
        compiler_params=_params("parallel", "arbitrary"),
        name="mm_res",
    )(lhs, w, res, g.reshape(1, n))


def _cross_attn_kernel(h_ref, g_ref, wq_ref, kv_ref, wo_ref, o_ref, *, n_heads):
    width = n_heads * HEAD_DIM
    hres = h_ref[...]
    ms = jnp.mean(hres * hres, axis=-1, keepdims=True)
    c = (hres * lax.rsqrt(ms + RMS_EPS) * g_ref[...]).astype(BF16)
    q_all = (jnp.dot(c, wq_ref[...], preferred_element_type=F32)
             * (HEAD_DIM ** -0.5 * LOG2E)).astype(BF16)
    heads = []
    for h in range(n_heads):
        q = q_all[:, h * HEAD_DIM:(h + 1) * HEAD_DIM]
        k = kv_ref[0, :, h * HEAD_DIM:(h + 1) * HEAD_DIM]
        v = kv_ref[0, :, width + h * HEAD_DIM:width + (h + 1) * HEAD_DIM]
        s = lax.dot_general(q, k, (((1,), (1,)), ((), ())), preferred_element_type=F32)
        m = jnp.max(s, axis=-1, keepdims=True)
        p = jnp.exp2(s - m)
        l = jnp.sum(p, axis=-1, keepdims=True)
        o = jnp.dot(p.astype(BF16), v, preferred_element_type=F32) * (1.0 / l)
        heads.append(o.astype(BF16))
    o_all = jnp.concatenate(heads, axis=1)
    o_ref[...] = hres + jnp.dot(o_all, wo_ref[...], preferred_element_type=F32)


def cross_attn(h, g, w_q, kv, w_o, *, batch, n_heads, tm=512):
    m, d = h.shape
    width = w_q.shape[1]
    n_mem = kv.shape[0] // batch
    tiles_per_batch = (m // batch) // tm
    kv3 = kv.reshape(batch, n_mem, 2 * width)
    return pl.pallas_call(
        functools.partial(_cross_attn_kernel, n_heads=n_heads),
        out_shape=jax.ShapeDtypeStruct((m, d), F32),
        grid=(m // tm,),
        in_specs=[pl.BlockSpec((tm, d), lambda i: (i, 0)),
                  pl.BlockSpec((1, d), lambda i: (0, 0)),
                  pl.BlockSpec((d, width), lambda i: (0, 0)),
                  pl.BlockSpec((1, n_mem, 2 * width), lambda i: (i // tiles_per_batch, 0, 0)),
                  pl.BlockSpec((width, d), lambda i: (0, 0))],
        out_specs=pl.BlockSpec((tm, d), lambda i: (i, 0)),
        compiler_params=_params("parallel"),
        name="cross_attn",
    )(h, g.reshape(1, d), w_q, kv3, w_o)


def kernel(x, mem, g_mix, w_in, b_forget, w_branch_moba, w_branch_fox, w_mix_out, rel_bias,
           g_cross, g_mem, w_cq, w_ck, w_cv, w_co, g_mlp, w_ff1, w_ff2, g_final):
    batch, seq, d = x.shape
    depth = w_in.shape[0]
    n_heads = rel_bias.shape[0]
    n_fox = b_forget.shape[1]
    wm = n_heads * HEAD_DIM
    wf = n_fox * HEAD_DIM
    m = batch * seq
    assert wm == wf and wm % COL_TILE == 0
    scale = HEAD_DIM ** -0.5
    mem2 = mem.reshape(-1, d)

    bias = moba_bias(rel_bias)
    h = x.reshape(m, d)
    for l in range(depth):
        wi = jnp.swapaxes(w_in[l], 0, 1).astype(BF16)
        qkv_w = 3 * (wm + wf)
        w_fl = jnp.zeros((d, LANES), F32).at[:, :n_fox].set(w_in[l][:, qkv_w:qkv_w + n_fox])
        w_g = wi[qkv_w + n_fox:]
        per = wm // COL_TILE

        def part_tiles(*parts):
            return tuple(p * per + t for p in parts for t in range(per))

        qv_t = rms_proj(h, g_mix[l], wi, mode="transposed", col_tiles=part_tiles(0, 3, 2, 5),
                        w_rows=True, scale=scale * LOG2E, n_scaled=2 * per)
        k_hm, f_logit = rms_proj(h, g_mix[l], wi, mode="headmajor", col_tiles=part_tiles(1, 4),
                                 w_rows=True, aux_w=w_fl)
        k_hm = k_hm.reshape(k_hm.shape[0], m // ATT_TILE, ATT_TILE, HEAD_DIM)
        gates = rms_proj(h, g_mix[l], w_g, w_rows=True, act="sigmoid")
        c_rep = forget_cumsum(f_logit, b_forget[l], batch=batch)

        o_a, w_ff1_16, w_mix_16 = moba_attn(qv_t, k_hm, bias, rel_bias, batch=batch, n_heads=n_heads,
                                            q_off=0, k_off=0, v_off=n_heads + n_fox,
                                            cast=(w_ff1[l], w_mix_out[l]))
        o_f, w_ff2_16 = fox_attn(qv_t, k_hm, c_rep, batch=batch, n_heads=n_fox,
                                 q_off=n_heads, k_off=n_heads, v_off=2 * n_heads + n_fox,
                                 cast=(w_ff2[l],))
        merged = gated_merge(o_a, o_f, w_branch_moba[l].astype(BF16), w_branch_fox[l].astype(BF16), gates)
        h = mm_res(merged, w_mix_16, h)

        cw = w_cq.shape[2]
        w_kv = jnp.concatenate([w_ck[l], w_cv[l]], axis=1).astype(BF16)
        kv = rms_proj(mem2, g_mem[l], w_kv, tn=2 * cw)
        h = cross_attn(h, g_cross[l], w_cq[l].astype(BF16), kv, w_co[l].astype(BF16),
                       batch=batch, n_heads=cw // HEAD_DIM)

        u = rms_proj(h, g_mlp[l], w_ff1_16, act="relu2")
        h = mm_res(u, w_ff2_16, h, g_final if l == depth - 1 else None)
    return h.reshape(batch, seq, d)
```

```python
import functools
import math

import jax
import jax.numpy as jnp
from jax import lax
from jax.experimental import pallas as pl
from jax.experimental.pallas import tpu as pltpu

F32 = jnp.float32
BF16 = jnp.bfloat16

HEAD_DIM = 128
MOBA_BLOCK = 256
MOBA_TOP_K = 3
NUM_BUCKETS = 32
MAX_DISTANCE = 1024
N_CROSS_HEADS = 4
RMS_EPS = 1e-6
LOG2E = math.log2(math.e)
NEG_INF = -1e30
LANES = 128
ATT_TILE = 256
TILE = 2 * ATT_TILE
AUG = 256
HEADS_PER_STEP = 2
LOOP_UNROLL = 4
UNDERFLOW_LOG2 = 160.0
NEAR_BLOCKS = 5
VMEM_LIMIT = 56 * 1024 * 1024
ROW_TILE = 1024
COL_TILE = 1024


def _bucket_thresholds():
    max_exact = NUM_BUCKETS // 2
    thr = list(range(1, max_exact + 1))
    for k in range(max_exact + 1, NUM_BUCKETS):
        v = max_exact * (MAX_DISTANCE / max_exact) ** ((k - max_exact) / (NUM_BUCKETS - max_exact))
        n = int(math.floor(v))
        while max_exact + int(math.log(n / max_exact) / math.log(MAX_DISTANCE / max_exact)
                              * (NUM_BUCKETS - max_exact)) < k:
            n += 1
        thr.append(n)
    return tuple(thr)


BUCKET_THRESHOLDS = _bucket_thresholds()
assert (NEAR_BLOCKS - 1) * MOBA_BLOCK + 1 >= BUCKET_THRESHOLDS[-1]


def _params(*sem):
    return pltpu.CompilerParams(dimension_semantics=sem, vmem_limit_bytes=VMEM_LIMIT)


def _bf16_dot(a, b):
    return jnp.dot(a.astype(BF16), b.astype(BF16), preferred_element_type=F32)


def _rms_proj_kernel(x_ref, g_ref, w_ref, o_ref, a_ref, *, act, w_rows):
    j = pl.program_id(1)

    @pl.when(j == 0)
    def _():
        x = x_ref[...]
        ms = jnp.mean(x * x, axis=-1, keepdims=True)
        a_ref[...] = (x * lax.rsqrt(ms + RMS_EPS) * g_ref[...]).astype(BF16)

    if w_rows:
        acc = lax.dot_general(a_ref[...], w_ref[...], (((1,), (1,)), ((), ())),
                              preferred_element_type=F32)
    else:
        acc = jnp.dot(a_ref[...], w_ref[...], preferred_element_type=F32)
    if act == "relu2":
        acc = jnp.square(jnp.maximum(acc, 0.0))
    elif act == "sigmoid":
        acc = jax.nn.sigmoid(acc)
    o_ref[...] = acc.astype(o_ref.dtype)


def rms_proj(x, g, w, *, act=None, w_rows=False, tm=ROW_TILE, tn=COL_TILE):
    m, d = x.shape
    n = w.shape[0] if w_rows else w.shape[1]
    tm, tn = min(tm, m), min(tn, n)
    return pl.pallas_call(
        functools.partial(_rms_proj_kernel, act=act, w_rows=w_rows),
        out_shape=jax.ShapeDtypeStruct((m, n), BF16),
        grid=(m // tm, n // tn),
        in_specs=[pl.BlockSpec((tm, d), lambda i, j: (i, 0)),
                  pl.BlockSpec((1, d), lambda i, j: (0, 0)),
                  pl.BlockSpec((tn, d), lambda i, j: (j, 0)) if w_rows else
                  pl.BlockSpec((d, tn), lambda i, j: (0, j))],
        out_specs=pl.BlockSpec((tm, tn), lambda i, j: (i, j)),
        scratch_shapes=[pltpu.VMEM((tm, d), BF16)],
        compiler_params=_params("parallel", "arbitrary"),
        name="rms_proj",
    )(x, g.reshape(1, d), w)


def _in_proj_kernel(x_ref, g_ref, w_ref, wf_ref, qv_ref, k_ref, f_ref, a_ref, *, n_q, n_qv, scale):
    j = pl.program_id(1)

    @pl.when(j == 0)
    def _():
        x = x_ref[...]
        ms = jnp.mean(x * x, axis=-1, keepdims=True)
        a_ref[...] = (x * lax.rsqrt(ms + RMS_EPS) * g_ref[...]).astype(BF16)
        w_hi, w_lo, _ = _split3(wf_ref[...])
        both = _bf16_dot(a_ref[...], jnp.concatenate([w_hi, w_lo], axis=1))
        f_ref[...] = both[:, :LANES] + both[:, LANES:]

    def project():
        return lax.dot_general(a_ref[...], w_ref[...], (((1,), (1,)), ((), ())),
                               preferred_element_type=F32)

    @pl.when(j < n_qv)
    def _():
        acc = project() * jnp.where(j < n_q, F32(scale), F32(1.0))
        tm, tn = acc.shape
        for c in range(tn // LANES):
            for r in range(tm // ATT_TILE):
                blk = acc[r * ATT_TILE:(r + 1) * ATT_TILE, c * LANES:(c + 1) * LANES]
                qv_ref[c, r] = blk.T.astype(qv_ref.dtype)

    @pl.when(j >= n_qv)
    def _():
        acc = project()
        for c in range(acc.shape[1] // LANES):
            k_ref[c] = acc[:, c * LANES:(c + 1) * LANES].astype(k_ref.dtype)


def in_proj(x, g, w_rows, w_f, *, parts_w, scale, tm=ROW_TILE, tn=COL_TILE):
    m, d = x.shape
    per = parts_w // tn

    def tiles(*parts):
        return [p * per + t for p in parts for t in range(per)]

    order = tiles(0, 3, 2, 5) + tiles(1, 4)
    n_q, n_qv, n_k = 2 * per, 4 * per, 2 * per

    def w_tile(j):
        idx = order[-1]
        for t, src in reversed(list(enumerate(order[:-1]))):
            idx = jnp.where(j == t, src, idx)
        return idx

    hb = tn // LANES
    return pl.pallas_call(
        functools.partial(_in_proj_kernel, n_q=n_q, n_qv=n_qv, scale=scale),
        out_shape=(jax.ShapeDtypeStruct((n_qv * hb, m // ATT_TILE, LANES, ATT_TILE), BF16),
                   jax.ShapeDtypeStruct((n_k * hb, m, LANES), BF16),
                   jax.ShapeDtypeStruct((m, LANES), F32)),
        grid=(m // tm, len(order)),
        in_specs=[pl.BlockSpec((tm, d), lambda i, j: (i, 0)),
                  pl.BlockSpec((1, d), lambda i, j: (0, 0)),
                  pl.BlockSpec((tn, d), lambda i, j: (w_tile(j), 0)),
                  pl.BlockSpec((d, LANES), lambda i, j: (0, 0))],
        out_specs=(pl.BlockSpec((hb, tm // ATT_TILE, LANES, ATT_TILE),
                                lambda i, j: (jnp.minimum(j, n_qv - 1), i, 0, 0)),
                   pl.BlockSpec((hb, tm, LANES),
                                lambda i, j: (jnp.clip(j - n_qv, 0, n_k - 1), i, 0)),
                   pl.BlockSpec((tm, LANES), lambda i, j: (i, 0))),
        scratch_shapes=[pltpu.VMEM((tm, d), BF16)],
        compiler_params=_params("parallel", "arbitrary"),
        name="in_proj",
    )(x, g.reshape(1, d), w_rows, w_f)


def _forget_cumsum_kernel(f_ref, b_ref, o_ref, carry_ref):
    t = pl.program_id(1)

    @pl.when(t == 0)
    def _():
        carry_ref[...] = jnp.zeros_like(carry_ref)

    f = f_ref[...] + b_ref[...]
    tm = f.shape[0]
    logf = jnp.minimum(f, 0.0) - jnp.log1p(jnp.exp(-jnp.abs(f)))
    logf = logf * LOG2E
    row = lax.broadcasted_iota(jnp.int32, (tm, tm), 0)
    col = lax.broadcasted_iota(jnp.int32, (tm, tm), 1)
    tri = jnp.where(col <= row, 1.0, 0.0).astype(BF16)
    hi, mid, lo = _split3(logf)
    c = (_bf16_dot(tri, hi) + (_bf16_dot(tri, mid) + _bf16_dot(tri, lo))) + carry_ref[0:1, :]
    o_ref[...] = c
    carry_ref[...] = jnp.broadcast_to(c[tm - 1:tm, :], carry_ref.shape)


def forget_cumsum(f, b_f, *, batch, tm=512):
    m = f.shape[0]
    nt = (m // batch) // tm
    b_pad = jnp.zeros((1, LANES), F32).at[0, :b_f.shape[0]].set(b_f)
    return pl.pallas_call(
        _forget_cumsum_kernel,
        out_shape=jax.ShapeDtypeStruct((m, LANES), F32),
        grid=(batch, nt),
        in_specs=[pl.BlockSpec((tm, LANES), lambda b, t: (b * nt + t, 0)),
                  pl.BlockSpec((1, LANES), lambda b, t: (0, 0))],
        out_specs=pl.BlockSpec((tm, LANES), lambda b, t: (b * nt + t, 0)),
        scratch_shapes=[pltpu.VMEM((8, LANES), F32)],
        compiler_params=_params("parallel", "arbitrary"),
        name="forget_cumsum",
    )(f, b_pad)


def _moba_bias_kernel(tab_ref, o_ref):
    h = pl.program_id(0)
    key = lax.broadcasted_iota(jnp.int32, (ATT_TILE, ATT_TILE), 0)
    qry = lax.broadcasted_iota(jnp.int32, (ATT_TILE, ATT_TILE), 1)
    far = tab_ref[h, NUM_BUCKETS - 1] * LOG2E

    def block(delta):
        dist = delta * MOBA_BLOCK + qry - key
        val = jnp.full((ATT_TILE, ATT_TILE), tab_ref[h, 0], F32)
        for k in range(1, NUM_BUCKETS):
            val = jnp.where(dist >= BUCKET_THRESHOLDS[k - 1], tab_ref[h, k], val)
        val = val * LOG2E
        if delta == 0:
            val = jnp.where(dist >= 0, val, NEG_INF)
        return val

    t = [block(delta) for delta in range(NEAR_BLOCKS + 1)]
    a = ATT_TILE
    o_ref[0, 0, 0:a, 0:a] = t[0]
    o_ref[0, 0, 0:a, a:2 * a] = t[1]
    o_ref[0, 0, a:2 * a, 0:a] = jnp.full((a, a), NEG_INF, F32)
    o_ref[0, 0, a:2 * a, a:2 * a] = t[0]
    for d in (1, 2):
        o_ref[0, d, 0:a, 0:a] = t[2 * d] - far
        o_ref[0, d, 0:a, a:2 * a] = t[2 * d + 1] - far
        o_ref[0, d, a:2 * a, 0:a] = t[2 * d - 1] - far
        o_ref[0, d, a:2 * a, a:2 * a] = t[2 * d] - far


def moba_bias(rel_bias):
    h = rel_bias.shape[0]
    return pl.pallas_call(
        _moba_bias_kernel,
        out_shape=jax.ShapeDtypeStruct((h, 3, TILE, TILE), F32),
        grid=(h,),
        in_specs=[pl.BlockSpec(memory_space=pltpu.SMEM)],
        out_specs=pl.BlockSpec((1, 3, TILE, TILE), lambda i: (i, 0, 0, 0)),
        compiler_params=_params("parallel"),
        name="moba_bias",
    )(rel_bias.astype(F32))


class _Softmax:
    def __init__(self, hh, s_ref, m_ref, l_ref, acc_ref, load_values):
        self.s_ref, self.m_ref, self.l_ref, self.acc_ref = s_ref.at[hh], m_ref.at[hh], l_ref.at[hh], acc_ref.at[hh]
        self.load_values = load_values

    def reset(self):
        self.m_ref[...] = jnp.full(self.m_ref.shape, NEG_INF, F32)
        self.l_ref[...] = jnp.zeros(self.l_ref.shape, F32)
        self.acc_ref[...] = jnp.zeros(self.acc_ref.shape, F32)

    def fold(self, slot, tile, mask=None):
        s = self.s_ref[slot]
        if mask is not None:
            s = jnp.where(mask, s, NEG_INF)
        m = self.m_ref[...]
        m_new = jnp.maximum(m, jnp.max(s, axis=0, keepdims=True))
        alpha = jnp.exp2(m - m_new)
        p = jnp.exp2(s - m_new)
        self.l_ref[...] = alpha * self.l_ref[...] + jnp.sum(p, axis=0, keepdims=True)
        self.m_ref[...] = m_new
        self.acc_ref[...] = alpha * self.acc_ref[...] + jnp.dot(
            self.load_values(tile), p.astype(BF16), preferred_element_type=F32)

    def result(self):
        return self.acc_ref[...] * (1.0 / self.l_ref[...])


def _softmax_scratch(t):
    nh = HEADS_PER_STEP
    return [pltpu.VMEM((nh, 2, t, t), F32),
            pltpu.VMEM((nh, 1, t), F32),
            pltpu.VMEM((nh, 1, t), F32),
            pltpu.VMEM((nh, HEAD_DIM, t), F32)]


def _fold_tile_run(heads, s_ref, logits, fold, base, count, last_tile):
    def step(r, slot, look_ahead=True):
        for hh in heads:
            if look_ahead:
                s_ref[hh, 1 - slot] = logits[hh](jnp.minimum(base + r + 1, last_tile))
            fold[hh].fold(slot, base + r)

    def unrolled_body(p, carry):
        for u in range(LOOP_UNROLL):
            step(LOOP_UNROLL * p + u, (u + 1) % 2)
        return carry

    lax.fori_loop(0, count // LOOP_UNROLL, unrolled_body, 0)
    rem = count % LOOP_UNROLL
    done = count - rem

    @pl.when(rem >= 2)
    def _():
        step(done, 1)
        step(done + 1, 0)

    @pl.when(rem % 2 == 1)
    def _():
        step(count - 1, 1, look_ahead=False)


def _top_k_bias(gate, eligible, blk):
    lowest = float(jnp.finfo(F32).min)
    blk_f = blk.astype(F32)
    g = jnp.where(eligible, gate, NEG_INF)
    bias = jnp.full(gate.shape, NEG_INF, F32)
    for _ in range(MOBA_TOP_K):
        best = jnp.max(g, axis=0, keepdims=True)
        first = jnp.min(jnp.where(g == best, blk_f, float(gate.shape[0])), axis=0, keepdims=True)
        hit = blk_f == first
        bias = jnp.where(hit, 0.0, bias)
        g = jnp.where(hit, lowest, g)
    return jnp.where(eligible, bias, NEG_INF)


def _split3(x):
    hi = x.astype(BF16).astype(F32)
    mid = (x - hi).astype(BF16).astype(F32)
    lo = (x - hi - mid).astype(BF16).astype(F32)
    return hi, mid, lo


def _slab_cast_specs(weights, n_steps, step_index):
    in_specs, out_specs, out_shapes = [], [], []
    for w in weights:
        rows = w.shape[0] // n_steps
        assert rows * n_steps == w.shape[0] and rows % 16 == 0
        in_specs.append(pl.BlockSpec((rows, w.shape[1]), lambda *g: (step_index(*g), 0)))
        out_specs.append(pl.BlockSpec((rows, w.shape[1]), lambda *g: (step_index(*g), 0)))
        out_shapes.append(jax.ShapeDtypeStruct(w.shape, BF16))
    return in_specs, out_specs, out_shapes


def _moba_attn_kernel(*refs, n_blocks, n_cast):
    tab_ref, q_ref, k_ref, v_ref, bias_ref = refs[:5]
    o_ref = refs[5 + n_cast]
    kaug_ref, kbar_ref, kb3_ref, s_ref, m_ref, l_ref, acc_ref = refs[6 + 2 * n_cast:]
    for src, dst in zip(refs[5:5 + n_cast], refs[6 + n_cast:6 + 2 * n_cast]):
        dst[...] = src[...].astype(dst.dtype)
    _moba_attn_body(tab_ref, q_ref, k_ref, v_ref, bias_ref, o_ref,
                    kaug_ref, kbar_ref, kb3_ref, s_ref, m_ref, l_ref, acc_ref, n_blocks=n_blocks)


def _moba_attn_body(tab_ref, q_ref, k_ref, v_ref, bias_ref, o_ref,
                    kaug_ref, kbar_ref, kb3_ref, s_ref, m_ref, l_ref, acc_ref, *, n_blocks):
    hp = pl.program_id(1)
    i = pl.program_id(2)
    n_tiles = n_blocks // 2
    heads = range(HEADS_PER_STEP)

    @pl.when(i == 0)
    def _():
        lane = lax.broadcasted_iota(jnp.int32, (TILE, AUG - HEAD_DIM), 1)
        row = lax.broadcasted_iota(jnp.int32, (TILE, AUG - HEAD_DIM), 0)
        ones_lane = jnp.where(lane == n_blocks, 1.0, jnp.where(lane == n_blocks + 1, 1.0, 0.0))
        for hh in heads:
            for n in range(n_blocks):
                kbar_ref[hh, n:n + 1, :] = jnp.sum(k_ref[hh, n].astype(F32), axis=0, keepdims=True)
            hi, mid, lo = _split3(kbar_ref[hh] * (1.0 / MOBA_BLOCK))
            kb3_ref[hh, 0:n_blocks, :] = hi.astype(BF16)
            kb3_ref[hh, n_blocks:2 * n_blocks, :] = mid.astype(BF16)
            kb3_ref[hh, 2 * n_blocks:3 * n_blocks, :] = lo.astype(BF16)
            for j in range(n_tiles):
                blk_of_row = jnp.where(row < ATT_TILE, 2 * j, 2 * j + 1)
                right = jnp.where(lane == blk_of_row, 1.0, ones_lane).astype(BF16)
                kaug_ref[hh, j] = jnp.concatenate(
                    [k_ref[hh, 2 * j:2 * j + 2].reshape(TILE, HEAD_DIM), right], axis=1)

    blk = lax.broadcasted_iota(jnp.int32, (n_blocks, TILE), 0)
    qlane = lax.broadcasted_iota(jnp.int32, (n_blocks, TILE), 1)
    own = 2 * i + jnp.where(qlane >= ATT_TILE, 1, 0)
    eligible = blk < own
    r16 = lax.broadcasted_iota(jnp.int32, (16, TILE), 0)

    j1 = jnp.where(i >= 1, i - 1, i + 1)
    j2 = jnp.where(i >= 2, i - 2, i + 1)
    n_far = jnp.maximum(i - 2, 0)

    far_logits, fold = [], []
    for hh in heads:
        qt = jnp.concatenate([q_ref[hh, 0], q_ref[hh, 1]], axis=1)
        g3 = jnp.dot(kb3_ref[hh], qt, preferred_element_type=F32)
        gate = g3[0:n_blocks] + g3[n_blocks:2 * n_blocks] + g3[2 * n_blocks:3 * n_blocks]
        selb = _top_k_bias(gate, eligible, blk)

        far = jnp.full((16, TILE), tab_ref[hp * HEADS_PER_STEP + hh, NUM_BUCKETS - 1] * LOG2E, F32)
        far_hi = far.astype(BF16).astype(F32)
        far_rows = jnp.where(r16 == 0, far_hi, jnp.where(r16 == 1, far - far_hi, 0.0))
        q_aug = jnp.concatenate(
            [qt, selb.astype(BF16), far_rows.astype(BF16),
             jnp.zeros((AUG - HEAD_DIM - n_blocks - 16, TILE), BF16)], axis=0)

        def values(j, hh=hh):
            return jnp.concatenate([v_ref[hh, 2 * j], v_ref[hh, 2 * j + 1]], axis=1)

        def head_far_logits(j, hh=hh, q_aug=q_aug):
            return jnp.dot(kaug_ref[hh, j], q_aug, preferred_element_type=F32)

        sm = _Softmax(hh, s_ref, m_ref, l_ref, acc_ref, values)
        far_logits.append(head_far_logits)
        fold.append(sm)

        sel_own = jnp.sum(jnp.where(blk == 2 * i, selb, 0.0), axis=0, keepdims=True)
        sel_own = jnp.where(qlane[0:1] >= ATT_TILE, sel_own, 0.0)
        kd = k_ref[hh, pl.ds(2 * i, 2)].reshape(TILE, HEAD_DIM)
        sd = jnp.dot(kd, qt, preferred_element_type=F32) + bias_ref[hh, 0]
        s_ref[hh, 0, 0:ATT_TILE, :] = sd[0:ATT_TILE] + sel_own
        s_ref[hh, 0, ATT_TILE:TILE, :] = sd[ATT_TILE:TILE]
        sm.reset()

    for hh in heads:
        s_ref[hh, 1] = far_logits[hh](j1) + bias_ref[hh, 1]
        fold[hh].fold(0, i)
    for hh in heads:
        s_ref[hh, 0] = far_logits[hh](j2) + bias_ref[hh, 2]
        fold[hh].fold(1, j1)
    for hh in heads:
        s_ref[hh, 1] = far_logits[hh](0)
        fold[hh].fold(0, j2)

    _fold_tile_run(heads, s_ref, far_logits, fold, 0, n_far, n_tiles - 1)

    for hh in heads:
        o_ref[:, hh * HEAD_DIM:(hh + 1) * HEAD_DIM] = fold[hh].result().T.astype(o_ref.dtype)


def moba_attn(qv_t, k, bias, rel_bias, *, batch, n_heads, q_off, k_off, v_off, cast=()):
    mb = k.shape[1]
    nb = mb // batch
    nt = nb // 2
    m = mb * ATT_TILE
    hb = HEADS_PER_STEP
    nhp = n_heads // hb
    assert nb + 16 <= AUG - HEAD_DIM and nt >= 4
    assert n_heads % hb == 0 and q_off % hb == 0 and k_off % hb == 0 and v_off % hb == 0
    c_in, c_out, c_shape = _slab_cast_specs(cast, batch * nhp * nt, lambda b, h, i: (b * nhp + h) * nt + i)
    return pl.pallas_call(
        functools.partial(_moba_attn_kernel, n_blocks=nb, n_cast=len(cast)),
        out_shape=[jax.ShapeDtypeStruct((m, n_heads * HEAD_DIM), BF16)] + c_shape,
        grid=(batch, nhp, nt),
        in_specs=[pl.BlockSpec(memory_space=pltpu.SMEM),
                  pl.BlockSpec((hb, 2, HEAD_DIM, ATT_TILE),
                               lambda b, h, i: (q_off // hb + h, b * nt + i, 0, 0)),
                  pl.BlockSpec((hb, nb, ATT_TILE, HEAD_DIM), lambda b, h, i: (k_off // hb + h, b, 0, 0)),
                  pl.BlockSpec((hb, nb, HEAD_DIM, ATT_TILE), lambda b, h, i: (v_off // hb + h, b, 0, 0)),
                  pl.BlockSpec((hb, 3, TILE, TILE), lambda b, h, i: (h, 0, 0, 0))] + c_in,
        out_specs=[pl.BlockSpec((TILE, hb * HEAD_DIM), lambda b, h, i: (b * nt + i, h))] + c_out,
        scratch_shapes=[pltpu.VMEM((hb, nt, TILE, AUG), BF16),
                        pltpu.VMEM((hb, nb, HEAD_DIM), F32),
                        pltpu.VMEM((hb, 3 * nb, HEAD_DIM), BF16)] + _softmax_scratch(TILE),
        compiler_params=_params("parallel", "parallel", "arbitrary"),
        name="moba_attn",
    )(rel_bias.astype(F32), qv_t, k, qv_t, bias, *cast)


def _fox_attn_kernel(*refs, n_tiles, n_cast):
    q_ref, k_ref, v_ref, c_ref = refs[:4]
    o_ref = refs[4 + n_cast]
    kaug_ref, bound_ref, s_ref, m_ref, l_ref, acc_ref = refs[5 + 2 * n_cast:]
    for src, dst in zip(refs[4:4 + n_cast], refs[5 + n_cast:5 + 2 * n_cast]):
        dst[...] = src[...].astype(dst.dtype)
    _fox_attn_body(q_ref, k_ref, v_ref, c_ref, o_ref, kaug_ref, bound_ref, s_ref, m_ref, l_ref,
                   acc_ref, n_tiles=n_tiles)


def _fox_attn_body(q_ref, k_ref, v_ref, c_ref, o_ref, kaug_ref, bound_ref, s_ref, m_ref, l_ref,
                   acc_ref, *, n_tiles):
    hp = pl.program_id(1)
    i = pl.program_id(2)
    t = TILE
    heads = range(HEADS_PER_STEP)

    lane1 = lax.broadcasted_iota(jnp.int32, (1, LANES), 1)

    @pl.when(i == 0)
    def _():
        lane = lax.broadcasted_iota(jnp.int32, (t, AUG - HEAD_DIM), 1)
        for hh in heads:
            c_first = jnp.zeros((1, LANES), F32)
            c_last = jnp.zeros((1, LANES), F32)
            k_norm2 = jnp.zeros((t, 1), F32)
            for j in range(n_tiles):
                c_all = c_ref[j * t:(j + 1) * t, :]
                c = jnp.sum(jnp.where(lane == hp * HEADS_PER_STEP + hh, c_all, 0.0),
                            axis=1, keepdims=True)
                hi, mid, lo = _split3(-c)
                right = jnp.where(lane == 0, hi, jnp.where(lane == 1, mid, jnp.where(lane == 2, lo, 0.0)))
                k = k_ref[hh, 2 * j:2 * j + 2].reshape(t, HEAD_DIM)
                kaug_ref[hh, j] = jnp.concatenate([k, right.astype(BF16)], axis=1)
                kf = k.astype(F32)
                k_norm2 = jnp.maximum(k_norm2, jnp.sum(kf * kf, axis=1, keepdims=True))
                c_first = jnp.where(lane1 == j, c[0:1, :], c_first)
                c_last = jnp.where(lane1 == j, c[t - 1:t, :], c_last)
            bound_ref[hh, 0:1, :] = c_first
            bound_ref[hh, 1:2, :] = c_last
            bound_ref[hh, 2:3, :] = jnp.broadcast_to(jnp.max(k_norm2, axis=0, keepdims=True), (1, LANES))

    r_aug = lax.broadcasted_iota(jnp.int32, (AUG - HEAD_DIM, t), 0)
    ones_rows = jnp.where(r_aug < 3, 1.0, 0.0).astype(BF16)

    logits, fold, skippable = [], [], []
    for hh in heads:
        qt = jnp.concatenate([q_ref[hh, 0], q_ref[hh, 1]], axis=1)
        q_aug = jnp.concatenate([qt, ones_rows], axis=0)

        qf = qt.astype(F32)
        q_norm2 = jnp.max(jnp.sum(qf * qf, axis=0, keepdims=True), axis=1, keepdims=True)
        c_here = jnp.sum(jnp.where(lane1 == i, bound_ref[hh, 0:1, :], 0.0), axis=1, keepdims=True)
        gap = (bound_ref[hh, 1:2, :] - c_here) - UNDERFLOW_LOG2
        dead = (lane1 < i) & (gap > 0.0) & (gap * gap > 4.0 * q_norm2 * bound_ref[hh, 2:3, :])
        skippable.append(jnp.sum(jnp.where(dead, 1.0, 0.0), axis=1, keepdims=True))

        def head_logits(n, hh=hh, q_aug=q_aug):
            return jnp.dot(kaug_ref[hh, n], q_aug, preferred_element_type=F32)

        def values(n, hh=hh):
            return jnp.concatenate([v_ref[hh, 2 * n], v_ref[hh, 2 * n + 1]], axis=1)

        logits.append(head_logits)
        fold.append(_Softmax(hh, s_ref, m_ref, l_ref, acc_ref, values))

    key = lax.broadcasted_iota(jnp.int32, (t, t), 0)
    qry = lax.broadcasted_iota(jnp.int32, (t, t), 1)
    causal = key <= qry
    j0 = functools.reduce(jnp.minimum, skippable)[0, 0].astype(jnp.int32)
    n_past = i - j0
    for hh in heads:
        s_ref[hh, 0] = logits[hh](i)
        fold[hh].reset()
    for hh in heads:
        s_ref[hh, 1] = logits[hh](j0)
        fold[hh].fold(0, i, mask=causal)

    _fold_tile_run(heads, s_ref, logits, fold, j0, n_past, n_tiles - 1)

    for hh in heads:
        o_ref[:, hh * HEAD_DIM:(hh + 1) * HEAD_DIM] = fold[hh].result().T.astype(o_ref.dtype)


def fox_attn(qv_t, k, c_rep, *, batch, n_heads, q_off, k_off, v_off, cast=()):
    mb = k.shape[1]
    nb = mb // batch
    nt = nb // 2
    m = mb * ATT_TILE
    t = TILE
    hb = HEADS_PER_STEP
    nhp = n_heads // hb
    assert n_heads % hb == 0 and q_off % hb == 0 and k_off % hb == 0 and v_off % hb == 0
    assert nt <= LANES and n_heads <= LANES
    c_in, c_out, c_shape = _slab_cast_specs(cast, batch * nhp * nt, lambda b, h, i: (b * nhp + h) * nt + i)
    return pl.pallas_call(
        functools.partial(_fox_attn_kernel, n_tiles=nt, n_cast=len(cast)),
        out_shape=[jax.ShapeDtypeStruct((m, n_heads * HEAD_DIM), BF16)] + c_shape,
        grid=(batch, nhp, nt),
        in_specs=[pl.BlockSpec((hb, 2, HEAD_DIM, ATT_TILE),
                               lambda b, h, i: (q_off // hb + h, b * nt + i, 0, 0)),
                  pl.BlockSpec((hb, nb, ATT_TILE, HEAD_DIM), lambda b, h, i: (k_off // hb + h, b, 0, 0)),
                  pl.BlockSpec((hb, nb, HEAD_DIM, ATT_TILE), lambda b, h, i: (v_off // hb + h, b, 0, 0)),
                  pl.BlockSpec((nb * ATT_TILE, LANES), lambda b, h, i: (b, 0))] + c_in,
        out_specs=[pl.BlockSpec((t, hb * HEAD_DIM), lambda b, h, i: (b * nt + i, h))] + c_out,
        scratch_shapes=[pltpu.VMEM((hb, nt, t, AUG), BF16),
                        pltpu.VMEM((hb, 8, LANES), F32)] + _softmax_scratch(t),
        compiler_params=_params("parallel", "parallel", "arbitrary"),
        name="fox_attn",
    )(qv_t, k, qv_t, c_rep, *cast)


def _gated_merge_kernel(oa_ref, of_ref, wa_ref, wf_ref, ga_ref, gf_ref, o_ref):
    ua = jnp.dot(oa_ref[...], wa_ref[...], preferred_element_type=F32)
    uf = jnp.dot(of_ref[...], wf_ref[...], preferred_element_type=F32)
    o_ref[...] = (ga_ref[...].astype(F32) * ua + gf_ref[...].astype(F32) * uf).astype(o_ref.dtype)


def gated_merge(o_a, o_f, w_a, w_f, gates, *, tm=ROW_TILE, tn=COL_TILE):
    m, ka = o_a.shape
    kf = o_f.shape[1]
    n = w_a.shape[1]
    nj = n // tn
    return pl.pallas_call(
        _gated_merge_kernel,
        out_shape=jax.ShapeDtypeStruct((m, n), BF16),
        grid=(m // tm, nj),
        in_specs=[pl.BlockSpec((tm, ka), lambda i, j: (i, 0)),
                  pl.BlockSpec((tm, kf), lambda i, j: (i, 0)),
                  pl.BlockSpec((ka, tn), lambda i, j: (0, j)),
                  pl.BlockSpec((kf, tn), lambda i, j: (0, j)),
                  pl.BlockSpec((tm, tn), lambda i, j: (i, j)),
                  pl.BlockSpec((tm, tn), lambda i, j: (i, nj + j))],
        out_specs=pl.BlockSpec((tm, tn), lambda i, j: (i, j)),
        compiler_params=_params("parallel", "parallel"),
        name="gated_merge",
    )(o_a, o_f, w_a, w_f, gates, gates)


def _mm_res_kernel(lhs_ref, w_ref, res_ref, g_ref, o_ref, *, final_norm):
    kk = pl.program_id(1)

    @pl.when(kk == 0)
    def _():
        o_ref[...] = res_ref[...]

    o_ref[...] += jnp.dot(lhs_ref[...], w_ref[...], preferred_element_type=F32)

    if final_norm:
        @pl.when(kk == pl.num_programs(1) - 1)
        def _():
            hres = o_ref[...]
            ms = jnp.mean(hres * hres, axis=-1, keepdims=True)
            o_ref[...] = hres * lax.rsqrt(ms + RMS_EPS) * g_ref[...]


def mm_res(lhs, w, res, g=None, *, tm=ROW_TILE, tk=1024):
    m, k = lhs.shape
    n = w.shape[1]
    tk = min(tk, k)
    final_norm = g is not None
    if g is None:
        g = jnp.ones((n,), F32)
    return pl.pallas_call(
        functools.partial(_mm_res_kernel, final_norm=final_norm),
        out_shape=jax.ShapeDtypeStruct((m, n), F32),
        grid=(m // tm, k // tk),
        in_specs=[pl.BlockSpec((tm, tk), lambda i, kk: (i, kk)),
                  pl.BlockSpec((tk, n), lambda i, kk: (kk, 0)),
                  pl.BlockSpec((tm, n), lambda i, kk: (i, 0)),
                  pl.BlockSpec((1, n), lambda i, kk: (0, 0))],
        out_specs=pl.BlockSpec((tm, n), lambda i, kk: (i, 0)),
        compiler_params=_params("parallel", "arbitrary"),
        name="mm_res",
    )(lhs, w, res, g.reshape(1, n))


def _cross_attn_kernel(h_ref, g_ref, wq_ref, kv_ref, wo_ref, o_ref, *, n_heads):
    width = n_heads * HEAD_DIM
    hres = h_ref[...]
    ms = jnp.mean(hres * hres, axis=-1, keepdims=True)
    c = (hres * lax.rsqrt(ms + RMS_EPS) * g_ref[...]).astype(BF16)
    q_all = (jnp.dot(c, wq_ref[...], preferred_element_type=F32)
             * (HEAD_DIM ** -0.5 * LOG2E)).astype(BF16)
    heads = []
    for h in range(n_heads):
        q = q_all[:, h * HEAD_DIM:(h + 1) * HEAD_DIM]
        k = kv_ref[0, :, h * HEAD_DIM:(h + 1) * HEAD_DIM]
        v = kv_ref[0, :, width + h * HEAD_DIM:width + (h + 1) * HEAD_DIM]
        s = lax.dot_general(q, k, (((1,), (1,)), ((), ())), preferred_element_type=F32)
        m = jnp.max(s, axis=-1, keepdims=True)
        p = jnp.exp2(s - m)
        l = jnp.sum(p, axis=-1, keepdims=True)
        o = jnp.dot(p.astype(BF16), v, preferred_element_type=F32) * (1.0 / l)
        heads.append(o.astype(BF16))
    o_all = jnp.concatenate(heads, axis=1)
    o_ref[...] = hres + jnp.dot(o_all, wo_ref[...], preferred_element_type=F32)


def cross_attn(h, g, w_q, kv, w_o, *, batch, n_heads, tm=512):
    m, d = h.shape
    width = w_q.shape[1]
    n_mem = kv.shape[0] // batch
    tiles_per_batch = (m // batch) // tm
    kv3 = kv.reshape(batch, n_mem, 2 * width)
    return pl.pallas_call(
        functools.partial(_cross_attn_kernel, n_heads=n_heads),
        out_shape=jax.ShapeDtypeStruct((m, d), F32),
        grid=(m // tm,),
        in_specs=[pl.BlockSpec((tm, d), lambda i: (i, 0)),
                  pl.BlockSpec((1, d), lambda i: (0, 0)),
                  pl.BlockSpec((d, width), lambda i: (0, 0)),
                  pl.BlockSpec((1, n_mem, 2 * width), lambda i: (i // tiles_per_batch, 0, 0)),
                  pl.BlockSpec((width, d), lambda i: (0, 0))],
        out_specs=pl.BlockSpec((tm, d), lambda i: (i, 0)),
        compiler_params=_params("parallel"),
        name="cross_attn",
    )(h, g.reshape(1, d), w_q, kv3, w_o)


def kernel(x, mem, g_mix, w_in, b_forget, w_branch_moba, w_branch_fox, w_mix_out, rel_bias,
           g_cross, g_mem, w_cq, w_ck, w_cv, w_co, g_mlp, w_ff1, w_ff2, g_final):
    batch, seq, d = x.shape
    depth = w_in.shape[0]
    n_heads = rel_bias.shape[0]
    n_fox = b_forget.shape[1]
    wm = n_heads * HEAD_DIM
    wf = n_fox * HEAD_DIM
    m = batch * seq
    assert wm == wf and wm % COL_TILE == 0
    scale = HEAD_DIM ** -0.5
    mem2 = mem.reshape(-1, d)

    bias = moba_bias(rel_bias)
    h = x.reshape(m, d)
    for l in range(depth):
        wi = jnp.swapaxes(w_in[l], 0, 1).astype(BF16)
        qkv_w = 3 * (wm + wf)
        w_fl = jnp.zeros((d, LANES), F32).at[:, :n_fox].set(w_in[l][:, qkv_w:qkv_w + n_fox])
        qv_t, k_hm, f_logit = in_proj(h, g_mix[l], wi, w_fl, parts_w=wm, scale=scale * LOG2E)
        k_hm = k_hm.reshape(k_hm.shape[0], m // ATT_TILE, ATT_TILE, HEAD_DIM)
        gates = rms_proj(h, g_mix[l], wi[qkv_w + n_fox:], w_rows=True, act="sigmoid")
        c_rep = forget_cumsum(f_logit, b_forget[l], batch=batch)

        o_a, w_ff1_16, w_mix_16 = moba_attn(qv_t, k_hm, bias, rel_bias, batch=batch, n_heads=n_heads,
                                            q_off=0, k_off=0, v_off=n_heads + n_fox,
                                            cast=(w_ff1[l], w_mix_out[l]))
        o_f, w_ff2_16 = fox_attn(qv_t, k_hm, c_rep, batch=batch, n_heads=n_fox,
                                 q_off=n_heads, k_off=n_heads, v_off=2 * n_heads + n_fox,
                                 cast=(w_ff2[l],))
        merged = gated_merge(o_a, o_f, w_branch_moba[l].astype(BF16), w_branch_fox[l].astype(BF16), gates)
        h = mm_res(merged, w_mix_16, h)

        cw = w_cq.shape[2]
        w_kv = jnp.concatenate([w_ck[l], w_cv[l]], axis=1).astype(BF16)
        kv = rms_proj(mem2, g_mem[l], w_kv, tn=2 * cw)
        h = cross_attn(h, g_cross[l], w_cq[l].astype(BF16), kv, w_co[l].astype(BF16),
                       batch=batch, n_heads=cw // HEAD_DIM)

        u = rms_proj(h, g_mlp[l], w_ff1_16, act="relu2")
        h = mm_res(u, w_ff2_16, h, g_final if l == depth - 1 else None)
    return h.reshape(batch, seq, d)
```

```python
import functools
import math

import jax
import jax.numpy as jnp
from jax import lax
from jax.experimental import pallas as pl
from jax.experimental.pallas import tpu as pltpu

F32 = jnp.float32
BF16 = jnp.bfloat16

HEAD_DIM = 128
MOBA_BLOCK = 256
MOBA_TOP_K = 3
NUM_BUCKETS = 32
MAX_DISTANCE = 1024
RMS_EPS = 1e-6
LOG2E = math.log2(math.e)
NEG_INF = -1e30
LANES = 128
ATT_TILE = 256
TILE = 2 * ATT_TILE
AUG = 256
HEADS_PER_STEP = 2
LOOP_UNROLL = 4
UNDERFLOW_LOG2 = 160.0
NEAR_BLOCKS = 5
VMEM_LIMIT = 56 * 1024 * 1024
ROW_TILE = 1024
COL_TILE = 1024


def _bucket_thresholds():
    max_exact = NUM_BUCKETS // 2
    thr = list(range(1, max_exact + 1))
    for k in range(max_exact + 1, NUM_BUCKETS):
        v = max_exact * (MAX_DISTANCE / max_exact) ** ((k - max_exact) / (NUM_BUCKETS - max_exact))
        n = int(math.floor(v))
        while max_exact + int(math.log(n / max_exact) / math.log(MAX_DISTANCE / max_exact)
                              * (NUM_BUCKETS - max_exact)) < k:
            n += 1
        thr.append(n)
    return tuple(thr)


BUCKET_THRESHOLDS = _bucket_thresholds()
assert (NEAR_BLOCKS - 1) * MOBA_BLOCK + 1 >= BUCKET_THRESHOLDS[-1]


def _params(*sem):
    return pltpu.CompilerParams(dimension_semantics=sem, vmem_limit_bytes=VMEM_LIMIT)


def _bf16_dot(a, b):
    return jnp.dot(a.astype(BF16), b.astype(BF16), preferred_element_type=F32)


def _rms_proj_kernel(x_ref, g_ref, w_ref, o_ref, a_ref, *, act):
    j = pl.program_id(1)

    @pl.when(j == 0)
    def _():
        x = x_ref[...]
        ms = jnp.mean(x * x, axis=-1, keepdims=True)
        a_ref[...] = (x * lax.rsqrt(ms + RMS_EPS) * g_ref[...]).astype(BF16)

    acc = jnp.dot(a_ref[...], w_ref[...], preferred_element_type=F32)
    if act == "relu2":
        acc = jnp.square(jnp.maximum(acc, 0.0))
    o_ref[...] = acc.astype(o_ref.dtype)


def rms_proj(x, g, w, *, act=None, tm=ROW_TILE, tn=COL_TILE):
    m, d = x.shape
    n = w.shape[1]
    tm, tn = min(tm, m), min(tn, n)
    return pl.pallas_call(
        functools.partial(_rms_proj_kernel, act=act),
        out_shape=jax.ShapeDtypeStruct((m, n), BF16),
        grid=(m // tm, n // tn),
        in_specs=[pl.BlockSpec((tm, d), lambda i, j: (i, 0)),
                  pl.BlockSpec((1, d), lambda i, j: (0, 0)),
                  pl.BlockSpec((d, tn), lambda i, j: (0, j))],
        out_specs=pl.BlockSpec((tm, tn), lambda i, j: (i, j)),
        scratch_shapes=[pltpu.VMEM((tm, d), BF16)],
        compiler_params=_params("parallel", "arbitrary"),
        name="rms_proj",
    )(x, g.reshape(1, d), w)


def _gate_proj_kernel(a_ref, w_ref, o_ref):
    acc = lax.dot_general(a_ref[...], w_ref[...], (((1,), (1,)), ((), ())),
                          preferred_element_type=F32)
    o_ref[...] = jax.nn.sigmoid(acc).astype(o_ref.dtype)


def gate_proj(a, w_rows, *, tm=ROW_TILE, tn=2 * COL_TILE):
    m, d = a.shape
    n = w_rows.shape[0]
    tn = min(tn, n)
    return pl.pallas_call(
        _gate_proj_kernel,
        out_shape=jax.ShapeDtypeStruct((m, n), BF16),
        grid=(m // tm, n // tn),
        in_specs=[pl.BlockSpec((tm, d), lambda i, j: (i, 0)),
                  pl.BlockSpec((tn, d), lambda i, j: (j, 0))],
        out_specs=pl.BlockSpec((tm, tn), lambda i, j: (i, j)),
        compiler_params=_params("parallel", "parallel"),
        name="gate_proj",
    )(a, w_rows)


def _in_proj_kernel(x_ref, g_ref, w_ref, wf_ref, qv_ref, k_ref, f_ref, a_ref, *, n_q, n_qv, scale):
    j = pl.program_id(1)

    @pl.when(j == 0)
    def _():
        x = x_ref[...]
        ms = jnp.mean(x * x, axis=-1, keepdims=True)
        a_ref[...] = (x * lax.rsqrt(ms + RMS_EPS) * g_ref[...]).astype(BF16)
        w_hi, w_lo, _ = _split3(wf_ref[...])
        both = _bf16_dot(a_ref[...], jnp.concatenate([w_hi, w_lo], axis=1))
        f_ref[...] = both[:, :LANES] + both[:, LANES:]

    def project():
        return lax.dot_general(a_ref[...], w_ref[...], (((1,), (1,)), ((), ())),
                               preferred_element_type=F32)

    @pl.when(j < n_qv)
    def _():
        acc = project() * jnp.where(j < n_q, F32(scale), F32(1.0))
        tm, tn = acc.shape
        for c in range(tn // LANES):
            for r in range(tm // ATT_TILE):
                blk = acc[r * ATT_TILE:(r + 1) * ATT_TILE, c * LANES:(c + 1) * LANES]
                qv_ref[c, r] = blk.T.astype(qv_ref.dtype)

    @pl.when(j >= n_qv)
    def _():
        acc = project()
        for c in range(acc.shape[1] // LANES):
            k_ref[c] = acc[:, c * LANES:(c + 1) * LANES].astype(k_ref.dtype)


def in_proj(x, g, w_rows, w_f, *, parts_w, scale, tm=ROW_TILE, tn=COL_TILE):
    m, d = x.shape
    per = parts_w // tn

    def tiles(*parts):
        return [p * per + t for p in parts for t in range(per)]

    order = tiles(0, 3, 2, 5) + tiles(1, 4)
    n_q, n_qv, n_k = 2 * per, 4 * per, 2 * per

    def w_tile(j):
        idx = order[-1]
        for t, src in reversed(list(enumerate(order[:-1]))):
            idx = jnp.where(j == t, src, idx)
        return idx

    hb = tn // LANES
    return pl.pallas_call(
        functools.partial(_in_proj_kernel, n_q=n_q, n_qv=n_qv, scale=scale),
        out_shape=(jax.ShapeDtypeStruct((n_qv * hb, m // ATT_TILE, LANES, ATT_TILE), BF16),
                   jax.ShapeDtypeStruct((n_k * hb, m, LANES), BF16),
                   jax.ShapeDtypeStruct((m, LANES), F32),
                   jax.ShapeDtypeStruct((m, d), BF16)),
        grid=(m // tm, len(order)),
        in_specs=[pl.BlockSpec((tm, d), lambda i, j: (i, 0)),
                  pl.BlockSpec((1, d), lambda i, j: (0, 0)),
                  pl.BlockSpec((tn, d), lambda i, j: (w_tile(j), 0)),
                  pl.BlockSpec((d, LANES), lambda i, j: (0, 0))],
        out_specs=(pl.BlockSpec((hb, tm // ATT_TILE, LANES, ATT_TILE),
                                lambda i, j: (jnp.minimum(j, n_qv - 1), i, 0, 0)),
                   pl.BlockSpec((hb, tm, LANES),
                                lambda i, j: (jnp.clip(j - n_qv, 0, n_k - 1), i, 0)),
                   pl.BlockSpec((tm, LANES), lambda i, j: (i, 0)),
                   pl.BlockSpec((tm, d), lambda i, j: (i, 0))),
        compiler_params=_params("parallel", "arbitrary"),
        name="in_proj",
    )(x, g.reshape(1, d), w_rows, w_f)


def _forget_cumsum_kernel(f_ref, b_ref, o_ref, carry_ref):
    t = pl.program_id(1)

    @pl.when(t == 0)
    def _():
        carry_ref[...] = jnp.zeros_like(carry_ref)

    f = f_ref[...] + b_ref[...]
    tm = f.shape[0]
    logf = jnp.minimum(f, 0.0) - jnp.log1p(jnp.exp(-jnp.abs(f)))
    logf = logf * LOG2E
    row = lax.broadcasted_iota(jnp.int32, (tm, tm), 0)
    col = lax.broadcasted_iota(jnp.int32, (tm, tm), 1)
    tri = jnp.where(col <= row, 1.0, 0.0).astype(BF16)
    hi, mid, lo = _split3(logf)
    c = (_bf16_dot(tri, hi) + (_bf16_dot(tri, mid) + _bf16_dot(tri, lo))) + carry_ref[0:1, :]
    o_ref[...] = c
    carry_ref[...] = jnp.broadcast_to(c[tm - 1:tm, :], carry_ref.shape)


def forget_cumsum(f, b_f, *, batch, tm=512):
    m = f.shape[0]
    nt = (m // batch) // tm
    b_pad = jnp.zeros((1, LANES), F32).at[0, :b_f.shape[0]].set(b_f)
    return pl.pallas_call(
        _forget_cumsum_kernel,
        out_shape=jax.ShapeDtypeStruct((m, LANES), F32),
        grid=(batch, nt),
        in_specs=[pl.BlockSpec((tm, LANES), lambda b, t: (b * nt + t, 0)),
                  pl.BlockSpec((1, LANES), lambda b, t: (0, 0))],
        out_specs=pl.BlockSpec((tm, LANES), lambda b, t: (b * nt + t, 0)),
        scratch_shapes=[pltpu.VMEM((8, LANES), F32)],
        compiler_params=_params("parallel", "arbitrary"),
        name="forget_cumsum",
    )(f, b_pad)


def _moba_bias_kernel(tab_ref, o_ref):
    h = pl.program_id(0)
    key = lax.broadcasted_iota(jnp.int32, (ATT_TILE, ATT_TILE), 0)
    qry = lax.broadcasted_iota(jnp.int32, (ATT_TILE, ATT_TILE), 1)
    far = tab_ref[h, NUM_BUCKETS - 1] * LOG2E

    def block(delta):
        dist = delta * MOBA_BLOCK + qry - key
        val = jnp.full((ATT_TILE, ATT_TILE), tab_ref[h, 0], F32)
        for k in range(1, NUM_BUCKETS):
            val = jnp.where(dist >= BUCKET_THRESHOLDS[k - 1], tab_ref[h, k], val)
        val = val * LOG2E
        if delta == 0:
            val = jnp.where(dist >= 0, val, NEG_INF)
        return val

    t = [block(delta) for delta in range(NEAR_BLOCKS + 1)]
    a = ATT_TILE
    o_ref[0, 0, 0:a, 0:a] = t[0]
    o_ref[0, 0, 0:a, a:2 * a] = t[1]
    o_ref[0, 0, a:2 * a, 0:a] = jnp.full((a, a), NEG_INF, F32)
    o_ref[0, 0, a:2 * a, a:2 * a] = t[0]
    for d in (1, 2):
        o_ref[0, d, 0:a, 0:a] = t[2 * d] - far
        o_ref[0, d, 0:a, a:2 * a] = t[2 * d + 1] - far
        o_ref[0, d, a:2 * a, 0:a] = t[2 * d - 1] - far
        o_ref[0, d, a:2 * a, a:2 * a] = t[2 * d] - far


def moba_bias(rel_bias):
    h = rel_bias.shape[0]
    return pl.pallas_call(
        _moba_bias_kernel,
        out_shape=jax.ShapeDtypeStruct((h, 3, TILE, TILE), F32),
        grid=(h,),
        in_specs=[pl.BlockSpec(memory_space=pltpu.SMEM)],
        out_specs=pl.BlockSpec((1, 3, TILE, TILE), lambda i: (i, 0, 0, 0)),
        compiler_params=_params("parallel"),
        name="moba_bias",
    )(rel_bias.astype(F32))


class _Softmax:
    def __init__(self, hh, s_ref, m_ref, l_ref, acc_ref, load_values):
        self.s_ref, self.m_ref, self.l_ref, self.acc_ref = s_ref.at[hh], m_ref.at[hh], l_ref.at[hh], acc_ref.at[hh]
        self.load_values = load_values

    def reset(self):
        self.m_ref[...] = jnp.full(self.m_ref.shape, NEG_INF, F32)
        self.l_ref[...] = jnp.zeros(self.l_ref.shape, F32)
        self.acc_ref[...] = jnp.zeros(self.acc_ref.shape, F32)

    def fold(self, slot, tile, mask=None):
        s = self.s_ref[slot]
        if mask is not None:
            s = jnp.where(mask, s, NEG_INF)
        m = self.m_ref[...]
        m_new = jnp.maximum(m, jnp.max(s, axis=0, keepdims=True))
        alpha = jnp.exp2(m - m_new)
        p = jnp.exp2(s - m_new)
        self.l_ref[...] = alpha * self.l_ref[...] + jnp.sum(p, axis=0, keepdims=True)
        self.m_ref[...] = m_new
        self.acc_ref[...] = alpha * self.acc_ref[...] + jnp.dot(
            self.load_values(tile), p.astype(BF16), preferred_element_type=F32)

    def result(self):
        return self.acc_ref[...] * (1.0 / self.l_ref[...])


def _softmax_scratch(t):
    nh = HEADS_PER_STEP
    return [pltpu.VMEM((nh, 2, t, t), F32),
            pltpu.VMEM((nh, 1, t), F32),
            pltpu.VMEM((nh, 1, t), F32),
            pltpu.VMEM((nh, HEAD_DIM, t), F32)]


def _fold_tile_run(heads, s_ref, logits, fold, base, count, last_tile):
    def step(r, slot, look_ahead=True):
        for hh in heads:
            if look_ahead:
                s_ref[hh, 1 - slot] = logits[hh](jnp.minimum(base + r + 1, last_tile))
            fold[hh].fold(slot, base + r)

    def unrolled_body(p, carry):
        for u in range(LOOP_UNROLL):
            step(LOOP_UNROLL * p + u, (u + 1) % 2)
        return carry

    lax.fori_loop(0, count // LOOP_UNROLL, unrolled_body, 0)
    rem = count % LOOP_UNROLL
    done = count - rem

    @pl.when(rem >= 2)
    def _():
        step(done, 1)
        step(done + 1, 0)

    @pl.when(rem % 2 == 1)
    def _():
        step(count - 1, 1, look_ahead=False)


def _top_k_bias(gate, eligible, blk):
    lowest = float(jnp.finfo(F32).min)
    blk_f = blk.astype(F32)
    g = jnp.where(eligible, gate, NEG_INF)
    bias = jnp.full(gate.shape, NEG_INF, F32)
    for _ in range(MOBA_TOP_K):
        best = jnp.max(g, axis=0, keepdims=True)
        first = jnp.min(jnp.where(g == best, blk_f, float(gate.shape[0])), axis=0, keepdims=True)
        hit = blk_f == first
        bias = jnp.where(hit, 0.0, bias)
        g = jnp.where(hit, lowest, g)
    return jnp.where(eligible, bias, NEG_INF)


def _split3(x):
    hi = x.astype(BF16).astype(F32)
    mid = (x - hi).astype(BF16).astype(F32)
    lo = (x - hi - mid).astype(BF16).astype(F32)
    return hi, mid, lo


def _slab_cast_specs(weights, n_steps, step_index):
    in_specs, out_specs, out_shapes = [], [], []
    for w in weights:
        rows = w.shape[0] // n_steps
        assert rows * n_steps == w.shape[0] and rows % 16 == 0
        in_specs.append(pl.BlockSpec((rows, w.shape[1]), lambda *g: (step_index(*g), 0)))
        out_specs.append(pl.BlockSpec((rows, w.shape[1]), lambda *g: (step_index(*g), 0)))
        out_shapes.append(jax.ShapeDtypeStruct(w.shape, BF16))
    return in_specs, out_specs, out_shapes


def _moba_attn_kernel(*refs, n_blocks, n_cast):
    tab_ref, q_ref, k_ref, v_ref, bias_ref = refs[:5]
    o_ref = refs[5 + n_cast]
    kaug_ref, kbar_ref, kb3_ref, s_ref, m_ref, l_ref, acc_ref = refs[6 + 2 * n_cast:]
    for src, dst in zip(refs[5:5 + n_cast], refs[6 + n_cast:6 + 2 * n_cast]):
        dst[...] = src[...].astype(dst.dtype)
    _moba_attn_body(tab_ref, q_ref, k_ref, v_ref, bias_ref, o_ref,
                    kaug_ref, kbar_ref, kb3_ref, s_ref, m_ref, l_ref, acc_ref, n_blocks=n_blocks)


def _moba_attn_body(tab_ref, q_ref, k_ref, v_ref, bias_ref, o_ref,
                    kaug_ref, kbar_ref, kb3_ref, s_ref, m_ref, l_ref, acc_ref, *, n_blocks):
    hp = pl.program_id(1)
    i = pl.program_id(2)
    n_tiles = n_blocks // 2
    heads = range(HEADS_PER_STEP)

    @pl.when(i == 0)
    def _():
        lane = lax.broadcasted_iota(jnp.int32, (TILE, AUG - HEAD_DIM), 1)
        row = lax.broadcasted_iota(jnp.int32, (TILE, AUG - HEAD_DIM), 0)
        ones_lane = jnp.where(lane == n_blocks, 1.0, jnp.where(lane == n_blocks + 1, 1.0, 0.0))
        for hh in heads:
            for n in range(n_blocks):
                kbar_ref[hh, n:n + 1, :] = jnp.sum(k_ref[hh, n].astype(F32), axis=0, keepdims=True)
            hi, mid, lo = _split3(kbar_ref[hh] * (1.0 / MOBA_BLOCK))
            kb3_ref[hh, 0:n_blocks, :] = hi.astype(BF16)
            kb3_ref[hh, n_blocks:2 * n_blocks, :] = mid.astype(BF16)
            kb3_ref[hh, 2 * n_blocks:3 * n_blocks, :] = lo.astype(BF16)
            for j in range(n_tiles):
                blk_of_row = jnp.where(row < ATT_TILE, 2 * j, 2 * j + 1)
                right = jnp.where(lane == blk_of_row, 1.0, ones_lane).astype(BF16)
                kaug_ref[hh, j] = jnp.concatenate(
                    [k_ref[hh, 2 * j:2 * j + 2].reshape(TILE, HEAD_DIM), right], axis=1)

    blk = lax.broadcasted_iota(jnp.int32, (n_blocks, TILE), 0)
    qlane = lax.broadcasted_iota(jnp.int32, (n_blocks, TILE), 1)
    own = 2 * i + jnp.where(qlane >= ATT_TILE, 1, 0)
    eligible = blk < own
    r16 = lax.broadcasted_iota(jnp.int32, (16, TILE), 0)

    j1 = jnp.where(i >= 1, i - 1, i + 1)
    j2 = jnp.where(i >= 2, i - 2, i + 1)
    n_far = jnp.maximum(i - 2, 0)

    far_logits, fold = [], []
    for hh in heads:
        qt = jnp.concatenate([q_ref[hh, 0], q_ref[hh, 1]], axis=1)
        g3 = jnp.dot(kb3_ref[hh], qt, preferred_element_type=F32)
        gate = g3[0:n_blocks] + g3[n_blocks:2 * n_blocks] + g3[2 * n_blocks:3 * n_blocks]
        selb = _top_k_bias(gate, eligible, blk)

        far = jnp.full((16, TILE), tab_ref[hp * HEADS_PER_STEP + hh, NUM_BUCKETS - 1] * LOG2E, F32)
        far_hi = far.astype(BF16).astype(F32)
        far_rows = jnp.where(r16 == 0, far_hi, jnp.where(r16 == 1, far - far_hi, 0.0))
        q_aug = jnp.concatenate(
            [qt, selb.astype(BF16), far_rows.astype(BF16),
             jnp.zeros((AUG - HEAD_DIM - n_blocks - 16, TILE), BF16)], axis=0)

        def values(j, hh=hh):
            return jnp.concatenate([v_ref[hh, 2 * j], v_ref[hh, 2 * j + 1]], axis=1)

        def head_far_logits(j, hh=hh, q_aug=q_aug):
            return jnp.dot(kaug_ref[hh, j], q_aug, preferred_element_type=F32)

        sm = _Softmax(hh, s_ref, m_ref, l_ref, acc_ref, values)
        far_logits.append(head_far_logits)
        fold.append(sm)

        sel_own = jnp.sum(jnp.where(blk == 2 * i, selb, 0.0), axis=0, keepdims=True)
        sel_own = jnp.where(qlane[0:1] >= ATT_TILE, sel_own, 0.0)
        kd = k_ref[hh, pl.ds(2 * i, 2)].reshape(TILE, HEAD_DIM)
        sd = jnp.dot(kd, qt, preferred_element_type=F32) + bias_ref[hh, 0]
        s_ref[hh, 0, 0:ATT_TILE, :] = sd[0:ATT_TILE] + sel_own
        s_ref[hh, 0, ATT_TILE:TILE, :] = sd[ATT_TILE:TILE]
        sm.reset()

    for hh in heads:
        s_ref[hh, 1] = far_logits[hh](j1) + bias_ref[hh, 1]
        fold[hh].fold(0, i)
    for hh in heads:
        s_ref[hh, 0] = far_logits[hh](j2) + bias_ref[hh, 2]
        fold[hh].fold(1, j1)
    for hh in heads:
        s_ref[hh, 1] = far_logits[hh](0)
        fold[hh].fold(0, j2)

    _fold_tile_run(heads, s_ref, far_logits, fold, 0, n_far, n_tiles - 1)

    for hh in heads:
        o_ref[:, hh * HEAD_DIM:(hh + 1) * HEAD_DIM] = fold[hh].result().T.astype(o_ref.dtype)


def moba_attn(qv_t, k, bias, rel_bias, *, batch, n_heads, q_off, k_off, v_off, cast=()):
    mb = k.shape[1]
    nb = mb // batch
    nt = nb // 2
    m = mb * ATT_TILE
    hb = HEADS_PER_STEP
    nhp = n_heads // hb
    assert nb + 16 <= AUG - HEAD_DIM and nt >= 4
    assert n_heads % hb == 0 and q_off % hb == 0 and k_off % hb == 0 and v_off % hb == 0
    c_in, c_out, c_shape = _slab_cast_specs(cast, batch * nhp * nt, lambda b, h, i: (b * nhp + h) * nt + i)
    return pl.pallas_call(
        functools.partial(_moba_attn_kernel, n_blocks=nb, n_cast=len(cast)),
        out_shape=[jax.ShapeDtypeStruct((m, n_heads * HEAD_DIM), BF16)] + c_shape,
        grid=(batch, nhp, nt),
        in_specs=[pl.BlockSpec(memory_space=pltpu.SMEM),
                  pl.BlockSpec((hb, 2, HEAD_DIM, ATT_TILE),
                               lambda b, h, i: (q_off // hb + h, b * nt + i, 0, 0)),
                  pl.BlockSpec((hb, nb, ATT_TILE, HEAD_DIM), lambda b, h, i: (k_off // hb + h, b, 0, 0)),
                  pl.BlockSpec((hb, nb, HEAD_DIM, ATT_TILE), lambda b, h, i: (v_off // hb + h, b, 0, 0)),
                  pl.BlockSpec((hb, 3, TILE, TILE), lambda b, h, i: (h, 0, 0, 0))] + c_in,
        out_specs=[pl.BlockSpec((TILE, hb * HEAD_DIM), lambda b, h, i: (b * nt + i, h))] + c_out,
        scratch_shapes=[pltpu.VMEM((hb, nt, TILE, AUG), BF16),
                        pltpu.VMEM((hb, nb, HEAD_DIM), F32),
                        pltpu.VMEM((hb, 3 * nb, HEAD_DIM), BF16)] + _softmax_scratch(TILE),
        compiler_params=_params("parallel", "parallel", "arbitrary"),
        name="moba_attn",
    )(rel_bias.astype(F32), qv_t, k, qv_t, bias, *cast)


def _fox_attn_kernel(*refs, n_tiles, n_cast):
    q_ref, k_ref, v_ref, c_ref = refs[:4]
    o_ref = refs[4 + n_cast]
    kaug_ref, bound_ref, s_ref, m_ref, l_ref, acc_ref = refs[5 + 2 * n_cast:]
    for src, dst in zip(refs[4:4 + n_cast], refs[5 + n_cast:5 + 2 * n_cast]):
        dst[...] = src[...].astype(dst.dtype)
    _fox_attn_body(q_ref, k_ref, v_ref, c_ref, o_ref, kaug_ref, bound_ref, s_ref, m_ref, l_ref,
                   acc_ref, n_tiles=n_tiles)


def _fox_attn_body(q_ref, k_ref, v_ref, c_ref, o_ref, kaug_ref, bound_ref, s_ref, m_ref, l_ref,
                   acc_ref, *, n_tiles):
    hp = pl.program_id(1)
    i = pl.program_id(2)
    t = TILE
    heads = range(HEADS_PER_STEP)

    lane1 = lax.broadcasted_iota(jnp.int32, (1, LANES), 1)

    @pl.when(i == 0)
    def _():
        lane = lax.broadcasted_iota(jnp.int32, (t, AUG - HEAD_DIM), 1)
        for hh in heads:
            c_first = jnp.zeros((1, LANES), F32)
            c_last = jnp.zeros((1, LANES), F32)
            k_norm2 = jnp.zeros((t, 1), F32)
            for j in range(n_tiles):
                c_all = c_ref[j * t:(j + 1) * t, :]
                c = jnp.sum(jnp.where(lane == hp * HEADS_PER_STEP + hh, c_all, 0.0),
                            axis=1, keepdims=True)
                hi, mid, lo = _split3(-c)
                right = jnp.where(lane == 0, hi, jnp.where(lane == 1, mid, jnp.where(lane == 2, lo, 0.0)))
                k = k_ref[hh, 2 * j:2 * j + 2].reshape(t, HEAD_DIM)
                kaug_ref[hh, j] = jnp.concatenate([k, right.astype(BF16)], axis=1)
                kf = k.astype(F32)
                k_norm2 = jnp.maximum(k_norm2, jnp.sum(kf * kf, axis=1, keepdims=True))
                c_first = jnp.where(lane1 == j, c[0:1, :], c_first)
                c_last = jnp.where(lane1 == j, c[t - 1:t, :], c_last)
            bound_ref[hh, 0:1, :] = c_first
            bound_ref[hh, 1:2, :] = c_last
            bound_ref[hh, 2:3, :] = jnp.broadcast_to(jnp.max(k_norm2, axis=0, keepdims=True), (1, LANES))

    r_aug = lax.broadcasted_iota(jnp.int32, (AUG - HEAD_DIM, t), 0)
    ones_rows = jnp.where(r_aug < 3, 1.0, 0.0).astype(BF16)

    logits, fold, skippable = [], [], []
    for hh in heads:
        qt = jnp.concatenate([q_ref[hh, 0], q_ref[hh, 1]], axis=1)
        q_aug = jnp.concatenate([qt, ones_rows], axis=0)

        qf = qt.astype(F32)
        q_norm2 = jnp.max(jnp.sum(qf * qf, axis=0, keepdims=True), axis=1, keepdims=True)
        c_here = jnp.sum(jnp.where(lane1 == i, bound_ref[hh, 0:1, :], 0.0), axis=1, keepdims=True)
        gap = (bound_ref[hh, 1:2, :] - c_here) - UNDERFLOW_LOG2
        dead = (lane1 < i) & (gap > 0.0) & (gap * gap > 4.0 * q_norm2 * bound_ref[hh, 2:3, :])
        skippable.append(jnp.sum(jnp.where(dead, 1.0, 0.0), axis=1, keepdims=True))

        def head_logits(n, hh=hh, q_aug=q_aug):
            return jnp.dot(kaug_ref[hh, n], q_aug, preferred_element_type=F32)

        def values(n, hh=hh):
            return jnp.concatenate([v_ref[hh, 2 * n], v_ref[hh, 2 * n + 1]], axis=1)

        logits.append(head_logits)
        fold.append(_Softmax(hh, s_ref, m_ref, l_ref, acc_ref, values))

    key = lax.broadcasted_iota(jnp.int32, (t, t), 0)
    qry = lax.broadcasted_iota(jnp.int32, (t, t), 1)
    causal = key <= qry
    j0 = functools.reduce(jnp.minimum, skippable)[0, 0].astype(jnp.int32)
    n_past = i - j0
    for hh in heads:
        s_ref[hh, 0] = logits[hh](i)
        fold[hh].reset()
    for hh in heads:
        s_ref[hh, 1] = logits[hh](j0)
        fold[hh].fold(0, i, mask=causal)

    _fold_tile_run(heads, s_ref, logits, fold, j0, n_past, n_tiles - 1)

    for hh in heads:
        o_ref[:, hh * HEAD_DIM:(hh + 1) * HEAD_DIM] = fold[hh].result().T.astype(o_ref.dtype)


def fox_attn(qv_t, k, c_rep, *, batch, n_heads, q_off, k_off, v_off, cast=()):
    mb = k.shape[1]
    nb = mb // batch
    nt = nb // 2
    m = mb * ATT_TILE
    t = TILE
    hb = HEADS_PER_STEP
    nhp = n_heads // hb
    assert n_heads % hb == 0 and q_off % hb == 0 and k_off % hb == 0 and v_off % hb == 0
    assert nt <= LANES and n_heads <= LANES
    c_in, c_out, c_shape = _slab_cast_specs(cast, batch * nhp * nt, lambda b, h, i: (b * nhp + h) * nt + i)
    return pl.pallas_call(
        functools.partial(_fox_attn_kernel, n_tiles=nt, n_cast=len(cast)),
        out_shape=[jax.ShapeDtypeStruct((m, n_heads * HEAD_DIM), BF16)] + c_shape,
        grid=(batch, nhp, nt),
        in_specs=[pl.BlockSpec((hb, 2, HEAD_DIM, ATT_TILE),
                               lambda b, h, i: (q_off // hb + h, b * nt + i, 0, 0)),
                  pl.BlockSpec((hb, nb, ATT_TILE, HEAD_DIM), lambda b, h, i: (k_off // hb + h, b, 0, 0)),
                  pl.BlockSpec((hb, nb, HEAD_DIM, ATT_TILE), lambda b, h, i: (v_off // hb + h, b, 0, 0)),
                  pl.BlockSpec((nb * ATT_TILE, LANES), lambda b, h, i: (b, 0))] + c_in,
        out_specs=[pl.BlockSpec((t, hb * HEAD_DIM), lambda b, h, i: (b * nt + i, h))] + c_out,
        scratch_shapes=[pltpu.VMEM((hb, nt, t, AUG), BF16),
                        pltpu.VMEM((hb, 8, LANES), F32)] + _softmax_scratch(t),
        compiler_params=_params("parallel", "parallel", "arbitrary"),
        name="fox_attn",
    )(qv_t, k, qv_t, c_rep, *cast)


def _gated_merge_kernel(oa_ref, of_ref, wa_ref, wf_ref, ga_ref, gf_ref, o_ref):
    ua = jnp.dot(oa_ref[...], wa_ref[...], preferred_element_type=F32)
    uf = jnp.dot(of_ref[...], wf_ref[...], preferred_element_type=F32)
    o_ref[...] = (ga_ref[...].astype(F32) * ua + gf_ref[...].astype(F32) * uf).astype(o_ref.dtype)


def gated_merge(o_a, o_f, w_a, w_f, gates, *, tm=ROW_TILE, tn=COL_TILE):
    m, ka = o_a.shape
    kf = o_f.shape[1]
    n = w_a.shape[1]
    nj = n // tn
    return pl.pallas_call(
        _gated_merge_kernel,
        out_shape=jax.ShapeDtypeStruct((m, n), BF16),
        grid=(m // tm, nj),
        in_specs=[pl.BlockSpec((tm, ka), lambda i, j: (i, 0)),
                  pl.BlockSpec((tm, kf), lambda i, j: (i, 0)),
                  pl.BlockSpec((ka, tn), lambda i, j: (0, j)),
                  pl.BlockSpec((kf, tn), lambda i, j: (0, j)),
                  pl.BlockSpec((tm, tn), lambda i, j: (i, j)),
                  pl.BlockSpec((tm, tn), lambda i, j: (i, nj + j))],
        out_specs=pl.BlockSpec((tm, tn), lambda i, j: (i, j)),
        compiler_params=_params("parallel", "parallel"),
        name="gated_merge",
    )(o_a, o_f, w_a, w_f, gates, gates)


def _mm_res_kernel(lhs_ref, w_ref, res_ref, g_ref, o_ref, *, final_norm):
    kk = pl.program_id(1)

    @pl.when(kk == 0)
    def _():
        o_ref[...] = res_ref[...]

    o_ref[...] += jnp.dot(lhs_ref[...], w_ref[...], preferred_element_type=F32)

    if final_norm:
        @pl.when(kk == pl.num_programs(1) - 1)
        def _():
            hres = o_ref[...]
            ms = jnp.mean(hres * hres, axis=-1, keepdims=True)
            o_ref[...] = hres * lax.rsqrt(ms + RMS_EPS) * g_ref[...]


def mm_res(lhs, w, res, g=None, *, tm=ROW_TILE, tk=1024):
    m, k = lhs.shape
    n = w.shape[1]
    tk = min(tk, k)
    final_norm = g is not None
    if g is None:
        g = jnp.ones((n,), F32)
    return pl.pallas_call(
        functools.partial(_mm_res_kernel, final_norm=final_norm),
        out_shape=jax.ShapeDtypeStruct((m, n), F32),
        grid=(m // tm, k // tk),
        in_specs=[pl.BlockSpec((tm, tk), lambda i, kk: (i, kk)),
                  pl.BlockSpec((tk, n), lambda i, kk: (kk, 0)),
                  pl.BlockSpec((tm, n), lambda i, kk: (i, 0)),
                  pl.BlockSpec((1, n), lambda i, kk: (0, 0))],
        out_specs=pl.BlockSpec((tm, n), lambda i, kk: (i, 0)),
        compiler_params=_params("parallel", "arbitrary"),
        name="mm_res",
    )(lhs, w, res, g.reshape(1, n))


def _cross_attn_kernel(h_ref, g_ref, wq_ref, kv_ref, wo_ref, o_ref, *, n_heads):
    width = n_heads * HEAD_DIM
    hres = h_ref[...]
    ms = jnp.mean(hres * hres, axis=-1, keepdims=True)
    c = (hres * lax.rsqrt(ms + RMS_EPS) * g_ref[...]).astype(BF16)
    q_all = (jnp.dot(c, wq_ref[...], preferred_element_type=F32)
             * (HEAD_DIM ** -0.5 * LOG2E)).astype(BF16)
    heads = []
    for h in range(n_heads):
        q = q_all[:, h * HEAD_DIM:(h + 1) * HEAD_DIM]
        k = kv_ref[0, :, h * HEAD_DIM:(h + 1) * HEAD_DIM]
        v = kv_ref[0, :, width + h * HEAD_DIM:width + (h + 1) * HEAD_DIM]
        s = lax.dot_general(q, k, (((1,), (1,)), ((), ())), preferred_element_type=F32)
        m = jnp.max(s, axis=-1, keepdims=True)
        p = jnp.exp2(s - m)
        l = jnp.sum(p, axis=-1, keepdims=True)
        o = jnp.dot(p.astype(BF16), v, preferred_element_type=F32) * (1.0 / l)
        heads.append(o.astype(BF16))
    o_all = jnp.concatenate(heads, axis=1)
    o_ref[...] = hres + jnp.dot(o_all, wo_ref[...], preferred_element_type=F32)


def cross_attn(h, g, w_q, kv, w_o, *, batch, n_heads, tm=512):
    m, d = h.shape
    width = w_q.shape[1]
    n_mem = kv.shape[0] // batch
    tiles_per_batch = (m // batch) // tm
    kv3 = kv.reshape(batch, n_mem, 2 * width)
    return pl.pallas_call(
        functools.partial(_cross_attn_kernel, n_heads=n_heads),
        out_shape=jax.ShapeDtypeStruct((m, d), F32),
        grid=(m // tm,),
        in_specs=[pl.BlockSpec((tm, d), lambda i: (i, 0)),
                  pl.BlockSpec((1, d), lambda i: (0, 0)),
                  pl.BlockSpec((d, width), lambda i: (0, 0)),
                  pl.BlockSpec((1, n_mem, 2 * width), lambda i: (i // tiles_per_batch, 0, 0)),
                  pl.BlockSpec((width, d), lambda i: (0, 0))],
        out_specs=pl.BlockSpec((tm, d), lambda i: (i, 0)),
        compiler_params=_params("parallel"),
        name="cross_attn",
    )(h, g.reshape(1, d), w_q, kv3, w_o)


def kernel(x, mem, g_mix, w_in, b_forget, w_branch_moba, w_branch_fox, w_mix_out, rel_bias,
           g_cross, g_mem, w_cq, w_ck, w_cv, w_co, g_mlp, w_ff1, w_ff2, g_final):
    batch, seq, d = x.shape
    depth = w_in.shape[0]
    n_heads = rel_bias.shape[0]
    n_fox = b_forget.shape[1]
    wm = n_heads * HEAD_DIM
    wf = n_fox * HEAD_DIM
    m = batch * seq
    assert wm == wf and wm % COL_TILE == 0
    scale = HEAD_DIM ** -0.5
    mem2 = mem.reshape(-1, d)

    bias = moba_bias(rel_bias)
    h = x.reshape(m, d)
    for l in range(depth):
        wi = jnp.swapaxes(w_in[l], 0, 1).astype(BF16)
        qkv_w = 3 * (wm + wf)
        w_fl = jnp.zeros((d, LANES), F32).at[:, :n_fox].set(w_in[l][:, qkv_w:qkv_w + n_fox])
        qv_t, k_hm, f_logit, a_mix = in_proj(h, g_mix[l], wi, w_fl, parts_w=wm, scale=scale * LOG2E)
        k_hm = k_hm.reshape(k_hm.shape[0], m // ATT_TILE, ATT_TILE, HEAD_DIM)
        gates = gate_proj(a_mix, wi[qkv_w + n_fox:])
        c_rep = forget_cumsum(f_logit, b_forget[l], batch=batch)

        o_a, w_ff1_16, w_mix_16 = moba_attn(qv_t, k_hm, bias, rel_bias, batch=batch, n_heads=n_heads,
                                            q_off=0, k_off=0, v_off=n_heads + n_fox,
                                            cast=(w_ff1[l], w_mix_out[l]))
        o_f, w_ff2_16 = fox_attn(qv_t, k_hm, c_rep, batch=batch, n_heads=n_fox,
                                 q_off=n_heads, k_off=n_heads, v_off=2 * n_heads + n_fox,
                                 cast=(w_ff2[l],))
        merged = gated_merge(o_a, o_f, w_branch_moba[l].astype(BF16), w_branch_fox[l].astype(BF16), gates)
        h = mm_res(merged, w_mix_16, h)

        cw = w_cq.shape[2]
        w_kv = jnp.concatenate([w_ck[l], w_cv[l]], axis=1).astype(BF16)
        kv = rms_proj(mem2, g_mem[l], w_kv, tn=2 * cw)
        h = cross_attn(h, g_cross[l], w_cq[l].astype(BF16), kv, w_co[l].astype(BF16),
                       batch=batch, n_heads=cw // HEAD_DIM)

        u = rms_proj(h, g_mlp[l], w_ff1_16, act="relu2")
        h = mm_res(u, w_ff2_16, h, g_final if l == depth - 1 else None)
    return h.reshape(batch, seq, d)
```

```python
import functools
import math

import jax
import jax.numpy as jnp
from jax import lax
from jax.experimental import pallas as pl
from jax.experimental.pallas import tpu as pltpu

F32 = jnp.float32
BF16 = jnp.bfloat16

HEAD_DIM = 128
MOBA_BLOCK = 256
MOBA_TOP_K = 3
NUM_BUCKETS = 32
MAX_DISTANCE = 1024
RMS_EPS = 1e-6
LOG2E = math.log2(math.e)
NEG_INF = -1e30
LANES = 128
ATT_TILE = 256
TILE = 2 * ATT_TILE
AUG = 256
HEADS_PER_STEP = 2
LOOP_UNROLL = 4
UNDERFLOW_LOG2 = 160.0
NEAR_BLOCKS = 5
VMEM_LIMIT = 56 * 1024 * 1024
ROW_TILE = 1024
COL_TILE = 1024


def _bucket_thresholds():
    max_exact = NUM_BUCKETS // 2
    thr = list(range(1, max_exact + 1))
    for k in range(max_exact + 1, NUM_BUCKETS):
        v = max_exact * (MAX_DISTANCE / max_exact) ** ((k - max_exact) / (NUM_BUCKETS - max_exact))
        n = int(math.floor(v))
        while max_exact + int(math.log(n / max_exact) / math.log(MAX_DISTANCE / max_exact)
                              * (NUM_BUCKETS - max_exact)) < k:
            n += 1
        thr.append(n)
    return tuple(thr)


BUCKET_THRESHOLDS = _bucket_thresholds()
assert (NEAR_BLOCKS - 1) * MOBA_BLOCK + 1 >= BUCKET_THRESHOLDS[-1]


def _params(*sem):
    return pltpu.CompilerParams(dimension_semantics=sem, vmem_limit_bytes=VMEM_LIMIT)


def _bf16_dot(a, b):
    return jnp.dot(a.astype(BF16), b.astype(BF16), preferred_element_type=F32)


def _rms_proj_kernel(x_ref, g_ref, w_ref, o_ref, a_ref, *, act):
    j = pl.program_id(1)

    @pl.when(j == 0)
    def _():
        x = x_ref[...]
        ms = jnp.mean(x * x, axis=-1, keepdims=True)
        a_ref[...] = (x * lax.rsqrt(ms + RMS_EPS) * g_ref[...]).astype(BF16)

    acc = jnp.dot(a_ref[...], w_ref[...], preferred_element_type=F32)
    if act == "relu2":
        acc = jnp.square(jnp.maximum(acc, 0.0))
    o_ref[...] = acc.astype(o_ref.dtype)


def rms_proj(x, g, w, *, act=None, tm=ROW_TILE, tn=COL_TILE):
    m, d = x.shape
    n = w.shape[1]
    tm, tn = min(tm, m), min(tn, n)
    return pl.pallas_call(
        functools.partial(_rms_proj_kernel, act=act),
        out_shape=jax.ShapeDtypeStruct((m, n), BF16),
        grid=(m // tm, n // tn),
        in_specs=[pl.BlockSpec((tm, d), lambda i, j: (i, 0)),
                  pl.BlockSpec((1, d), lambda i, j: (0, 0)),
                  pl.BlockSpec((d, tn), lambda i, j: (0, j))],
        out_specs=pl.BlockSpec((tm, tn), lambda i, j: (i, j)),
        scratch_shapes=[pltpu.VMEM((tm, d), BF16)],
        compiler_params=_params("parallel", "arbitrary"),
        name="rms_proj",
    )(x, g.reshape(1, d), w)


def _gate_proj_kernel(a_ref, w_ref, o_ref):
    acc = lax.dot_general(a_ref[...], w_ref[...], (((1,), (1,)), ((), ())),
                          preferred_element_type=F32)
    o_ref[...] = jax.nn.sigmoid(acc).astype(o_ref.dtype)


def gate_proj(a, w_rows, *, tm=ROW_TILE, tn=2 * COL_TILE):
    m, d = a.shape
    n = w_rows.shape[0]
    tn = min(tn, n)
    return pl.pallas_call(
        _gate_proj_kernel,
        out_shape=jax.ShapeDtypeStruct((m, n), BF16),
        grid=(m // tm, n // tn),
        in_specs=[pl.BlockSpec((tm, d), lambda i, j: (i, 0)),
                  pl.BlockSpec((tn, d), lambda i, j: (j, 0))],
        out_specs=pl.BlockSpec((tm, tn), lambda i, j: (i, j)),
        compiler_params=_params("parallel", "parallel"),
        name="gate_proj",
    )(a, w_rows)


def _in_proj_kernel(x_ref, g_ref, w_ref, wf_ref, qv_ref, k_ref, f_ref, a_ref, *, n_q, n_qv, scale):
    j = pl.program_id(1)

    @pl.when(j == 0)
    def _():
        x = x_ref[...]
        ms = jnp.mean(x * x, axis=-1, keepdims=True)
        a_ref[...] = (x * lax.rsqrt(ms + RMS_EPS) * g_ref[...]).astype(BF16)
        w_hi, w_lo, _ = _split3(wf_ref[...])
        both = _bf16_dot(a_ref[...], jnp.concatenate([w_hi, w_lo], axis=1))
        f_ref[...] = both[:, :LANES] + both[:, LANES:]

    def project():
        return lax.dot_general(a_ref[...], w_ref[...], (((1,), (1,)), ((), ())),
                               preferred_element_type=F32)

    @pl.when(j < n_qv)
    def _():
        acc = project() * jnp.where(j < n_q, F32(scale), F32(1.0))
        tm, tn = acc.shape
        for c in range(tn // LANES):
            for r in range(tm // ATT_TILE):
                blk = acc[r * ATT_TILE:(r + 1) * ATT_TILE, c * LANES:(c + 1) * LANES]
                qv_ref[c, r] = blk.T.astype(qv_ref.dtype)

    @pl.when(j >= n_qv)
    def _():
        acc = project()
        for c in range(acc.shape[1] // LANES):
            k_ref[c] = acc[:, c * LANES:(c + 1) * LANES].astype(k_ref.dtype)


def in_proj(x, g, w_rows, w_f, *, parts_w, scale, tm=ROW_TILE, tn=COL_TILE):
    m, d = x.shape
    per = parts_w // tn

    def tiles(*parts):
        return [p * per + t for p in parts for t in range(per)]

    order = tiles(0, 3, 2, 5) + tiles(1, 4)
    n_q, n_qv, n_k = 2 * per, 4 * per, 2 * per

    def w_tile(j):
        idx = order[-1]
        for t, src in reversed(list(enumerate(order[:-1]))):
            idx = jnp.where(j == t, src, idx)
        return idx

    hb = tn // LANES
    return pl.pallas_call(
        functools.partial(_in_proj_kernel, n_q=n_q, n_qv=n_qv, scale=scale),
        out_shape=(jax.ShapeDtypeStruct((n_qv * hb, m // ATT_TILE, LANES, ATT_TILE), BF16),
                   jax.ShapeDtypeStruct((n_k * hb, m, LANES), BF16),
                   jax.ShapeDtypeStruct((m, LANES), F32),
                   jax.ShapeDtypeStruct((m, d), BF16)),
        grid=(m // tm, len(order)),
        in_specs=[pl.BlockSpec((tm, d), lambda i, j: (i, 0)),
                  pl.BlockSpec((1, d), lambda i, j: (0, 0)),
                  pl.BlockSpec((tn, d), lambda i, j: (w_tile(j), 0)),
                  pl.BlockSpec((d, LANES), lambda i, j: (0, 0))],
        out_specs=(pl.BlockSpec((hb, tm // ATT_TILE, LANES, ATT_TILE),
                                lambda i, j: (jnp.minimum(j, n_qv - 1), i, 0, 0)),
                   pl.BlockSpec((hb, tm, LANES),
                                lambda i, j: (jnp.clip(j - n_qv, 0, n_k - 1), i, 0)),
                   pl.BlockSpec((tm, LANES), lambda i, j: (i, 0)),
                   pl.BlockSpec((tm, d), lambda i, j: (i, 0))),
        compiler_params=_params("parallel", "arbitrary"),
        name="in_proj",
    )(x, g.reshape(1, d), w_rows, w_f)


def _forget_cumsum_kernel(f_ref, b_ref, o_ref, carry_ref):
    t = pl.program_id(1)

    @pl.when(t == 0)
    def _():
        carry_ref[...] = jnp.zeros_like(carry_ref)

    f = f_ref[...] + b_ref[...]
    tm = f.shape[0]
    logf = jnp.minimum(f, 0.0) - jnp.log1p(jnp.exp(-jnp.abs(f)))
    logf = logf * LOG2E
    row = lax.broadcasted_iota(jnp.int32, (tm, tm), 0)
    col = lax.broadcasted_iota(jnp.int32, (tm, tm), 1)
    tri = jnp.where(col <= row, 1.0, 0.0).astype(BF16)
    hi, mid, lo = _split3(logf)
    c = (_bf16_dot(tri, hi) + (_bf16_dot(tri, mid) + _bf16_dot(tri, lo))) + carry_ref[0:1, :]
    o_ref[...] = c
    carry_ref[...] = jnp.broadcast_to(c[tm - 1:tm, :], carry_ref.shape)


def forget_cumsum(f, b_f, *, batch, tm=512):
    m = f.shape[0]
    nt = (m // batch) // tm
    b_pad = jnp.zeros((1, LANES), F32).at[0, :b_f.shape[0]].set(b_f)
    return pl.pallas_call(
        _forget_cumsum_kernel,
        out_shape=jax.ShapeDtypeStruct((m, LANES), F32),
        grid=(batch, nt),
        in_specs=[pl.BlockSpec((tm, LANES), lambda b, t: (b * nt + t, 0)),
                  pl.BlockSpec((1, LANES), lambda b, t: (0, 0))],
        out_specs=pl.BlockSpec((tm, LANES), lambda b, t: (b * nt + t, 0)),
        scratch_shapes=[pltpu.VMEM((8, LANES), F32)],
        compiler_params=_params("parallel", "arbitrary"),
        name="forget_cumsum",
    )(f, b_pad)


def _moba_bias_kernel(tab_ref, o_ref):
    h = pl.program_id(0)
    key = lax.broadcasted_iota(jnp.int32, (ATT_TILE, ATT_TILE), 0)
    qry = lax.broadcasted_iota(jnp.int32, (ATT_TILE, ATT_TILE), 1)
    far = tab_ref[h, NUM_BUCKETS - 1] * LOG2E

    def block(delta):
        dist = delta * MOBA_BLOCK + qry - key
        val = jnp.full((ATT_TILE, ATT_TILE), tab_ref[h, 0], F32)
        for k in range(1, NUM_BUCKETS):
            val = jnp.where(dist >= BUCKET_THRESHOLDS[k - 1], tab_ref[h, k], val)
        val = val * LOG2E
        if delta == 0:
            val = jnp.where(dist >= 0, val, NEG_INF)
        return val

    t = [block(delta) for delta in range(NEAR_BLOCKS + 1)]
    a = ATT_TILE
    o_ref[0, 0, 0:a, 0:a] = t[0]
    o_ref[0, 0, 0:a, a:2 * a] = t[1]
    o_ref[0, 0, a:2 * a, 0:a] = jnp.full((a, a), NEG_INF, F32)
    o_ref[0, 0, a:2 * a, a:2 * a] = t[0]
    for d in (1, 2):
        o_ref[0, d, 0:a, 0:a] = t[2 * d] - far
        o_ref[0, d, 0:a, a:2 * a] = t[2 * d + 1] - far
        o_ref[0, d, a:2 * a, 0:a] = t[2 * d - 1] - far
        o_ref[0, d, a:2 * a, a:2 * a] = t[2 * d] - far


def moba_bias(rel_bias):
    h = rel_bias.shape[0]
    return pl.pallas_call(
        _moba_bias_kernel,
        out_shape=jax.ShapeDtypeStruct((h, 3, TILE, TILE), F32),
        grid=(h,),
        in_specs=[pl.BlockSpec(memory_space=pltpu.SMEM)],
        out_specs=pl.BlockSpec((1, 3, TILE, TILE), lambda i: (i, 0, 0, 0)),
        compiler_params=_params("parallel"),
        name="moba_bias",
    )(rel_bias.astype(F32))


class _Softmax:
    def __init__(self, hh, s_ref, m_ref, l_ref, acc_ref, load_values):
        self.s_ref, self.m_ref, self.l_ref, self.acc_ref = s_ref.at[hh], m_ref.at[hh], l_ref.at[hh], acc_ref.at[hh]
        self.load_values = load_values

    def reset(self):
        self.m_ref[...] = jnp.full(self.m_ref.shape, NEG_INF, F32)
        self.l_ref[...] = jnp.zeros(self.l_ref.shape, F32)
        self.acc_ref[...] = jnp.zeros(self.acc_ref.shape, F32)

    def fold(self, slot, tile, mask=None):
        for half in range(2):
            self.fold_half(slot, tile, half, mask)

    def fold_half(self, slot, tile, half, mask=None):
        width = self.m_ref.shape[-1] // 2
        cols = slice(half * width, (half + 1) * width)
        s = self.s_ref[slot, :, cols]
        if mask is not None:
            s = jnp.where(mask[:, cols], s, NEG_INF)
        m = self.m_ref[:, cols]
        m_new = jnp.maximum(m, jnp.max(s, axis=0, keepdims=True))
        alpha = jnp.exp2(m - m_new)
        p = jnp.exp2(s - m_new)
        self.acc_ref[:, cols] = alpha * self.acc_ref[:, cols] + jnp.dot(
            self.load_values(tile), p.astype(BF16), preferred_element_type=F32)
        self.l_ref[:, cols] = alpha * self.l_ref[:, cols] + jnp.sum(p, axis=0, keepdims=True)
        self.m_ref[:, cols] = m_new

    def result(self):
        return self.acc_ref[...] * (1.0 / self.l_ref[...])


def _softmax_scratch(t):
    nh = HEADS_PER_STEP
    return [pltpu.VMEM((nh, 2, t, t), F32),
            pltpu.VMEM((nh, 1, t), F32),
            pltpu.VMEM((nh, 1, t), F32),
            pltpu.VMEM((nh, HEAD_DIM, t), F32)]


def _fold_tile_run(heads, s_ref, logits, fold, base, count, last_tile):
    def step(r, slot, look_ahead=True):
        for hh in heads:
            if look_ahead:
                s_ref[hh, 1 - slot] = logits[hh](jnp.minimum(base + r + 1, last_tile))
            fold[hh].fold(slot, base + r)

    def unrolled_body(p, carry):
        for u in range(LOOP_UNROLL):
            step(LOOP_UNROLL * p + u, (u + 1) % 2)
        return carry

    lax.fori_loop(0, count // LOOP_UNROLL, unrolled_body, 0)
    rem = count % LOOP_UNROLL
    done = count - rem

    @pl.when(rem >= 2)
    def _():
        step(done, 1)
        step(done + 1, 0)

    @pl.when(rem % 2 == 1)
    def _():
        step(count - 1, 1, look_ahead=False)


def _top_k_bias(gate, eligible, blk):
    lowest = float(jnp.finfo(F32).min)
    blk_f = blk.astype(F32)
    g = jnp.where(eligible, gate, NEG_INF)
    bias = jnp.full(gate.shape, NEG_INF, F32)
    for _ in range(MOBA_TOP_K):
        best = jnp.max(g, axis=0, keepdims=True)
        first = jnp.min(jnp.where(g == best, blk_f, float(gate.shape[0])), axis=0, keepdims=True)
        hit = blk_f == first
        bias = jnp.where(hit, 0.0, bias)
        g = jnp.where(hit, lowest, g)
    return jnp.where(eligible, bias, NEG_INF)


def _split3(x):
    hi = x.astype(BF16).astype(F32)
    mid = (x - hi).astype(BF16).astype(F32)
    lo = (x - hi - mid).astype(BF16).astype(F32)
    return hi, mid, lo


def _slab_cast_specs(weights, n_steps, step_index):
    in_specs, out_specs, out_shapes = [], [], []
    for w in weights:
        rows = w.shape[0] // n_steps
        assert rows * n_steps == w.shape[0] and rows % 16 == 0
        in_specs.append(pl.BlockSpec((rows, w.shape[1]), lambda *g: (step_index(*g), 0)))
        out_specs.append(pl.BlockSpec((rows, w.shape[1]), lambda *g: (step_index(*g), 0)))
        out_shapes.append(jax.ShapeDtypeStruct(w.shape, BF16))
    return in_specs, out_specs, out_shapes


def _moba_attn_kernel(*refs, n_blocks, n_cast):
    tab_ref, q_ref, k_ref, v_ref, bias_ref = refs[:5]
    o_ref = refs[5 + n_cast]
    kaug_ref, kbar_ref, kb3_ref, s_ref, m_ref, l_ref, acc_ref = refs[6 + 2 * n_cast:]
    for src, dst in zip(refs[5:5 + n_cast], refs[6 + n_cast:6 + 2 * n_cast]):
        dst[...] = src[...].astype(dst.dtype)
    _moba_attn_body(tab_ref, q_ref, k_ref, v_ref, bias_ref, o_ref,
                    kaug_ref, kbar_ref, kb3_ref, s_ref, m_ref, l_ref, acc_ref, n_blocks=n_blocks)


def _moba_attn_body(tab_ref, q_ref, k_ref, v_ref, bias_ref, o_ref,
                    kaug_ref, kbar_ref, kb3_ref, s_ref, m_ref, l_ref, acc_ref, *, n_blocks):
    hp = pl.program_id(1)
    i = pl.program_id(2)
    n_tiles = n_blocks // 2
    heads = range(HEADS_PER_STEP)

    @pl.when(i == 0)
    def _():
        lane = lax.broadcasted_iota(jnp.int32, (TILE, AUG - HEAD_DIM), 1)
        row = lax.broadcasted_iota(jnp.int32, (TILE, AUG - HEAD_DIM), 0)
        ones_lane = jnp.where(lane == n_blocks, 1.0, jnp.where(lane == n_blocks + 1, 1.0, 0.0))
        for hh in heads:
            for n in range(n_blocks):
                kbar_ref[hh, n:n + 1, :] = jnp.sum(k_ref[hh, n].astype(F32), axis=0, keepdims=True)
            hi, mid, lo = _split3(kbar_ref[hh] * (1.0 / MOBA_BLOCK))
            kb3_ref[hh, 0:n_blocks, :] = hi.astype(BF16)
            kb3_ref[hh, n_blocks:2 * n_blocks, :] = mid.astype(BF16)
            kb3_ref[hh, 2 * n_blocks:3 * n_blocks, :] = lo.astype(BF16)
            for j in range(n_tiles):
                blk_of_row = jnp.where(row < ATT_TILE, 2 * j, 2 * j + 1)
                right = jnp.where(lane == blk_of_row, 1.0, ones_lane).astype(BF16)
                kaug_ref[hh, j] = jnp.concatenate(
                    [k_ref[hh, 2 * j:2 * j + 2].reshape(TILE, HEAD_DIM), right], axis=1)

    blk = lax.broadcasted_iota(jnp.int32, (n_blocks, TILE), 0)
    qlane = lax.broadcasted_iota(jnp.int32, (n_blocks, TILE), 1)
    own = 2 * i + jnp.where(qlane >= ATT_TILE, 1, 0)
    eligible = blk < own
    r16 = lax.broadcasted_iota(jnp.int32, (16, TILE), 0)

    j1 = jnp.where(i >= 1, i - 1, i + 1)
    j2 = jnp.where(i >= 2, i - 2, i + 1)
    n_far = jnp.maximum(i - 2, 0)

    far_logits, fold = [], []
    for hh in heads:
        qt = jnp.concatenate([q_ref[hh, 0], q_ref[hh, 1]], axis=1)
        g3 = jnp.dot(kb3_ref[hh], qt, preferred_element_type=F32)
        gate = g3[0:n_blocks] + g3[n_blocks:2 * n_blocks] + g3[2 * n_blocks:3 * n_blocks]
        selb = _top_k_bias(gate, eligible, blk)

        far = jnp.full((16, TILE), tab_ref[hp * HEADS_PER_STEP + hh, NUM_BUCKETS - 1] * LOG2E, F32)
        far_hi = far.astype(BF16).astype(F32)
        far_rows = jnp.where(r16 == 0, far_hi, jnp.where(r16 == 1, far - far_hi, 0.0))
        q_aug = jnp.concatenate(
            [qt, selb.astype(BF16), far_rows.astype(BF16),
             jnp.zeros((AUG - HEAD_DIM - n_blocks - 16, TILE), BF16)], axis=0)

        def values(j, hh=hh):
            return jnp.concatenate([v_ref[hh, 2 * j], v_ref[hh, 2 * j + 1]], axis=1)

        def head_far_logits(j, hh=hh, q_aug=q_aug):
            return jnp.dot(kaug_ref[hh, j], q_aug, preferred_element_type=F32)

        sm = _Softmax(hh, s_ref, m_ref, l_ref, acc_ref, values)
        far_logits.append(head_far_logits)
        fold.append(sm)

        sel_own = jnp.sum(jnp.where(blk == 2 * i, selb, 0.0), axis=0, keepdims=True)
        sel_own = jnp.where(qlane[0:1] >= ATT_TILE, sel_own, 0.0)
        kd = k_ref[hh, pl.ds(2 * i, 2)].reshape(TILE, HEAD_DIM)
        sd = jnp.dot(kd, qt, preferred_element_type=F32) + bias_ref[hh, 0]
        s_ref[hh, 0, 0:ATT_TILE, :] = sd[0:ATT_TILE] + sel_own
        s_ref[hh, 0, ATT_TILE:TILE, :] = sd[ATT_TILE:TILE]
        sm.reset()

    for hh in heads:
        s_ref[hh, 1] = far_logits[hh](j1) + bias_ref[hh, 1]
        fold[hh].fold(0, i)
    for hh in heads:
        s_ref[hh, 0] = far_logits[hh](j2) + bias_ref[hh, 2]
        fold[hh].fold(1, j1)
    for hh in heads:
        s_ref[hh, 1] = far_logits[hh](0)
        fold[hh].fold(0, j2)

    _fold_tile_run(heads, s_ref, far_logits, fold, 0, n_far, n_tiles - 1)

    for hh in heads:
        o_ref[:, hh * HEAD_DIM:(hh + 1) * HEAD_DIM] = fold[hh].result().T.astype(o_ref.dtype)


def moba_attn(qv_t, k, bias, rel_bias, *, batch, n_heads, q_off, k_off, v_off, cast=()):
    mb = k.shape[1]
    nb = mb // batch
    nt = nb // 2
    m = mb * ATT_TILE
    hb = HEADS_PER_STEP
    nhp = n_heads // hb
    assert nb + 16 <= AUG - HEAD_DIM and nt >= 4
    assert n_heads % hb == 0 and q_off % hb == 0 and k_off % hb == 0 and v_off % hb == 0
    c_in, c_out, c_shape = _slab_cast_specs(cast, batch * nhp * nt, lambda b, h, i: (b * nhp + h) * nt + i)
    return pl.pallas_call(
        functools.partial(_moba_attn_kernel, n_blocks=nb, n_cast=len(cast)),
        out_shape=[jax.ShapeDtypeStruct((m, n_heads * HEAD_DIM), BF16)] + c_shape,
        grid=(batch, nhp, nt),
        in_specs=[pl.BlockSpec(memory_space=pltpu.SMEM),
                  pl.BlockSpec((hb, 2, HEAD_DIM, ATT_TILE),
                               lambda b, h, i: (q_off // hb + h, b * nt + i, 0, 0)),
                  pl.BlockSpec((hb, nb, ATT_TILE, HEAD_DIM), lambda b, h, i: (k_off // hb + h, b, 0, 0)),
                  pl.BlockSpec((hb, nb, HEAD_DIM, ATT_TILE), lambda b, h, i: (v_off // hb + h, b, 0, 0)),
                  pl.BlockSpec((hb, 3, TILE, TILE), lambda b, h, i: (h, 0, 0, 0))] + c_in,
        out_specs=[pl.BlockSpec((TILE, hb * HEAD_DIM), lambda b, h, i: (b * nt + i, h))] + c_out,
        scratch_shapes=[pltpu.VMEM((hb, nt, TILE, AUG), BF16),
                        pltpu.VMEM((hb, nb, HEAD_DIM), F32),
                        pltpu.VMEM((hb, 3 * nb, HEAD_DIM), BF16)] + _softmax_scratch(TILE),
        compiler_params=_params("parallel", "parallel", "arbitrary"),
        name="moba_attn",
    )(rel_bias.astype(F32), qv_t, k, qv_t, bias, *cast)


def _fox_attn_kernel(*refs, n_tiles, n_cast):
    q_ref, k_ref, v_ref, c_ref = refs[:4]
    o_ref = refs[4 + n_cast]
    kaug_ref, bound_ref, s_ref, m_ref, l_ref, acc_ref = refs[5 + 2 * n_cast:]
    for src, dst in zip(refs[4:4 + n_cast], refs[5 + n_cast:5 + 2 * n_cast]):
        dst[...] = src[...].astype(dst.dtype)
    _fox_attn_body(q_ref, k_ref, v_ref, c_ref, o_ref, kaug_ref, bound_ref, s_ref, m_ref, l_ref,
                   acc_ref, n_tiles=n_tiles)


def _fox_attn_body(q_ref, k_ref, v_ref, c_ref, o_ref, kaug_ref, bound_ref, s_ref, m_ref, l_ref,
                   acc_ref, *, n_tiles):
    hp = pl.program_id(1)
    i = pl.program_id(2)
    t = TILE
    heads = range(HEADS_PER_STEP)

    lane1 = lax.broadcasted_iota(jnp.int32, (1, LANES), 1)

    @pl.when(i == 0)
    def _():
        lane = lax.broadcasted_iota(jnp.int32, (t, AUG - HEAD_DIM), 1)
        for hh in heads:
            c_first = jnp.zeros((1, LANES), F32)
            c_last = jnp.zeros((1, LANES), F32)
            k_norm2 = jnp.zeros((t, 1), F32)
            for j in range(n_tiles):
                c_all = c_ref[j * t:(j + 1) * t, :]
                c = jnp.sum(jnp.where(lane == hp * HEADS_PER_STEP + hh, c_all, 0.0),
                            axis=1, keepdims=True)
                hi, mid, lo = _split3(-c)
                right = jnp.where(lane == 0, hi, jnp.where(lane == 1, mid, jnp.where(lane == 2, lo, 0.0)))
                k = k_ref[hh, 2 * j:2 * j + 2].reshape(t, HEAD_DIM)
                kaug_ref[hh, j] = jnp.concatenate([k, right.astype(BF16)], axis=1)
                kf = k.astype(F32)
                k_norm2 = jnp.maximum(k_norm2, jnp.sum(kf * kf, axis=1, keepdims=True))
                c_first = jnp.where(lane1 == j, c[0:1, :], c_first)
                c_last = jnp.where(lane1 == j, c[t - 1:t, :], c_last)
            bound_ref[hh, 0:1, :] = c_first
            bound_ref[hh, 1:2, :] = c_last
            bound_ref[hh, 2:3, :] = jnp.broadcast_to(jnp.max(k_norm2, axis=0, keepdims=True), (1, LANES))

    r_aug = lax.broadcasted_iota(jnp.int32, (AUG - HEAD_DIM, t), 0)
    ones_rows = jnp.where(r_aug < 3, 1.0, 0.0).astype(BF16)

    logits, fold, skippable = [], [], []
    for hh in heads:
        qt = jnp.concatenate([q_ref[hh, 0], q_ref[hh, 1]], axis=1)
        q_aug = jnp.concatenate([qt, ones_rows], axis=0)

        qf = qt.astype(F32)
        q_norm2 = jnp.max(jnp.sum(qf * qf, axis=0, keepdims=True), axis=1, keepdims=True)
        c_here = jnp.sum(jnp.where(lane1 == i, bound_ref[hh, 0:1, :], 0.0), axis=1, keepdims=True)
        gap = (bound_ref[hh, 1:2, :] - c_here) - UNDERFLOW_LOG2
        dead = (lane1 < i) & (gap > 0.0) & (gap * gap > 4.0 * q_norm2 * bound_ref[hh, 2:3, :])
        skippable.append(jnp.sum(jnp.where(dead, 1.0, 0.0), axis=1, keepdims=True))

        def head_logits(n, hh=hh, q_aug=q_aug):
            return jnp.dot(kaug_ref[hh, n], q_aug, preferred_element_type=F32)

        def values(n, hh=hh):
            return jnp.concatenate([v_ref[hh, 2 * n], v_ref[hh, 2 * n + 1]], axis=1)

        logits.append(head_logits)
        fold.append(_Softmax(hh, s_ref, m_ref, l_ref, acc_ref, values))

    key = lax.broadcasted_iota(jnp.int32, (t, t), 0)
    qry = lax.broadcasted_iota(jnp.int32, (t, t), 1)
    causal = key <= qry
    j0 = functools.reduce(jnp.minimum, skippable)[0, 0].astype(jnp.int32)
    n_past = i - j0
    for hh in heads:
        s_ref[hh, 0] = logits[hh](i)
        fold[hh].reset()
    for hh in heads:
        s_ref[hh, 1] = logits[hh](j0)
        fold[hh].fold(0, i, mask=causal)

    _fold_tile_run(heads, s_ref, logits, fold, j0, n_past, n_tiles - 1)

    for hh in heads:
        o_ref[:, hh * HEAD_DIM:(hh + 1) * HEAD_DIM] = fold[hh].result().T.astype(o_ref.dtype)


def fox_attn(qv_t, k, c_rep, *, batch, n_heads, q_off, k_off, v_off, cast=()):
    mb = k.shape[1]
    nb = mb // batch
    nt = nb // 2
    m = mb * ATT_TILE
    t = TILE
    hb = HEADS_PER_STEP
    nhp = n_heads // hb
    assert n_heads % hb == 0 and q_off % hb == 0 and k_off % hb == 0 and v_off % hb == 0
    assert nt <= LANES and n_heads <= LANES
    c_in, c_out, c_shape = _slab_cast_specs(cast, batch * nhp * nt, lambda b, h, i: (b * nhp + h) * nt + i)
    return pl.pallas_call(
        functools.partial(_fox_attn_kernel, n_tiles=nt, n_cast=len(cast)),
        out_shape=[jax.ShapeDtypeStruct((m, n_heads * HEAD_DIM), BF16)] + c_shape,
        grid=(batch, nhp, nt),
        in_specs=[pl.BlockSpec((hb, 2, HEAD_DIM, ATT_TILE),
                               lambda b, h, i: (q_off // hb + h, b * nt + i, 0, 0)),
                  pl.BlockSpec((hb, nb, ATT_TILE, HEAD_DIM), lambda b, h, i: (k_off // hb + h, b, 0, 0)),
                  pl.BlockSpec((hb, nb, HEAD_DIM, ATT_TILE), lambda b, h, i: (v_off // hb + h, b, 0, 0)),
                  pl.BlockSpec((nb * ATT_TILE, LANES), lambda b, h, i: (b, 0))] + c_in,
        out_specs=[pl.BlockSpec((t, hb * HEAD_DIM), lambda b, h, i: (b * nt + i, h))] + c_out,
        scratch_shapes=[pltpu.VMEM((hb, nt, t, AUG), BF16),
                        pltpu.VMEM((hb, 8, LANES), F32)] + _softmax_scratch(t),
        compiler_params=_params("parallel", "parallel", "arbitrary"),
        name="fox_attn",
    )(qv_t, k, qv_t, c_rep, *cast)


def _gated_merge_kernel(oa_ref, of_ref, wa_ref, wf_ref, ga_ref, gf_ref, o_ref):
    ua = jnp.dot(oa_ref[...], wa_ref[...], preferred_element_type=F32)
    uf = jnp.dot(of_ref[...], wf_ref[...], preferred_element_type=F32)
    o_ref[...] = (ga_ref[...].astype(F32) * ua + gf_ref[...].astype(F32) * uf).astype(o_ref.dtype)


def gated_merge(o_a, o_f, w_a, w_f, gates, *, tm=ROW_TILE, tn=COL_TILE):
    m, ka = o_a.shape
    kf = o_f.shape[1]
    n = w_a.shape[1]
    nj = n // tn
    return pl.pallas_call(
        _gated_merge_kernel,
        out_shape=jax.ShapeDtypeStruct((m, n), BF16),
        grid=(m // tm, nj),
        in_specs=[pl.BlockSpec((tm, ka), lambda i, j: (i, 0)),
                  pl.BlockSpec((tm, kf), lambda i, j: (i, 0)),
                  pl.BlockSpec((ka, tn), lambda i, j: (0, j)),
                  pl.BlockSpec((kf, tn), lambda i, j: (0, j)),
                  pl.BlockSpec((tm, tn), lambda i, j: (i, j)),
                  pl.BlockSpec((tm, tn), lambda i, j: (i, nj + j))],
        out_specs=pl.BlockSpec((tm, tn), lambda i, j: (i, j)),
        compiler_params=_params("parallel", "parallel"),
        name="gated_merge",
    )(o_a, o_f, w_a, w_f, gates, gates)


def _mm_res_kernel(lhs_ref, w_ref, res_ref, g_ref, o_ref, *, final_norm):
    kk = pl.program_id(1)

    @pl.when(kk == 0)
    def _():
        o_ref[...] = res_ref[...]

    o_ref[...] += jnp.dot(lhs_ref[...], w_ref[...], preferred_element_type=F32)

    if final_norm:
        @pl.when(kk == pl.num_programs(1) - 1)
        def _():
            hres = o_ref[...]
            ms = jnp.mean(hres * hres, axis=-1, keepdims=True)
            o_ref[...] = hres * lax.rsqrt(ms + RMS_EPS) * g_ref[...]


def mm_res(lhs, w, res, g=None, *, tm=ROW_TILE, tk=1024):
    m, k = lhs.shape
    n = w.shape[1]
    tk = min(tk, k)
    final_norm = g is not None
    if g is None:
        g = jnp.ones((n,), F32)
    return pl.pallas_call(
        functools.partial(_mm_res_kernel, final_norm=final_norm),
        out_shape=jax.ShapeDtypeStruct((m, n), F32),
        grid=(m // tm, k // tk),
        in_specs=[pl.BlockSpec((tm, tk), lambda i, kk: (i, kk)),
                  pl.BlockSpec((tk, n), lambda i, kk: (kk, 0)),
                  pl.BlockSpec((tm, n), lambda i, kk: (i, 0)),
                  pl.BlockSpec((1, n), lambda i, kk: (0, 0))],
        out_specs=pl.BlockSpec((tm, n), lambda i, kk: (i, 0)),
        compiler_params=_params("parallel", "arbitrary"),
        name="mm_res",
    )(lhs, w, res, g.reshape(1, n))


def _cross_attn_kernel(h_ref, g_ref, wq_ref, kv_ref, wo_ref, o_ref, *, n_heads):
    width = n_heads * HEAD_DIM
    hres = h_ref[...]
    ms = jnp.mean(hres * hres, axis=-1, keepdims=True)
    c = (hres * lax.rsqrt(ms + RMS_EPS) * g_ref[...]).astype(BF16)
    q_all = (jnp.dot(c, wq_ref[...], preferred_element_type=F32)
             * (HEAD_DIM ** -0.5 * LOG2E)).astype(BF16)
    heads = []
    for h in range(n_heads):
        q = q_all[:, h * HEAD_DIM:(h + 1) * HEAD_DIM]
        k = kv_ref[0, :, h * HEAD_DIM:(h + 1) * HEAD_DIM]
        v = kv_ref[0, :, width + h * HEAD_DIM:width + (h + 1) * HEAD_DIM]
        s = lax.dot_general(q, k, (((1,), (1,)), ((), ())), preferred_element_type=F32)
        m = jnp.max(s, axis=-1, keepdims=True)
        p = jnp.exp2(s - m)
        l = jnp.sum(p, axis=-1, keepdims=True)
        o = jnp.dot(p.astype(BF16), v, preferred_element_type=F32) * (1.0 / l)
        heads.append(o.astype(BF16))
    o_all = jnp.concatenate(heads, axis=1)
    o_ref[...] = hres + jnp.dot(o_all, wo_ref[...], preferred_element_type=F32)


def cross_attn(h, g, w_q, kv, w_o, *, batch, n_heads, tm=512):
    m, d = h.shape
    width = w_q.shape[1]
    n_mem = kv.shape[0] // batch
    tiles_per_batch = (m // batch) // tm
    kv3 = kv.reshape(batch, n_mem, 2 * width)
    return pl.pallas_call(
        functools.partial(_cross_attn_kernel, n_heads=n_heads),
        out_shape=jax.ShapeDtypeStruct((m, d), F32),
        grid=(m // tm,),
        in_specs=[pl.BlockSpec((tm, d), lambda i: (i, 0)),
                  pl.BlockSpec((1, d), lambda i: (0, 0)),
                  pl.BlockSpec((d, width), lambda i: (0, 0)),
                  pl.BlockSpec((1, n_mem, 2 * width), lambda i: (i // tiles_per_batch, 0, 0)),
                  pl.BlockSpec((width, d), lambda i: (0, 0))],
        out_specs=pl.BlockSpec((tm, d), lambda i: (i, 0)),
        compiler_params=_params("parallel"),
        name="cross_attn",
    )(h, g.reshape(1, d), w_q, kv3, w_o)


def kernel(x, mem, g_mix, w_in, b_forget, w_branch_moba, w_branch_fox, w_mix_out, rel_bias,
           g_cross, g_mem, w_cq, w_ck, w_cv, w_co, g_mlp, w_ff1, w_ff2, g_final):
    batch, seq, d = x.shape
    depth = w_in.shape[0]
    n_heads = rel_bias.shape[0]
    n_fox = b_forget.shape[1]
    wm = n_heads * HEAD_DIM
    wf = n_fox * HEAD_DIM
    m = batch * seq
    assert wm == wf and wm % COL_TILE == 0
    scale = HEAD_DIM ** -0.5
    mem2 = mem.reshape(-1, d)

    bias = moba_bias(rel_bias)
    h = x.reshape(m, d)
    for l in range(depth):
        wi = jnp.swapaxes(w_in[l], 0, 1).astype(BF16)
        qkv_w = 3 * (wm + wf)
        w_fl = jnp.zeros((d, LANES), F32).at[:, :n_fox].set(w_in[l][:, qkv_w:qkv_w + n_fox])
        qv_t, k_hm, f_logit, a_mix = in_proj(h, g_mix[l], wi, w_fl, parts_w=wm, scale=scale * LOG2E)
        k_hm = k_hm.reshape(k_hm.shape[0], m // ATT_TILE, ATT_TILE, HEAD_DIM)
        gates = gate_proj(a_mix, wi[qkv_w + n_fox:])
        c_rep = forget_cumsum(f_logit, b_forget[l], batch=batch)

        o_a, w_ff1_16, w_mix_16 = moba_attn(qv_t, k_hm, bias, rel_bias, batch=batch, n_heads=n_heads,
                                            q_off=0, k_off=0, v_off=n_heads + n_fox,
                                            cast=(w_ff1[l], w_mix_out[l]))
        o_f, w_ff2_16 = fox_attn(qv_t, k_hm, c_rep, batch=batch, n_heads=n_fox,
                                 q_off=n_heads, k_off=n_heads, v_off=2 * n_heads + n_fox,
                                 cast=(w_ff2[l],))
        merged = gated_merge(o_a, o_f, w_branch_moba[l].astype(BF16), w_branch_fox[l].astype(BF16), gates)
        h = mm_res(merged, w_mix_16, h)

        cw = w_cq.shape[2]
        w_kv = jnp.concatenate([w_ck[l], w_cv[l]], axis=1).astype(BF16)
        kv = rms_proj(mem2, g_mem[l], w_kv, tn=2 * cw)
        h = cross_attn(h, g_cross[l], w_cq[l].astype(BF16), kv, w_co[l].astype(BF16),
                       batch=batch, n_heads=cw // HEAD_DIM)

        u = rms_proj(h, g_mlp[l], w_ff1_16, act="relu2")
        h = mm_res(u, w_ff2_16, h, g_final if l == depth - 1 else None)
    return h.reshape(batch, seq, d)
```

```python
import functools
import math

import jax
import jax.numpy as jnp
from jax import lax
from jax.experimental import pallas as pl
from jax.experimental.pallas import tpu as pltpu

F32 = jnp.float32
BF16 = jnp.bfloat16

HEAD_DIM = 128
MOBA_BLOCK = 256
MOBA_TOP_K = 3
NUM_BUCKETS = 32
MAX_DISTANCE = 1024
RMS_EPS = 1e-6
LOG2E = math.log2(math.e)
NEG_INF = -1e30
LANES = 128
ATT_TILE = 256
TILE = 2 * ATT_TILE
AUG = 256
HEADS_PER_STEP = 2
LOOP_UNROLL = 4
UNDERFLOW_LOG2 = 160.0
NEAR_BLOCKS = 5
VMEM_LIMIT = 56 * 1024 * 1024
ROW_TILE = 1024
COL_TILE = 1024


def _bucket_thresholds():
    max_exact = NUM_BUCKETS // 2
    thr = list(range(1, max_exact + 1))
    for k in range(max_exact + 1, NUM_BUCKETS):
        v = max_exact * (MAX_DISTANCE / max_exact) ** ((k - max_exact) / (NUM_BUCKETS - max_exact))
        n = int(math.floor(v))
        while max_exact + int(math.log(n / max_exact) / math.log(MAX_DISTANCE / max_exact)
                              * (NUM_BUCKETS - max_exact)) < k:
            n += 1
        thr.append(n)
    return tuple(thr)


BUCKET_THRESHOLDS = _bucket_thresholds()
assert (NEAR_BLOCKS - 1) * MOBA_BLOCK + 1 >= BUCKET_THRESHOLDS[-1]


def _params(*sem):
    return pltpu.CompilerParams(dimension_semantics=sem, vmem_limit_bytes=VMEM_LIMIT)


def _bf16_dot(a, b):
    return jnp.dot(a.astype(BF16), b.astype(BF16), preferred_element_type=F32)


def _rms_proj_kernel(x_ref, g_ref, w_ref, o_ref, a_ref, *, act):
    j = pl.program_id(1)

    @pl.when(j == 0)
    def _():
        x = x_ref[...]
        ms = jnp.mean(x * x, axis=-1, keepdims=True)
        a_ref[...] = (x * lax.rsqrt(ms + RMS_EPS) * g_ref[...]).astype(BF16)

    acc = jnp.dot(a_ref[...], w_ref[...], preferred_element_type=F32)
    if act == "relu2":
        acc = jnp.square(jnp.maximum(acc, 0.0))
    o_ref[...] = acc.astype(o_ref.dtype)


def rms_proj(x, g, w, *, act=None, tm=ROW_TILE, tn=COL_TILE):
    m, d = x.shape
    n = w.shape[1]
    tm, tn = min(tm, m), min(tn, n)
    return pl.pallas_call(
        functools.partial(_rms_proj_kernel, act=act),
        out_shape=jax.ShapeDtypeStruct((m, n), BF16),
        grid=(m // tm, n // tn),
        in_specs=[pl.BlockSpec((tm, d), lambda i, j: (i, 0)),
                  pl.BlockSpec((1, d), lambda i, j: (0, 0)),
                  pl.BlockSpec((d, tn), lambda i, j: (0, j))],
        out_specs=pl.BlockSpec((tm, tn), lambda i, j: (i, j)),
        scratch_shapes=[pltpu.VMEM((tm, d), BF16)],
        compiler_params=_params("parallel", "arbitrary"),
        name="rms_proj",
    )(x, g.reshape(1, d), w)


def _gate_proj_kernel(a_ref, w_ref, o_ref):
    acc = lax.dot_general(a_ref[...], w_ref[...], (((1,), (1,)), ((), ())),
                          preferred_element_type=F32)
    o_ref[...] = jax.nn.sigmoid(acc).astype(o_ref.dtype)


def gate_proj(a, w_rows, *, tm=ROW_TILE, tn=2 * COL_TILE):
    m, d = a.shape
    n = w_rows.shape[0]
    tn = min(tn, n)
    return pl.pallas_call(
        _gate_proj_kernel,
        out_shape=jax.ShapeDtypeStruct((m, n), BF16),
        grid=(m // tm, n // tn),
        in_specs=[pl.BlockSpec((tm, d), lambda i, j: (i, 0)),
                  pl.BlockSpec((tn, d), lambda i, j: (j, 0))],
        out_specs=pl.BlockSpec((tm, tn), lambda i, j: (i, j)),
        compiler_params=_params("parallel", "parallel"),
        name="gate_proj",
    )(a, w_rows)


def _in_proj_kernel(x_ref, g_ref, w_ref, wf_ref, qv_ref, k_ref, f_ref, a_ref, *, n_q, n_qv, scale):
    j = pl.program_id(1)

    @pl.when(j == 0)
    def _():
        x = x_ref[...]
        ms = jnp.mean(x * x, axis=-1, keepdims=True)
        a_ref[...] = (x * lax.rsqrt(ms + RMS_EPS) * g_ref[...]).astype(BF16)
        w_hi, w_lo, _ = _split3(wf_ref[...])
        both = _bf16_dot(a_ref[...], jnp.concatenate([w_hi, w_lo], axis=1))
        f_ref[...] = both[:, :LANES] + both[:, LANES:]

    def project():
        return lax.dot_general(a_ref[...], w_ref[...], (((1,), (1,)), ((), ())),
                               preferred_element_type=F32)

    @pl.when(j < n_qv)
    def _():
        acc = project() * jnp.where(j < n_q, F32(scale), F32(1.0))
        tm, tn = acc.shape
        for c in range(tn // LANES):
            for r in range(tm // ATT_TILE):
                blk = acc[r * ATT_TILE:(r + 1) * ATT_TILE, c * LANES:(c + 1) * LANES]
                qv_ref[c, r] = blk.T.astype(qv_ref.dtype)

    @pl.when(j >= n_qv)
    def _():
        acc = project()
        for c in range(acc.shape[1] // LANES):
            k_ref[c] = acc[:, c * LANES:(c + 1) * LANES].astype(k_ref.dtype)


def in_proj(x, g, w_rows, w_f, *, parts_w, scale, tm=ROW_TILE, tn=COL_TILE):
    m, d = x.shape
    per = parts_w // tn

    def tiles(*parts):
        return [p * per + t for p in parts for t in range(per)]

    order = tiles(0, 3, 2, 5) + tiles(1, 4)
    n_q, n_qv, n_k = 2 * per, 4 * per, 2 * per

    def w_tile(j):
        idx = order[-1]
        for t, src in reversed(list(enumerate(order[:-1]))):
            idx = jnp.where(j == t, src, idx)
        return idx

    hb = tn // LANES
    return pl.pallas_call(
        functools.partial(_in_proj_kernel, n_q=n_q, n_qv=n_qv, scale=scale),
        out_shape=(jax.ShapeDtypeStruct((n_qv * hb, m // ATT_TILE, LANES, ATT_TILE), BF16),
                   jax.ShapeDtypeStruct((n_k * hb, m, LANES), BF16),
                   jax.ShapeDtypeStruct((m, LANES), F32),
                   jax.ShapeDtypeStruct((m, d), BF16)),
        grid=(m // tm, len(order)),
        in_specs=[pl.BlockSpec((tm, d), lambda i, j: (i, 0)),
                  pl.BlockSpec((1, d), lambda i, j: (0, 0)),
                  pl.BlockSpec((tn, d), lambda i, j: (w_tile(j), 0)),
                  pl.BlockSpec((d, LANES), lambda i, j: (0, 0))],
        out_specs=(pl.BlockSpec((hb, tm // ATT_TILE, LANES, ATT_TILE),
                                lambda i, j: (jnp.minimum(j, n_qv - 1), i, 0, 0)),
                   pl.BlockSpec((hb, tm, LANES),
                                lambda i, j: (jnp.clip(j - n_qv, 0, n_k - 1), i, 0)),
                   pl.BlockSpec((tm, LANES), lambda i, j: (i, 0)),
                   pl.BlockSpec((tm, d), lambda i, j: (i, 0))),
        compiler_params=_params("parallel", "arbitrary"),
        name="in_proj",
    )(x, g.reshape(1, d), w_rows, w_f)


def _forget_cumsum_kernel(f_ref, b_ref, o_ref, carry_ref):
    t = pl.program_id(1)

    @pl.when(t == 0)
    def _():
        carry_ref[...] = jnp.zeros_like(carry_ref)

    f = f_ref[...] + b_ref[...]
    tm = f.shape[0]
    logf = jnp.minimum(f, 0.0) - jnp.log1p(jnp.exp(-jnp.abs(f)))
    logf = logf * LOG2E
    row = lax.broadcasted_iota(jnp.int32, (tm, tm), 0)
    col = lax.broadcasted_iota(jnp.int32, (tm, tm), 1)
    tri = jnp.where(col <= row, 1.0, 0.0).astype(BF16)
    hi, mid, lo = _split3(logf)
    c = (_bf16_dot(tri, hi) + (_bf16_dot(tri, mid) + _bf16_dot(tri, lo))) + carry_ref[0:1, :]
    o_ref[...] = c
    carry_ref[...] = jnp.broadcast_to(c[tm - 1:tm, :], carry_ref.shape)


def forget_cumsum(f, b_f, *, batch, tm=512):
    m = f.shape[0]
    nt = (m // batch) // tm
    b_pad = jnp.zeros((1, LANES), F32).at[0, :b_f.shape[0]].set(b_f)
    return pl.pallas_call(
        _forget_cumsum_kernel,
        out_shape=jax.ShapeDtypeStruct((m, LANES), F32),
        grid=(batch, nt),
        in_specs=[pl.BlockSpec((tm, LANES), lambda b, t: (b * nt + t, 0)),
                  pl.BlockSpec((1, LANES), lambda b, t: (0, 0))],
        out_specs=pl.BlockSpec((tm, LANES), lambda b, t: (b * nt + t, 0)),
        scratch_shapes=[pltpu.VMEM((8, LANES), F32)],
        compiler_params=_params("parallel", "arbitrary"),
        name="forget_cumsum",
    )(f, b_pad)


def _moba_bias_kernel(tab_ref, o_ref):
    h = pl.program_id(0)
    key = lax.broadcasted_iota(jnp.int32, (ATT_TILE, ATT_TILE), 0)
    qry = lax.broadcasted_iota(jnp.int32, (ATT_TILE, ATT_TILE), 1)
    far = tab_ref[h, NUM_BUCKETS - 1] * LOG2E

    def block(delta):
        dist = delta * MOBA_BLOCK + qry - key
        val = jnp.full((ATT_TILE, ATT_TILE), tab_ref[h, 0], F32)
        for k in range(1, NUM_BUCKETS):
            val = jnp.where(dist >= BUCKET_THRESHOLDS[k - 1], tab_ref[h, k], val)
        val = val * LOG2E
        if delta == 0:
            val = jnp.where(dist >= 0, val, NEG_INF)
        return val

    t = [block(delta) for delta in range(NEAR_BLOCKS + 1)]
    a = ATT_TILE
    o_ref[0, 0, 0:a, 0:a] = t[0]
    o_ref[0, 0, 0:a, a:2 * a] = t[1]
    o_ref[0, 0, a:2 * a, 0:a] = jnp.full((a, a), NEG_INF, F32)
    o_ref[0, 0, a:2 * a, a:2 * a] = t[0]
    for d in (1, 2):
        o_ref[0, d, 0:a, 0:a] = t[2 * d] - far
        o_ref[0, d, 0:a, a:2 * a] = t[2 * d + 1] - far
        o_ref[0, d, a:2 * a, 0:a] = t[2 * d - 1] - far
        o_ref[0, d, a:2 * a, a:2 * a] = t[2 * d] - far


def moba_bias(rel_bias):
    h = rel_bias.shape[0]
    return pl.pallas_call(
        _moba_bias_kernel,
        out_shape=jax.ShapeDtypeStruct((h, 3, TILE, TILE), F32),
        grid=(h,),
        in_specs=[pl.BlockSpec(memory_space=pltpu.SMEM)],
        out_specs=pl.BlockSpec((1, 3, TILE, TILE), lambda i: (i, 0, 0, 0)),
        compiler_params=_params("parallel"),
        name="moba_bias",
    )(rel_bias.astype(F32))


class _Softmax:
    def __init__(self, hh, s_ref, m_ref, l_ref, acc_ref, load_values):
        self.s_ref, self.m_ref, self.l_ref, self.acc_ref = s_ref.at[hh], m_ref.at[hh], l_ref.at[hh], acc_ref.at[hh]
        self.load_values = load_values

    def reset(self):
        self.m_ref[...] = jnp.full(self.m_ref.shape, NEG_INF, F32)
        self.l_ref[...] = jnp.zeros(self.l_ref.shape, F32)
        self.acc_ref[...] = jnp.zeros(self.acc_ref.shape, F32)

    def fold(self, slot, tile, mask=None):
        for half in range(2):
            self.fold_half(slot, tile, half, mask)

    def fold_half(self, slot, tile, half, mask=None):
        width = self.m_ref.shape[-1] // 2
        cols = slice(half * width, (half + 1) * width)
        s = self.s_ref[slot, :, cols]
        if mask is not None:
            s = jnp.where(mask[:, cols], s, NEG_INF)
        m = self.m_ref[:, cols]
        m_new = jnp.maximum(m, jnp.max(s, axis=0, keepdims=True))
        alpha = jnp.exp2(m - m_new)
        p = jnp.exp2(s - m_new)
        self.acc_ref[:, cols] = alpha * self.acc_ref[:, cols] + jnp.dot(
            self.load_values(tile), p.astype(BF16), preferred_element_type=F32)
        self.l_ref[:, cols] = alpha * self.l_ref[:, cols] + jnp.sum(p, axis=0, keepdims=True)
        self.m_ref[:, cols] = m_new

    def result(self):
        return self.acc_ref[...] * (1.0 / self.l_ref[...])


def _softmax_scratch(t):
    nh = HEADS_PER_STEP
    return [pltpu.VMEM((nh, 2, t, t), F32),
            pltpu.VMEM((nh, 1, t), F32),
            pltpu.VMEM((nh, 1, t), F32),
            pltpu.VMEM((nh, HEAD_DIM, t), F32)]


def _fold_tile_run(heads, s_ref, logits, fold, base, count, last_tile):
    def step(r, slot, look_ahead=True):
        for hh in heads:
            if look_ahead:
                s_ref[hh, 1 - slot] = logits[hh](jnp.minimum(base + r + 1, last_tile))
            fold[hh].fold(slot, base + r)

    def unrolled_body(p, carry):
        for u in range(LOOP_UNROLL):
            step(LOOP_UNROLL * p + u, (u + 1) % 2)
        return carry

    lax.fori_loop(0, count // LOOP_UNROLL, unrolled_body, 0)
    rem = count % LOOP_UNROLL
    done = count - rem

    @pl.when(rem >= 2)
    def _():
        step(done, 1)
        step(done + 1, 0)

    @pl.when(rem % 2 == 1)
    def _():
        step(count - 1, 1, look_ahead=False)


def _top_k_bias(gate, eligible, blk):
    lowest = float(jnp.finfo(F32).min)
    blk_f = blk.astype(F32)
    g = jnp.where(eligible, gate, NEG_INF)
    bias = jnp.full(gate.shape, NEG_INF, F32)
    for _ in range(MOBA_TOP_K):
        best = jnp.max(g, axis=0, keepdims=True)
        first = jnp.min(jnp.where(g == best, blk_f, float(gate.shape[0])), axis=0, keepdims=True)
        hit = blk_f == first
        bias = jnp.where(hit, 0.0, bias)
        g = jnp.where(hit, lowest, g)
    return jnp.where(eligible, bias, NEG_INF)


def _split3(x):
    hi = x.astype(BF16).astype(F32)
    mid = (x - hi).astype(BF16).astype(F32)
    lo = (x - hi - mid).astype(BF16).astype(F32)
    return hi, mid, lo


def _slab_cast_specs(weights, n_steps, step_index):
    in_specs, out_specs, out_shapes = [], [], []
    for w in weights:
        if isinstance(w, tuple):
            w, first_row, n_rows = w
        else:
            first_row, n_rows = 0, w.shape[0]
        rows = n_rows // n_steps
        assert rows * n_steps == n_rows and rows % 16 == 0
        if first_row == 0:
            in_specs.append(pl.BlockSpec((rows, w.shape[1]), lambda *g: (step_index(*g), 0)))
        else:
            in_specs.append(pl.BlockSpec(
                (pl.Element(rows), pl.Element(w.shape[1])),
                lambda *g, first_row=first_row, rows=rows: (
                    pl.multiple_of(first_row + step_index(*g) * rows, 8), 0)))
        out_specs.append(pl.BlockSpec((rows, w.shape[1]), lambda *g: (step_index(*g), 0)))
        out_shapes.append(jax.ShapeDtypeStruct((n_rows, w.shape[1]), BF16))
    return in_specs, out_specs, out_shapes


def _moba_attn_kernel(*refs, n_blocks, n_cast):
    tab_ref, q_ref, k_ref, v_ref, bias_ref = refs[:5]
    o_ref = refs[5 + n_cast]
    kaug_ref, kbar_ref, kb3_ref, s_ref, m_ref, l_ref, acc_ref = refs[6 + 2 * n_cast:]
    for src, dst in zip(refs[5:5 + n_cast], refs[6 + n_cast:6 + 2 * n_cast]):
        dst[...] = src[...].astype(dst.dtype)
    _moba_attn_body(tab_ref, q_ref, k_ref, v_ref, bias_ref, o_ref,
                    kaug_ref, kbar_ref, kb3_ref, s_ref, m_ref, l_ref, acc_ref, n_blocks=n_blocks)


def _moba_attn_body(tab_ref, q_ref, k_ref, v_ref, bias_ref, o_ref,
                    kaug_ref, kbar_ref, kb3_ref, s_ref, m_ref, l_ref, acc_ref, *, n_blocks):
    hp = pl.program_id(1)
    i = pl.program_id(2)
    n_tiles = n_blocks // 2
    heads = range(HEADS_PER_STEP)

    @pl.when(i == 0)
    def _():
        lane = lax.broadcasted_iota(jnp.int32, (TILE, AUG - HEAD_DIM), 1)
        row = lax.broadcasted_iota(jnp.int32, (TILE, AUG - HEAD_DIM), 0)
        ones_lane = jnp.where(lane == n_blocks, 1.0, jnp.where(lane == n_blocks + 1, 1.0, 0.0))
        for hh in heads:
            for n in range(n_blocks):
                kbar_ref[hh, n:n + 1, :] = jnp.sum(k_ref[hh, n].astype(F32), axis=0, keepdims=True)
            hi, mid, lo = _split3(kbar_ref[hh] * (1.0 / MOBA_BLOCK))
            kb3_ref[hh, 0:n_blocks, :] = hi.astype(BF16)
            kb3_ref[hh, n_blocks:2 * n_blocks, :] = mid.astype(BF16)
            kb3_ref[hh, 2 * n_blocks:3 * n_blocks, :] = lo.astype(BF16)
            for j in range(n_tiles):
                blk_of_row = jnp.where(row < ATT_TILE, 2 * j, 2 * j + 1)
                right = jnp.where(lane == blk_of_row, 1.0, ones_lane).astype(BF16)
                kaug_ref[hh, j] = jnp.concatenate(
                    [k_ref[hh, 2 * j:2 * j + 2].reshape(TILE, HEAD_DIM), right], axis=1)

    blk = lax.broadcasted_iota(jnp.int32, (n_blocks, TILE), 0)
    qlane = lax.broadcasted_iota(jnp.int32, (n_blocks, TILE), 1)
    own = 2 * i + jnp.where(qlane >= ATT_TILE, 1, 0)
    eligible = blk < own
    r16 = lax.broadcasted_iota(jnp.int32, (16, TILE), 0)

    j1 = jnp.where(i >= 1, i - 1, i + 1)
    j2 = jnp.where(i >= 2, i - 2, i + 1)
    n_far = jnp.maximum(i - 2, 0)

    far_logits, fold = [], []
    for hh in heads:
        qt = jnp.concatenate([q_ref[hh, 0], q_ref[hh, 1]], axis=1)
        g3 = jnp.dot(kb3_ref[hh], qt, preferred_element_type=F32)
        gate = g3[0:n_blocks] + g3[n_blocks:2 * n_blocks] + g3[2 * n_blocks:3 * n_blocks]
        selb = _top_k_bias(gate, eligible, blk)

        far = jnp.full((16, TILE), tab_ref[hp * HEADS_PER_STEP + hh, NUM_BUCKETS - 1] * LOG2E, F32)
        far_hi = far.astype(BF16).astype(F32)
        far_rows = jnp.where(r16 == 0, far_hi, jnp.where(r16 == 1, far - far_hi, 0.0))
        q_aug = jnp.concatenate(
            [qt, selb.astype(BF16), far_rows.astype(BF16),
             jnp.zeros((AUG - HEAD_DIM - n_blocks - 16, TILE), BF16)], axis=0)

        def values(j, hh=hh):
            return jnp.concatenate([v_ref[hh, 2 * j], v_ref[hh, 2 * j + 1]], axis=1)

        def head_far_logits(j, hh=hh, q_aug=q_aug):
            return jnp.dot(kaug_ref[hh, j], q_aug, preferred_element_type=F32)

        sm = _Softmax(hh, s_ref, m_ref, l_ref, acc_ref, values)
        far_logits.append(head_far_logits)
        fold.append(sm)

        sel_own = jnp.sum(jnp.where(blk == 2 * i, selb, 0.0), axis=0, keepdims=True)
        sel_own = jnp.where(qlane[0:1] >= ATT_TILE, sel_own, 0.0)
        kd = k_ref[hh, pl.ds(2 * i, 2)].reshape(TILE, HEAD_DIM)
        sd = jnp.dot(kd, qt, preferred_element_type=F32) + bias_ref[hh, 0]
        s_ref[hh, 0, 0:ATT_TILE, :] = sd[0:ATT_TILE] + sel_own
        s_ref[hh, 0, ATT_TILE:TILE, :] = sd[ATT_TILE:TILE]
        sm.reset()

    for hh in heads:
        s_ref[hh, 1] = far_logits[hh](j1) + bias_ref[hh, 1]
        fold[hh].fold(0, i)
    for hh in heads:
        s_ref[hh, 0] = far_logits[hh](j2) + bias_ref[hh, 2]
        fold[hh].fold(1, j1)
    for hh in heads:
        s_ref[hh, 1] = far_logits[hh](0)
        fold[hh].fold(0, j2)

    _fold_tile_run(heads, s_ref, far_logits, fold, 0, n_far, n_tiles - 1)

    for hh in heads:
        o_ref[:, hh * HEAD_DIM:(hh + 1) * HEAD_DIM] = fold[hh].result().T.astype(o_ref.dtype)


def moba_attn(qv_t, k, bias, rel_bias, *, batch, n_heads, q_off, k_off, v_off, cast=()):
    mb = k.shape[1]
    nb = mb // batch
    nt = nb // 2
    m = mb * ATT_TILE
    hb = HEADS_PER_STEP
    nhp = n_heads // hb
    assert nb + 16 <= AUG - HEAD_DIM and nt >= 4
    assert n_heads % hb == 0 and q_off % hb == 0 and k_off % hb == 0 and v_off % hb == 0
    c_in, c_out, c_shape = _slab_cast_specs(cast, batch * nhp * nt, lambda b, h, i: (b * nhp + h) * nt + i)
    return pl.pallas_call(
        functools.partial(_moba_attn_kernel, n_blocks=nb, n_cast=len(cast)),
        out_shape=[jax.ShapeDtypeStruct((m, n_heads * HEAD_DIM), BF16)] + c_shape,
        grid=(batch, nhp, nt),
        in_specs=[pl.BlockSpec(memory_space=pltpu.SMEM),
                  pl.BlockSpec((hb, 2, HEAD_DIM, ATT_TILE),
                               lambda b, h, i: (q_off // hb + h, b * nt + i, 0, 0)),
                  pl.BlockSpec((hb, nb, ATT_TILE, HEAD_DIM), lambda b, h, i: (k_off // hb + h, b, 0, 0)),
                  pl.BlockSpec((hb, nb, HEAD_DIM, ATT_TILE), lambda b, h, i: (v_off // hb + h, b, 0, 0)),
                  pl.BlockSpec((hb, 3, TILE, TILE), lambda b, h, i: (h, 0, 0, 0))] + c_in,
        out_specs=[pl.BlockSpec((TILE, hb * HEAD_DIM), lambda b, h, i: (b * nt + i, h))] + c_out,
        scratch_shapes=[pltpu.VMEM((hb, nt, TILE, AUG), BF16),
                        pltpu.VMEM((hb, nb, HEAD_DIM), F32),
                        pltpu.VMEM((hb, 3 * nb, HEAD_DIM), BF16)] + _softmax_scratch(TILE),
        compiler_params=_params("parallel", "parallel", "arbitrary"),
        name="moba_attn",
    )(rel_bias.astype(F32), qv_t, k, qv_t, bias, *[c[0] if isinstance(c, tuple) else c for c in cast])


def _fox_attn_kernel(*refs, n_tiles, n_cast):
    q_ref, k_ref, v_ref, c_ref = refs[:4]
    o_ref = refs[4 + n_cast]
    kaug_ref, bound_ref, s_ref, m_ref, l_ref, acc_ref = refs[5 + 2 * n_cast:]
    for src, dst in zip(refs[4:4 + n_cast], refs[5 + n_cast:5 + 2 * n_cast]):
        dst[...] = src[...].astype(dst.dtype)
    _fox_attn_body(q_ref, k_ref, v_ref, c_ref, o_ref, kaug_ref, bound_ref, s_ref, m_ref, l_ref,
                   acc_ref, n_tiles=n_tiles)


def _fox_attn_body(q_ref, k_ref, v_ref, c_ref, o_ref, kaug_ref, bound_ref, s_ref, m_ref, l_ref,
                   acc_ref, *, n_tiles):
    hp = pl.program_id(1)
    i = pl.program_id(2)
    t = TILE
    heads = range(HEADS_PER_STEP)

    lane1 = lax.broadcasted_iota(jnp.int32, (1, LANES), 1)

    @pl.when(i == 0)
    def _():
        lane = lax.broadcasted_iota(jnp.int32, (t, AUG - HEAD_DIM), 1)
        for hh in heads:
            c_first = jnp.zeros((1, LANES), F32)
            c_last = jnp.zeros((1, LANES), F32)
            k_norm2 = jnp.zeros((t, 1), F32)
            for j in range(n_tiles):
                c_all = c_ref[j * t:(j + 1) * t, :]
                c = jnp.sum(jnp.where(lane == hp * HEADS_PER_STEP + hh, c_all, 0.0),
                            axis=1, keepdims=True)
                hi, mid, lo = _split3(-c)
                right = jnp.where(lane == 0, hi, jnp.where(lane == 1, mid, jnp.where(lane == 2, lo, 0.0)))
                k = k_ref[hh, 2 * j:2 * j + 2].reshape(t, HEAD_DIM)
                kaug_ref[hh, j] = jnp.concatenate([k, right.astype(BF16)], axis=1)
                kf = k.astype(F32)
                k_norm2 = jnp.maximum(k_norm2, jnp.sum(kf * kf, axis=1, keepdims=True))
                c_first = jnp.where(lane1 == j, c[0:1, :], c_first)
                c_last = jnp.where(lane1 == j, c[t - 1:t, :], c_last)
            bound_ref[hh, 0:1, :] = c_first
            bound_ref[hh, 1:2, :] = c_last
            bound_ref[hh, 2:3, :] = jnp.broadcast_to(jnp.max(k_norm2, axis=0, keepdims=True), (1, LANES))

    r_aug = lax.broadcasted_iota(jnp.int32, (AUG - HEAD_DIM, t), 0)
    ones_rows = jnp.where(r_aug < 3, 1.0, 0.0).astype(BF16)

    logits, fold, skippable = [], [], []
    for hh in heads:
        qt = jnp.concatenate([q_ref[hh, 0], q_ref[hh, 1]], axis=1)
        q_aug = jnp.concatenate([qt, ones_rows], axis=0)

        qf = qt.astype(F32)
        q_norm2 = jnp.max(jnp.sum(qf * qf, axis=0, keepdims=True), axis=1, keepdims=True)
        c_here = jnp.sum(jnp.where(lane1 == i, bound_ref[hh, 0:1, :], 0.0), axis=1, keepdims=True)
        gap = (bound_ref[hh, 1:2, :] - c_here) - UNDERFLOW_LOG2
        dead = (lane1 < i) & (gap > 0.0) & (gap * gap > 4.0 * q_norm2 * bound_ref[hh, 2:3, :])
        skippable.append(jnp.sum(jnp.where(dead, 1.0, 0.0), axis=1, keepdims=True))

        def head_logits(n, hh=hh, q_aug=q_aug):
            return jnp.dot(kaug_ref[hh, n], q_aug, preferred_element_type=F32)

        def values(n, hh=hh):
            return jnp.concatenate([v_ref[hh, 2 * n], v_ref[hh, 2 * n + 1]], axis=1)

        logits.append(head_logits)
        fold.append(_Softmax(hh, s_ref, m_ref, l_ref, acc_ref, values))

    key = lax.broadcasted_iota(jnp.int32, (t, t), 0)
    qry = lax.broadcasted_iota(jnp.int32, (t, t), 1)
    causal = key <= qry
    j0 = functools.reduce(jnp.minimum, skippable)[0, 0].astype(jnp.int32)
    n_past = i - j0
    for hh in heads:
        s_ref[hh, 0] = logits[hh](i)
        fold[hh].reset()
    for hh in heads:
        s_ref[hh, 1] = logits[hh](j0)
        fold[hh].fold(0, i, mask=causal)

    _fold_tile_run(heads, s_ref, logits, fold, j0, n_past, n_tiles - 1)

    for hh in heads:
        o_ref[:, hh * HEAD_DIM:(hh + 1) * HEAD_DIM] = fold[hh].result().T.astype(o_ref.dtype)


def fox_attn(qv_t, k, c_rep, *, batch, n_heads, q_off, k_off, v_off, cast=()):
    mb = k.shape[1]
    nb = mb // batch
    nt = nb // 2
    m = mb * ATT_TILE
    t = TILE
    hb = HEADS_PER_STEP
    nhp = n_heads // hb
    assert n_heads % hb == 0 and q_off % hb == 0 and k_off % hb == 0 and v_off % hb == 0
    assert nt <= LANES and n_heads <= LANES
    c_in, c_out, c_shape = _slab_cast_specs(cast, batch * nhp * nt, lambda b, h, i: (b * nhp + h) * nt + i)
    return pl.pallas_call(
        functools.partial(_fox_attn_kernel, n_tiles=nt, n_cast=len(cast)),
        out_shape=[jax.ShapeDtypeStruct((m, n_heads * HEAD_DIM), BF16)] + c_shape,
        grid=(batch, nhp, nt),
        in_specs=[pl.BlockSpec((hb, 2, HEAD_DIM, ATT_TILE),
                               lambda b, h, i: (q_off // hb + h, b * nt + i, 0, 0)),
                  pl.BlockSpec((hb, nb, ATT_TILE, HEAD_DIM), lambda b, h, i: (k_off // hb + h, b, 0, 0)),
                  pl.BlockSpec((hb, nb, HEAD_DIM, ATT_TILE), lambda b, h, i: (v_off // hb + h, b, 0, 0)),
                  pl.BlockSpec((nb * ATT_TILE, LANES), lambda b, h, i: (b, 0))] + c_in,
        out_specs=[pl.BlockSpec((t, hb * HEAD_DIM), lambda b, h, i: (b * nt + i, h))] + c_out,
        scratch_shapes=[pltpu.VMEM((hb, nt, t, AUG), BF16),
                        pltpu.VMEM((hb, 8, LANES), F32)] + _softmax_scratch(t),
        compiler_params=_params("parallel", "parallel", "arbitrary"),
        name="fox_attn",
    )(qv_t, k, qv_t, c_rep, *[c[0] if isinstance(c, tuple) else c for c in cast])


def _gated_merge_kernel(oa_ref, of_ref, wa_ref, wf_ref, ga_ref, gf_ref, o_ref):
    ua = jnp.dot(oa_ref[...], wa_ref[...], preferred_element_type=F32)
    uf = jnp.dot(of_ref[...], wf_ref[...], preferred_element_type=F32)
    o_ref[...] = (ga_ref[...].astype(F32) * ua + gf_ref[...].astype(F32) * uf).astype(o_ref.dtype)


def gated_merge(o_a, o_f, w_a, w_f, gates, *, tm=ROW_TILE, tn=COL_TILE):
    m, ka = o_a.shape
    kf = o_f.shape[1]
    n = w_a.shape[1]
    nj = n // tn
    return pl.pallas_call(
        _gated_merge_kernel,
        out_shape=jax.ShapeDtypeStruct((m, n), BF16),
        grid=(m // tm, nj),
        in_specs=[pl.BlockSpec((tm, ka), lambda i, j: (i, 0)),
                  pl.BlockSpec((tm, kf), lambda i, j: (i, 0)),
                  pl.BlockSpec((ka, tn), lambda i, j: (0, j)),
                  pl.BlockSpec((kf, tn), lambda i, j: (0, j)),
                  pl.BlockSpec((tm, tn), lambda i, j: (i, j)),
                  pl.BlockSpec((tm, tn), lambda i, j: (i, nj + j))],
        out_specs=pl.BlockSpec((tm, tn), lambda i, j: (i, j)),
        compiler_params=_params("parallel", "parallel"),
        name="gated_merge",
    )(o_a, o_f, w_a, w_f, gates, gates)


def _mm_res_kernel(lhs_ref, w_ref, res_ref, g_ref, o_ref, *, final_norm):
    kk = pl.program_id(1)

    @pl.when(kk == 0)
    def _():
        o_ref[...] = res_ref[...]

    o_ref[...] += jnp.dot(lhs_ref[...], w_ref[...], preferred_element_type=F32)

    if final_norm:
        @pl.when(kk == pl.num_programs(1) - 1)
        def _():
            hres = o_ref[...]
            ms = jnp.mean(hres * hres, axis=-1, keepdims=True)
            o_ref[...] = hres * lax.rsqrt(ms + RMS_EPS) * g_ref[...]


def mm_res(lhs, w, res, g=None, *, tm=ROW_TILE, tk=1024):
    m, k = lhs.shape
    n = w.shape[1]
    tk = min(tk, k)
    final_norm = g is not None
    if g is None:
        g = jnp.ones((n,), F32)
    return pl.pallas_call(
        functools.partial(_mm_res_kernel, final_norm=final_norm),
        out_shape=jax.ShapeDtypeStruct((m, n), F32),
        grid=(m // tm, k // tk),
        in_specs=[pl.BlockSpec((tm, tk), lambda i, kk: (i, kk)),
                  pl.BlockSpec((tk, n), lambda i, kk: (kk, 0)),
                  pl.BlockSpec((tm, n), lambda i, kk: (i, 0)),
                  pl.BlockSpec((1, n), lambda i, kk: (0, 0))],
        out_specs=pl.BlockSpec((tm, n), lambda i, kk: (i, 0)),
        compiler_params=_params("parallel", "arbitrary"),
        name="mm_res",
    )(lhs, w, res, g.reshape(1, n))


def _cross_attn_kernel(h_ref, g_ref, wq_ref, kv_ref, wo_ref, o_ref, *, n_heads):
    width = n_heads * HEAD_DIM
    hres = h_ref[...]
    ms = jnp.mean(hres * hres, axis=-1, keepdims=True)
    c = (hres * lax.rsqrt(ms + RMS_EPS) * g_ref[...]).astype(BF16)
    q_all = (jnp.dot(c, wq_ref[...], preferred_element_type=F32)
             * (HEAD_DIM ** -0.5 * LOG2E)).astype(BF16)
    heads = []
    for h in range(n_heads):
        q = q_all[:, h * HEAD_DIM:(h + 1) * HEAD_DIM]
        k = kv_ref[0, :, h * HEAD_DIM:(h + 1) * HEAD_DIM]
        v = kv_ref[0, :, width + h * HEAD_DIM:width + (h + 1) * HEAD_DIM]
        s = lax.dot_general(q, k, (((1,), (1,)), ((), ())), preferred_element_type=F32)
        m = jnp.max(s, axis=-1, keepdims=True)
        p = jnp.exp2(s - m)
        l = jnp.sum(p, axis=-1, keepdims=True)
        o = jnp.dot(p.astype(BF16), v, preferred_element_type=F32) * (1.0 / l)
        heads.append(o.astype(BF16))
    o_all = jnp.concatenate(heads, axis=1)
    o_ref[...] = hres + jnp.dot(o_all, wo_ref[...], preferred_element_type=F32)


def cross_attn(h, g, w_q, kv, w_o, *, batch, n_heads, tm=512):
    m, d = h.shape
    width = w_q.shape[1]
    n_mem = kv.shape[0] // batch
    tiles_per_batch = (m // batch) // tm
    kv3 = kv.reshape(batch, n_mem, 2 * width)
    return pl.pallas_call(
        functools.partial(_cross_attn_kernel, n_heads=n_heads),
        out_shape=jax.ShapeDtypeStruct((m, d), F32),
        grid=(m // tm,),
        in_specs=[pl.BlockSpec((tm, d), lambda i: (i, 0)),
                  pl.BlockSpec((1, d), lambda i: (0, 0)),
                  pl.BlockSpec((d, width), lambda i: (0, 0)),
                  pl.BlockSpec((1, n_mem, 2 * width), lambda i: (i // tiles_per_batch, 0, 0)),
                  pl.BlockSpec((width, d), lambda i: (0, 0))],
        out_specs=pl.BlockSpec((tm, d), lambda i: (i, 0)),
        compiler_params=_params("parallel"),
        name="cross_attn",
    )(h, g.reshape(1, d), w_q, kv3, w_o)


def kernel(x, mem, g_mix, w_in, b_forget, w_branch_moba, w_branch_fox, w_mix_out, rel_bias,
           g_cross, g_mem, w_cq, w_ck, w_cv, w_co, g_mlp, w_ff1, w_ff2, g_final):
    batch, seq, d = x.shape
    depth = w_in.shape[0]
    n_heads = rel_bias.shape[0]
    n_fox = b_forget.shape[1]
    wm = n_heads * HEAD_DIM
    wf = n_fox * HEAD_DIM
    m = batch * seq
    assert wm == wf and wm % COL_TILE == 0
    scale = HEAD_DIM ** -0.5
    mem2 = mem.reshape(-1, d)

    bias = moba_bias(rel_bias)
    h = x.reshape(m, d)
    for l in range(depth):
        wt = jnp.swapaxes(w_in[l], 0, 1)
        qkv_w = 3 * (wm + wf)
        w_fl = jnp.zeros((d, LANES), F32).at[:, :n_fox].set(w_in[l][:, qkv_w:qkv_w + n_fox])
        qv_t, k_hm, f_logit, a_mix = in_proj(h, g_mix[l], wt[:qkv_w].astype(BF16), w_fl,
                                             parts_w=wm, scale=scale * LOG2E)
        k_hm = k_hm.reshape(k_hm.shape[0], m // ATT_TILE, ATT_TILE, HEAD_DIM)
        c_rep = forget_cumsum(f_logit, b_forget[l], batch=batch)

        o_a, w_ff1_16, w_mix_16 = moba_attn(qv_t, k_hm, bias, rel_bias, batch=batch, n_heads=n_heads,
                                            q_off=0, k_off=0, v_off=n_heads + n_fox,
                                            cast=(w_ff1[l], w_mix_out[l]))
        o_f, w_ff2_16, w_g16 = fox_attn(qv_t, k_hm, c_rep, batch=batch, n_heads=n_fox,
                                        q_off=n_heads, k_off=n_heads, v_off=2 * n_heads + n_fox,
                                        cast=(w_ff2[l], (wt, qkv_w + n_fox, 2 * d)))
        gates = gate_proj(a_mix, w_g16)
        merged = gated_merge(o_a, o_f, w_branch_moba[l].astype(BF16), w_branch_fox[l].astype(BF16), gates)
        h = mm_res(merged, w_mix_16, h)

        cw = w_cq.shape[2]
        w_kv = jnp.concatenate([w_ck[l], w_cv[l]], axis=1).astype(BF16)
        kv = rms_proj(mem2, g_mem[l], w_kv, tn=2 * cw)
        h = cross_attn(h, g_cross[l], w_cq[l].astype(BF16), kv, w_co[l].astype(BF16),
                       batch=batch, n_heads=cw // HEAD_DIM)

        u = rms_proj(h, g_mlp[l], w_ff1_16, act="relu2")
        h = mm_res(u, w_ff2_16, h, g_final if l == depth - 1 else None)
    return h.reshape(batch, seq, d)
```

```python
import functools
import math

import jax
import jax.numpy as jnp
from jax import lax
from jax.experimental import pallas as pl
from jax.experimental.pallas import tpu as pltpu

F32 = jnp.float32
BF16 = jnp.bfloat16

HEAD_DIM = 128
MOBA_BLOCK = 256
MOBA_TOP_K = 3
NUM_BUCKETS = 32
MAX_DISTANCE = 1024
RMS_EPS = 1e-6
LOG2E = math.log2(math.e)
NEG_INF = -1e30
LANES = 128
ATT_TILE = 256
TILE = 2 * ATT_TILE
AUG = 256
HEADS_PER_STEP = 2
LOOP_UNROLL = 4
UNDERFLOW_LOG2 = 160.0
NEAR_BLOCKS = 5
VMEM_LIMIT = 56 * 1024 * 1024
ROW_TILE = 1024
COL_TILE = 1024


def _bucket_thresholds():
    max_exact = NUM_BUCKETS // 2
    thr = list(range(1, max_exact + 1))
    for k in range(max_exact + 1, NUM_BUCKETS):
        v = max_exact * (MAX_DISTANCE / max_exact) ** ((k - max_exact) / (NUM_BUCKETS - max_exact))
        n = int(math.floor(v))
        while max_exact + int(math.log(n / max_exact) / math.log(MAX_DISTANCE / max_exact)
                              * (NUM_BUCKETS - max_exact)) < k:
            n += 1
        thr.append(n)
    return tuple(thr)


BUCKET_THRESHOLDS = _bucket_thresholds()
assert (NEAR_BLOCKS - 1) * MOBA_BLOCK + 1 >= BUCKET_THRESHOLDS[-1]


def _params(*sem):
    return pltpu.CompilerParams(dimension_semantics=sem, vmem_limit_bytes=VMEM_LIMIT)


def _bf16_dot(a, b):
    return jnp.dot(a.astype(BF16), b.astype(BF16), preferred_element_type=F32)


def _rms_proj_kernel(x_ref, g_ref, w_ref, o_ref, a_ref, *, act):
    j = pl.program_id(1)

    @pl.when(j == 0)
    def _():
        x = x_ref[...]
        ms = jnp.mean(x * x, axis=-1, keepdims=True)
        a_ref[...] = (x * lax.rsqrt(ms + RMS_EPS) * g_ref[...]).astype(BF16)

    acc = jnp.dot(a_ref[...], w_ref[...], preferred_element_type=F32)
    if act == "relu2":
        acc = jnp.square(jnp.maximum(acc, 0.0))
    o_ref[...] = acc.astype(o_ref.dtype)


def rms_proj(x, g, w, *, act=None, tm=ROW_TILE, tn=COL_TILE):
    m, d = x.shape
    n = w.shape[1]
    tm, tn = min(tm, m), min(tn, n)
    return pl.pallas_call(
        functools.partial(_rms_proj_kernel, act=act),
        out_shape=jax.ShapeDtypeStruct((m, n), BF16),
        grid=(m // tm, n // tn),
        in_specs=[pl.BlockSpec((tm, d), lambda i, j: (i, 0)),
                  pl.BlockSpec((1, d), lambda i, j: (0, 0)),
                  pl.BlockSpec((d, tn), lambda i, j: (0, j))],
        out_specs=pl.BlockSpec((tm, tn), lambda i, j: (i, j)),
        scratch_shapes=[pltpu.VMEM((tm, d), BF16)],
        compiler_params=_params("parallel", "arbitrary"),
        name="rms_proj",
    )(x, g.reshape(1, d), w)


def _gate_proj_kernel(a_ref, w_ref, o_ref):
    acc = lax.dot_general(a_ref[...], w_ref[...], (((1,), (1,)), ((), ())),
                          preferred_element_type=F32)
    o_ref[...] = jax.nn.sigmoid(acc).astype(o_ref.dtype)


def gate_proj(a, w_rows, *, tm=ROW_TILE, tn=2 * COL_TILE):
    m, d = a.shape
    n = w_rows.shape[0]
    tn = min(tn, n)
    return pl.pallas_call(
        _gate_proj_kernel,
        out_shape=jax.ShapeDtypeStruct((m, n), BF16),
        grid=(m // tm, n // tn),
        in_specs=[pl.BlockSpec((tm, d), lambda i, j: (i, 0)),
                  pl.BlockSpec((tn, d), lambda i, j: (j, 0))],
        out_specs=pl.BlockSpec((tm, tn), lambda i, j: (i, j)),
        compiler_params=_params("parallel", "parallel"),
        name="gate_proj",
    )(a, w_rows)


def _in_proj_kernel(x_ref, g_ref, w_ref, wf_ref, qv_ref, k_ref, f_ref, a_ref, *, n_q, n_qv, scale):
    j = pl.program_id(1)

    @pl.when(j == 0)
    def _():
        x = x_ref[...]
        ms = jnp.mean(x * x, axis=-1, keepdims=True)
        a_ref[...] = (x * lax.rsqrt(ms + RMS_EPS) * g_ref[...]).astype(BF16)
        w_hi, w_lo, _ = _split3(wf_ref[...])
        both = _bf16_dot(a_ref[...], jnp.concatenate([w_hi, w_lo], axis=1))
        f_ref[...] = both[:, :LANES] + both[:, LANES:]

    def project():
        return lax.dot_general(a_ref[...], w_ref[...], (((1,), (1,)), ((), ())),
                               preferred_element_type=F32)

    @pl.when(j < n_qv)
    def _():
        acc = project() * jnp.where(j < n_q, F32(scale), F32(1.0))
        tm, tn = acc.shape
        for c in range(tn // LANES):
            for r in range(tm // ATT_TILE):
                blk = acc[r * ATT_TILE:(r + 1) * ATT_TILE, c * LANES:(c + 1) * LANES]
                qv_ref[c, r] = blk.T.astype(qv_ref.dtype)

    @pl.when(j >= n_qv)
    def _():
        acc = project()
        for c in range(acc.shape[1] // LANES):
            k_ref[c] = acc[:, c * LANES:(c + 1) * LANES].astype(k_ref.dtype)


def in_proj(x, g, w_rows, w_f, *, parts_w, scale, tm=ROW_TILE, tn=COL_TILE):
    m, d = x.shape
    per = parts_w // tn

    def tiles(*parts):
        return [p * per + t for p in parts for t in range(per)]

    order = tiles(0, 3, 2, 5) + tiles(1, 4)
    n_q, n_qv, n_k = 2 * per, 4 * per, 2 * per

    def w_tile(j):
        idx = order[-1]
        for t, src in reversed(list(enumerate(order[:-1]))):
            idx = jnp.where(j == t, src, idx)
        return idx

    hb = tn // LANES
    return pl.pallas_call(
        functools.partial(_in_proj_kernel, n_q=n_q, n_qv=n_qv, scale=scale),
        out_shape=(jax.ShapeDtypeStruct((n_qv * hb, m // ATT_TILE, LANES, ATT_TILE), BF16),
                   jax.ShapeDtypeStruct((n_k * hb, m, LANES), BF16),
                   jax.ShapeDtypeStruct((m, LANES), F32),
                   jax.ShapeDtypeStruct((m, d), BF16)),
        grid=(m // tm, len(order)),
        in_specs=[pl.BlockSpec((tm, d), lambda i, j: (i, 0)),
                  pl.BlockSpec((1, d), lambda i, j: (0, 0)),
                  pl.BlockSpec((tn, d), lambda i, j: (w_tile(j), 0)),
                  pl.BlockSpec((d, LANES), lambda i, j: (0, 0))],
        out_specs=(pl.BlockSpec((hb, tm // ATT_TILE, LANES, ATT_TILE),
                                lambda i, j: (jnp.minimum(j, n_qv - 1), i, 0, 0)),
                   pl.BlockSpec((hb, tm, LANES),
                                lambda i, j: (jnp.clip(j - n_qv, 0, n_k - 1), i, 0)),
                   pl.BlockSpec((tm, LANES), lambda i, j: (i, 0)),
                   pl.BlockSpec((tm, d), lambda i, j: (i, 0))),
        compiler_params=_params("parallel", "arbitrary"),
        name="in_proj",
    )(x, g.reshape(1, d), w_rows, w_f)


def _forget_cumsum_kernel(f_ref, b_ref, o_ref, carry_ref):
    t = pl.program_id(1)

    @pl.when(t == 0)
    def _():
        carry_ref[...] = jnp.zeros_like(carry_ref)

    f = f_ref[...] + b_ref[...]
    tm = f.shape[0]
    logf = jnp.minimum(f, 0.0) - jnp.log1p(jnp.exp(-jnp.abs(f)))
    logf = logf * LOG2E
    row = lax.broadcasted_iota(jnp.int32, (tm, tm), 0)
    col = lax.broadcasted_iota(jnp.int32, (tm, tm), 1)
    tri = jnp.where(col <= row, 1.0, 0.0).astype(BF16)
    hi, mid, lo = _split3(logf)
    c = (_bf16_dot(tri, hi) + (_bf16_dot(tri, mid) + _bf16_dot(tri, lo))) + carry_ref[0:1, :]
    o_ref[...] = c
    carry_ref[...] = jnp.broadcast_to(c[tm - 1:tm, :], carry_ref.shape)


def forget_cumsum(f, b_f, *, batch, tm=512):
    m = f.shape[0]
    nt = (m // batch) // tm
    b_pad = jnp.zeros((1, LANES), F32).at[0, :b_f.shape[0]].set(b_f)
    return pl.pallas_call(
        _forget_cumsum_kernel,
        out_shape=jax.ShapeDtypeStruct((m, LANES), F32),
        grid=(batch, nt),
        in_specs=[pl.BlockSpec((tm, LANES), lambda b, t: (b * nt + t, 0)),
                  pl.BlockSpec((1, LANES), lambda b, t: (0, 0))],
        out_specs=pl.BlockSpec((tm, LANES), lambda b, t: (b * nt + t, 0)),
        scratch_shapes=[pltpu.VMEM((8, LANES), F32)],
        compiler_params=_params("parallel", "arbitrary"),
        name="forget_cumsum",
    )(f, b_pad)


def _moba_bias_kernel(tab_ref, o_ref):
    h = pl.program_id(0)
    key = lax.broadcasted_iota(jnp.int32, (ATT_TILE, ATT_TILE), 0)
    qry = lax.broadcasted_iota(jnp.int32, (ATT_TILE, ATT_TILE), 1)
    far = tab_ref[h, NUM_BUCKETS - 1] * LOG2E

    def block(delta):
        dist = delta * MOBA_BLOCK + qry - key
        val = jnp.full((ATT_TILE, ATT_TILE), tab_ref[h, 0], F32)
        for k in range(1, NUM_BUCKETS):
            val = jnp.where(dist >= BUCKET_THRESHOLDS[k - 1], tab_ref[h, k], val)
        val = val * LOG2E
        if delta == 0:
            val = jnp.where(dist >= 0, val, NEG_INF)
        return val

    t = [block(delta) for delta in range(NEAR_BLOCKS + 1)]
    a = ATT_TILE
    o_ref[0, 0, 0:a, 0:a] = t[0]
    o_ref[0, 0, 0:a, a:2 * a] = t[1]
    o_ref[0, 0, a:2 * a, 0:a] = jnp.full((a, a), NEG_INF, F32)
    o_ref[0, 0, a:2 * a, a:2 * a] = t[0]
    for d in (1, 2):
        o_ref[0, d, 0:a, 0:a] = t[2 * d] - far
        o_ref[0, d, 0:a, a:2 * a] = t[2 * d + 1] - far
        o_ref[0, d, a:2 * a, 0:a] = t[2 * d - 1] - far
        o_ref[0, d, a:2 * a, a:2 * a] = t[2 * d] - far


def moba_bias(rel_bias):
    h = rel_bias.shape[0]
    return pl.pallas_call(
        _moba_bias_kernel,
        out_shape=jax.ShapeDtypeStruct((h, 3, TILE, TILE), F32),
        grid=(h,),
        in_specs=[pl.BlockSpec(memory_space=pltpu.SMEM)],
        out_specs=pl.BlockSpec((1, 3, TILE, TILE), lambda i: (i, 0, 0, 0)),
        compiler_params=_params("parallel"),
        name="moba_bias",
    )(rel_bias.astype(F32))


class _Softmax:
    def __init__(self, hh, s_ref, m_ref, l_ref, acc_ref, load_values):
        self.s_ref, self.m_ref, self.l_ref, self.acc_ref = s_ref.at[hh], m_ref.at[hh], l_ref.at[hh], acc_ref.at[hh]
        self.load_values = load_values

    def reset(self):
        self.m_ref[...] = jnp.full(self.m_ref.shape, NEG_INF, F32)
        self.l_ref[...] = jnp.zeros(self.l_ref.shape, F32)
        self.acc_ref[...] = jnp.zeros(self.acc_ref.shape, F32)

    def fold(self, slot, tile, mask=None):
        for half in range(2):
            self.fold_half(slot, tile, half, mask)

    def fold_half(self, slot, tile, half, mask=None):
        width = self.m_ref.shape[-1] // 2
        cols = slice(half * width, (half + 1) * width)
        s = self.s_ref[slot, :, cols]
        if mask is not None:
            s = jnp.where(mask[:, cols], s, NEG_INF)
        m = self.m_ref[:, cols]
        m_new = jnp.maximum(m, jnp.max(s, axis=0, keepdims=True))
        alpha = jnp.exp2(m - m_new)
        p = jnp.exp2(s - m_new)
        self.acc_ref[:, cols] = alpha * self.acc_ref[:, cols] + jnp.dot(
            self.load_values(tile), p.astype(BF16), preferred_element_type=F32)
        self.l_ref[:, cols] = alpha * self.l_ref[:, cols] + jnp.sum(p, axis=0, keepdims=True)
        self.m_ref[:, cols] = m_new

    def result(self):
        return self.acc_ref[...] * (1.0 / self.l_ref[...])


def _softmax_scratch(t):
    nh = HEADS_PER_STEP
    return [pltpu.VMEM((nh, 2, t, t), F32),
            pltpu.VMEM((nh, 1, t), F32),
            pltpu.VMEM((nh, 1, t), F32),
            pltpu.VMEM((nh, HEAD_DIM, t), F32)]


def _fold_tile_run(heads, s_ref, logits, fold, base, count, last_tile):
    def step(r, slot, look_ahead=True):
        for hh in heads:
            if look_ahead:
                s_ref[hh, 1 - slot] = logits[hh](jnp.minimum(base + r + 1, last_tile))
            fold[hh].fold(slot, base + r)

    def unrolled_body(p, carry):
        for u in range(LOOP_UNROLL):
            step(LOOP_UNROLL * p + u, (u + 1) % 2)
        return carry

    lax.fori_loop(0, count // LOOP_UNROLL, unrolled_body, 0)
    rem = count % LOOP_UNROLL
    done = count - rem

    @pl.when(rem >= 2)
    def _():
        step(done, 1)
        step(done + 1, 0)

    @pl.when(rem % 2 == 1)
    def _():
        step(count - 1, 1, look_ahead=False)


def _top_k_bias(gate, eligible, blk):
    lowest = float(jnp.finfo(F32).min)
    blk_f = blk.astype(F32)
    g = jnp.where(eligible, gate, NEG_INF)
    bias = jnp.full(gate.shape, NEG_INF, F32)
    for _ in range(MOBA_TOP_K):
        best = jnp.max(g, axis=0, keepdims=True)
        first = jnp.min(jnp.where(g == best, blk_f, float(gate.shape[0])), axis=0, keepdims=True)
        hit = blk_f == first
        bias = jnp.where(hit, 0.0, bias)
        g = jnp.where(hit, lowest, g)
    return jnp.where(eligible, bias, NEG_INF)


def _split3(x):
    hi = x.astype(BF16).astype(F32)
    mid = (x - hi).astype(BF16).astype(F32)
    lo = (x - hi - mid).astype(BF16).astype(F32)
    return hi, mid, lo


def _slab_cast_specs(weights, n_steps, step_index):
    in_specs, out_specs, out_shapes = [], [], []
    for w in weights:
        if isinstance(w, tuple):
            w, first_row, n_rows = w
        else:
            first_row, n_rows = 0, w.shape[0]
        rows = max(n_rows // n_steps, 16)
        n_slabs = n_rows // rows
        assert n_slabs * rows == n_rows and rows % 16 == 0 and n_steps % n_slabs == 0
        share = n_steps // n_slabs

        def slab(*g, share=share):
            return step_index(*g) // share

        if first_row == 0:
            in_specs.append(pl.BlockSpec((rows, w.shape[1]), lambda *g, slab=slab: (slab(*g), 0)))
        else:
            in_specs.append(pl.BlockSpec(
                (pl.Element(rows), pl.Element(w.shape[1])),
                lambda *g, first_row=first_row, rows=rows, slab=slab: (
                    pl.multiple_of(first_row + slab(*g) * rows, 8), 0)))
        out_specs.append(pl.BlockSpec((rows, w.shape[1]), lambda *g, slab=slab: (slab(*g), 0)))
        out_shapes.append(jax.ShapeDtypeStruct((n_rows, w.shape[1]), BF16))
    return in_specs, out_specs, out_shapes


def _moba_attn_kernel(*refs, n_blocks, n_cast):
    tab_ref, q_ref, k_ref, v_ref, bias_ref = refs[:5]
    o_ref = refs[5 + n_cast]
    kaug_ref, kbar_ref, kb3_ref, s_ref, m_ref, l_ref, acc_ref = refs[6 + 2 * n_cast:]
    for src, dst in zip(refs[5:5 + n_cast], refs[6 + n_cast:6 + 2 * n_cast]):
        dst[...] = src[...].astype(dst.dtype)
    _moba_attn_body(tab_ref, q_ref, k_ref, v_ref, bias_ref, o_ref,
                    kaug_ref, kbar_ref, kb3_ref, s_ref, m_ref, l_ref, acc_ref, n_blocks=n_blocks)


def _moba_attn_body(tab_ref, q_ref, k_ref, v_ref, bias_ref, o_ref,
                    kaug_ref, kbar_ref, kb3_ref, s_ref, m_ref, l_ref, acc_ref, *, n_blocks):
    hp = pl.program_id(1)
    i = pl.program_id(2)
    n_tiles = n_blocks // 2
    heads = range(HEADS_PER_STEP)

    @pl.when(i == 0)
    def _():
        lane = lax.broadcasted_iota(jnp.int32, (TILE, AUG - HEAD_DIM), 1)
        row = lax.broadcasted_iota(jnp.int32, (TILE, AUG - HEAD_DIM), 0)
        ones_lane = jnp.where(lane == n_blocks, 1.0, jnp.where(lane == n_blocks + 1, 1.0, 0.0))
        for hh in heads:
            for n in range(n_blocks):
                kbar_ref[hh, n:n + 1, :] = jnp.sum(k_ref[hh, n].astype(F32), axis=0, keepdims=True)
            hi, mid, lo = _split3(kbar_ref[hh] * (1.0 / MOBA_BLOCK))
            kb3_ref[hh, 0:n_blocks, :] = hi.astype(BF16)
            kb3_ref[hh, n_blocks:2 * n_blocks, :] = mid.astype(BF16)
            kb3_ref[hh, 2 * n_blocks:3 * n_blocks, :] = lo.astype(BF16)
            for j in range(n_tiles):
                blk_of_row = jnp.where(row < ATT_TILE, 2 * j, 2 * j + 1)
                right = jnp.where(lane == blk_of_row, 1.0, ones_lane).astype(BF16)
                kaug_ref[hh, j] = jnp.concatenate(
                    [k_ref[hh, 2 * j:2 * j + 2].reshape(TILE, HEAD_DIM), right], axis=1)

    blk = lax.broadcasted_iota(jnp.int32, (n_blocks, TILE), 0)
    qlane = lax.broadcasted_iota(jnp.int32, (n_blocks, TILE), 1)
    own = 2 * i + jnp.where(qlane >= ATT_TILE, 1, 0)
    eligible = blk < own
    r16 = lax.broadcasted_iota(jnp.int32, (16, TILE), 0)

    j1 = jnp.where(i >= 1, i - 1, i + 1)
    j2 = jnp.where(i >= 2, i - 2, i + 1)
    n_far = jnp.maximum(i - 2, 0)

    far_logits, fold = [], []
    for hh in heads:
        qt = jnp.concatenate([q_ref[hh, 0], q_ref[hh, 1]], axis=1)
        g3 = jnp.dot(kb3_ref[hh], qt, preferred_element_type=F32)
        gate = g3[0:n_blocks] + g3[n_blocks:2 * n_blocks] + g3[2 * n_blocks:3 * n_blocks]
        selb = _top_k_bias(gate, eligible, blk)

        far = jnp.full((16, TILE), tab_ref[hp * HEADS_PER_STEP + hh, NUM_BUCKETS - 1] * LOG2E, F32)
        far_hi = far.astype(BF16).astype(F32)
        far_rows = jnp.where(r16 == 0, far_hi, jnp.where(r16 == 1, far - far_hi, 0.0))
        q_aug = jnp.concatenate(
            [qt, selb.astype(BF16), far_rows.astype(BF16),
             jnp.zeros((AUG - HEAD_DIM - n_blocks - 16, TILE), BF16)], axis=0)

        def values(j, hh=hh):
            return jnp.concatenate([v_ref[hh, 2 * j], v_ref[hh, 2 * j + 1]], axis=1)

        def head_far_logits(j, hh=hh, q_aug=q_aug):
            return jnp.dot(kaug_ref[hh, j], q_aug, preferred_element_type=F32)

        sm = _Softmax(hh, s_ref, m_ref, l_ref, acc_ref, values)
        far_logits.append(head_far_logits)
        fold.append(sm)

        sel_own = jnp.sum(jnp.where(blk == 2 * i, selb, 0.0), axis=0, keepdims=True)
        sel_own = jnp.where(qlane[0:1] >= ATT_TILE, sel_own, 0.0)
        kd = k_ref[hh, pl.ds(2 * i, 2)].reshape(TILE, HEAD_DIM)
        sd = jnp.dot(kd, qt, preferred_element_type=F32) + bias_ref[hh, 0]
        s_ref[hh, 0, 0:ATT_TILE, :] = sd[0:ATT_TILE] + sel_own
        s_ref[hh, 0, ATT_TILE:TILE, :] = sd[ATT_TILE:TILE]
        sm.reset()

    for hh in heads:
        s_ref[hh, 1] = far_logits[hh](j1) + bias_ref[hh, 1]
        fold[hh].fold(0, i)
    for hh in heads:
        s_ref[hh, 0] = far_logits[hh](j2) + bias_ref[hh, 2]
        fold[hh].fold(1, j1)
    for hh in heads:
        s_ref[hh, 1] = far_logits[hh](0)
        fold[hh].fold(0, j2)

    _fold_tile_run(heads, s_ref, far_logits, fold, 0, n_far, n_tiles - 1)

    for hh in heads:
        o_ref[:, hh * HEAD_DIM:(hh + 1) * HEAD_DIM] = fold[hh].result().T.astype(o_ref.dtype)


def moba_attn(qv_t, k, bias, rel_bias, *, batch, n_heads, q_off, k_off, v_off, cast=()):
    mb = k.shape[1]
    nb = mb // batch
    nt = nb // 2
    m = mb * ATT_TILE
    hb = HEADS_PER_STEP
    nhp = n_heads // hb
    assert nb + 16 <= AUG - HEAD_DIM and nt >= 4
    assert n_heads % hb == 0 and q_off % hb == 0 and k_off % hb == 0 and v_off % hb == 0
    c_in, c_out, c_shape = _slab_cast_specs(cast, batch * nhp * nt, lambda b, h, i: (b * nhp + h) * nt + i)
    return pl.pallas_call(
        functools.partial(_moba_attn_kernel, n_blocks=nb, n_cast=len(cast)),
        out_shape=[jax.ShapeDtypeStruct((m, n_heads * HEAD_DIM), BF16)] + c_shape,
        grid=(batch, nhp, nt),
        in_specs=[pl.BlockSpec(memory_space=pltpu.SMEM),
                  pl.BlockSpec((hb, 2, HEAD_DIM, ATT_TILE),
                               lambda b, h, i: (q_off // hb + h, b * nt + i, 0, 0)),
                  pl.BlockSpec((hb, nb, ATT_TILE, HEAD_DIM), lambda b, h, i: (k_off // hb + h, b, 0, 0)),
                  pl.BlockSpec((hb, nb, HEAD_DIM, ATT_TILE), lambda b, h, i: (v_off // hb + h, b, 0, 0)),
                  pl.BlockSpec((hb, 3, TILE, TILE), lambda b, h, i: (h, 0, 0, 0))] + c_in,
        out_specs=[pl.BlockSpec((TILE, hb * HEAD_DIM), lambda b, h, i: (b * nt + i, h))] + c_out,
        scratch_shapes=[pltpu.VMEM((hb, nt, TILE, AUG), BF16),
                        pltpu.VMEM((hb, nb, HEAD_DIM), F32),
                        pltpu.VMEM((hb, 3 * nb, HEAD_DIM), BF16)] + _softmax_scratch(TILE),
        compiler_params=_params("parallel", "parallel", "arbitrary"),
        name="moba_attn",
    )(rel_bias.astype(F32), qv_t, k, qv_t, bias, *[c[0] if isinstance(c, tuple) else c for c in cast])


def _fox_attn_kernel(*refs, n_tiles, n_cast):
    q_ref, k_ref, v_ref, c_ref = refs[:4]
    o_ref = refs[4 + n_cast]
    kaug_ref, bound_ref, s_ref, m_ref, l_ref, acc_ref = refs[5 + 2 * n_cast:]
    for src, dst in zip(refs[4:4 + n_cast], refs[5 + n_cast:5 + 2 * n_cast]):
        dst[...] = src[...].astype(dst.dtype)
    _fox_attn_body(q_ref, k_ref, v_ref, c_ref, o_ref, kaug_ref, bound_ref, s_ref, m_ref, l_ref,
                   acc_ref, n_tiles=n_tiles)


def _fox_attn_body(q_ref, k_ref, v_ref, c_ref, o_ref, kaug_ref, bound_ref, s_ref, m_ref, l_ref,
                   acc_ref, *, n_tiles):
    hp = pl.program_id(1)
    i = pl.program_id(2)
    t = TILE
    heads = range(HEADS_PER_STEP)

    lane1 = lax.broadcasted_iota(jnp.int32, (1, LANES), 1)

    @pl.when(i == 0)
    def _():
        lane = lax.broadcasted_iota(jnp.int32, (t, AUG - HEAD_DIM), 1)
        for hh in heads:
            c_first = jnp.zeros((1, LANES), F32)
            c_last = jnp.zeros((1, LANES), F32)
            k_norm2 = jnp.zeros((t, 1), F32)
            for j in range(n_tiles):
                c_all = c_ref[j * t:(j + 1) * t, :]
                c = jnp.sum(jnp.where(lane == hp * HEADS_PER_STEP + hh, c_all, 0.0),
                            axis=1, keepdims=True)
                hi, mid, lo = _split3(-c)
                right = jnp.where(lane == 0, hi, jnp.where(lane == 1, mid, jnp.where(lane == 2, lo, 0.0)))
                k = k_ref[hh, 2 * j:2 * j + 2].reshape(t, HEAD_DIM)
                kaug_ref[hh, j] = jnp.concatenate([k, right.astype(BF16)], axis=1)
                kf = k.astype(F32)
                k_norm2 = jnp.maximum(k_norm2, jnp.sum(kf * kf, axis=1, keepdims=True))
                c_first = jnp.where(lane1 == j, c[0:1, :], c_first)
                c_last = jnp.where(lane1 == j, c[t - 1:t, :], c_last)
            bound_ref[hh, 0:1, :] = c_first
            bound_ref[hh, 1:2, :] = c_last
            bound_ref[hh, 2:3, :] = jnp.broadcast_to(jnp.max(k_norm2, axis=0, keepdims=True), (1, LANES))

    r_aug = lax.broadcasted_iota(jnp.int32, (AUG - HEAD_DIM, t), 0)
    ones_rows = jnp.where(r_aug < 3, 1.0, 0.0).astype(BF16)

    logits, fold, skippable = [], [], []
    for hh in heads:
        qt = jnp.concatenate([q_ref[hh, 0], q_ref[hh, 1]], axis=1)
        q_aug = jnp.concatenate([qt, ones_rows], axis=0)

        qf = qt.astype(F32)
        q_norm2 = jnp.max(jnp.sum(qf * qf, axis=0, keepdims=True), axis=1, keepdims=True)
        c_here = jnp.sum(jnp.where(lane1 == i, bound_ref[hh, 0:1, :], 0.0), axis=1, keepdims=True)
        gap = (bound_ref[hh, 1:2, :] - c_here) - UNDERFLOW_LOG2
        dead = (lane1 < i) & (gap > 0.0) & (gap * gap > 4.0 * q_norm2 * bound_ref[hh, 2:3, :])
        skippable.append(jnp.sum(jnp.where(dead, 1.0, 0.0), axis=1, keepdims=True))

        def head_logits(n, hh=hh, q_aug=q_aug):
            return jnp.dot(kaug_ref[hh, n], q_aug, preferred_element_type=F32)

        def values(n, hh=hh):
            return jnp.concatenate([v_ref[hh, 2 * n], v_ref[hh, 2 * n + 1]], axis=1)

        logits.append(head_logits)
        fold.append(_Softmax(hh, s_ref, m_ref, l_ref, acc_ref, values))

    key = lax.broadcasted_iota(jnp.int32, (t, t), 0)
    qry = lax.broadcasted_iota(jnp.int32, (t, t), 1)
    causal = key <= qry
    j0 = functools.reduce(jnp.minimum, skippable)[0, 0].astype(jnp.int32)
    n_past = i - j0
    for hh in heads:
        s_ref[hh, 0] = logits[hh](i)
        fold[hh].reset()
    for hh in heads:
        s_ref[hh, 1] = logits[hh](j0)
        fold[hh].fold(0, i, mask=causal)

    _fold_tile_run(heads, s_ref, logits, fold, j0, n_past, n_tiles - 1)

    for hh in heads:
        o_ref[:, hh * HEAD_DIM:(hh + 1) * HEAD_DIM] = fold[hh].result().T.astype(o_ref.dtype)


def fox_attn(qv_t, k, c_rep, *, batch, n_heads, q_off, k_off, v_off, cast=()):
    mb = k.shape[1]
    nb = mb // batch
    nt = nb // 2
    m = mb * ATT_TILE
    t = TILE
    hb = HEADS_PER_STEP
    nhp = n_heads // hb
    assert n_heads % hb == 0 and q_off % hb == 0 and k_off % hb == 0 and v_off % hb == 0
    assert nt <= LANES and n_heads <= LANES
    c_in, c_out, c_shape = _slab_cast_specs(cast, batch * nhp * nt, lambda b, h, i: (b * nhp + h) * nt + i)
    return pl.pallas_call(
        functools.partial(_fox_attn_kernel, n_tiles=nt, n_cast=len(cast)),
        out_shape=[jax.ShapeDtypeStruct((m, n_heads * HEAD_DIM), BF16)] + c_shape,
        grid=(batch, nhp, nt),
        in_specs=[pl.BlockSpec((hb, 2, HEAD_DIM, ATT_TILE),
                               lambda b, h, i: (q_off // hb + h, b * nt + i, 0, 0)),
                  pl.BlockSpec((hb, nb, ATT_TILE, HEAD_DIM), lambda b, h, i: (k_off // hb + h, b, 0, 0)),
                  pl.BlockSpec((hb, nb, HEAD_DIM, ATT_TILE), lambda b, h, i: (v_off // hb + h, b, 0, 0)),
                  pl.BlockSpec((nb * ATT_TILE, LANES), lambda b, h, i: (b, 0))] + c_in,
        out_specs=[pl.BlockSpec((t, hb * HEAD_DIM), lambda b, h, i: (b * nt + i, h))] + c_out,
        scratch_shapes=[pltpu.VMEM((hb, nt, t, AUG), BF16),
                        pltpu.VMEM((hb, 8, LANES), F32)] + _softmax_scratch(t),
        compiler_params=_params("parallel", "parallel", "arbitrary"),
        name="fox_attn",
    )(qv_t, k, qv_t, c_rep, *[c[0] if isinstance(c, tuple) else c for c in cast])


def _gated_merge_kernel(oa_ref, of_ref, wa_ref, wf_ref, ga_ref, gf_ref, o_ref):
    ua = jnp.dot(oa_ref[...], wa_ref[...], preferred_element_type=F32)
    uf = jnp.dot(of_ref[...], wf_ref[...], preferred_element_type=F32)
    o_ref[...] = (ga_ref[...].astype(F32) * ua + gf_ref[...].astype(F32) * uf).astype(o_ref.dtype)


def gated_merge(o_a, o_f, w_a, w_f, gates, *, tm=ROW_TILE, tn=COL_TILE):
    m, ka = o_a.shape
    kf = o_f.shape[1]
    n = w_a.shape[1]
    nj = n // tn
    return pl.pallas_call(
        _gated_merge_kernel,
        out_shape=jax.ShapeDtypeStruct((m, n), BF16),
        grid=(m // tm, nj),
        in_specs=[pl.BlockSpec((tm, ka), lambda i, j: (i, 0)),
                  pl.BlockSpec((tm, kf), lambda i, j: (i, 0)),
                  pl.BlockSpec((ka, tn), lambda i, j: (0, j)),
                  pl.BlockSpec((kf, tn), lambda i, j: (0, j)),
                  pl.BlockSpec((tm, tn), lambda i, j: (i, j)),
                  pl.BlockSpec((tm, tn), lambda i, j: (i, nj + j))],
        out_specs=pl.BlockSpec((tm, tn), lambda i, j: (i, j)),
        compiler_params=_params("parallel", "parallel"),
        name="gated_merge",
    )(o_a, o_f, w_a, w_f, gates, gates)


def _mm_res_kernel(lhs_ref, w_ref, res_ref, g_ref, o_ref, *, final_norm):
    kk = pl.program_id(1)

    @pl.when(kk == 0)
    def _():
        o_ref[...] = res_ref[...]

    o_ref[...] += jnp.dot(lhs_ref[...], w_ref[...], preferred_element_type=F32)

    if final_norm:
        @pl.when(kk == pl.num_programs(1) - 1)
        def _():
            hres = o_ref[...]
            ms = jnp.mean(hres * hres, axis=-1, keepdims=True)
            o_ref[...] = hres * lax.rsqrt(ms + RMS_EPS) * g_ref[...]


def mm_res(lhs, w, res, g=None, *, tm=ROW_TILE, tk=1024):
    m, k = lhs.shape
    n = w.shape[1]
    tk = min(tk, k)
    final_norm = g is not None
    if g is None:
        g = jnp.ones((n,), F32)
    return pl.pallas_call(
        functools.partial(_mm_res_kernel, final_norm=final_norm),
        out_shape=jax.ShapeDtypeStruct((m, n), F32),
        grid=(m // tm, k // tk),
        in_specs=[pl.BlockSpec((tm, tk), lambda i, kk: (i, kk)),
                  pl.BlockSpec((tk, n), lambda i, kk: (kk, 0)),
                  pl.BlockSpec((tm, n), lambda i, kk: (i, 0)),
                  pl.BlockSpec((1, n), lambda i, kk: (0, 0))],
        out_specs=pl.BlockSpec((tm, n), lambda i, kk: (i, 0)),
        compiler_params=_params("parallel", "arbitrary"),
        name="mm_res",
    )(lhs, w, res, g.reshape(1, n))


def _cross_attn_kernel(h_ref, g_ref, wq_ref, kv_ref, wo_ref, o_ref, *, n_heads):
    width = n_heads * HEAD_DIM
    hres = h_ref[...]
    ms = jnp.mean(hres * hres, axis=-1, keepdims=True)
    c = (hres * lax.rsqrt(ms + RMS_EPS) * g_ref[...]).astype(BF16)
    q_all = (jnp.dot(c, wq_ref[...], preferred_element_type=F32)
             * (HEAD_DIM ** -0.5 * LOG2E)).astype(BF16)
    heads = []
    for h in range(n_heads):
        q = q_all[:, h * HEAD_DIM:(h + 1) * HEAD_DIM]
        k = kv_ref[0, :, h * HEAD_DIM:(h + 1) * HEAD_DIM]
        v = kv_ref[0, :, width + h * HEAD_DIM:width + (h + 1) * HEAD_DIM]
        s = lax.dot_general(q, k, (((1,), (1,)), ((), ())), preferred_element_type=F32)
        m = jnp.max(s, axis=-1, keepdims=True)
        p = jnp.exp2(s - m)
        l = jnp.sum(p, axis=-1, keepdims=True)
        o = jnp.dot(p.astype(BF16), v, preferred_element_type=F32) * (1.0 / l)
        heads.append(o.astype(BF16))
    o_all = jnp.concatenate(heads, axis=1)
    o_ref[...] = hres + jnp.dot(o_all, wo_ref[...], preferred_element_type=F32)


def cross_attn(h, g, w_q, kv, w_o, *, batch, n_heads, tm=512):
    m, d = h.shape
    width = w_q.shape[1]
    n_mem = kv.shape[0] // batch
    tiles_per_batch = (m // batch) // tm
    kv3 = kv.reshape(batch, n_mem, 2 * width)
    return pl.pallas_call(
        functools.partial(_cross_attn_kernel, n_heads=n_heads),
        out_shape=jax.ShapeDtypeStruct((m, d), F32),
        grid=(m // tm,),
        in_specs=[pl.BlockSpec((tm, d), lambda i: (i, 0)),
                  pl.BlockSpec((1, d), lambda i: (0, 0)),
                  pl.BlockSpec((d, width), lambda i: (0, 0)),
                  pl.BlockSpec((1, n_mem, 2 * width), lambda i: (i // tiles_per_batch, 0, 0)),
                  pl.BlockSpec((width, d), lambda i: (0, 0))],
        out_specs=pl.BlockSpec((tm, d), lambda i: (i, 0)),
        compiler_params=_params("parallel"),
        name="cross_attn",
    )(h, g.reshape(1, d), w_q, kv3, w_o)


def kernel(x, mem, g_mix, w_in, b_forget, w_branch_moba, w_branch_fox, w_mix_out, rel_bias,
           g_cross, g_mem, w_cq, w_ck, w_cv, w_co, g_mlp, w_ff1, w_ff2, g_final):
    batch, seq, d = x.shape
    depth = w_in.shape[0]
    n_heads = rel_bias.shape[0]
    n_fox = b_forget.shape[1]
    wm = n_heads * HEAD_DIM
    wf = n_fox * HEAD_DIM
    m = batch * seq
    assert wm == wf and wm % COL_TILE == 0
    scale = HEAD_DIM ** -0.5
    mem2 = mem.reshape(-1, d)

    bias = moba_bias(rel_bias)
    h = x.reshape(m, d)
    for l in range(depth):
        wt = jnp.swapaxes(w_in[l], 0, 1)
        qkv_w = 3 * (wm + wf)
        w_fl = jnp.zeros((d, LANES), F32).at[:, :n_fox].set(w_in[l][:, qkv_w:qkv_w + n_fox])
        qv_t, k_hm, f_logit, a_mix = in_proj(h, g_mix[l], wt[:qkv_w].astype(BF16), w_fl,
                                             parts_w=wm, scale=scale * LOG2E)
        k_hm = k_hm.reshape(k_hm.shape[0], m // ATT_TILE, ATT_TILE, HEAD_DIM)
        c_rep = forget_cumsum(f_logit, b_forget[l], batch=batch)

        o_a, w_ff1_16, w_mix_16, w_bm16, w_bf16 = moba_attn(
            qv_t, k_hm, bias, rel_bias, batch=batch, n_heads=n_heads,
            q_off=0, k_off=0, v_off=n_heads + n_fox,
            cast=(w_ff1[l], w_mix_out[l], w_branch_moba[l], w_branch_fox[l]))
        o_f, w_ff2_16, w_g16, w_cq16, w_co16 = fox_attn(
            qv_t, k_hm, c_rep, batch=batch, n_heads=n_fox,
            q_off=n_heads, k_off=n_heads, v_off=2 * n_heads + n_fox,
            cast=(w_ff2[l], (wt, qkv_w + n_fox, 2 * d), w_cq[l], w_co[l]))
        gates = gate_proj(a_mix, w_g16)
        merged = gated_merge(o_a, o_f, w_bm16, w_bf16, gates)
        h = mm_res(merged, w_mix_16, h)

        cw = w_cq.shape[2]
        w_kv = jnp.concatenate([w_ck[l], w_cv[l]], axis=1).astype(BF16)
        kv = rms_proj(mem2, g_mem[l], w_kv, tn=2 * cw)
        h = cross_attn(h, g_cross[l], w_cq16, kv, w_co16, batch=batch, n_heads=cw // HEAD_DIM)

        u = rms_proj(h, g_mlp[l], w_ff1_16, act="relu2")
        h = mm_res(u, w_ff2_16, h, g_final if l == depth - 1 else None)
    return h.reshape(batch, seq, d)
```

```python
import functools
import math

import jax
import jax.numpy as jnp
from jax import lax
from jax.experimental import pallas as pl
from jax.experimental.pallas import tpu as pltpu

F32 = jnp.float32
BF16 = jnp.bfloat16

HEAD_DIM = 128
MOBA_BLOCK = 256
MOBA_TOP_K = 3
NUM_BUCKETS = 32
MAX_DISTANCE = 1024
RMS_EPS = 1e-6
LOG2E = math.log2(math.e)
NEG_INF = -1e30
LANES = 128
ATT_TILE = 256
TILE = 2 * ATT_TILE
AUG = 256
HEADS_PER_STEP = 2
LOOP_UNROLL = 4
UNDERFLOW_LOG2 = 160.0
NEAR_BLOCKS = 5
VMEM_LIMIT = 56 * 1024 * 1024
ROW_TILE = 1024
COL_TILE = 1024


def _bucket_thresholds():
    max_exact = NUM_BUCKETS // 2
    thr = list(range(1, max_exact + 1))
    for k in range(max_exact + 1, NUM_BUCKETS):
        v = max_exact * (MAX_DISTANCE / max_exact) ** ((k - max_exact) / (NUM_BUCKETS - max_exact))
        n = int(math.floor(v))
        while max_exact + int(math.log(n / max_exact) / math.log(MAX_DISTANCE / max_exact)
                              * (NUM_BUCKETS - max_exact)) < k:
            n += 1
        thr.append(n)
    return tuple(thr)


BUCKET_THRESHOLDS = _bucket_thresholds()
assert (NEAR_BLOCKS - 1) * MOBA_BLOCK + 1 >= BUCKET_THRESHOLDS[-1]


def _params(*sem):
    return pltpu.CompilerParams(dimension_semantics=sem, vmem_limit_bytes=VMEM_LIMIT)


def _bf16_dot(a, b):
    return jnp.dot(a.astype(BF16), b.astype(BF16), preferred_element_type=F32)


def _rms_proj_kernel(x_ref, g_ref, w_ref, o_ref, a_ref, *, act):
    j = pl.program_id(1)

    @pl.when(j == 0)
    def _():
        x = x_ref[...]
        ms = jnp.mean(x * x, axis=-1, keepdims=True)
        a_ref[...] = (x * lax.rsqrt(ms + RMS_EPS) * g_ref[...]).astype(BF16)

    acc = jnp.dot(a_ref[...], w_ref[...], preferred_element_type=F32)
    if act == "relu2":
        acc = jnp.square(jnp.maximum(acc, 0.0))
    o_ref[...] = acc.astype(o_ref.dtype)


def rms_proj(x, g, w, *, act=None, tm=ROW_TILE, tn=COL_TILE):
    m, d = x.shape
    n = w.shape[1]
    tm, tn = min(tm, m), min(tn, n)
    return pl.pallas_call(
        functools.partial(_rms_proj_kernel, act=act),
        out_shape=jax.ShapeDtypeStruct((m, n), BF16),
        grid=(m // tm, n // tn),
        in_specs=[pl.BlockSpec((tm, d), lambda i, j: (i, 0)),
                  pl.BlockSpec((1, d), lambda i, j: (0, 0)),
                  pl.BlockSpec((d, tn), lambda i, j: (0, j))],
        out_specs=pl.BlockSpec((tm, tn), lambda i, j: (i, j)),
        scratch_shapes=[pltpu.VMEM((tm, d), BF16)],
        compiler_params=_params("parallel", "arbitrary"),
        name="rms_proj",
    )(x, g.reshape(1, d), w)


def _gate_proj_kernel(a_ref, w_ref, o_ref):
    acc = lax.dot_general(a_ref[...], w_ref[...], (((1,), (1,)), ((), ())),
                          preferred_element_type=F32)
    o_ref[...] = jax.nn.sigmoid(acc).astype(o_ref.dtype)


def gate_proj(a, w_rows, *, tm=ROW_TILE, tn=2 * COL_TILE):
    m, d = a.shape
    n = w_rows.shape[0]
    tn = min(tn, n)
    return pl.pallas_call(
        _gate_proj_kernel,
        out_shape=jax.ShapeDtypeStruct((m, n), BF16),
        grid=(m // tm, n // tn),
        in_specs=[pl.BlockSpec((tm, d), lambda i, j: (i, 0)),
                  pl.BlockSpec((tn, d), lambda i, j: (j, 0))],
        out_specs=pl.BlockSpec((tm, tn), lambda i, j: (i, j)),
        compiler_params=_params("parallel", "parallel"),
        name="gate_proj",
    )(a, w_rows)


def _in_proj_kernel(x_ref, g_ref, w_ref, wf_ref, qv_ref, k_ref, f_ref, a_ref, *, n_q, n_qv, scale):
    j = pl.program_id(1)

    @pl.when(j == 0)
    def _():
        x = x_ref[...]
        ms = jnp.mean(x * x, axis=-1, keepdims=True)
        a_ref[...] = (x * lax.rsqrt(ms + RMS_EPS) * g_ref[...]).astype(BF16)
        w_hi, w_lo, _ = _split3(wf_ref[...])
        both = _bf16_dot(a_ref[...], jnp.concatenate([w_hi, w_lo], axis=1))
        f_ref[...] = both[:, :LANES] + both[:, LANES:]

    def project():
        return lax.dot_general(a_ref[...], w_ref[...], (((1,), (1,)), ((), ())),
                               preferred_element_type=F32)

    @pl.when(j < n_qv)
    def _():
        acc = project() * jnp.where(j < n_q, F32(scale), F32(1.0))
        tm, tn = acc.shape
        for c in range(tn // LANES):
            for r in range(tm // ATT_TILE):
                blk = acc[r * ATT_TILE:(r + 1) * ATT_TILE, c * LANES:(c + 1) * LANES]
                qv_ref[c, r] = blk.T.astype(qv_ref.dtype)

    @pl.when(j >= n_qv)
    def _():
        acc = project()
        for c in range(acc.shape[1] // LANES):
            k_ref[c] = acc[:, c * LANES:(c + 1) * LANES].astype(k_ref.dtype)


def in_proj(x, g, w_rows, w_f, *, parts_w, scale, tm=ROW_TILE, tn=COL_TILE):
    m, d = x.shape
    per = parts_w // tn

    def tiles(*parts):
        return [p * per + t for p in parts for t in range(per)]

    order = tiles(0, 3, 2, 5) + tiles(1, 4)
    n_q, n_qv, n_k = 2 * per, 4 * per, 2 * per

    def w_tile(j):
        idx = order[-1]
        for t, src in reversed(list(enumerate(order[:-1]))):
            idx = jnp.where(j == t, src, idx)
        return idx

    hb = tn // LANES
    return pl.pallas_call(
        functools.partial(_in_proj_kernel, n_q=n_q, n_qv=n_qv, scale=scale),
        out_shape=(jax.ShapeDtypeStruct((n_qv * hb, m // ATT_TILE, LANES, ATT_TILE), BF16),
                   jax.ShapeDtypeStruct((n_k * hb, m, LANES), BF16),
                   jax.ShapeDtypeStruct((m, LANES), F32),
                   jax.ShapeDtypeStruct((m, d), BF16)),
        grid=(m // tm, len(order)),
        in_specs=[pl.BlockSpec((tm, d), lambda i, j: (i, 0)),
                  pl.BlockSpec((1, d), lambda i, j: (0, 0)),
                  pl.BlockSpec((tn, d), lambda i, j: (w_tile(j), 0)),
                  pl.BlockSpec((d, LANES), lambda i, j: (0, 0))],
        out_specs=(pl.BlockSpec((hb, tm // ATT_TILE, LANES, ATT_TILE),
                                lambda i, j: (jnp.minimum(j, n_qv - 1), i, 0, 0)),
                   pl.BlockSpec((hb, tm, LANES),
                                lambda i, j: (jnp.clip(j - n_qv, 0, n_k - 1), i, 0)),
                   pl.BlockSpec((tm, LANES), lambda i, j: (i, 0)),
                   pl.BlockSpec((tm, d), lambda i, j: (i, 0))),
        compiler_params=_params("parallel", "arbitrary"),
        name="in_proj",
    )(x, g.reshape(1, d), w_rows, w_f)


def _forget_cumsum_kernel(f_ref, b_ref, o_ref, carry_ref):
    t = pl.program_id(1)

    @pl.when(t == 0)
    def _():
        carry_ref[...] = jnp.zeros_like(carry_ref)

    f = f_ref[...] + b_ref[...]
    tm = f.shape[0]
    logf = jnp.minimum(f, 0.0) - jnp.log1p(jnp.exp(-jnp.abs(f)))
    logf = logf * LOG2E
    row = lax.broadcasted_iota(jnp.int32, (tm, tm), 0)
    col = lax.broadcasted_iota(jnp.int32, (tm, tm), 1)
    tri = jnp.where(col <= row, 1.0, 0.0).astype(BF16)
    hi, mid, lo = _split3(logf)
    c = (_bf16_dot(tri, hi) + (_bf16_dot(tri, mid) + _bf16_dot(tri, lo))) + carry_ref[0:1, :]
    o_ref[...] = c
    carry_ref[...] = jnp.broadcast_to(c[tm - 1:tm, :], carry_ref.shape)


def forget_cumsum(f, b_f, *, batch, tm=512):
    m = f.shape[0]
    nt = (m // batch) // tm
    b_pad = jnp.zeros((1, LANES), F32).at[0, :b_f.shape[0]].set(b_f)
    return pl.pallas_call(
        _forget_cumsum_kernel,
        out_shape=jax.ShapeDtypeStruct((m, LANES), F32),
        grid=(batch, nt),
        in_specs=[pl.BlockSpec((tm, LANES), lambda b, t: (b * nt + t, 0)),
                  pl.BlockSpec((1, LANES), lambda b, t: (0, 0))],
        out_specs=pl.BlockSpec((tm, LANES), lambda b, t: (b * nt + t, 0)),
        scratch_shapes=[pltpu.VMEM((8, LANES), F32)],
        compiler_params=_params("parallel", "arbitrary"),
        name="forget_cumsum",
    )(f, b_pad)


def _moba_bias_kernel(tab_ref, o_ref):
    h = pl.program_id(0)
    key = lax.broadcasted_iota(jnp.int32, (ATT_TILE, ATT_TILE), 0)
    qry = lax.broadcasted_iota(jnp.int32, (ATT_TILE, ATT_TILE), 1)
    far = tab_ref[h, NUM_BUCKETS - 1] * LOG2E

    def block(delta):
        dist = delta * MOBA_BLOCK + qry - key
        val = jnp.full((ATT_TILE, ATT_TILE), tab_ref[h, 0], F32)
        for k in range(1, NUM_BUCKETS):
            val = jnp.where(dist >= BUCKET_THRESHOLDS[k - 1], tab_ref[h, k], val)
        val = val * LOG2E
        if delta == 0:
            val = jnp.where(dist >= 0, val, NEG_INF)
        return val

    t = [block(delta) for delta in range(NEAR_BLOCKS + 1)]
    a = ATT_TILE
    o_ref[0, 0, 0:a, 0:a] = t[0]
    o_ref[0, 0, 0:a, a:2 * a] = t[1]
    o_ref[0, 0, a:2 * a, 0:a] = jnp.full((a, a), NEG_INF, F32)
    o_ref[0, 0, a:2 * a, a:2 * a] = t[0]
    for d in (1, 2):
        o_ref[0, d, 0:a, 0:a] = t[2 * d] - far
        o_ref[0, d, 0:a, a:2 * a] = t[2 * d + 1] - far
        o_ref[0, d, a:2 * a, 0:a] = t[2 * d - 1] - far
        o_ref[0, d, a:2 * a, a:2 * a] = t[2 * d] - far


def moba_bias(rel_bias):
    h = rel_bias.shape[0]
    return pl.pallas_call(
        _moba_bias_kernel,
        out_shape=jax.ShapeDtypeStruct((h, 3, TILE, TILE), F32),
        grid=(h,),
        in_specs=[pl.BlockSpec(memory_space=pltpu.SMEM)],
        out_specs=pl.BlockSpec((1, 3, TILE, TILE), lambda i: (i, 0, 0, 0)),
        compiler_params=_params("parallel"),
        name="moba_bias",
    )(rel_bias.astype(F32))


class _Softmax:
    def __init__(self, hh, s_ref, m_ref, l_ref, acc_ref, load_values):
        self.s_ref, self.m_ref, self.l_ref, self.acc_ref = s_ref.at[hh], m_ref.at[hh], l_ref.at[hh], acc_ref.at[hh]
        self.load_values = load_values

    def reset(self):
        self.m_ref[...] = jnp.full(self.m_ref.shape, NEG_INF, F32)
        self.l_ref[...] = jnp.zeros(self.l_ref.shape, F32)
        self.acc_ref[...] = jnp.zeros(self.acc_ref.shape, F32)

    def fold(self, slot, tile, mask=None):
        for half in range(2):
            self.fold_half(slot, tile, half, mask)

    def fold_half(self, slot, tile, half, mask=None):
        width = self.m_ref.shape[-1] // 2
        cols = slice(half * width, (half + 1) * width)
        s = self.s_ref[slot, :, cols]
        if mask is not None:
            s = jnp.where(mask[:, cols], s, NEG_INF)
        m = self.m_ref[:, cols]
        m_new = jnp.maximum(m, jnp.max(s, axis=0, keepdims=True))
        alpha = jnp.exp2(m - m_new)
        p = jnp.exp2(s - m_new)
        self.acc_ref[:, cols] = alpha * self.acc_ref[:, cols] + jnp.dot(
            self.load_values(tile), p.astype(BF16), preferred_element_type=F32)
        self.l_ref[:, cols] = alpha * self.l_ref[:, cols] + jnp.sum(p, axis=0, keepdims=True)
        self.m_ref[:, cols] = m_new

    def result(self):
        return self.acc_ref[...] * (1.0 / self.l_ref[...])


def _softmax_scratch(t):
    nh = HEADS_PER_STEP
    return [pltpu.VMEM((nh, 2, t, t), F32),
            pltpu.VMEM((nh, 1, t), F32),
            pltpu.VMEM((nh, 1, t), F32),
            pltpu.VMEM((nh, HEAD_DIM, t), F32)]


def _fold_tile_run(heads, s_ref, logits, fold, base, count, last_tile):
    def step(r, slot, look_ahead=True):
        for hh in heads:
            if look_ahead:
                s_ref[hh, 1 - slot] = logits[hh](jnp.minimum(base + r + 1, last_tile))
            fold[hh].fold(slot, base + r)

    def unrolled_body(p, carry):
        for u in range(LOOP_UNROLL):
            step(LOOP_UNROLL * p + u, (u + 1) % 2)
        return carry

    lax.fori_loop(0, count // LOOP_UNROLL, unrolled_body, 0)
    rem = count % LOOP_UNROLL
    done = count - rem

    @pl.when(rem >= 2)
    def _():
        step(done, 1)
        step(done + 1, 0)

    @pl.when(rem % 2 == 1)
    def _():
        step(count - 1, 1, look_ahead=False)


def _top_k_bias(gate, eligible, blk):
    lowest = float(jnp.finfo(F32).min)
    blk_f = blk.astype(F32)
    g = jnp.where(eligible, gate, NEG_INF)
    bias = jnp.full(gate.shape, NEG_INF, F32)
    for _ in range(MOBA_TOP_K):
        best = jnp.max(g, axis=0, keepdims=True)
        first = jnp.min(jnp.where(g == best, blk_f, float(gate.shape[0])), axis=0, keepdims=True)
        hit = blk_f == first
        bias = jnp.where(hit, 0.0, bias)
        g = jnp.where(hit, lowest, g)
    return jnp.where(eligible, bias, NEG_INF)


def _split3(x):
    hi = x.astype(BF16).astype(F32)
    mid = (x - hi).astype(BF16).astype(F32)
    lo = (x - hi - mid).astype(BF16).astype(F32)
    return hi, mid, lo


def _slab_cast_specs(weights, n_steps, step_index):
    in_specs, out_specs, out_shapes = [], [], []
    for w in weights:
        if isinstance(w, tuple):
            w, first_row, n_rows = w
        else:
            first_row, n_rows = 0, w.shape[0]
        rows = max(n_rows // n_steps, 16)
        n_slabs = n_rows // rows
        assert n_slabs * rows == n_rows and rows % 16 == 0 and n_steps % n_slabs == 0
        share = n_steps // n_slabs

        def slab(*g, share=share):
            return step_index(*g) // share

        if first_row == 0:
            in_specs.append(pl.BlockSpec((rows, w.shape[1]), lambda *g, slab=slab: (slab(*g), 0)))
        else:
            in_specs.append(pl.BlockSpec(
                (pl.Element(rows), pl.Element(w.shape[1])),
                lambda *g, first_row=first_row, rows=rows, slab=slab: (
                    pl.multiple_of(first_row + slab(*g) * rows, 8), 0)))
        out_specs.append(pl.BlockSpec((rows, w.shape[1]), lambda *g, slab=slab: (slab(*g), 0)))
        out_shapes.append(jax.ShapeDtypeStruct((n_rows, w.shape[1]), BF16))
    return in_specs, out_specs, out_shapes


def _moba_attn_kernel(*refs, n_blocks, n_cast):
    tab_ref, q_ref, k_ref, v_ref, bias_ref = refs[:5]
    o_ref = refs[5 + n_cast]
    kaug_ref, kbar_ref, kb3_ref, s_ref, m_ref, l_ref, acc_ref = refs[6 + 2 * n_cast:]
    for src, dst in zip(refs[5:5 + n_cast], refs[6 + n_cast:6 + 2 * n_cast]):
        dst[...] = src[...].astype(dst.dtype)
    _moba_attn_body(tab_ref, q_ref, k_ref, v_ref, bias_ref, o_ref,
                    kaug_ref, kbar_ref, kb3_ref, s_ref, m_ref, l_ref, acc_ref, n_blocks=n_blocks)


def _moba_attn_body(tab_ref, q_ref, k_ref, v_ref, bias_ref, o_ref,
                    kaug_ref, kbar_ref, kb3_ref, s_ref, m_ref, l_ref, acc_ref, *, n_blocks):
    hp = pl.program_id(1)
    i = pl.program_id(2)
    n_tiles = n_blocks // 2
    heads = range(HEADS_PER_STEP)

    @pl.when(i == 0)
    def _():
        lane = lax.broadcasted_iota(jnp.int32, (TILE, AUG - HEAD_DIM), 1)
        row = lax.broadcasted_iota(jnp.int32, (TILE, AUG - HEAD_DIM), 0)
        ones_lane = jnp.where(lane == n_blocks, 1.0, jnp.where(lane == n_blocks + 1, 1.0, 0.0))
        for hh in heads:
            for n in range(n_blocks):
                kbar_ref[hh, n:n + 1, :] = jnp.sum(k_ref[hh, n].astype(F32), axis=0, keepdims=True)
            hi, mid, lo = _split3(kbar_ref[hh] * (1.0 / MOBA_BLOCK))
            kb3_ref[hh, 0:n_blocks, :] = hi.astype(BF16)
            kb3_ref[hh, n_blocks:2 * n_blocks, :] = mid.astype(BF16)
            kb3_ref[hh, 2 * n_blocks:3 * n_blocks, :] = lo.astype(BF16)
            for j in range(n_tiles):
                blk_of_row = jnp.where(row < ATT_TILE, 2 * j, 2 * j + 1)
                right = jnp.where(lane == blk_of_row, 1.0, ones_lane).astype(BF16)
                kaug_ref[hh, j] = jnp.concatenate(
                    [k_ref[hh, 2 * j:2 * j + 2].reshape(TILE, HEAD_DIM), right], axis=1)

    blk = lax.broadcasted_iota(jnp.int32, (n_blocks, TILE), 0)
    qlane = lax.broadcasted_iota(jnp.int32, (n_blocks, TILE), 1)
    own = 2 * i + jnp.where(qlane >= ATT_TILE, 1, 0)
    eligible = blk < own
    r16 = lax.broadcasted_iota(jnp.int32, (16, TILE), 0)

    j1 = jnp.where(i >= 1, i - 1, i + 1)
    j2 = jnp.where(i >= 2, i - 2, i + 1)
    n_far = jnp.maximum(i - 2, 0)

    far_logits, fold = [], []
    for hh in heads:
        qt = jnp.concatenate([q_ref[hh, 0], q_ref[hh, 1]], axis=1)
        g3 = jnp.dot(kb3_ref[hh], qt, preferred_element_type=F32)
        gate = g3[0:n_blocks] + g3[n_blocks:2 * n_blocks] + g3[2 * n_blocks:3 * n_blocks]
        selb = _top_k_bias(gate, eligible, blk)

        far = jnp.full((16, TILE), tab_ref[hp * HEADS_PER_STEP + hh, NUM_BUCKETS - 1] * LOG2E, F32)
        far_hi = far.astype(BF16).astype(F32)
        far_rows = jnp.where(r16 == 0, far_hi, jnp.where(r16 == 1, far - far_hi, 0.0))
        q_aug = jnp.concatenate(
            [qt, selb.astype(BF16), far_rows.astype(BF16),
             jnp.zeros((AUG - HEAD_DIM - n_blocks - 16, TILE), BF16)], axis=0)

        def values(j, hh=hh):
            return jnp.concatenate([v_ref[hh, 2 * j], v_ref[hh, 2 * j + 1]], axis=1)

        def head_far_logits(j, hh=hh, q_aug=q_aug):
            return jnp.dot(kaug_ref[hh, j], q_aug, preferred_element_type=F32)

        sm = _Softmax(hh, s_ref, m_ref, l_ref, acc_ref, values)
        far_logits.append(head_far_logits)
        fold.append(sm)

        sel_own = jnp.sum(jnp.where(blk == 2 * i, selb, 0.0), axis=0, keepdims=True)
        sel_own = jnp.where(qlane[0:1] >= ATT_TILE, sel_own, 0.0)
        kd = k_ref[hh, pl.ds(2 * i, 2)].reshape(TILE, HEAD_DIM)
        sd = jnp.dot(kd, qt, preferred_element_type=F32) + bias_ref[hh, 0]
        s_ref[hh, 0, 0:ATT_TILE, :] = sd[0:ATT_TILE] + sel_own
        s_ref[hh, 0, ATT_TILE:TILE, :] = sd[ATT_TILE:TILE]
        sm.reset()

    for hh in heads:
        s_ref[hh, 1] = far_logits[hh](j1) + bias_ref[hh, 1]
        fold[hh].fold(0, i)
    for hh in heads:
        s_ref[hh, 0] = far_logits[hh](j2) + bias_ref[hh, 2]
        fold[hh].fold(1, j1)
    for hh in heads:
        s_ref[hh, 1] = far_logits[hh](0)
        fold[hh].fold(0, j2)

    _fold_tile_run(heads, s_ref, far_logits, fold, 0, n_far, n_tiles - 1)

    for hh in heads:
        o_ref[:, hh * HEAD_DIM:(hh + 1) * HEAD_DIM] = fold[hh].result().T.astype(o_ref.dtype)


def moba_attn(qv_t, k, bias, rel_bias, *, batch, n_heads, q_off, k_off, v_off, cast=()):
    mb = k.shape[1]
    nb = mb // batch
    nt = nb // 2
    m = mb * ATT_TILE
    hb = HEADS_PER_STEP
    nhp = n_heads // hb
    assert nb + 16 <= AUG - HEAD_DIM and nt >= 4
    assert n_heads % hb == 0 and q_off % hb == 0 and k_off % hb == 0 and v_off % hb == 0
    c_in, c_out, c_shape = _slab_cast_specs(cast, batch * nhp * nt, lambda b, h, i: (b * nhp + h) * nt + i)
    return pl.pallas_call(
        functools.partial(_moba_attn_kernel, n_blocks=nb, n_cast=len(cast)),
        out_shape=[jax.ShapeDtypeStruct((m, n_heads * HEAD_DIM), BF16)] + c_shape,
        grid=(batch, nhp, nt),
        in_specs=[pl.BlockSpec(memory_space=pltpu.SMEM),
                  pl.BlockSpec((hb, 2, HEAD_DIM, ATT_TILE),
                               lambda b, h, i: (q_off // hb + h, b * nt + i, 0, 0)),
                  pl.BlockSpec((hb, nb, ATT_TILE, HEAD_DIM), lambda b, h, i: (k_off // hb + h, b, 0, 0)),
                  pl.BlockSpec((hb, nb, HEAD_DIM, ATT_TILE), lambda b, h, i: (v_off // hb + h, b, 0, 0)),
                  pl.BlockSpec((hb, 3, TILE, TILE), lambda b, h, i: (h, 0, 0, 0))] + c_in,
        out_specs=[pl.BlockSpec((TILE, hb * HEAD_DIM), lambda b, h, i: (b * nt + i, h))] + c_out,
        scratch_shapes=[pltpu.VMEM((hb, nt, TILE, AUG), BF16),
                        pltpu.VMEM((hb, nb, HEAD_DIM), F32),
                        pltpu.VMEM((hb, 3 * nb, HEAD_DIM), BF16)] + _softmax_scratch(TILE),
        compiler_params=_params("parallel", "parallel", "arbitrary"),
        name="moba_attn",
    )(rel_bias.astype(F32), qv_t, k, qv_t, bias, *[c[0] if isinstance(c, tuple) else c for c in cast])


def _fox_attn_kernel(*refs, n_tiles, n_cast):
    q_ref, k_ref, v_ref, c_ref = refs[:4]
    o_ref = refs[4 + n_cast]
    kaug_ref, bound_ref, s_ref, m_ref, l_ref, acc_ref = refs[5 + 2 * n_cast:]
    for src, dst in zip(refs[4:4 + n_cast], refs[5 + n_cast:5 + 2 * n_cast]):
        dst[...] = src[...].astype(dst.dtype)
    _fox_attn_body(q_ref, k_ref, v_ref, c_ref, o_ref, kaug_ref, bound_ref, s_ref, m_ref, l_ref,
                   acc_ref, n_tiles=n_tiles)


def _fox_attn_body(q_ref, k_ref, v_ref, c_ref, o_ref, kaug_ref, bound_ref, s_ref, m_ref, l_ref,
                   acc_ref, *, n_tiles):
    hp = pl.program_id(1)
    i = pl.program_id(2)
    t = TILE
    heads = range(HEADS_PER_STEP)

    lane1 = lax.broadcasted_iota(jnp.int32, (1, LANES), 1)

    @pl.when(i == 0)
    def _():
        lane = lax.broadcasted_iota(jnp.int32, (t, AUG - HEAD_DIM), 1)
        for hh in heads:
            c_first = jnp.zeros((1, LANES), F32)
            c_last = jnp.zeros((1, LANES), F32)
            k_norm2 = jnp.zeros((t, 1), F32)
            for j in range(n_tiles):
                c_all = c_ref[j * t:(j + 1) * t, :]
                c = jnp.sum(jnp.where(lane == hp * HEADS_PER_STEP + hh, c_all, 0.0),
                            axis=1, keepdims=True)
                hi, mid, lo = _split3(-c)
                right = jnp.where(lane == 0, hi, jnp.where(lane == 1, mid, jnp.where(lane == 2, lo, 0.0)))
                k = k_ref[hh, 2 * j:2 * j + 2].reshape(t, HEAD_DIM)
                kaug_ref[hh, j] = jnp.concatenate([k, right.astype(BF16)], axis=1)
                kf = k.astype(F32)
                k_norm2 = jnp.maximum(k_norm2, jnp.sum(kf * kf, axis=1, keepdims=True))
                c_first = jnp.where(lane1 == j, c[0:1, :], c_first)
                c_last = jnp.where(lane1 == j, c[t - 1:t, :], c_last)
            bound_ref[hh, 0:1, :] = c_first
            bound_ref[hh, 1:2, :] = c_last
            bound_ref[hh, 2:3, :] = jnp.broadcast_to(jnp.max(k_norm2, axis=0, keepdims=True), (1, LANES))

    r_aug = lax.broadcasted_iota(jnp.int32, (AUG - HEAD_DIM, t), 0)
    ones_rows = jnp.where(r_aug < 3, 1.0, 0.0).astype(BF16)

    logits, fold, skippable = [], [], []
    for hh in heads:
        qt = jnp.concatenate([q_ref[hh, 0], q_ref[hh, 1]], axis=1)
        q_aug = jnp.concatenate([qt, ones_rows], axis=0)

        qf = qt.astype(F32)
        q_norm2 = jnp.max(jnp.sum(qf * qf, axis=0, keepdims=True), axis=1, keepdims=True)
        c_here = jnp.sum(jnp.where(lane1 == i, bound_ref[hh, 0:1, :], 0.0), axis=1, keepdims=True)
        gap = (bound_ref[hh, 1:2, :] - c_here) - UNDERFLOW_LOG2
        dead = (lane1 < i) & (gap > 0.0) & (gap * gap > 4.0 * q_norm2 * bound_ref[hh, 2:3, :])
        skippable.append(jnp.sum(jnp.where(dead, 1.0, 0.0), axis=1, keepdims=True))

        def head_logits(n, hh=hh, q_aug=q_aug):
            return jnp.dot(kaug_ref[hh, n], q_aug, preferred_element_type=F32)

        def values(n, hh=hh):
            return jnp.concatenate([v_ref[hh, 2 * n], v_ref[hh, 2 * n + 1]], axis=1)

        logits.append(head_logits)
        fold.append(_Softmax(hh, s_ref, m_ref, l_ref, acc_ref, values))

    key = lax.broadcasted_iota(jnp.int32, (t, t), 0)
    qry = lax.broadcasted_iota(jnp.int32, (t, t), 1)
    causal = key <= qry
    j0 = functools.reduce(jnp.minimum, skippable)[0, 0].astype(jnp.int32)
    n_past = i - j0
    for hh in heads:
        s_ref[hh, 0] = logits[hh](i)
        fold[hh].reset()
    for hh in heads:
        s_ref[hh, 1] = logits[hh](j0)
        fold[hh].fold(0, i, mask=causal)

    _fold_tile_run(heads, s_ref, logits, fold, j0, n_past, n_tiles - 1)

    for hh in heads:
        o_ref[:, hh * HEAD_DIM:(hh + 1) * HEAD_DIM] = fold[hh].result().T.astype(o_ref.dtype)


def fox_attn(qv_t, k, c_rep, *, batch, n_heads, q_off, k_off, v_off, cast=()):
    mb = k.shape[1]
    nb = mb // batch
    nt = nb // 2
    m = mb * ATT_TILE
    t = TILE
    hb = HEADS_PER_STEP
    nhp = n_heads // hb
    assert n_heads % hb == 0 and q_off % hb == 0 and k_off % hb == 0 and v_off % hb == 0
    assert nt <= LANES and n_heads <= LANES
    c_in, c_out, c_shape = _slab_cast_specs(cast, batch * nhp * nt, lambda b, h, i: (b * nhp + h) * nt + i)
    return pl.pallas_call(
        functools.partial(_fox_attn_kernel, n_tiles=nt, n_cast=len(cast)),
        out_shape=[jax.ShapeDtypeStruct((m, n_heads * HEAD_DIM), BF16)] + c_shape,
        grid=(batch, nhp, nt),
        in_specs=[pl.BlockSpec((hb, 2, HEAD_DIM, ATT_TILE),
                               lambda b, h, i: (q_off // hb + h, b * nt + i, 0, 0)),
                  pl.BlockSpec((hb, nb, ATT_TILE, HEAD_DIM), lambda b, h, i: (k_off // hb + h, b, 0, 0)),
                  pl.BlockSpec((hb, nb, HEAD_DIM, ATT_TILE), lambda b, h, i: (v_off // hb + h, b, 0, 0)),
                  pl.BlockSpec((nb * ATT_TILE, LANES), lambda b, h, i: (b, 0))] + c_in,
        out_specs=[pl.BlockSpec((t, hb * HEAD_DIM), lambda b, h, i: (b * nt + i, h))] + c_out,
        scratch_shapes=[pltpu.VMEM((hb, nt, t, AUG), BF16),
                        pltpu.VMEM((hb, 8, LANES), F32)] + _softmax_scratch(t),
        compiler_params=_params("parallel", "parallel", "arbitrary"),
        name="fox_attn",
    )(qv_t, k, qv_t, c_rep, *[c[0] if isinstance(c, tuple) else c for c in cast])


def _gated_merge_kernel(oa_ref, of_ref, wa_ref, wf_ref, ga_ref, gf_ref, o_ref):
    ua = jnp.dot(oa_ref[...], wa_ref[...], preferred_element_type=F32)
    uf = jnp.dot(of_ref[...], wf_ref[...], preferred_element_type=F32)
    o_ref[...] = (ga_ref[...].astype(F32) * ua + gf_ref[...].astype(F32) * uf).astype(o_ref.dtype)


def gated_merge(o_a, o_f, w_a, w_f, gates, *, tm=ROW_TILE, tn=COL_TILE):
    m, ka = o_a.shape
    kf = o_f.shape[1]
    n = w_a.shape[1]
    nj = n // tn
    return pl.pallas_call(
        _gated_merge_kernel,
        out_shape=jax.ShapeDtypeStruct((m, n), BF16),
        grid=(m // tm, nj),
        in_specs=[pl.BlockSpec((tm, ka), lambda i, j: (i, 0)),
                  pl.BlockSpec((tm, kf), lambda i, j: (i, 0)),
                  pl.BlockSpec((ka, tn), lambda i, j: (0, j)),
                  pl.BlockSpec((kf, tn), lambda i, j: (0, j)),
                  pl.BlockSpec((tm, tn), lambda i, j: (i, j)),
                  pl.BlockSpec((tm, tn), lambda i, j: (i, nj + j))],
        out_specs=pl.BlockSpec((tm, tn), lambda i, j: (i, j)),
        compiler_params=_params("parallel", "parallel"),
        name="gated_merge",
    )(o_a, o_f, w_a, w_f, gates, gates)


def _mm_res_kernel(lhs_ref, w_ref, res_ref, g_ref, o_ref, *, final_norm):
    kk = pl.program_id(1)

    @pl.when(kk == 0)
    def _():
        o_ref[...] = res_ref[...]

    o_ref[...] += jnp.dot(lhs_ref[...], w_ref[...], preferred_element_type=F32)

    if final_norm:
        @pl.when(kk == pl.num_programs(1) - 1)
        def _():
            hres = o_ref[...]
            ms = jnp.mean(hres * hres, axis=-1, keepdims=True)
            o_ref[...] = hres * lax.rsqrt(ms + RMS_EPS) * g_ref[...]


def mm_res(lhs, w, res, g=None, *, tm=ROW_TILE, tk=1024):
    m, k = lhs.shape
    n = w.shape[1]
    tk = min(tk, k)
    final_norm = g is not None
    if g is None:
        g = jnp.ones((n,), F32)
    return pl.pallas_call(
        functools.partial(_mm_res_kernel, final_norm=final_norm),
        out_shape=jax.ShapeDtypeStruct((m, n), F32),
        grid=(m // tm, k // tk),
        in_specs=[pl.BlockSpec((tm, tk), lambda i, kk: (i, kk)),
                  pl.BlockSpec((tk, n), lambda i, kk: (kk, 0)),
                  pl.BlockSpec((tm, n), lambda i, kk: (i, 0)),
                  pl.BlockSpec((1, n), lambda i, kk: (0, 0))],
        out_specs=pl.BlockSpec((tm, n), lambda i, kk: (i, 0)),
        compiler_params=_params("parallel", "arbitrary"),
        name="mm_res",
    )(lhs, w, res, g.reshape(1, n))


def _mm_res_cols_kernel(lhs_ref, w_ref, res_ref, o_ref):
    o_ref[...] = res_ref[...] + jnp.dot(lhs_ref[...], w_ref[...], preferred_element_type=F32)


def mm_res_cols(lhs, w, res, *, tm=ROW_TILE, tn=COL_TILE):
    m, k = lhs.shape
    n = w.shape[1]
    return pl.pallas_call(
        _mm_res_cols_kernel,
        out_shape=jax.ShapeDtypeStruct((m, n), F32),
        grid=(m // tm, n // tn),
        in_specs=[pl.BlockSpec((tm, k), lambda i, j: (i, 0)),
                  pl.BlockSpec((k, tn), lambda i, j: (0, j)),
                  pl.BlockSpec((tm, tn), lambda i, j: (i, j))],
        out_specs=pl.BlockSpec((tm, tn), lambda i, j: (i, j)),
        compiler_params=_params("parallel", "parallel"),
        name="mm_res_cols",
    )(lhs, w, res)


def _cross_attn_kernel(h_ref, g_ref, wq_ref, kv_ref, wo_ref, o_ref, *, n_heads):
    width = n_heads * HEAD_DIM
    hres = h_ref[...]
    ms = jnp.mean(hres * hres, axis=-1, keepdims=True)
    c = (hres * lax.rsqrt(ms + RMS_EPS) * g_ref[...]).astype(BF16)
    q_all = (jnp.dot(c, wq_ref[...], preferred_element_type=F32)
             * (HEAD_DIM ** -0.5 * LOG2E)).astype(BF16)
    heads = []
    for h in range(n_heads):
        q = q_all[:, h * HEAD_DIM:(h + 1) * HEAD_DIM]
        k = kv_ref[0, :, h * HEAD_DIM:(h + 1) * HEAD_DIM]
        v = kv_ref[0, :, width + h * HEAD_DIM:width + (h + 1) * HEAD_DIM]
        s = lax.dot_general(q, k, (((1,), (1,)), ((), ())), preferred_element_type=F32)
        m = jnp.max(s, axis=-1, keepdims=True)
        p = jnp.exp2(s - m)
        l = jnp.sum(p, axis=-1, keepdims=True)
        o = jnp.dot(p.astype(BF16), v, preferred_element_type=F32) * (1.0 / l)
        heads.append(o.astype(BF16))
    o_all = jnp.concatenate(heads, axis=1)
    o_ref[...] = hres + jnp.dot(o_all, wo_ref[...], preferred_element_type=F32)


def cross_attn(h, g, w_q, kv, w_o, *, batch, n_heads, tm=512):
    m, d = h.shape
    width = w_q.shape[1]
    n_mem = kv.shape[0] // batch
    tiles_per_batch = (m // batch) // tm
    kv3 = kv.reshape(batch, n_mem, 2 * width)
    return pl.pallas_call(
        functools.partial(_cross_attn_kernel, n_heads=n_heads),
        out_shape=jax.ShapeDtypeStruct((m, d), F32),
        grid=(m // tm,),
        in_specs=[pl.BlockSpec((tm, d), lambda i: (i, 0)),
                  pl.BlockSpec((1, d), lambda i: (0, 0)),
                  pl.BlockSpec((d, width), lambda i: (0, 0)),
                  pl.BlockSpec((1, n_mem, 2 * width), lambda i: (i // tiles_per_batch, 0, 0)),
                  pl.BlockSpec((width, d), lambda i: (0, 0))],
        out_specs=pl.BlockSpec((tm, d), lambda i: (i, 0)),
        compiler_params=_params("parallel"),
        name="cross_attn",
    )(h, g.reshape(1, d), w_q, kv3, w_o)


def kernel(x, mem, g_mix, w_in, b_forget, w_branch_moba, w_branch_fox, w_mix_out, rel_bias,
           g_cross, g_mem, w_cq, w_ck, w_cv, w_co, g_mlp, w_ff1, w_ff2, g_final):
    batch, seq, d = x.shape
    depth = w_in.shape[0]
    n_heads = rel_bias.shape[0]
    n_fox = b_forget.shape[1]
    wm = n_heads * HEAD_DIM
    wf = n_fox * HEAD_DIM
    m = batch * seq
    assert wm == wf and wm % COL_TILE == 0
    scale = HEAD_DIM ** -0.5
    mem2 = mem.reshape(-1, d)

    bias = moba_bias(rel_bias)
    h = x.reshape(m, d)
    for l in range(depth):
        wt = jnp.swapaxes(w_in[l], 0, 1)
        qkv_w = 3 * (wm + wf)
        w_fl = jnp.zeros((d, LANES), F32).at[:, :n_fox].set(w_in[l][:, qkv_w:qkv_w + n_fox])
        qv_t, k_hm, f_logit, a_mix = in_proj(h, g_mix[l], wt[:qkv_w].astype(BF16), w_fl,
                                             parts_w=wm, scale=scale * LOG2E)
        k_hm = k_hm.reshape(k_hm.shape[0], m // ATT_TILE, ATT_TILE, HEAD_DIM)
        c_rep = forget_cumsum(f_logit, b_forget[l], batch=batch)

        o_a, w_ff1_16, w_mix_16, w_bm16, w_bf16 = moba_attn(
            qv_t, k_hm, bias, rel_bias, batch=batch, n_heads=n_heads,
            q_off=0, k_off=0, v_off=n_heads + n_fox,
            cast=(w_ff1[l], w_mix_out[l], w_branch_moba[l], w_branch_fox[l]))
        o_f, w_ff2_16, w_g16, w_cq16, w_co16 = fox_attn(
            qv_t, k_hm, c_rep, batch=batch, n_heads=n_fox,
            q_off=n_heads, k_off=n_heads, v_off=2 * n_heads + n_fox,
            cast=(w_ff2[l], (wt, qkv_w + n_fox, 2 * d), w_cq[l], w_co[l]))
        gates = gate_proj(a_mix, w_g16)
        merged = gated_merge(o_a, o_f, w_bm16, w_bf16, gates)
        h = mm_res_cols(merged, w_mix_16, h)

        cw = w_cq.shape[2]
        w_kv = jnp.concatenate([w_ck[l], w_cv[l]], axis=1).astype(BF16)
        kv = rms_proj(mem2, g_mem[l], w_kv, tn=2 * cw)
        h = cross_attn(h, g_cross[l], w_cq16, kv, w_co16, batch=batch, n_heads=cw // HEAD_DIM)

        u = rms_proj(h, g_mlp[l], w_ff1_16, act="relu2")
        h = mm_res(u, w_ff2_16, h, g_final if l == depth - 1 else None)
    return h.reshape(batch, seq, d)
```

```python
import functools
import math

import jax
import jax.numpy as jnp
from jax import lax
from jax.experimental import pallas as pl
from jax.experimental.pallas import tpu as pltpu

F32 = jnp.float32
BF16 = jnp.bfloat16

HEAD_DIM = 128
MOBA_BLOCK = 256
MOBA_TOP_K = 3
NUM_BUCKETS = 32
MAX_DISTANCE = 1024
RMS_EPS = 1e-6
LOG2E = math.log2(math.e)
NEG_INF = -1e30
LANES = 128
ATT_TILE = 256
TILE = 2 * ATT_TILE
AUG = 256
HEADS_PER_STEP = 2
LOOP_UNROLL = 4
UNDERFLOW_LOG2 = 160.0
NEAR_BLOCKS = 5
VMEM_LIMIT = 56 * 1024 * 1024
ROW_TILE = 1024
COL_TILE = 1024


def _bucket_thresholds():
    max_exact = NUM_BUCKETS // 2
    thr = list(range(1, max_exact + 1))
    for k in range(max_exact + 1, NUM_BUCKETS):
        v = max_exact * (MAX_DISTANCE / max_exact) ** ((k - max_exact) / (NUM_BUCKETS - max_exact))
        n = int(math.floor(v))
        while max_exact + int(math.log(n / max_exact) / math.log(MAX_DISTANCE / max_exact)
                              * (NUM_BUCKETS - max_exact)) < k:
            n += 1
        thr.append(n)
    return tuple(thr)


BUCKET_THRESHOLDS = _bucket_thresholds()
assert (NEAR_BLOCKS - 1) * MOBA_BLOCK + 1 >= BUCKET_THRESHOLDS[-1]


def _params(*sem):
    return pltpu.CompilerParams(dimension_semantics=sem, vmem_limit_bytes=VMEM_LIMIT)


def _bf16_dot(a, b):
    return jnp.dot(a.astype(BF16), b.astype(BF16), preferred_element_type=F32)


def _rms_proj_kernel(x_ref, g_ref, w_ref, o_ref, a_ref, *, act):
    j = pl.program_id(1)

    @pl.when(j == 0)
    def _():
        x = x_ref[...]
        ms = jnp.mean(x * x, axis=-1, keepdims=True)
        a_ref[...] = (x * lax.rsqrt(ms + RMS_EPS) * g_ref[...]).astype(BF16)

    acc = jnp.dot(a_ref[...], w_ref[...], preferred_element_type=F32)
    if act == "relu2":
        acc = jnp.square(jnp.maximum(acc, 0.0))
    o_ref[...] = acc.astype(o_ref.dtype)


def rms_proj(x, g, w, *, act=None, tm=ROW_TILE, tn=COL_TILE):
    m, d = x.shape
    n = w.shape[1]
    tm, tn = min(tm, m), min(tn, n)
    return pl.pallas_call(
        functools.partial(_rms_proj_kernel, act=act),
        out_shape=jax.ShapeDtypeStruct((m, n), BF16),
        grid=(m // tm, n // tn),
        in_specs=[pl.BlockSpec((tm, d), lambda i, j: (i, 0)),
                  pl.BlockSpec((1, d), lambda i, j: (0, 0)),
                  pl.BlockSpec((d, tn), lambda i, j: (0, j))],
        out_specs=pl.BlockSpec((tm, tn), lambda i, j: (i, j)),
        scratch_shapes=[pltpu.VMEM((tm, d), BF16)],
        compiler_params=_params("parallel", "arbitrary"),
        name="rms_proj",
    )(x, g.reshape(1, d), w)


def _gate_proj_kernel(a_ref, w_ref, o_ref):
    acc = lax.dot_general(a_ref[...], w_ref[...], (((1,), (1,)), ((), ())),
                          preferred_element_type=F32)
    o_ref[...] = jax.nn.sigmoid(acc).astype(o_ref.dtype)


def gate_proj(a, w_rows, *, tm=ROW_TILE, tn=2 * COL_TILE):
    m, d = a.shape
    n = w_rows.shape[0]
    tn = min(tn, n)
    return pl.pallas_call(
        _gate_proj_kernel,
        out_shape=jax.ShapeDtypeStruct((m, n), BF16),
        grid=(m // tm, n // tn),
        in_specs=[pl.BlockSpec((tm, d), lambda i, j: (i, 0)),
                  pl.BlockSpec((tn, d), lambda i, j: (j, 0))],
        out_specs=pl.BlockSpec((tm, tn), lambda i, j: (i, j)),
        compiler_params=_params("parallel", "parallel"),
        name="gate_proj",
    )(a, w_rows)


def _in_proj_kernel(x_ref, g_ref, w_ref, wf_ref, qv_ref, k_ref, f_ref, a_ref, *, n_q, n_qv, scale):
    j = pl.program_id(1)

    @pl.when(j == 0)
    def _():
        x = x_ref[...]
        ms = jnp.mean(x * x, axis=-1, keepdims=True)
        a_ref[...] = (x * lax.rsqrt(ms + RMS_EPS) * g_ref[...]).astype(BF16)
        w_hi, w_lo, _ = _split3(wf_ref[...])
        both = _bf16_dot(a_ref[...], jnp.concatenate([w_hi, w_lo], axis=1))
        f_ref[...] = both[:, :LANES] + both[:, LANES:]

    def project():
        return lax.dot_general(a_ref[...], w_ref[...], (((1,), (1,)), ((), ())),
                               preferred_element_type=F32)

    @pl.when(j < n_qv)
    def _():
        acc = project() * jnp.where(j < n_q, F32(scale), F32(1.0))
        tm, tn = acc.shape
        for c in range(tn // LANES):
            for r in range(tm // ATT_TILE):
                blk = acc[r * ATT_TILE:(r + 1) * ATT_TILE, c * LANES:(c + 1) * LANES]
                qv_ref[c, r] = blk.T.astype(qv_ref.dtype)

    @pl.when(j >= n_qv)
    def _():
        acc = project()
        for c in range(acc.shape[1] // LANES):
            k_ref[c] = acc[:, c * LANES:(c + 1) * LANES].astype(k_ref.dtype)


def in_proj(x, g, w_rows, w_f, *, parts_w, scale, tm=ROW_TILE, tn=COL_TILE):
    m, d = x.shape
    per = parts_w // tn

    def tiles(*parts):
        return [p * per + t for p in parts for t in range(per)]

    order = tiles(0, 3, 2, 5) + tiles(1, 4)
    n_q, n_qv, n_k = 2 * per, 4 * per, 2 * per

    def w_tile(j):
        idx = order[-1]
        for t, src in reversed(list(enumerate(order[:-1]))):
            idx = jnp.where(j == t, src, idx)
        return idx

    hb = tn // LANES
    return pl.pallas_call(
        functools.partial(_in_proj_kernel, n_q=n_q, n_qv=n_qv, scale=scale),
        out_shape=(jax.ShapeDtypeStruct((n_qv * hb, m // ATT_TILE, LANES, ATT_TILE), BF16),
                   jax.ShapeDtypeStruct((n_k * hb, m, LANES), BF16),
                   jax.ShapeDtypeStruct((m, LANES), F32),
                   jax.ShapeDtypeStruct((m, d), BF16)),
        grid=(m // tm, len(order)),
        in_specs=[pl.BlockSpec((tm, d), lambda i, j: (i, 0)),
                  pl.BlockSpec((1, d), lambda i, j: (0, 0)),
                  pl.BlockSpec((tn, d), lambda i, j: (w_tile(j), 0)),
                  pl.BlockSpec((d, LANES), lambda i, j: (0, 0))],
        out_specs=(pl.BlockSpec((hb, tm // ATT_TILE, LANES, ATT_TILE),
                                lambda i, j: (jnp.minimum(j, n_qv - 1), i, 0, 0)),
                   pl.BlockSpec((hb, tm, LANES),
                                lambda i, j: (jnp.clip(j - n_qv, 0, n_k - 1), i, 0)),
                   pl.BlockSpec((tm, LANES), lambda i, j: (i, 0)),
                   pl.BlockSpec((tm, d), lambda i, j: (i, 0))),
        compiler_params=_params("parallel", "arbitrary"),
        name="in_proj",
    )(x, g.reshape(1, d), w_rows, w_f)


def _forget_cumsum_kernel(f_ref, b_ref, o_ref, carry_ref):
    t = pl.program_id(1)

    @pl.when(t == 0)
    def _():
        carry_ref[...] = jnp.zeros_like(carry_ref)

    f = f_ref[...] + b_ref[...]
    tm = f.shape[0]
    logf = jnp.minimum(f, 0.0) - jnp.log1p(jnp.exp(-jnp.abs(f)))
    logf = logf * LOG2E
    row = lax.broadcasted_iota(jnp.int32, (tm, tm), 0)
    col = lax.broadcasted_iota(jnp.int32, (tm, tm), 1)
    tri = jnp.where(col <= row, 1.0, 0.0).astype(BF16)
    hi, mid, lo = _split3(logf)
    c = (_bf16_dot(tri, hi) + (_bf16_dot(tri, mid) + _bf16_dot(tri, lo))) + carry_ref[0:1, :]
    o_ref[...] = c
    carry_ref[...] = jnp.broadcast_to(c[tm - 1:tm, :], carry_ref.shape)


def forget_cumsum(f, b_f, *, batch, tm=512):
    m = f.shape[0]
    nt = (m // batch) // tm
    b_pad = jnp.zeros((1, LANES), F32).at[0, :b_f.shape[0]].set(b_f)
    return pl.pallas_call(
        _forget_cumsum_kernel,
        out_shape=jax.ShapeDtypeStruct((m, LANES), F32),
        grid=(batch, nt),
        in_specs=[pl.BlockSpec((tm, LANES), lambda b, t: (b * nt + t, 0)),
                  pl.BlockSpec((1, LANES), lambda b, t: (0, 0))],
        out_specs=pl.BlockSpec((tm, LANES), lambda b, t: (b * nt + t, 0)),
        scratch_shapes=[pltpu.VMEM((8, LANES), F32)],
        compiler_params=_params("parallel", "arbitrary"),
        name="forget_cumsum",
    )(f, b_pad)


def _moba_bias_kernel(tab_ref, o_ref):
    h = pl.program_id(0)
    key = lax.broadcasted_iota(jnp.int32, (ATT_TILE, ATT_TILE), 0)
    qry = lax.broadcasted_iota(jnp.int32, (ATT_TILE, ATT_TILE), 1)
    far = tab_ref[h, NUM_BUCKETS - 1] * LOG2E

    def block(delta):
        dist = delta * MOBA_BLOCK + qry - key
        val = jnp.full((ATT_TILE, ATT_TILE), tab_ref[h, 0], F32)
        for k in range(1, NUM_BUCKETS):
            val = jnp.where(dist >= BUCKET_THRESHOLDS[k - 1], tab_ref[h, k], val)
        val = val * LOG2E
        if delta == 0:
            val = jnp.where(dist >= 0, val, NEG_INF)
        return val

    t = [block(delta) for delta in range(NEAR_BLOCKS + 1)]
    a = ATT_TILE
    o_ref[0, 0, 0:a, 0:a] = t[0]
    o_ref[0, 0, 0:a, a:2 * a] = t[1]
    o_ref[0, 0, a:2 * a, 0:a] = jnp.full((a, a), NEG_INF, F32)
    o_ref[0, 0, a:2 * a, a:2 * a] = t[0]
    for d in (1, 2):
        o_ref[0, d, 0:a, 0:a] = t[2 * d] - far
        o_ref[0, d, 0:a, a:2 * a] = t[2 * d + 1] - far
        o_ref[0, d, a:2 * a, 0:a] = t[2 * d - 1] - far
        o_ref[0, d, a:2 * a, a:2 * a] = t[2 * d] - far


def moba_bias(rel_bias):
    h = rel_bias.shape[0]
    return pl.pallas_call(
        _moba_bias_kernel,
        out_shape=jax.ShapeDtypeStruct((h, 3, TILE, TILE), F32),
        grid=(h,),
        in_specs=[pl.BlockSpec(memory_space=pltpu.SMEM)],
        out_specs=pl.BlockSpec((1, 3, TILE, TILE), lambda i: (i, 0, 0, 0)),
        compiler_params=_params("parallel"),
        name="moba_bias",
    )(rel_bias.astype(F32))


class _Softmax:
    def __init__(self, hh, s_ref, m_ref, l_ref, acc_ref, load_values):
        self.s_ref, self.m_ref, self.l_ref, self.acc_ref = s_ref.at[hh], m_ref.at[hh], l_ref.at[hh], acc_ref.at[hh]
        self.load_values = load_values

    def reset(self):
        self.m_ref[...] = jnp.full(self.m_ref.shape, NEG_INF, F32)
        self.l_ref[...] = jnp.zeros(self.l_ref.shape, F32)
        self.acc_ref[...] = jnp.zeros(self.acc_ref.shape, F32)

    def fold(self, slot, tile, mask=None):
        for half in range(2):
            self.fold_half(slot, tile, half, mask)

    def fold_half(self, slot, tile, half, mask=None):
        width = self.m_ref.shape[-1] // 2
        cols = slice(half * width, (half + 1) * width)
        s = self.s_ref[slot, :, cols]
        if mask is not None:
            s = jnp.where(mask[:, cols], s, NEG_INF)
        m = self.m_ref[:, cols]
        m_new = jnp.maximum(m, jnp.max(s, axis=0, keepdims=True))
        alpha = jnp.exp2(m - m_new)
        p = jnp.exp2(s - m_new)
        self.acc_ref[:, cols] = alpha * self.acc_ref[:, cols] + jnp.dot(
            self.load_values(tile), p.astype(BF16), preferred_element_type=F32)
        self.l_ref[:, cols] = alpha * self.l_ref[:, cols] + jnp.sum(p, axis=0, keepdims=True)
        self.m_ref[:, cols] = m_new

    def result(self):
        return self.acc_ref[...] * (1.0 / self.l_ref[...])


def _softmax_scratch(t):
    nh = HEADS_PER_STEP
    return [pltpu.VMEM((nh, 2, t, t), F32),
            pltpu.VMEM((nh, 1, t), F32),
            pltpu.VMEM((nh, 1, t), F32),
            pltpu.VMEM((nh, HEAD_DIM, t), F32)]


def _fold_tile_run(heads, s_ref, logits, fold, base, count, last_tile):
    def step(r, slot, look_ahead=True):
        for hh in heads:
            if look_ahead:
                s_ref[hh, 1 - slot] = logits[hh](jnp.minimum(base + r + 1, last_tile))
            fold[hh].fold(slot, base + r)

    def unrolled_body(p, carry):
        for u in range(LOOP_UNROLL):
            step(LOOP_UNROLL * p + u, (u + 1) % 2)
        return carry

    lax.fori_loop(0, count // LOOP_UNROLL, unrolled_body, 0)
    rem = count % LOOP_UNROLL
    done = count - rem

    @pl.when(rem >= 2)
    def _():
        step(done, 1)
        step(done + 1, 0)

    @pl.when(rem % 2 == 1)
    def _():
        step(count - 1, 1, look_ahead=False)


def _top_k_bias(gate, eligible, blk):
    lowest = float(jnp.finfo(F32).min)
    blk_f = blk.astype(F32)
    g = jnp.where(eligible, gate, NEG_INF)
    bias = jnp.full(gate.shape, NEG_INF, F32)
    for _ in range(MOBA_TOP_K):
        best = jnp.max(g, axis=0, keepdims=True)
        first = jnp.min(jnp.where(g == best, blk_f, float(gate.shape[0])), axis=0, keepdims=True)
        hit = blk_f == first
        bias = jnp.where(hit, 0.0, bias)
        g = jnp.where(hit, lowest, g)
    return jnp.where(eligible, bias, NEG_INF)


def _split3(x):
    hi = x.astype(BF16).astype(F32)
    mid = (x - hi).astype(BF16).astype(F32)
    lo = (x - hi - mid).astype(BF16).astype(F32)
    return hi, mid, lo


def _slab_cast_specs(weights, n_steps, step_index):
    in_specs, out_specs, out_shapes = [], [], []
    for w in weights:
        if isinstance(w, tuple):
            w, first_row, n_rows = w
        else:
            first_row, n_rows = 0, w.shape[0]
        rows = max(n_rows // n_steps, 16)
        n_slabs = n_rows // rows
        assert n_slabs * rows == n_rows and rows % 16 == 0 and n_steps % n_slabs == 0
        share = n_steps // n_slabs

        def slab(*g, share=share):
            return step_index(*g) // share

        if first_row == 0:
            in_specs.append(pl.BlockSpec((rows, w.shape[1]), lambda *g, slab=slab: (slab(*g), 0)))
        else:
            in_specs.append(pl.BlockSpec(
                (pl.Element(rows), pl.Element(w.shape[1])),
                lambda *g, first_row=first_row, rows=rows, slab=slab: (
                    pl.multiple_of(first_row + slab(*g) * rows, 8), 0)))
        out_specs.append(pl.BlockSpec((rows, w.shape[1]), lambda *g, slab=slab: (slab(*g), 0)))
        out_shapes.append(jax.ShapeDtypeStruct((n_rows, w.shape[1]), BF16))
    return in_specs, out_specs, out_shapes


def _moba_attn_kernel(*refs, n_blocks, n_cast):
    tab_ref, q_ref, k_ref, v_ref, bias_ref = refs[:5]
    o_ref = refs[5 + n_cast]
    kaug_ref, kbar_ref, kb3_ref, s_ref, m_ref, l_ref, acc_ref = refs[6 + 2 * n_cast:]
    for src, dst in zip(refs[5:5 + n_cast], refs[6 + n_cast:6 + 2 * n_cast]):
        dst[...] = src[...].astype(dst.dtype)
    _moba_attn_body(tab_ref, q_ref, k_ref, v_ref, bias_ref, o_ref,
                    kaug_ref, kbar_ref, kb3_ref, s_ref, m_ref, l_ref, acc_ref, n_blocks=n_blocks)


def _moba_attn_body(tab_ref, q_ref, k_ref, v_ref, bias_ref, o_ref,
                    kaug_ref, kbar_ref, kb3_ref, s_ref, m_ref, l_ref, acc_ref, *, n_blocks):
    hp = pl.program_id(1)
    i = pl.program_id(2)
    n_tiles = n_blocks // 2
    heads = range(HEADS_PER_STEP)

    @pl.when(i == 0)
    def _():
        lane = lax.broadcasted_iota(jnp.int32, (TILE, AUG - HEAD_DIM), 1)
        row = lax.broadcasted_iota(jnp.int32, (TILE, AUG - HEAD_DIM), 0)
        ones_lane = jnp.where(lane == n_blocks, 1.0, jnp.where(lane == n_blocks + 1, 1.0, 0.0))
        for hh in heads:
            for n in range(n_blocks):
                kbar_ref[hh, n:n + 1, :] = jnp.sum(k_ref[hh, n].astype(F32), axis=0, keepdims=True)
            hi, mid, lo = _split3(kbar_ref[hh] * (1.0 / MOBA_BLOCK))
            kb3_ref[hh, 0:n_blocks, :] = hi.astype(BF16)
            kb3_ref[hh, n_blocks:2 * n_blocks, :] = mid.astype(BF16)
            kb3_ref[hh, 2 * n_blocks:3 * n_blocks, :] = lo.astype(BF16)
            for j in range(n_tiles):
                blk_of_row = jnp.where(row < ATT_TILE, 2 * j, 2 * j + 1)
                right = jnp.where(lane == blk_of_row, 1.0, ones_lane).astype(BF16)
                kaug_ref[hh, j] = jnp.concatenate(
                    [k_ref[hh, 2 * j:2 * j + 2].reshape(TILE, HEAD_DIM), right], axis=1)

    blk = lax.broadcasted_iota(jnp.int32, (n_blocks, TILE), 0)
    qlane = lax.broadcasted_iota(jnp.int32, (n_blocks, TILE), 1)
    own = 2 * i + jnp.where(qlane >= ATT_TILE, 1, 0)
    eligible = blk < own
    r16 = lax.broadcasted_iota(jnp.int32, (16, TILE), 0)

    j1 = jnp.where(i >= 1, i - 1, i + 1)
    j2 = jnp.where(i >= 2, i - 2, i + 1)
    n_far = jnp.maximum(i - 2, 0)

    far_logits, fold = [], []
    for hh in heads:
        qt = jnp.concatenate([q_ref[hh, 0], q_ref[hh, 1]], axis=1)
        g3 = jnp.dot(kb3_ref[hh], qt, preferred_element_type=F32)
        gate = g3[0:n_blocks] + g3[n_blocks:2 * n_blocks] + g3[2 * n_blocks:3 * n_blocks]
        selb = _top_k_bias(gate, eligible, blk)

        far = jnp.full((16, TILE), tab_ref[hp * HEADS_PER_STEP + hh, NUM_BUCKETS - 1] * LOG2E, F32)
        far_hi = far.astype(BF16).astype(F32)
        far_rows = jnp.where(r16 == 0, far_hi, jnp.where(r16 == 1, far - far_hi, 0.0))
        q_aug = jnp.concatenate(
            [qt, selb.astype(BF16), far_rows.astype(BF16),
             jnp.zeros((AUG - HEAD_DIM - n_blocks - 16, TILE), BF16)], axis=0)

        def values(j, hh=hh):
            return jnp.concatenate([v_ref[hh, 2 * j], v_ref[hh, 2 * j + 1]], axis=1)

        def head_far_logits(j, hh=hh, q_aug=q_aug):
            return jnp.dot(kaug_ref[hh, j], q_aug, preferred_element_type=F32)

        sm = _Softmax(hh, s_ref, m_ref, l_ref, acc_ref, values)
        far_logits.append(head_far_logits)
        fold.append(sm)

        sel_own = jnp.sum(jnp.where(blk == 2 * i, selb, 0.0), axis=0, keepdims=True)
        sel_own = jnp.where(qlane[0:1] >= ATT_TILE, sel_own, 0.0)
        kd = k_ref[hh, pl.ds(2 * i, 2)].reshape(TILE, HEAD_DIM)
        sd = jnp.dot(kd, qt, preferred_element_type=F32) + bias_ref[hh, 0]
        s_ref[hh, 0, 0:ATT_TILE, :] = sd[0:ATT_TILE] + sel_own
        s_ref[hh, 0, ATT_TILE:TILE, :] = sd[ATT_TILE:TILE]
        sm.reset()

    for hh in heads:
        s_ref[hh, 1] = far_logits[hh](j1) + bias_ref[hh, 1]
        fold[hh].fold(0, i)
    for hh in heads:
        s_ref[hh, 0] = far_logits[hh](j2) + bias_ref[hh, 2]
        fold[hh].fold(1, j1)
    for hh in heads:
        s_ref[hh, 1] = far_logits[hh](0)
        fold[hh].fold(0, j2)

    _fold_tile_run(heads, s_ref, far_logits, fold, 0, n_far, n_tiles - 1)

    for hh in heads:
        o_ref[:, hh * HEAD_DIM:(hh + 1) * HEAD_DIM] = fold[hh].result().T.astype(o_ref.dtype)


def moba_attn(qv_t, k, bias, rel_bias, *, batch, n_heads, q_off, k_off, v_off, cast=()):
    mb = k.shape[1]
    nb = mb // batch
    nt = nb // 2
    m = mb * ATT_TILE
    hb = HEADS_PER_STEP
    nhp = n_heads // hb
    assert nb + 16 <= AUG - HEAD_DIM and nt >= 4
    assert n_heads % hb == 0 and q_off % hb == 0 and k_off % hb == 0 and v_off % hb == 0
    c_in, c_out, c_shape = _slab_cast_specs(cast, batch * nhp * nt, lambda b, h, i: (b * nhp + h) * nt + i)
    return pl.pallas_call(
        functools.partial(_moba_attn_kernel, n_blocks=nb, n_cast=len(cast)),
        out_shape=[jax.ShapeDtypeStruct((m, n_heads * HEAD_DIM), BF16)] + c_shape,
        grid=(batch, nhp, nt),
        in_specs=[pl.BlockSpec(memory_space=pltpu.SMEM),
                  pl.BlockSpec((hb, 2, HEAD_DIM, ATT_TILE),
                               lambda b, h, i: (q_off // hb + h, b * nt + i, 0, 0)),
                  pl.BlockSpec((hb, nb, ATT_TILE, HEAD_DIM), lambda b, h, i: (k_off // hb + h, b, 0, 0)),
                  pl.BlockSpec((hb, nb, HEAD_DIM, ATT_TILE), lambda b, h, i: (v_off // hb + h, b, 0, 0)),
                  pl.BlockSpec((hb, 3, TILE, TILE), lambda b, h, i: (h, 0, 0, 0))] + c_in,
        out_specs=[pl.BlockSpec((TILE, hb * HEAD_DIM), lambda b, h, i: (b * nt + i, h))] + c_out,
        scratch_shapes=[pltpu.VMEM((hb, nt, TILE, AUG), BF16),
                        pltpu.VMEM((hb, nb, HEAD_DIM), F32),
                        pltpu.VMEM((hb, 3 * nb, HEAD_DIM), BF16)] + _softmax_scratch(TILE),
        compiler_params=_params("parallel", "parallel", "arbitrary"),
        name="moba_attn",
    )(rel_bias.astype(F32), qv_t, k, qv_t, bias, *[c[0] if isinstance(c, tuple) else c for c in cast])


def _fox_attn_kernel(*refs, n_tiles, n_cast):
    q_ref, k_ref, v_ref, c_ref = refs[:4]
    o_ref = refs[4 + n_cast]
    kaug_ref, bound_ref, s_ref, m_ref, l_ref, acc_ref = refs[5 + 2 * n_cast:]
    for src, dst in zip(refs[4:4 + n_cast], refs[5 + n_cast:5 + 2 * n_cast]):
        dst[...] = src[...].astype(dst.dtype)
    _fox_attn_body(q_ref, k_ref, v_ref, c_ref, o_ref, kaug_ref, bound_ref, s_ref, m_ref, l_ref,
                   acc_ref, n_tiles=n_tiles)


def _fox_attn_body(q_ref, k_ref, v_ref, c_ref, o_ref, kaug_ref, bound_ref, s_ref, m_ref, l_ref,
                   acc_ref, *, n_tiles):
    hp = pl.program_id(1)
    i = pl.program_id(2)
    t = TILE
    heads = range(HEADS_PER_STEP)

    lane1 = lax.broadcasted_iota(jnp.int32, (1, LANES), 1)

    @pl.when(i == 0)
    def _():
        nh = len(heads)
        src = lax.broadcasted_iota(jnp.int32, (3 * LANES, nh * LANES), 0)
        dst = lax.broadcasted_iota(jnp.int32, (3 * LANES, nh * LANES), 1)
        place = jnp.where((src % LANES == hp * nh + dst // LANES) & (src // LANES == dst % LANES),
                          1.0, 0.0).astype(BF16)
        c_first = [jnp.zeros((1, LANES), F32) for _ in heads]
        c_last = [jnp.zeros((1, LANES), F32) for _ in heads]
        k_norm2 = [jnp.zeros((t, 1), F32) for _ in heads]
        for j in range(n_tiles):
            c_all = c_ref[j * t:(j + 1) * t, :]
            terms = jnp.concatenate([x.astype(BF16) for x in _split3(-c_all)], axis=1)
            right = jnp.dot(terms, place, preferred_element_type=F32).astype(BF16)
            for hh in heads:
                k = k_ref[hh, 2 * j:2 * j + 2].reshape(t, HEAD_DIM)
                kaug_ref[hh, j] = jnp.concatenate([k, right[:, hh * LANES:(hh + 1) * LANES]], axis=1)
                kf = k.astype(F32)
                k_norm2[hh] = jnp.maximum(k_norm2[hh], jnp.sum(kf * kf, axis=1, keepdims=True))
                ends = jnp.concatenate([c_all[0:1, :], c_all[t - 1:t, :]], axis=0)
                ends = jnp.sum(jnp.where(lane1 == hp * nh + hh, ends, 0.0), axis=1, keepdims=True)
                c_first[hh] = jnp.where(lane1 == j, ends[0:1, :], c_first[hh])
                c_last[hh] = jnp.where(lane1 == j, ends[1:2, :], c_last[hh])
        for hh in heads:
            bound_ref[hh, 0:1, :] = c_first[hh]
            bound_ref[hh, 1:2, :] = c_last[hh]
            bound_ref[hh, 2:3, :] = jnp.broadcast_to(
                jnp.max(k_norm2[hh], axis=0, keepdims=True), (1, LANES))

    r_aug = lax.broadcasted_iota(jnp.int32, (AUG - HEAD_DIM, t), 0)
    ones_rows = jnp.where(r_aug < 3, 1.0, 0.0).astype(BF16)

    logits, fold, skippable = [], [], []
    for hh in heads:
        qt = jnp.concatenate([q_ref[hh, 0], q_ref[hh, 1]], axis=1)
        q_aug = jnp.concatenate([qt, ones_rows], axis=0)

        qf = qt.astype(F32)
        q_norm2 = jnp.max(jnp.sum(qf * qf, axis=0, keepdims=True), axis=1, keepdims=True)
        c_here = jnp.sum(jnp.where(lane1 == i, bound_ref[hh, 0:1, :], 0.0), axis=1, keepdims=True)
        gap = (bound_ref[hh, 1:2, :] - c_here) - UNDERFLOW_LOG2
        dead = (lane1 < i) & (gap > 0.0) & (gap * gap > 4.0 * q_norm2 * bound_ref[hh, 2:3, :])
        skippable.append(jnp.sum(jnp.where(dead, 1.0, 0.0), axis=1, keepdims=True))

        def head_logits(n, hh=hh, q_aug=q_aug):
            return jnp.dot(kaug_ref[hh, n], q_aug, preferred_element_type=F32)

        def values(n, hh=hh):
            return jnp.concatenate([v_ref[hh, 2 * n], v_ref[hh, 2 * n + 1]], axis=1)

        logits.append(head_logits)
        fold.append(_Softmax(hh, s_ref, m_ref, l_ref, acc_ref, values))

    key = lax.broadcasted_iota(jnp.int32, (t, t), 0)
    qry = lax.broadcasted_iota(jnp.int32, (t, t), 1)
    causal = key <= qry
    j0 = functools.reduce(jnp.minimum, skippable)[0, 0].astype(jnp.int32)
    n_past = i - j0
    for hh in heads:
        s_ref[hh, 0] = logits[hh](i)
        fold[hh].reset()
    for hh in heads:
        s_ref[hh, 1] = logits[hh](j0)
        fold[hh].fold(0, i, mask=causal)

    _fold_tile_run(heads, s_ref, logits, fold, j0, n_past, n_tiles - 1)

    for hh in heads:
        o_ref[:, hh * HEAD_DIM:(hh + 1) * HEAD_DIM] = fold[hh].result().T.astype(o_ref.dtype)


def fox_attn(qv_t, k, c_rep, *, batch, n_heads, q_off, k_off, v_off, cast=()):
    mb = k.shape[1]
    nb = mb // batch
    nt = nb // 2
    m = mb * ATT_TILE
    t = TILE
    hb = HEADS_PER_STEP
    nhp = n_heads // hb
    assert n_heads % hb == 0 and q_off % hb == 0 and k_off % hb == 0 and v_off % hb == 0
    assert nt <= LANES and n_heads <= LANES
    c_in, c_out, c_shape = _slab_cast_specs(cast, batch * nhp * nt, lambda b, h, i: (b * nhp + h) * nt + i)
    return pl.pallas_call(
        functools.partial(_fox_attn_kernel, n_tiles=nt, n_cast=len(cast)),
        out_shape=[jax.ShapeDtypeStruct((m, n_heads * HEAD_DIM), BF16)] + c_shape,
        grid=(batch, nhp, nt),
        in_specs=[pl.BlockSpec((hb, 2, HEAD_DIM, ATT_TILE),
                               lambda b, h, i: (q_off // hb + h, b * nt + i, 0, 0)),
                  pl.BlockSpec((hb, nb, ATT_TILE, HEAD_DIM), lambda b, h, i: (k_off // hb + h, b, 0, 0)),
                  pl.BlockSpec((hb, nb, HEAD_DIM, ATT_TILE), lambda b, h, i: (v_off // hb + h, b, 0, 0)),
                  pl.BlockSpec((nb * ATT_TILE, LANES), lambda b, h, i: (b, 0))] + c_in,
        out_specs=[pl.BlockSpec((t, hb * HEAD_DIM), lambda b, h, i: (b * nt + i, h))] + c_out,
        scratch_shapes=[pltpu.VMEM((hb, nt, t, AUG), BF16),
                        pltpu.VMEM((hb, 8, LANES), F32)] + _softmax_scratch(t),
        compiler_params=_params("parallel", "parallel", "arbitrary"),
        name="fox_attn",
    )(qv_t, k, qv_t, c_rep, *[c[0] if isinstance(c, tuple) else c for c in cast])


def _gated_merge_kernel(oa_ref, of_ref, wa_ref, wf_ref, ga_ref, gf_ref, o_ref):
    ua = jnp.dot(oa_ref[...], wa_ref[...], preferred_element_type=F32)
    uf = jnp.dot(of_ref[...], wf_ref[...], preferred_element_type=F32)
    o_ref[...] = (ga_ref[...].astype(F32) * ua + gf_ref[...].astype(F32) * uf).astype(o_ref.dtype)


def gated_merge(o_a, o_f, w_a, w_f, gates, *, tm=ROW_TILE, tn=COL_TILE):
    m, ka = o_a.shape
    kf = o_f.shape[1]
    n = w_a.shape[1]
    nj = n // tn
    return pl.pallas_call(
        _gated_merge_kernel,
        out_shape=jax.ShapeDtypeStruct((m, n), BF16),
        grid=(m // tm, nj),
        in_specs=[pl.BlockSpec((tm, ka), lambda i, j: (i, 0)),
                  pl.BlockSpec((tm, kf), lambda i, j: (i, 0)),
                  pl.BlockSpec((ka, tn), lambda i, j: (0, j)),
                  pl.BlockSpec((kf, tn), lambda i, j: (0, j)),
                  pl.BlockSpec((tm, tn), lambda i, j: (i, j)),
                  pl.BlockSpec((tm, tn), lambda i, j: (i, nj + j))],
        out_specs=pl.BlockSpec((tm, tn), lambda i, j: (i, j)),
        compiler_params=_params("parallel", "parallel"),
        name="gated_merge",
    )(o_a, o_f, w_a, w_f, gates, gates)


def _mm_res_kernel(lhs_ref, w_ref, res_ref, g_ref, o_ref, *, final_norm):
    kk = pl.program_id(1)

    @pl.when(kk == 0)
    def _():
        o_ref[...] = res_ref[...]

    o_ref[...] += jnp.dot(lhs_ref[...], w_ref[...], preferred_element_type=F32)

    if final_norm:
        @pl.when(kk == pl.num_programs(1) - 1)
        def _():
            hres = o_ref[...]
            ms = jnp.mean(hres * hres, axis=-1, keepdims=True)
            o_ref[...] = hres * lax.rsqrt(ms + RMS_EPS) * g_ref[...]


def mm_res(lhs, w, res, g=None, *, tm=ROW_TILE, tk=1024):
    m, k = lhs.shape
    n = w.shape[1]
    tk = min(tk, k)
    final_norm = g is not None
    if g is None:
        g = jnp.ones((n,), F32)
    return pl.pallas_call(
        functools.partial(_mm_res_kernel, final_norm=final_norm),
        out_shape=jax.ShapeDtypeStruct((m, n), F32),
        grid=(m // tm, k // tk),
        in_specs=[pl.BlockSpec((tm, tk), lambda i, kk: (i, kk)),
                  pl.BlockSpec((tk, n), lambda i, kk: (kk, 0)),
                  pl.BlockSpec((tm, n), lambda i, kk: (i, 0)),
                  pl.BlockSpec((1, n), lambda i, kk: (0, 0))],
        out_specs=pl.BlockSpec((tm, n), lambda i, kk: (i, 0)),
        compiler_params=_params("parallel", "arbitrary"),
        name="mm_res",
    )(lhs, w, res, g.reshape(1, n))


def _cross_attn_kernel(h_ref, g_ref, wq_ref, kv_ref, wo_ref, o_ref, *, n_heads):
    width = n_heads * HEAD_DIM
    hres = h_ref[...]
    ms = jnp.mean(hres * hres, axis=-1, keepdims=True)
    c = (hres * lax.rsqrt(ms + RMS_EPS) * g_ref[...]).astype(BF16)
    q_all = (jnp.dot(c, wq_ref[...], preferred_element_type=F32)
             * (HEAD_DIM ** -0.5 * LOG2E)).astype(BF16)
    heads = []
    for h in range(n_heads):
        q = q_all[:, h * HEAD_DIM:(h + 1) * HEAD_DIM]
        k = kv_ref[0, :, h * HEAD_DIM:(h + 1) * HEAD_DIM]
        v = kv_ref[0, :, width + h * HEAD_DIM:width + (h + 1) * HEAD_DIM]
        s = lax.dot_general(q, k, (((1,), (1,)), ((), ())), preferred_element_type=F32)
        m = jnp.max(s, axis=-1, keepdims=True)
        p = jnp.exp2(s - m)
        l = jnp.sum(p, axis=-1, keepdims=True)
        o = jnp.dot(p.astype(BF16), v, preferred_element_type=F32) * (1.0 / l)
        heads.append(o.astype(BF16))
    o_all = jnp.concatenate(heads, axis=1)
    o_ref[...] = hres + jnp.dot(o_all, wo_ref[...], preferred_element_type=F32)


def cross_attn(h, g, w_q, kv, w_o, *, batch, n_heads, tm=512):
    m, d = h.shape
    width = w_q.shape[1]
    n_mem = kv.shape[0] // batch
    tiles_per_batch = (m // batch) // tm
    kv3 = kv.reshape(batch, n_mem, 2 * width)
    return pl.pallas_call(
        functools.partial(_cross_attn_kernel, n_heads=n_heads),
        out_shape=jax.ShapeDtypeStruct((m, d), F32),
        grid=(m // tm,),
        in_specs=[pl.BlockSpec((tm, d), lambda i: (i, 0)),
                  pl.BlockSpec((1, d), lambda i: (0, 0)),
                  pl.BlockSpec((d, width), lambda i: (0, 0)),
                  pl.BlockSpec((1, n_mem, 2 * width), lambda i: (i // tiles_per_batch, 0, 0)),
                  pl.BlockSpec((width, d), lambda i: (0, 0))],
        out_specs=pl.BlockSpec((tm, d), lambda i: (i, 0)),
        compiler_params=_params("parallel"),
        name="cross_attn",
    )(h, g.reshape(1, d), w_q, kv3, w_o)


def kernel(x, mem, g_mix, w_in, b_forget, w_branch_moba, w_branch_fox, w_mix_out, rel_bias,
           g_cross, g_mem, w_cq, w_ck, w_cv, w_co, g_mlp, w_ff1, w_ff2, g_final):
    batch, seq, d = x.shape
    depth = w_in.shape[0]
    n_heads = rel_bias.shape[0]
    n_fox = b_forget.shape[1]
    wm = n_heads * HEAD_DIM
    wf = n_fox * HEAD_DIM
    m = batch * seq
    assert wm == wf and wm % COL_TILE == 0
    scale = HEAD_DIM ** -0.5
    mem2 = mem.reshape(-1, d)

    bias = moba_bias(rel_bias)
    h = x.reshape(m, d)
    for l in range(depth):
        wt = jnp.swapaxes(w_in[l], 0, 1)
        qkv_w = 3 * (wm + wf)
        w_fl = jnp.zeros((d, LANES), F32).at[:, :n_fox].set(w_in[l][:, qkv_w:qkv_w + n_fox])
        qv_t, k_hm, f_logit, a_mix = in_proj(h, g_mix[l], wt[:qkv_w].astype(BF16), w_fl,
                                             parts_w=wm, scale=scale * LOG2E)
        k_hm = k_hm.reshape(k_hm.shape[0], m // ATT_TILE, ATT_TILE, HEAD_DIM)
        c_rep = forget_cumsum(f_logit, b_forget[l], batch=batch)

        o_a, w_ff1_16, w_mix_16, w_bm16, w_bf16 = moba_attn(
            qv_t, k_hm, bias, rel_bias, batch=batch, n_heads=n_heads,
            q_off=0, k_off=0, v_off=n_heads + n_fox,
            cast=(w_ff1[l], w_mix_out[l], w_branch_moba[l], w_branch_fox[l]))
        o_f, w_ff2_16, w_g16, w_cq16, w_co16 = fox_attn(
            qv_t, k_hm, c_rep, batch=batch, n_heads=n_fox,
            q_off=n_heads, k_off=n_heads, v_off=2 * n_heads + n_fox,
            cast=(w_ff2[l], (wt, qkv_w + n_fox, 2 * d), w_cq[l], w_co[l]))
        gates = gate_proj(a_mix, w_g16)
        merged = gated_merge(o_a, o_f, w_bm16, w_bf16, gates)
        h = mm_res(merged, w_mix_16, h)

        cw = w_cq.shape[2]
        w_kv = jnp.concatenate([w_ck[l], w_cv[l]], axis=1).astype(BF16)
        kv = rms_proj(mem2, g_mem[l], w_kv, tn=2 * cw)
        h = cross_attn(h, g_cross[l], w_cq16, kv, w_co16, batch=batch, n_heads=cw // HEAD_DIM)

        u = rms_proj(h, g_mlp[l], w_ff1_16, act="relu2")
        h = mm_res(u, w_ff2_16, h, g_final if l == depth - 1 else None)
    return h.reshape(batch, seq, d)
```

```python
import functools
import math

import jax
import jax.numpy as jnp
from jax import lax
from jax.experimental import pallas as pl
from jax.experimental.pallas import tpu as pltpu

F32 = jnp.float32
BF16 = jnp.bfloat16

HEAD_DIM = 128
MOBA_BLOCK = 256
MOBA_TOP_K = 3
NUM_BUCKETS = 32
MAX_DISTANCE = 1024
RMS_EPS = 1e-6
LOG2E = math.log2(math.e)
NEG_INF = -1e30
LANES = 128
ATT_TILE = 256
TILE = 2 * ATT_TILE
AUG = 256
HEADS_PER_STEP = 2
LOOP_UNROLL = 4
UNDERFLOW_LOG2 = 160.0
NEAR_BLOCKS = 5
VMEM_LIMIT = 56 * 1024 * 1024
ROW_TILE = 1024
COL_TILE = 1024


def _bucket_thresholds():
    max_exact = NUM_BUCKETS // 2
    thr = list(range(1, max_exact + 1))
    for k in range(max_exact + 1, NUM_BUCKETS):
        v = max_exact * (MAX_DISTANCE / max_exact) ** ((k - max_exact) / (NUM_BUCKETS - max_exact))
        n = int(math.floor(v))
        while max_exact + int(math.log(n / max_exact) / math.log(MAX_DISTANCE / max_exact)
                              * (NUM_BUCKETS - max_exact)) < k:
            n += 1
        thr.append(n)
    return tuple(thr)


BUCKET_THRESHOLDS = _bucket_thresholds()
assert (NEAR_BLOCKS - 1) * MOBA_BLOCK + 1 >= BUCKET_THRESHOLDS[-1]


def _params(*sem):
    return pltpu.CompilerParams(dimension_semantics=sem, vmem_limit_bytes=VMEM_LIMIT)


def _bf16_dot(a, b):
    return jnp.dot(a.astype(BF16), b.astype(BF16), preferred_element_type=F32)


def _rms_proj_kernel(x_ref, g_ref, w_ref, o_ref, a_ref, *, act):
    j = pl.program_id(1)

    @pl.when(j == 0)
    def _():
        x = x_ref[...]
        ms = jnp.mean(x * x, axis=-1, keepdims=True)
        a_ref[...] = (x * lax.rsqrt(ms + RMS_EPS) * g_ref[...]).astype(BF16)

    acc = jnp.dot(a_ref[...], w_ref[...], preferred_element_type=F32)
    if act == "relu2":
        acc = jnp.square(jnp.maximum(acc, 0.0))
    o_ref[...] = acc.astype(o_ref.dtype)


def rms_proj(x, g, w, *, act=None, tm=ROW_TILE, tn=COL_TILE):
    m, d = x.shape
    n = w.shape[1]
    tm, tn = min(tm, m), min(tn, n)
    return pl.pallas_call(
        functools.partial(_rms_proj_kernel, act=act),
        out_shape=jax.ShapeDtypeStruct((m, n), BF16),
        grid=(m // tm, n // tn),
        in_specs=[pl.BlockSpec((tm, d), lambda i, j: (i, 0)),
                  pl.BlockSpec((1, d), lambda i, j: (0, 0)),
                  pl.BlockSpec((d, tn), lambda i, j: (0, j))],
        out_specs=pl.BlockSpec((tm, tn), lambda i, j: (i, j)),
        scratch_shapes=[pltpu.VMEM((tm, d), BF16)],
        compiler_params=_params("parallel", "arbitrary"),
        name="rms_proj",
    )(x, g.reshape(1, d), w)


def _gate_proj_kernel(a_ref, w_ref, o_ref):
    acc = lax.dot_general(a_ref[...], w_ref[...], (((1,), (1,)), ((), ())),
                          preferred_element_type=F32)
    o_ref[...] = jax.nn.sigmoid(acc).astype(o_ref.dtype)


def gate_proj(a, w_rows, *, tm=ROW_TILE, tn=2 * COL_TILE):
    m, d = a.shape
    n = w_rows.shape[0]
    tn = min(tn, n)
    return pl.pallas_call(
        _gate_proj_kernel,
        out_shape=jax.ShapeDtypeStruct((m, n), BF16),
        grid=(m // tm, n // tn),
        in_specs=[pl.BlockSpec((tm, d), lambda i, j: (i, 0)),
                  pl.BlockSpec((tn, d), lambda i, j: (j, 0))],
        out_specs=pl.BlockSpec((tm, tn), lambda i, j: (i, j)),
        compiler_params=_params("parallel", "parallel"),
        name="gate_proj",
    )(a, w_rows)


def _in_proj_kernel(x_ref, g_ref, w_ref, wf_ref, qv_ref, k_ref, f_ref, a_ref, *, n_q, n_qv, scale):
    j = pl.program_id(1)

    @pl.when(j == 0)
    def _():
        x = x_ref[...]
        ms = jnp.mean(x * x, axis=-1, keepdims=True)
        a_ref[...] = (x * lax.rsqrt(ms + RMS_EPS) * g_ref[...]).astype(BF16)
        w_hi, w_lo, _ = _split3(wf_ref[...])
        both = _bf16_dot(a_ref[...], jnp.concatenate([w_hi, w_lo], axis=1))
        f_ref[...] = both[:, :LANES] + both[:, LANES:]

    def project():
        return lax.dot_general(a_ref[...], w_ref[...], (((1,), (1,)), ((), ())),
                               preferred_element_type=F32)

    @pl.when(j < n_qv)
    def _():
        acc = project() * jnp.where(j < n_q, F32(scale), F32(1.0))
        tm, tn = acc.shape
        for c in range(tn // LANES):
            for r in range(tm // ATT_TILE):
                blk = acc[r * ATT_TILE:(r + 1) * ATT_TILE, c * LANES:(c + 1) * LANES]
                qv_ref[c, r] = blk.T.astype(qv_ref.dtype)

    @pl.when(j >= n_qv)
    def _():
        acc = project()
        for c in range(acc.shape[1] // LANES):
            k_ref[c] = acc[:, c * LANES:(c + 1) * LANES].astype(k_ref.dtype)


def in_proj(x, g, w_rows, w_f, *, parts_w, scale, tm=ROW_TILE, tn=COL_TILE):
    m, d = x.shape
    per = parts_w // tn

    def tiles(*parts):
        return [p * per + t for p in parts for t in range(per)]

    order = tiles(0, 3, 2, 5) + tiles(1, 4)
    n_q, n_qv, n_k = 2 * per, 4 * per, 2 * per

    def w_tile(j):
        idx = order[-1]
        for t, src in reversed(list(enumerate(order[:-1]))):
            idx = jnp.where(j == t, src, idx)
        return idx

    hb = tn // LANES
    return pl.pallas_call(
        functools.partial(_in_proj_kernel, n_q=n_q, n_qv=n_qv, scale=scale),
        out_shape=(jax.ShapeDtypeStruct((n_qv * hb, m // ATT_TILE, LANES, ATT_TILE), BF16),
                   jax.ShapeDtypeStruct((n_k * hb, m, LANES), BF16),
                   jax.ShapeDtypeStruct((m, LANES), F32),
                   jax.ShapeDtypeStruct((m, d), BF16)),
        grid=(m // tm, len(order)),
        in_specs=[pl.BlockSpec((tm, d), lambda i, j: (i, 0)),
                  pl.BlockSpec((1, d), lambda i, j: (0, 0)),
                  pl.BlockSpec((tn, d), lambda i, j: (w_tile(j), 0)),
                  pl.BlockSpec((d, LANES), lambda i, j: (0, 0))],
        out_specs=(pl.BlockSpec((hb, tm // ATT_TILE, LANES, ATT_TILE),
                                lambda i, j: (jnp.minimum(j, n_qv - 1), i, 0, 0)),
                   pl.BlockSpec((hb, tm, LANES),
                                lambda i, j: (jnp.clip(j - n_qv, 0, n_k - 1), i, 0)),
                   pl.BlockSpec((tm, LANES), lambda i, j: (i, 0)),
                   pl.BlockSpec((tm, d), lambda i, j: (i, 0))),
        compiler_params=_params("parallel", "arbitrary"),
        name="in_proj",
    )(x, g.reshape(1, d), w_rows, w_f)


def _forget_cumsum_kernel(f_ref, b_ref, o_ref, carry_ref):
    t = pl.program_id(1)

    @pl.when(t == 0)
    def _():
        carry_ref[...] = jnp.zeros_like(carry_ref)

    f = f_ref[...] + b_ref[...]
    tm = f.shape[0]
    logf = jnp.minimum(f, 0.0) - jnp.log1p(jnp.exp(-jnp.abs(f)))
    logf = logf * LOG2E
    row = lax.broadcasted_iota(jnp.int32, (tm, tm), 0)
    col = lax.broadcasted_iota(jnp.int32, (tm, tm), 1)
    tri = jnp.where(col <= row, 1.0, 0.0).astype(BF16)
    hi, mid, lo = _split3(logf)
    c = (_bf16_dot(tri, hi) + (_bf16_dot(tri, mid) + _bf16_dot(tri, lo))) + carry_ref[0:1, :]
    o_ref[...] = c
    carry_ref[...] = jnp.broadcast_to(c[tm - 1:tm, :], carry_ref.shape)


def forget_cumsum(f, b_f, *, batch, tm=512):
    m = f.shape[0]
    nt = (m // batch) // tm
    b_pad = jnp.zeros((1, LANES), F32).at[0, :b_f.shape[0]].set(b_f)
    return pl.pallas_call(
        _forget_cumsum_kernel,
        out_shape=jax.ShapeDtypeStruct((m, LANES), F32),
        grid=(batch, nt),
        in_specs=[pl.BlockSpec((tm, LANES), lambda b, t: (b * nt + t, 0)),
                  pl.BlockSpec((1, LANES), lambda b, t: (0, 0))],
        out_specs=pl.BlockSpec((tm, LANES), lambda b, t: (b * nt + t, 0)),
        scratch_shapes=[pltpu.VMEM((8, LANES), F32)],
        compiler_params=_params("parallel", "arbitrary"),
        name="forget_cumsum",
    )(f, b_pad)


def _moba_bias_kernel(tab_ref, o_ref):
    h = pl.program_id(0)
    key = lax.broadcasted_iota(jnp.int32, (ATT_TILE, ATT_TILE), 0)
    qry = lax.broadcasted_iota(jnp.int32, (ATT_TILE, ATT_TILE), 1)
    far = tab_ref[h, NUM_BUCKETS - 1] * LOG2E

    def block(delta):
        dist = delta * MOBA_BLOCK + qry - key
        val = jnp.full((ATT_TILE, ATT_TILE), tab_ref[h, 0], F32)
        for k in range(1, NUM_BUCKETS):
            val = jnp.where(dist >= BUCKET_THRESHOLDS[k - 1], tab_ref[h, k], val)
        val = val * LOG2E
        if delta == 0:
            val = jnp.where(dist >= 0, val, NEG_INF)
        return val

    t = [block(delta) for delta in range(NEAR_BLOCKS + 1)]
    a = ATT_TILE
    o_ref[0, 0, 0:a, 0:a] = t[0]
    o_ref[0, 0, 0:a, a:2 * a] = t[1]
    o_ref[0, 0, a:2 * a, 0:a] = jnp.full((a, a), NEG_INF, F32)
    o_ref[0, 0, a:2 * a, a:2 * a] = t[0]
    for d in (1, 2):
        o_ref[0, d, 0:a, 0:a] = t[2 * d] - far
        o_ref[0, d, 0:a, a:2 * a] = t[2 * d + 1] - far
        o_ref[0, d, a:2 * a, 0:a] = t[2 * d - 1] - far
        o_ref[0, d, a:2 * a, a:2 * a] = t[2 * d] - far


def moba_bias(rel_bias):
    h = rel_bias.shape[0]
    return pl.pallas_call(
        _moba_bias_kernel,
        out_shape=jax.ShapeDtypeStruct((h, 3, TILE, TILE), F32),
        grid=(h,),
        in_specs=[pl.BlockSpec(memory_space=pltpu.SMEM)],
        out_specs=pl.BlockSpec((1, 3, TILE, TILE), lambda i: (i, 0, 0, 0)),
        compiler_params=_params("parallel"),
        name="moba_bias",
    )(rel_bias.astype(F32))


class _Softmax:
    def __init__(self, hh, s_ref, m_ref, l_ref, acc_ref, load_values):
        self.s_ref, self.m_ref, self.l_ref, self.acc_ref = s_ref.at[hh], m_ref.at[hh], l_ref.at[hh], acc_ref.at[hh]
        self.load_values = load_values

    def reset(self):
        self.m_ref[...] = jnp.full(self.m_ref.shape, NEG_INF, F32)
        self.l_ref[...] = jnp.zeros(self.l_ref.shape, F32)
        self.acc_ref[...] = jnp.zeros(self.acc_ref.shape, F32)

    def fold(self, slot, tile, mask=None):
        for half in range(2):
            self.fold_half(slot, tile, half, mask)

    def fold_half(self, slot, tile, half, mask=None):
        width = self.m_ref.shape[-1] // 2
        cols = slice(half * width, (half + 1) * width)
        s = self.s_ref[slot, :, cols]
        if mask is not None:
            s = jnp.where(mask[:, cols], s, NEG_INF)
        m = self.m_ref[:, cols]
        m_new = jnp.maximum(m, jnp.max(s, axis=0, keepdims=True))
        alpha = jnp.exp2(m - m_new)
        p = jnp.exp2(s - m_new)
        self.acc_ref[:, cols] = alpha * self.acc_ref[:, cols] + jnp.dot(
            self.load_values(tile), p.astype(BF16), preferred_element_type=F32)
        self.l_ref[:, cols] = alpha * self.l_ref[:, cols] + jnp.sum(p, axis=0, keepdims=True)
        self.m_ref[:, cols] = m_new

    def result(self):
        return self.acc_ref[...] * (1.0 / self.l_ref[...])


def _softmax_scratch(t):
    nh = HEADS_PER_STEP
    return [pltpu.VMEM((nh, 2, t, t), F32),
            pltpu.VMEM((nh, 1, t), F32),
            pltpu.VMEM((nh, 1, t), F32),
            pltpu.VMEM((nh, HEAD_DIM, t), F32)]


def _fold_tile_run(heads, s_ref, logits, fold, base, count, last_tile):
    def step(r, slot, look_ahead=True):
        for hh in heads:
            if look_ahead:
                s_ref[hh, 1 - slot] = logits[hh](jnp.minimum(base + r + 1, last_tile))
            fold[hh].fold(slot, base + r)

    def unrolled_body(p, carry):
        for u in range(LOOP_UNROLL):
            step(LOOP_UNROLL * p + u, (u + 1) % 2)
        return carry

    lax.fori_loop(0, count // LOOP_UNROLL, unrolled_body, 0)
    assert LOOP_UNROLL == 4
    rem = count % LOOP_UNROLL
    done = count - rem

    @pl.when(rem >= 2)
    def _():
        step(done, 1)
        step(done + 1, 0)

    @pl.when(rem % 2 == 1)
    def _():
        step(count - 1, 1, look_ahead=False)


def _top_k_bias(gate, eligible, blk):
    lowest = float(jnp.finfo(F32).min)
    blk_f = blk.astype(F32)
    g = jnp.where(eligible, gate, NEG_INF)
    bias = jnp.full(gate.shape, NEG_INF, F32)
    for _ in range(MOBA_TOP_K):
        best = jnp.max(g, axis=0, keepdims=True)
        first = jnp.min(jnp.where(g == best, blk_f, float(gate.shape[0])), axis=0, keepdims=True)
        hit = blk_f == first
        bias = jnp.where(hit, 0.0, bias)
        g = jnp.where(hit, lowest, g)
    return jnp.where(eligible, bias, NEG_INF)


def _split3(x):
    hi = x.astype(BF16).astype(F32)
    mid = (x - hi).astype(BF16).astype(F32)
    lo = (x - hi - mid).astype(BF16).astype(F32)
    return hi, mid, lo


def _slab_cast_specs(weights, n_steps, inner_steps, step_index):
    in_specs, out_specs, out_shapes = [], [], []
    for w in weights:
        if isinstance(w, tuple):
            w, first_row, n_rows = w
        else:
            first_row, n_rows = 0, w.shape[0]
        rows = max(n_rows // n_steps, 16)
        n_slabs = n_rows // rows
        assert n_slabs * rows == n_rows and rows % 16 == 0 and n_steps % n_slabs == 0
        share = n_steps // n_slabs
        assert inner_steps % share == 0

        def slab(*g, share=share):
            return step_index(*g) // share

        if first_row == 0:
            in_specs.append(pl.BlockSpec((rows, w.shape[1]), lambda *g, slab=slab: (slab(*g), 0)))
        else:
            in_specs.append(pl.BlockSpec(
                (pl.Element(rows), pl.Element(w.shape[1])),
                lambda *g, first_row=first_row, rows=rows, slab=slab: (
                    pl.multiple_of(first_row + slab(*g) * rows, 8), 0)))
        out_specs.append(pl.BlockSpec((rows, w.shape[1]), lambda *g, slab=slab: (slab(*g), 0)))
        out_shapes.append(jax.ShapeDtypeStruct((n_rows, w.shape[1]), BF16))
    return in_specs, out_specs, out_shapes


def _moba_attn_kernel(*refs, n_blocks, n_cast):
    tab_ref, q_ref, k_ref, v_ref, bias_ref = refs[:5]
    o_ref = refs[5 + n_cast]
    kaug_ref, kbar_ref, kb3_ref, s_ref, m_ref, l_ref, acc_ref = refs[6 + 2 * n_cast:]
    for src, dst in zip(refs[5:5 + n_cast], refs[6 + n_cast:6 + 2 * n_cast]):
        dst[...] = src[...].astype(dst.dtype)
    _moba_attn_body(tab_ref, q_ref, k_ref, v_ref, bias_ref, o_ref,
                    kaug_ref, kbar_ref, kb3_ref, s_ref, m_ref, l_ref, acc_ref, n_blocks=n_blocks)


def _moba_attn_body(tab_ref, q_ref, k_ref, v_ref, bias_ref, o_ref,
                    kaug_ref, kbar_ref, kb3_ref, s_ref, m_ref, l_ref, acc_ref, *, n_blocks):
    hp = pl.program_id(1)
    i = pl.program_id(2)
    n_tiles = n_blocks // 2
    heads = range(HEADS_PER_STEP)

    @pl.when(i == 0)
    def _():
        lane = lax.broadcasted_iota(jnp.int32, (TILE, AUG - HEAD_DIM), 1)
        row = lax.broadcasted_iota(jnp.int32, (TILE, AUG - HEAD_DIM), 0)
        ones_lane = jnp.where(lane == n_blocks, 1.0, jnp.where(lane == n_blocks + 1, 1.0, 0.0))
        for hh in heads:
            for n in range(n_blocks):
                kbar_ref[hh, n:n + 1, :] = jnp.sum(k_ref[hh, n].astype(F32), axis=0, keepdims=True)
            hi, mid, lo = _split3(kbar_ref[hh] * (1.0 / MOBA_BLOCK))
            kb3_ref[hh, 0:n_blocks, :] = hi.astype(BF16)
            kb3_ref[hh, n_blocks:2 * n_blocks, :] = mid.astype(BF16)
            kb3_ref[hh, 2 * n_blocks:3 * n_blocks, :] = lo.astype(BF16)
            for j in range(n_tiles):
                blk_of_row = jnp.where(row < ATT_TILE, 2 * j, 2 * j + 1)
                right = jnp.where(lane == blk_of_row, 1.0, ones_lane).astype(BF16)
                kaug_ref[hh, j] = jnp.concatenate(
                    [k_ref[hh, 2 * j:2 * j + 2].reshape(TILE, HEAD_DIM), right], axis=1)

    blk = lax.broadcasted_iota(jnp.int32, (n_blocks, TILE), 0)
    qlane = lax.broadcasted_iota(jnp.int32, (n_blocks, TILE), 1)
    own = 2 * i + jnp.where(qlane >= ATT_TILE, 1, 0)
    eligible = blk < own
    r16 = lax.broadcasted_iota(jnp.int32, (16, TILE), 0)

    j1 = jnp.where(i >= 1, i - 1, i + 1)
    j2 = jnp.where(i >= 2, i - 2, i + 1)
    n_far = jnp.maximum(i - 2, 0)

    far_logits, fold = [], []
    for hh in heads:
        qt = jnp.concatenate([q_ref[hh, 0], q_ref[hh, 1]], axis=1)
        g3 = jnp.dot(kb3_ref[hh], qt, preferred_element_type=F32)
        gate = g3[0:n_blocks] + g3[n_blocks:2 * n_blocks] + g3[2 * n_blocks:3 * n_blocks]
        selb = _top_k_bias(gate, eligible, blk)

        far = jnp.full((16, TILE), tab_ref[hp * HEADS_PER_STEP + hh, NUM_BUCKETS - 1] * LOG2E, F32)
        far_hi = far.astype(BF16).astype(F32)
        far_rows = jnp.where(r16 == 0, far_hi, jnp.where(r16 == 1, far - far_hi, 0.0))
        q_aug = jnp.concatenate(
            [qt, selb.astype(BF16), far_rows.astype(BF16),
             jnp.zeros((AUG - HEAD_DIM - n_blocks - 16, TILE), BF16)], axis=0)

        def values(j, hh=hh):
            return jnp.concatenate([v_ref[hh, 2 * j], v_ref[hh, 2 * j + 1]], axis=1)

        def head_far_logits(j, hh=hh, q_aug=q_aug):
            return jnp.dot(kaug_ref[hh, j], q_aug, preferred_element_type=F32)

        sm = _Softmax(hh, s_ref, m_ref, l_ref, acc_ref, values)
        far_logits.append(head_far_logits)
        fold.append(sm)

        sel_own = jnp.sum(jnp.where(blk == 2 * i, selb, 0.0), axis=0, keepdims=True)
        sel_own = jnp.where(qlane[0:1] >= ATT_TILE, sel_own, 0.0)
        kd = k_ref[hh, pl.ds(2 * i, 2)].reshape(TILE, HEAD_DIM)
        sd = jnp.dot(kd, qt, preferred_element_type=F32) + bias_ref[hh, 0]
        s_ref[hh, 0, 0:ATT_TILE, :] = sd[0:ATT_TILE] + sel_own
        s_ref[hh, 0, ATT_TILE:TILE, :] = sd[ATT_TILE:TILE]
        sm.reset()

    for hh in heads:
        s_ref[hh, 1] = far_logits[hh](j1) + bias_ref[hh, 1]
        fold[hh].fold(0, i)
    for hh in heads:
        s_ref[hh, 0] = far_logits[hh](j2) + bias_ref[hh, 2]
        fold[hh].fold(1, j1)
    for hh in heads:
        s_ref[hh, 1] = far_logits[hh](0)
        fold[hh].fold(0, j2)

    _fold_tile_run(heads, s_ref, far_logits, fold, 0, n_far, n_tiles - 1)

    for hh in heads:
        o_ref[:, hh * HEAD_DIM:(hh + 1) * HEAD_DIM] = fold[hh].result().T.astype(o_ref.dtype)


def moba_attn(qv_t, k, bias, rel_bias, *, batch, n_heads, q_off, k_off, v_off, cast=()):
    mb = k.shape[1]
    nb = mb // batch
    nt = nb // 2
    m = mb * ATT_TILE
    hb = HEADS_PER_STEP
    nhp = n_heads // hb
    assert nb + 16 <= AUG - HEAD_DIM and nt >= 4
    assert n_heads % hb == 0 and q_off % hb == 0 and k_off % hb == 0 and v_off % hb == 0
    c_in, c_out, c_shape = _slab_cast_specs(cast, batch * nhp * nt, nt,
                                            lambda b, h, i: (b * nhp + h) * nt + i)
    return pl.pallas_call(
        functools.partial(_moba_attn_kernel, n_blocks=nb, n_cast=len(cast)),
        out_shape=[jax.ShapeDtypeStruct((m, n_heads * HEAD_DIM), BF16)] + c_shape,
        grid=(batch, nhp, nt),
        in_specs=[pl.BlockSpec(memory_space=pltpu.SMEM),
                  pl.BlockSpec((hb, 2, HEAD_DIM, ATT_TILE),
                               lambda b, h, i: (q_off // hb + h, b * nt + i, 0, 0)),
                  pl.BlockSpec((hb, nb, ATT_TILE, HEAD_DIM), lambda b, h, i: (k_off // hb + h, b, 0, 0)),
                  pl.BlockSpec((hb, nb, HEAD_DIM, ATT_TILE), lambda b, h, i: (v_off // hb + h, b, 0, 0)),
                  pl.BlockSpec((hb, 3, TILE, TILE), lambda b, h, i: (h, 0, 0, 0))] + c_in,
        out_specs=[pl.BlockSpec((TILE, hb * HEAD_DIM), lambda b, h, i: (b * nt + i, h))] + c_out,
        scratch_shapes=[pltpu.VMEM((hb, nt, TILE, AUG), BF16),
                        pltpu.VMEM((hb, nb, HEAD_DIM), F32),
                        pltpu.VMEM((hb, 3 * nb, HEAD_DIM), BF16)] + _softmax_scratch(TILE),
        compiler_params=_params("parallel", "parallel", "arbitrary"),
        name="moba_attn",
    )(rel_bias.astype(F32), qv_t, k, qv_t, bias, *[c[0] if isinstance(c, tuple) else c for c in cast])


def _fox_attn_kernel(*refs, n_tiles, n_cast):
    q_ref, k_ref, v_ref, c_ref = refs[:4]
    o_ref = refs[4 + n_cast]
    kaug_ref, bound_ref, s_ref, m_ref, l_ref, acc_ref = refs[5 + 2 * n_cast:]
    for src, dst in zip(refs[4:4 + n_cast], refs[5 + n_cast:5 + 2 * n_cast]):
        dst[...] = src[...].astype(dst.dtype)
    _fox_attn_body(q_ref, k_ref, v_ref, c_ref, o_ref, kaug_ref, bound_ref, s_ref, m_ref, l_ref,
                   acc_ref, n_tiles=n_tiles)


def _fox_attn_body(q_ref, k_ref, v_ref, c_ref, o_ref, kaug_ref, bound_ref, s_ref, m_ref, l_ref,
                   acc_ref, *, n_tiles):
    hp = pl.program_id(1)
    i = pl.program_id(2)
    t = TILE
    heads = range(HEADS_PER_STEP)

    lane1 = lax.broadcasted_iota(jnp.int32, (1, LANES), 1)

    @pl.when(i == 0)
    def _():
        nh = len(heads)
        src = lax.broadcasted_iota(jnp.int32, (3 * LANES, nh * LANES), 0)
        dst = lax.broadcasted_iota(jnp.int32, (3 * LANES, nh * LANES), 1)
        place = jnp.where((src % LANES == hp * nh + dst // LANES) & (src // LANES == dst % LANES),
                          1.0, 0.0).astype(BF16)
        c_first = [jnp.zeros((1, LANES), F32) for _ in heads]
        c_last = [jnp.zeros((1, LANES), F32) for _ in heads]
        k_norm2 = [jnp.zeros((t, 1), F32) for _ in heads]
        for j in range(n_tiles):
            c_all = c_ref[j * t:(j + 1) * t, :]
            terms = jnp.concatenate([x.astype(BF16) for x in _split3(-c_all)], axis=1)
            right = jnp.dot(terms, place, preferred_element_type=F32).astype(BF16)
            for hh in heads:
                k = k_ref[hh, 2 * j:2 * j + 2].reshape(t, HEAD_DIM)
                kaug_ref[hh, j] = jnp.concatenate([k, right[:, hh * LANES:(hh + 1) * LANES]], axis=1)
                kf = k.astype(F32)
                k_norm2[hh] = jnp.maximum(k_norm2[hh], jnp.sum(kf * kf, axis=1, keepdims=True))
                ends = jnp.concatenate([c_all[0:1, :], c_all[t - 1:t, :]], axis=0)
                ends = jnp.sum(jnp.where(lane1 == hp * nh + hh, ends, 0.0), axis=1, keepdims=True)
                c_first[hh] = jnp.where(lane1 == j, ends[0:1, :], c_first[hh])
                c_last[hh] = jnp.where(lane1 == j, ends[1:2, :], c_last[hh])
        for hh in heads:
            bound_ref[hh, 0:1, :] = c_first[hh]
            bound_ref[hh, 1:2, :] = c_last[hh]
            bound_ref[hh, 2:3, :] = jnp.broadcast_to(
                jnp.max(k_norm2[hh], axis=0, keepdims=True), (1, LANES))

    r_aug = lax.broadcasted_iota(jnp.int32, (AUG - HEAD_DIM, t), 0)
    ones_rows = jnp.where(r_aug < 3, 1.0, 0.0).astype(BF16)

    logits, fold, skippable = [], [], []
    for hh in heads:
        qt = jnp.concatenate([q_ref[hh, 0], q_ref[hh, 1]], axis=1)
        q_aug = jnp.concatenate([qt, ones_rows], axis=0)

        qf = qt.astype(F32)
        q_norm2 = jnp.max(jnp.sum(qf * qf, axis=0, keepdims=True), axis=1, keepdims=True)
        c_here = jnp.sum(jnp.where(lane1 == i, bound_ref[hh, 0:1, :], 0.0), axis=1, keepdims=True)
        gap = (bound_ref[hh, 1:2, :] - c_here) - UNDERFLOW_LOG2
        dead = (lane1 < i) & (gap > 0.0) & (gap * gap > 4.0 * q_norm2 * bound_ref[hh, 2:3, :])
        skippable.append(jnp.sum(jnp.where(dead, 1.0, 0.0), axis=1, keepdims=True))

        def head_logits(n, hh=hh, q_aug=q_aug):
            return jnp.dot(kaug_ref[hh, n], q_aug, preferred_element_type=F32)

        def values(n, hh=hh):
            return jnp.concatenate([v_ref[hh, 2 * n], v_ref[hh, 2 * n + 1]], axis=1)

        logits.append(head_logits)
        fold.append(_Softmax(hh, s_ref, m_ref, l_ref, acc_ref, values))

    key = lax.broadcasted_iota(jnp.int32, (t, t), 0)
    qry = lax.broadcasted_iota(jnp.int32, (t, t), 1)
    causal = key <= qry
    j0 = functools.reduce(jnp.minimum, skippable)[0, 0].astype(jnp.int32)
    n_past = i - j0
    for hh in heads:
        s_ref[hh, 0] = logits[hh](i)
        fold[hh].reset()
    for hh in heads:
        s_ref[hh, 1] = logits[hh](j0)
        fold[hh].fold(0, i, mask=causal)

    _fold_tile_run(heads, s_ref, logits, fold, j0, n_past, n_tiles - 1)

    for hh in heads:
        o_ref[:, hh * HEAD_DIM:(hh + 1) * HEAD_DIM] = fold[hh].result().T.astype(o_ref.dtype)


def fox_attn(qv_t, k, c_rep, *, batch, n_heads, q_off, k_off, v_off, cast=()):
    mb = k.shape[1]
    nb = mb // batch
    nt = nb // 2
    m = mb * ATT_TILE
    t = TILE
    hb = HEADS_PER_STEP
    nhp = n_heads // hb
    assert n_heads % hb == 0 and q_off % hb == 0 and k_off % hb == 0 and v_off % hb == 0
    assert nt <= LANES and n_heads <= LANES
    c_in, c_out, c_shape = _slab_cast_specs(cast, batch * nhp * nt, nt,
                                            lambda b, h, i: (b * nhp + h) * nt + i)
    return pl.pallas_call(
        functools.partial(_fox_attn_kernel, n_tiles=nt, n_cast=len(cast)),
        out_shape=[jax.ShapeDtypeStruct((m, n_heads * HEAD_DIM), BF16)] + c_shape,
        grid=(batch, nhp, nt),
        in_specs=[pl.BlockSpec((hb, 2, HEAD_DIM, ATT_TILE),
                               lambda b, h, i: (q_off // hb + h, b * nt + i, 0, 0)),
                  pl.BlockSpec((hb, nb, ATT_TILE, HEAD_DIM), lambda b, h, i: (k_off // hb + h, b, 0, 0)),
                  pl.BlockSpec((hb, nb, HEAD_DIM, ATT_TILE), lambda b, h, i: (v_off // hb + h, b, 0, 0)),
                  pl.BlockSpec((nb * ATT_TILE, LANES), lambda b, h, i: (b, 0))] + c_in,
        out_specs=[pl.BlockSpec((t, hb * HEAD_DIM), lambda b, h, i: (b * nt + i, h))] + c_out,
        scratch_shapes=[pltpu.VMEM((hb, nt, t, AUG), BF16),
                        pltpu.VMEM((hb, 8, LANES), F32)] + _softmax_scratch(t),
        compiler_params=_params("parallel", "parallel", "arbitrary"),
        name="fox_attn",
    )(qv_t, k, qv_t, c_rep, *[c[0] if isinstance(c, tuple) else c for c in cast])


def _gated_merge_kernel(oa_ref, of_ref, wa_ref, wf_ref, ga_ref, gf_ref, o_ref):
    ua = jnp.dot(oa_ref[...], wa_ref[...], preferred_element_type=F32)
    uf = jnp.dot(of_ref[...], wf_ref[...], preferred_element_type=F32)
    o_ref[...] = (ga_ref[...].astype(F32) * ua + gf_ref[...].astype(F32) * uf).astype(o_ref.dtype)


def gated_merge(o_a, o_f, w_a, w_f, gates, *, tm=ROW_TILE, tn=COL_TILE):
    m, ka = o_a.shape
    kf = o_f.shape[1]
    n = w_a.shape[1]
    nj = n // tn
    return pl.pallas_call(
        _gated_merge_kernel,
        out_shape=jax.ShapeDtypeStruct((m, n), BF16),
        grid=(m // tm, nj),
        in_specs=[pl.BlockSpec((tm, ka), lambda i, j: (i, 0)),
                  pl.BlockSpec((tm, kf), lambda i, j: (i, 0)),
                  pl.BlockSpec((ka, tn), lambda i, j: (0, j)),
                  pl.BlockSpec((kf, tn), lambda i, j: (0, j)),
                  pl.BlockSpec((tm, tn), lambda i, j: (i, j)),
                  pl.BlockSpec((tm, tn), lambda i, j: (i, nj + j))],
        out_specs=pl.BlockSpec((tm, tn), lambda i, j: (i, j)),
        compiler_params=_params("parallel", "parallel"),
        name="gated_merge",
    )(o_a, o_f, w_a, w_f, gates, gates)


def _mm_res_kernel(lhs_ref, w_ref, res_ref, g_ref, o_ref, *, final_norm):
    kk = pl.program_id(1)

    @pl.when(kk == 0)
    def _():
        o_ref[...] = res_ref[...]

    o_ref[...] += jnp.dot(lhs_ref[...], w_ref[...], preferred_element_type=F32)

    if final_norm:
        @pl.when(kk == pl.num_programs(1) - 1)
        def _():
            hres = o_ref[...]
            ms = jnp.mean(hres * hres, axis=-1, keepdims=True)
            o_ref[...] = hres * lax.rsqrt(ms + RMS_EPS) * g_ref[...]


def mm_res(lhs, w, res, g=None, *, tm=ROW_TILE, tk=1024):
    m, k = lhs.shape
    n = w.shape[1]
    tk = min(tk, k)
    final_norm = g is not None
    if g is None:
        g = jnp.ones((n,), F32)
    return pl.pallas_call(
        functools.partial(_mm_res_kernel, final_norm=final_norm),
        out_shape=jax.ShapeDtypeStruct((m, n), F32),
        grid=(m // tm, k // tk),
        in_specs=[pl.BlockSpec((tm, tk), lambda i, kk: (i, kk)),
                  pl.BlockSpec((tk, n), lambda i, kk: (kk, 0)),
                  pl.BlockSpec((tm, n), lambda i, kk: (i, 0)),
                  pl.BlockSpec((1, n), lambda i, kk: (0, 0))],
        out_specs=pl.BlockSpec((tm, n), lambda i, kk: (i, 0)),
        compiler_params=_params("parallel", "arbitrary"),
        name="mm_res",
    )(lhs, w, res, g.reshape(1, n))


def _cross_attn_kernel(h_ref, g_ref, wq_ref, kv_ref, wo_ref, o_ref, *, n_heads):
    width = n_heads * HEAD_DIM
    hres = h_ref[...]
    ms = jnp.mean(hres * hres, axis=-1, keepdims=True)
    c = (hres * lax.rsqrt(ms + RMS_EPS) * g_ref[...]).astype(BF16)
    q_all = (jnp.dot(c, wq_ref[...], preferred_element_type=F32)
             * (HEAD_DIM ** -0.5 * LOG2E)).astype(BF16)
    heads = []
    for h in range(n_heads):
        q = q_all[:, h * HEAD_DIM:(h + 1) * HEAD_DIM]
        k = kv_ref[0, :, h * HEAD_DIM:(h + 1) * HEAD_DIM]
        v = kv_ref[0, :, width + h * HEAD_DIM:width + (h + 1) * HEAD_DIM]
        s = lax.dot_general(q, k, (((1,), (1,)), ((), ())), preferred_element_type=F32)
        m = jnp.max(s, axis=-1, keepdims=True)
        p = jnp.exp2(s - m)
        l = jnp.sum(p, axis=-1, keepdims=True)
        o = jnp.dot(p.astype(BF16), v, preferred_element_type=F32) * (1.0 / l)
        heads.append(o.astype(BF16))
    o_all = jnp.concatenate(heads, axis=1)
    o_ref[...] = hres + jnp.dot(o_all, wo_ref[...], preferred_element_type=F32)


def cross_attn(h, g, w_q, kv, w_o, *, batch, n_heads, tm=512):
    m, d = h.shape
    width = w_q.shape[1]
    n_mem = kv.shape[0] // batch
    tiles_per_batch = (m // batch) // tm
    kv3 = kv.reshape(batch, n_mem, 2 * width)
    return pl.pallas_call(
        functools.partial(_cross_attn_kernel, n_heads=n_heads),
        out_shape=jax.ShapeDtypeStruct((m, d), F32),
        grid=(m // tm,),
        in_specs=[pl.BlockSpec((tm, d), lambda i: (i, 0)),
                  pl.BlockSpec((1, d), lambda i: (0, 0)),
                  pl.BlockSpec((d, width), lambda i: (0, 0)),
                  pl.BlockSpec((1, n_mem, 2 * width), lambda i: (i // tiles_per_batch, 0, 0)),
                  pl.BlockSpec((width, d), lambda i: (0, 0))],
        out_specs=pl.BlockSpec((tm, d), lambda i: (i, 0)),
        compiler_params=_params("parallel"),
        name="cross_attn",
    )(h, g.reshape(1, d), w_q, kv3, w_o)


def kernel(x, mem, g_mix, w_in, b_forget, w_branch_moba, w_branch_fox, w_mix_out, rel_bias,
           g_cross, g_mem, w_cq, w_ck, w_cv, w_co, g_mlp, w_ff1, w_ff2, g_final):
    batch, seq, d = x.shape
    depth = w_in.shape[0]
    n_heads = rel_bias.shape[0]
    n_fox = b_forget.shape[1]
    wm = n_heads * HEAD_DIM
    wf = n_fox * HEAD_DIM
    m = batch * seq
    assert wm == wf and wm % COL_TILE == 0
    scale = HEAD_DIM ** -0.5
    mem2 = mem.reshape(-1, d)

    bias = moba_bias(rel_bias)
    h = x.reshape(m, d)
    for l in range(depth):
        wt = jnp.swapaxes(w_in[l], 0, 1)
        qkv_w = 3 * (wm + wf)
        w_fl = jnp.zeros((d, LANES), F32).at[:, :n_fox].set(w_in[l][:, qkv_w:qkv_w + n_fox])
        qv_t, k_hm, f_logit, a_mix = in_proj(h, g_mix[l], wt[:qkv_w].astype(BF16), w_fl,
                                             parts_w=wm, scale=scale * LOG2E)
        k_hm = k_hm.reshape(k_hm.shape[0], m // ATT_TILE, ATT_TILE, HEAD_DIM)
        c_rep = forget_cumsum(f_logit, b_forget[l], batch=batch)

        o_a, w_ff1_16, w_mix_16, w_bm16, w_bf16 = moba_attn(
            qv_t, k_hm, bias, rel_bias, batch=batch, n_heads=n_heads,
            q_off=0, k_off=0, v_off=n_heads + n_fox,
            cast=(w_ff1[l], w_mix_out[l], w_branch_moba[l], w_branch_fox[l]))
        o_f, w_ff2_16, w_g16, w_cq16, w_co16 = fox_attn(
            qv_t, k_hm, c_rep, batch=batch, n_heads=n_fox,
            q_off=n_heads, k_off=n_heads, v_off=2 * n_heads + n_fox,
            cast=(w_ff2[l], (wt, qkv_w + n_fox, 2 * d), w_cq[l], w_co[l]))
        gates = gate_proj(a_mix, w_g16)
        merged = gated_merge(o_a, o_f, w_bm16, w_bf16, gates)
        h = mm_res(merged, w_mix_16, h)

        cw = w_cq.shape[2]
        w_kv = jnp.concatenate([w_ck[l], w_cv[l]], axis=1).astype(BF16)
        kv = rms_proj(mem2, g_mem[l], w_kv, tn=2 * cw)
        h = cross_attn(h, g_cross[l], w_cq16, kv, w_co16, batch=batch, n_heads=cw // HEAD_DIM)

        u = rms_proj(h, g_mlp[l], w_ff1_16, act="relu2")
        h = mm_res(u, w_ff2_16, h, g_final if l == depth - 1 else None)
    return h.reshape(batch, seq, d)
```

```python
import functools
import math

import jax
import jax.numpy as jnp
from jax import lax
from jax.experimental import pallas as pl
from jax.experimental.pallas import tpu as pltpu

F32 = jnp.float32
BF16 = jnp.bfloat16

HEAD_DIM = 128
MOBA_BLOCK = 256
MOBA_TOP_K = 3
NUM_BUCKETS = 32
MAX_DISTANCE = 1024
RMS_EPS = 1e-6
LOG2E = math.log2(math.e)
NEG_INF = -1e30
LANES = 128
ATT_TILE = 256
TILE = 2 * ATT_TILE
AUG = 256
HEADS_PER_STEP = 2
LOOP_UNROLL = 4
UNDERFLOW_LOG2 = 160.0
NEAR_BLOCKS = 5
VMEM_LIMIT = 56 * 1024 * 1024
ROW_TILE = 1024
COL_TILE = 1024


def _bucket_thresholds():
    max_exact = NUM_BUCKETS // 2
    thr = list(range(1, max_exact + 1))
    for k in range(max_exact + 1, NUM_BUCKETS):
        v = max_exact * (MAX_DISTANCE / max_exact) ** ((k - max_exact) / (NUM_BUCKETS - max_exact))
        n = int(math.floor(v))
        while max_exact + int(math.log(n / max_exact) / math.log(MAX_DISTANCE / max_exact)
                              * (NUM_BUCKETS - max_exact)) < k:
            n += 1
        thr.append(n)
    return tuple(thr)


BUCKET_THRESHOLDS = _bucket_thresholds()
assert (NEAR_BLOCKS - 1) * MOBA_BLOCK + 1 >= BUCKET_THRESHOLDS[-1]


def _params(*sem):
    return pltpu.CompilerParams(dimension_semantics=sem, vmem_limit_bytes=VMEM_LIMIT)


def _bf16_dot(a, b):
    return jnp.dot(a.astype(BF16), b.astype(BF16), preferred_element_type=F32)


def _rms_proj_kernel(x_ref, g_ref, w_ref, o_ref, a_ref, *, act):
    j = pl.program_id(1)

    @pl.when(j == 0)
    def _():
        x = x_ref[...]
        ms = jnp.mean(x * x, axis=-1, keepdims=True)
        a_ref[...] = (x * lax.rsqrt(ms + RMS_EPS) * g_ref[...]).astype(BF16)

    acc = jnp.dot(a_ref[...], w_ref[...], preferred_element_type=F32)
    if act == "relu2":
        acc = jnp.square(jnp.maximum(acc, 0.0))
    o_ref[...] = acc.astype(o_ref.dtype)


def rms_proj(x, g, w, *, act=None, tm=ROW_TILE, tn=COL_TILE):
    m, d = x.shape
    n = w.shape[1]
    tm, tn = min(tm, m), min(tn, n)
    return pl.pallas_call(
        functools.partial(_rms_proj_kernel, act=act),
        out_shape=jax.ShapeDtypeStruct((m, n), BF16),
        grid=(m // tm, n // tn),
        in_specs=[pl.BlockSpec((tm, d), lambda i, j: (i, 0)),
                  pl.BlockSpec((1, d), lambda i, j: (0, 0)),
                  pl.BlockSpec((d, tn), lambda i, j: (0, j))],
        out_specs=pl.BlockSpec((tm, tn), lambda i, j: (i, j)),
        scratch_shapes=[pltpu.VMEM((tm, d), BF16)],
        compiler_params=_params("parallel", "arbitrary"),
        name="rms_proj",
    )(x, g.reshape(1, d), w)


def _gate_proj_kernel(a_ref, w_ref, o_ref):
    acc = lax.dot_general(a_ref[...], w_ref[...], (((1,), (1,)), ((), ())),
                          preferred_element_type=F32)
    o_ref[...] = jax.nn.sigmoid(acc).astype(o_ref.dtype)


def gate_proj(a, w_rows, *, tm=ROW_TILE, tn=2 * COL_TILE):
    m, d = a.shape
    n = w_rows.shape[0]
    tn = min(tn, n)
    return pl.pallas_call(
        _gate_proj_kernel,
        out_shape=jax.ShapeDtypeStruct((m, n), BF16),
        grid=(m // tm, n // tn),
        in_specs=[pl.BlockSpec((tm, d), lambda i, j: (i, 0)),
                  pl.BlockSpec((tn, d), lambda i, j: (j, 0))],
        out_specs=pl.BlockSpec((tm, tn), lambda i, j: (i, j)),
        compiler_params=_params("parallel", "parallel"),
        name="gate_proj",
    )(a, w_rows)


def _in_proj_kernel(x_ref, g_ref, w_ref, wf_ref, qv_ref, k_ref, f_ref, a_ref, *, n_q, n_qv, scale):
    j = pl.program_id(1)

    @pl.when(j == 0)
    def _():
        x = x_ref[...]
        ms = jnp.mean(x * x, axis=-1, keepdims=True)
        a_ref[...] = (x * lax.rsqrt(ms + RMS_EPS) * g_ref[...]).astype(BF16)
        w_hi, w_lo, _ = _split3(wf_ref[...])
        both = _bf16_dot(a_ref[...], jnp.concatenate([w_hi, w_lo], axis=1))
        f_ref[...] = both[:, :LANES] + both[:, LANES:]

    def project():
        return lax.dot_general(a_ref[...], w_ref[...], (((1,), (1,)), ((), ())),
                               preferred_element_type=F32)

    @pl.when(j < n_qv)
    def _():
        acc = project() * jnp.where(j < n_q, F32(scale), F32(1.0))
        tm, tn = acc.shape
        for c in range(tn // LANES):
            for r in range(tm // ATT_TILE):
                blk = acc[r * ATT_TILE:(r + 1) * ATT_TILE, c * LANES:(c + 1) * LANES]
                qv_ref[c, r] = blk.T.astype(qv_ref.dtype)

    @pl.when(j >= n_qv)
    def _():
        acc = project()
        for c in range(acc.shape[1] // LANES):
            k_ref[c] = acc[:, c * LANES:(c + 1) * LANES].astype(k_ref.dtype)


def in_proj(x, g, w_rows, w_f, *, parts_w, scale, tm=ROW_TILE, tn=COL_TILE):
    m, d = x.shape
    per = parts_w // tn

    def tiles(*parts):
        return [p * per + t for p in parts for t in range(per)]

    order = tiles(0, 3, 2, 5) + tiles(1, 4)
    n_q, n_qv, n_k = 2 * per, 4 * per, 2 * per

    def w_tile(j):
        idx = order[-1]
        for t, src in reversed(list(enumerate(order[:-1]))):
            idx = jnp.where(j == t, src, idx)
        return idx

    hb = tn // LANES
    return pl.pallas_call(
        functools.partial(_in_proj_kernel, n_q=n_q, n_qv=n_qv, scale=scale),
        out_shape=(jax.ShapeDtypeStruct((n_qv * hb, m // ATT_TILE, LANES, ATT_TILE), BF16),
                   jax.ShapeDtypeStruct((n_k * hb, m, LANES), BF16),
                   jax.ShapeDtypeStruct((m, LANES), F32),
                   jax.ShapeDtypeStruct((m, d), BF16)),
        grid=(m // tm, len(order)),
        in_specs=[pl.BlockSpec((tm, d), lambda i, j: (i, 0)),
                  pl.BlockSpec((1, d), lambda i, j: (0, 0)),
                  pl.BlockSpec((tn, d), lambda i, j: (w_tile(j), 0)),
                  pl.BlockSpec((d, LANES), lambda i, j: (0, 0))],
        out_specs=(pl.BlockSpec((hb, tm // ATT_TILE, LANES, ATT_TILE),
                                lambda i, j: (jnp.minimum(j, n_qv - 1), i, 0, 0)),
                   pl.BlockSpec((hb, tm, LANES),
                                lambda i, j: (jnp.clip(j - n_qv, 0, n_k - 1), i, 0)),
                   pl.BlockSpec((tm, LANES), lambda i, j: (i, 0)),
                   pl.BlockSpec((tm, d), lambda i, j: (i, 0))),
        compiler_params=_params("parallel", "arbitrary"),
        name="in_proj",
    )(x, g.reshape(1, d), w_rows, w_f)


def _forget_cumsum_kernel(f_ref, b_ref, o_ref, carry_ref):
    t = pl.program_id(1)

    @pl.when(t == 0)
    def _():
        carry_ref[...] = jnp.zeros_like(carry_ref)

    f = f_ref[...] + b_ref[...]
    tm = f.shape[0]
    logf = jnp.minimum(f, 0.0) - jnp.log1p(jnp.exp(-jnp.abs(f)))
    logf = logf * LOG2E
    row = lax.broadcasted_iota(jnp.int32, (tm, tm), 0)
    col = lax.broadcasted_iota(jnp.int32, (tm, tm), 1)
    tri = jnp.where(col <= row, 1.0, 0.0).astype(BF16)
    hi, mid, lo = _split3(logf)
    c = (_bf16_dot(tri, hi) + (_bf16_dot(tri, mid) + _bf16_dot(tri, lo))) + carry_ref[0:1, :]
    o_ref[...] = c
    carry_ref[...] = jnp.broadcast_to(c[tm - 1:tm, :], carry_ref.shape)


def forget_cumsum(f, b_f, *, batch, tm=512):
    m = f.shape[0]
    nt = (m // batch) // tm
    b_pad = jnp.zeros((1, LANES), F32).at[0, :b_f.shape[0]].set(b_f)
    return pl.pallas_call(
        _forget_cumsum_kernel,
        out_shape=jax.ShapeDtypeStruct((m, LANES), F32),
        grid=(batch, nt),
        in_specs=[pl.BlockSpec((tm, LANES), lambda b, t: (b * nt + t, 0)),
                  pl.BlockSpec((1, LANES), lambda b, t: (0, 0))],
        out_specs=pl.BlockSpec((tm, LANES), lambda b, t: (b * nt + t, 0)),
        scratch_shapes=[pltpu.VMEM((8, LANES), F32)],
        compiler_params=_params("parallel", "arbitrary"),
        name="forget_cumsum",
    )(f, b_pad)


def _moba_bias_kernel(tab_ref, o_ref):
    h = pl.program_id(0)
    key = lax.broadcasted_iota(jnp.int32, (ATT_TILE, ATT_TILE), 0)
    qry = lax.broadcasted_iota(jnp.int32, (ATT_TILE, ATT_TILE), 1)
    far = tab_ref[h, NUM_BUCKETS - 1] * LOG2E

    def block(delta):
        dist = delta * MOBA_BLOCK + qry - key
        val = jnp.full((ATT_TILE, ATT_TILE), tab_ref[h, 0], F32)
        for k in range(1, NUM_BUCKETS):
            val = jnp.where(dist >= BUCKET_THRESHOLDS[k - 1], tab_ref[h, k], val)
        val = val * LOG2E
        if delta == 0:
            val = jnp.where(dist >= 0, val, NEG_INF)
        return val

    t = [block(delta) for delta in range(NEAR_BLOCKS + 1)]
    a = ATT_TILE
    o_ref[0, 0, 0:a, 0:a] = t[0]
    o_ref[0, 0, 0:a, a:2 * a] = t[1]
    o_ref[0, 0, a:2 * a, 0:a] = jnp.full((a, a), NEG_INF, F32)
    o_ref[0, 0, a:2 * a, a:2 * a] = t[0]
    for d in (1, 2):
        o_ref[0, d, 0:a, 0:a] = t[2 * d] - far
        o_ref[0, d, 0:a, a:2 * a] = t[2 * d + 1] - far
        o_ref[0, d, a:2 * a, 0:a] = t[2 * d - 1] - far
        o_ref[0, d, a:2 * a, a:2 * a] = t[2 * d] - far


def moba_bias(rel_bias):
    h = rel_bias.shape[0]
    return pl.pallas_call(
        _moba_bias_kernel,
        out_shape=jax.ShapeDtypeStruct((h, 3, TILE, TILE), F32),
        grid=(h,),
        in_specs=[pl.BlockSpec(memory_space=pltpu.SMEM)],
        out_specs=pl.BlockSpec((1, 3, TILE, TILE), lambda i: (i, 0, 0, 0)),
        compiler_params=_params("parallel"),
        name="moba_bias",
    )(rel_bias.astype(F32))


class _Softmax:
    def __init__(self, hh, s_ref, m_ref, l_ref, acc_ref, load_values):
        self.s_ref, self.m_ref, self.l_ref, self.acc_ref = s_ref.at[hh], m_ref.at[hh], l_ref.at[hh], acc_ref.at[hh]
        self.load_values = load_values

    def reset(self):
        self.m_ref[...] = jnp.full(self.m_ref.shape, NEG_INF, F32)
        self.l_ref[...] = jnp.zeros(self.l_ref.shape, F32)
        self.acc_ref[...] = jnp.zeros(self.acc_ref.shape, F32)

    def fold(self, slot, tile, mask=None):
        for half in range(2):
            self.fold_half(slot, tile, half, mask)

    def fold_half(self, slot, tile, half, mask=None):
        width = self.m_ref.shape[-1] // 2
        cols = slice(half * width, (half + 1) * width)
        s = self.s_ref[slot, :, cols]
        if mask is not None:
            s = jnp.where(mask[:, cols], s, NEG_INF)
        m = self.m_ref[:, cols]
        m_new = jnp.maximum(m, jnp.max(s, axis=0, keepdims=True))
        alpha = jnp.exp2(m - m_new)
        p = jnp.exp2(s - m_new)
        self.acc_ref[:, cols] = alpha * self.acc_ref[:, cols] + jnp.dot(
            self.load_values(tile), p.astype(BF16), preferred_element_type=F32)
        self.l_ref[:, cols] = alpha * self.l_ref[:, cols] + jnp.sum(p, axis=0, keepdims=True)
        self.m_ref[:, cols] = m_new

    def result(self):
        return self.acc_ref[...] * (1.0 / self.l_ref[...])


def _softmax_scratch(t):
    nh = HEADS_PER_STEP
    return [pltpu.VMEM((nh, 2, t, t), F32),
            pltpu.VMEM((nh, 1, t), F32),
            pltpu.VMEM((nh, 1, t), F32),
            pltpu.VMEM((nh, HEAD_DIM, t), F32)]


def _fold_tile_run(heads, s_ref, logits, fold, base, count, last_tile):
    def step(r, slot, look_ahead=True):
        for hh in heads:
            if look_ahead:
                s_ref[hh, 1 - slot] = logits[hh](jnp.minimum(base + r + 1, last_tile))
            fold[hh].fold(slot, base + r)

    def unrolled_body(p, carry):
        for u in range(LOOP_UNROLL):
            step(LOOP_UNROLL * p + u, (u + 1) % 2)
        return carry

    lax.fori_loop(0, count // LOOP_UNROLL, unrolled_body, 0)
    assert LOOP_UNROLL == 4
    rem = count % LOOP_UNROLL
    done = count - rem

    @pl.when(rem >= 2)
    def _():
        step(done, 1)
        step(done + 1, 0)

    @pl.when(rem % 2 == 1)
    def _():
        step(count - 1, 1, look_ahead=False)


def _top_k_bias(gate, eligible, blk):
    lowest = float(jnp.finfo(F32).min)
    blk_f = blk.astype(F32)
    g = jnp.where(eligible, gate, NEG_INF)
    bias = jnp.full(gate.shape, NEG_INF, F32)
    for _ in range(MOBA_TOP_K):
        best = jnp.max(g, axis=0, keepdims=True)
        first = jnp.min(jnp.where(g == best, blk_f, float(gate.shape[0])), axis=0, keepdims=True)
        hit = blk_f == first
        bias = jnp.where(hit, 0.0, bias)
        g = jnp.where(hit, lowest, g)
    return jnp.where(eligible, bias, NEG_INF)


def _split3(x):
    hi = x.astype(BF16).astype(F32)
    mid = (x - hi).astype(BF16).astype(F32)
    lo = (x - hi - mid).astype(BF16).astype(F32)
    return hi, mid, lo


def _slab_cast_specs(weights, n_steps, inner_steps, step_index):
    in_specs, out_specs, out_shapes = [], [], []
    for w in weights:
        if isinstance(w, tuple):
            w, first_row, n_rows = w
        else:
            first_row, n_rows = 0, w.shape[0]
        rows = max(n_rows // n_steps, 16)
        n_slabs = n_rows // rows
        assert n_slabs * rows == n_rows and rows % 16 == 0 and n_steps % n_slabs == 0
        share = n_steps // n_slabs
        assert inner_steps % share == 0

        def slab(*g, share=share):
            return step_index(*g) // share

        if first_row == 0:
            in_specs.append(pl.BlockSpec((rows, w.shape[1]), lambda *g, slab=slab: (slab(*g), 0)))
        else:
            in_specs.append(pl.BlockSpec(
                (pl.Element(rows), pl.Element(w.shape[1])),
                lambda *g, first_row=first_row, rows=rows, slab=slab: (
                    pl.multiple_of(first_row + slab(*g) * rows, 8), 0)))
        out_specs.append(pl.BlockSpec((rows, w.shape[1]), lambda *g, slab=slab: (slab(*g), 0)))
        out_shapes.append(jax.ShapeDtypeStruct((n_rows, w.shape[1]), BF16))
    return in_specs, out_specs, out_shapes


def _moba_attn_kernel(*refs, n_blocks, n_cast):
    tab_ref, q_ref, k_ref, v_ref, bias_ref = refs[:5]
    o_ref = refs[5 + n_cast]
    kaug_ref, kbar_ref, kb3_ref, s_ref, m_ref, l_ref, acc_ref = refs[6 + 2 * n_cast:]
    for src, dst in zip(refs[5:5 + n_cast], refs[6 + n_cast:6 + 2 * n_cast]):
        dst[...] = src[...].astype(dst.dtype)
    _moba_attn_body(tab_ref, q_ref, k_ref, v_ref, bias_ref, o_ref,
                    kaug_ref, kbar_ref, kb3_ref, s_ref, m_ref, l_ref, acc_ref, n_blocks=n_blocks)


def _moba_attn_body(tab_ref, q_ref, k_ref, v_ref, bias_ref, o_ref,
                    kaug_ref, kbar_ref, kb3_ref, s_ref, m_ref, l_ref, acc_ref, *, n_blocks):
    hp = pl.program_id(1)
    i = pl.program_id(2)
    n_tiles = n_blocks // 2
    heads = range(HEADS_PER_STEP)

    @pl.when(i == 0)
    def _():
        lane = lax.broadcasted_iota(jnp.int32, (TILE, AUG - HEAD_DIM), 1)
        row = lax.broadcasted_iota(jnp.int32, (TILE, AUG - HEAD_DIM), 0)
        ones_lane = jnp.where(lane == n_blocks, 1.0, jnp.where(lane == n_blocks + 1, 1.0, 0.0))
        for hh in heads:
            for n in range(n_blocks):
                kbar_ref[hh, n:n + 1, :] = jnp.sum(k_ref[hh, n].astype(F32), axis=0, keepdims=True)
            hi, mid, lo = _split3(kbar_ref[hh] * (1.0 / MOBA_BLOCK))
            kb3_ref[hh, 0:n_blocks, :] = hi.astype(BF16)
            kb3_ref[hh, n_blocks:2 * n_blocks, :] = mid.astype(BF16)
            kb3_ref[hh, 2 * n_blocks:3 * n_blocks, :] = lo.astype(BF16)
            for j in range(n_tiles):
                blk_of_row = jnp.where(row < ATT_TILE, 2 * j, 2 * j + 1)
                right = jnp.where(lane == blk_of_row, 1.0, ones_lane).astype(BF16)
                kaug_ref[hh, j] = jnp.concatenate(
                    [k_ref[hh, 2 * j:2 * j + 2].reshape(TILE, HEAD_DIM), right], axis=1)

    blk = lax.broadcasted_iota(jnp.int32, (n_blocks, TILE), 0)
    qlane = lax.broadcasted_iota(jnp.int32, (n_blocks, TILE), 1)
    own = 2 * i + jnp.where(qlane >= ATT_TILE, 1, 0)
    eligible = blk < own
    r16 = lax.broadcasted_iota(jnp.int32, (16, TILE), 0)

    j1 = jnp.where(i >= 1, i - 1, i + 1)
    j2 = jnp.where(i >= 2, i - 2, i + 1)
    n_far = jnp.maximum(i - 2, 0)

    far_logits, fold = [], []
    for hh in heads:
        qt = jnp.concatenate([q_ref[hh, 0], q_ref[hh, 1]], axis=1)
        g3 = jnp.dot(kb3_ref[hh], qt, preferred_element_type=F32)
        gate = g3[0:n_blocks] + g3[n_blocks:2 * n_blocks] + g3[2 * n_blocks:3 * n_blocks]
        selb = _top_k_bias(gate, eligible, blk)

        far = jnp.full((16, TILE), tab_ref[hp * HEADS_PER_STEP + hh, NUM_BUCKETS - 1] * LOG2E, F32)
        far_hi = far.astype(BF16).astype(F32)
        far_rows = jnp.where(r16 == 0, far_hi, jnp.where(r16 == 1, far - far_hi, 0.0))
        q_aug = jnp.concatenate(
            [qt, selb.astype(BF16), far_rows.astype(BF16),
             jnp.zeros((AUG - HEAD_DIM - n_blocks - 16, TILE), BF16)], axis=0)

        def values(j, hh=hh):
            return jnp.concatenate([v_ref[hh, 2 * j], v_ref[hh, 2 * j + 1]], axis=1)

        def head_far_logits(j, hh=hh, q_aug=q_aug):
            return jnp.dot(kaug_ref[hh, j], q_aug, preferred_element_type=F32)

        sm = _Softmax(hh, s_ref, m_ref, l_ref, acc_ref, values)
        far_logits.append(head_far_logits)
        fold.append(sm)

        sel_own = jnp.sum(jnp.where(blk == 2 * i, selb, 0.0), axis=0, keepdims=True)
        sel_own = jnp.where(qlane[0:1] >= ATT_TILE, sel_own, 0.0)
        kd = k_ref[hh, pl.ds(2 * i, 2)].reshape(TILE, HEAD_DIM)
        sd = jnp.dot(kd, qt, preferred_element_type=F32) + bias_ref[hh, 0]
        s_ref[hh, 0, 0:ATT_TILE, :] = sd[0:ATT_TILE] + sel_own
        s_ref[hh, 0, ATT_TILE:TILE, :] = sd[ATT_TILE:TILE]
        sm.reset()

    for hh in heads:
        s_ref[hh, 1] = far_logits[hh](j1) + bias_ref[hh, 1]
        fold[hh].fold(0, i)
    for hh in heads:
        s_ref[hh, 0] = far_logits[hh](j2) + bias_ref[hh, 2]
        fold[hh].fold(1, j1)
    for hh in heads:
        s_ref[hh, 1] = far_logits[hh](0)
        fold[hh].fold(0, j2)

    _fold_tile_run(heads, s_ref, far_logits, fold, 0, n_far, n_tiles - 1)

    for hh in heads:
        o_ref[:, hh * HEAD_DIM:(hh + 1) * HEAD_DIM] = fold[hh].result().T.astype(o_ref.dtype)


def moba_attn(qv_t, k, bias, rel_bias, *, batch, n_heads, q_off, k_off, v_off, cast=()):
    mb = k.shape[1]
    nb = mb // batch
    nt = nb // 2
    m = mb * ATT_TILE
    hb = HEADS_PER_STEP
    nhp = n_heads // hb
    assert nb + 16 <= AUG - HEAD_DIM and nt >= 4
    assert n_heads % hb == 0 and q_off % hb == 0 and k_off % hb == 0 and v_off % hb == 0
    c_in, c_out, c_shape = _slab_cast_specs(cast, batch * nhp * nt, nt,
                                            lambda b, h, i: (b * nhp + h) * nt + i)
    return pl.pallas_call(
        functools.partial(_moba_attn_kernel, n_blocks=nb, n_cast=len(cast)),
        out_shape=[jax.ShapeDtypeStruct((m, n_heads * HEAD_DIM), BF16)] + c_shape,
        grid=(batch, nhp, nt),
        in_specs=[pl.BlockSpec(memory_space=pltpu.SMEM),
                  pl.BlockSpec((hb, 2, HEAD_DIM, ATT_TILE),
                               lambda b, h, i: (q_off // hb + h, b * nt + i, 0, 0)),
                  pl.BlockSpec((hb, nb, ATT_TILE, HEAD_DIM), lambda b, h, i: (k_off // hb + h, b, 0, 0)),
                  pl.BlockSpec((hb, nb, HEAD_DIM, ATT_TILE), lambda b, h, i: (v_off // hb + h, b, 0, 0)),
                  pl.BlockSpec((hb, 3, TILE, TILE), lambda b, h, i: (h, 0, 0, 0))] + c_in,
        out_specs=[pl.BlockSpec((TILE, hb * HEAD_DIM), lambda b, h, i: (b * nt + i, h))] + c_out,
        scratch_shapes=[pltpu.VMEM((hb, nt, TILE, AUG), BF16),
                        pltpu.VMEM((hb, nb, HEAD_DIM), F32),
                        pltpu.VMEM((hb, 3 * nb, HEAD_DIM), BF16)] + _softmax_scratch(TILE),
        compiler_params=_params("parallel", "parallel", "arbitrary"),
        name="moba_attn",
    )(rel_bias.astype(F32), qv_t, k, qv_t, bias, *[c[0] if isinstance(c, tuple) else c for c in cast])


def _fox_attn_kernel(*refs, n_tiles, n_cast):
    q_ref, k_ref, v_ref, c_ref = refs[:4]
    o_ref = refs[4 + n_cast]
    kaug_ref, bound_ref, s_ref, m_ref, l_ref, acc_ref = refs[5 + 2 * n_cast:]
    for src, dst in zip(refs[4:4 + n_cast], refs[5 + n_cast:5 + 2 * n_cast]):
        dst[...] = src[...].astype(dst.dtype)
    _fox_attn_body(q_ref, k_ref, v_ref, c_ref, o_ref, kaug_ref, bound_ref, s_ref, m_ref, l_ref,
                   acc_ref, n_tiles=n_tiles)


def _fox_attn_body(q_ref, k_ref, v_ref, c_ref, o_ref, kaug_ref, bound_ref, s_ref, m_ref, l_ref,
                   acc_ref, *, n_tiles):
    hp = pl.program_id(1)
    i = pl.program_id(2)
    t = TILE
    heads = range(HEADS_PER_STEP)

    lane1 = lax.broadcasted_iota(jnp.int32, (1, LANES), 1)

    @pl.when(i == 0)
    def _():
        nh = len(heads)
        src = lax.broadcasted_iota(jnp.int32, (3 * LANES, nh * LANES), 0)
        dst = lax.broadcasted_iota(jnp.int32, (3 * LANES, nh * LANES), 1)
        place = jnp.where((src % LANES == hp * nh + dst // LANES) & (src // LANES == dst % LANES),
                          1.0, 0.0).astype(BF16)
        c_first = [jnp.zeros((1, LANES), F32) for _ in heads]
        c_last = [jnp.zeros((1, LANES), F32) for _ in heads]
        k_norm2 = [jnp.zeros((t, 1), F32) for _ in heads]
        for j in range(n_tiles):
            c_all = c_ref[j * t:(j + 1) * t, :]
            terms = jnp.concatenate([x.astype(BF16) for x in _split3(-c_all)], axis=1)
            right = jnp.dot(terms, place, preferred_element_type=F32).astype(BF16)
            for hh in heads:
                k = k_ref[hh, 2 * j:2 * j + 2].reshape(t, HEAD_DIM)
                kaug_ref[hh, j] = jnp.concatenate([k, right[:, hh * LANES:(hh + 1) * LANES]], axis=1)
                kf = k.astype(F32)
                k_norm2[hh] = jnp.maximum(k_norm2[hh], jnp.sum(kf * kf, axis=1, keepdims=True))
                ends = jnp.concatenate([c_all[0:1, :], c_all[t - 1:t, :]], axis=0)
                ends = jnp.sum(jnp.where(lane1 == hp * nh + hh, ends, 0.0), axis=1, keepdims=True)
                c_first[hh] = jnp.where(lane1 == j, ends[0:1, :], c_first[hh])
                c_last[hh] = jnp.where(lane1 == j, ends[1:2, :], c_last[hh])
        for hh in heads:
            bound_ref[hh, 0:1, :] = c_first[hh]
            bound_ref[hh, 1:2, :] = c_last[hh]
            bound_ref[hh, 2:3, :] = jnp.broadcast_to(
                jnp.max(k_norm2[hh], axis=0, keepdims=True), (1, LANES))

    r_aug = lax.broadcasted_iota(jnp.int32, (AUG - HEAD_DIM, t), 0)
    ones_rows = jnp.where(r_aug < 3, 1.0, 0.0).astype(BF16)

    logits, fold, skippable = [], [], []
    for hh in heads:
        qt = jnp.concatenate([q_ref[hh, 0], q_ref[hh, 1]], axis=1)
        q_aug = jnp.concatenate([qt, ones_rows], axis=0)

        qf = qt.astype(F32)
        q_norm2 = jnp.max(jnp.sum(qf * qf, axis=0, keepdims=True), axis=1, keepdims=True)
        c_here = jnp.sum(jnp.where(lane1 == i, bound_ref[hh, 0:1, :], 0.0), axis=1, keepdims=True)
        gap = (bound_ref[hh, 1:2, :] - c_here) - UNDERFLOW_LOG2
        dead = (lane1 < i) & (gap > 0.0) & (gap * gap > 4.0 * q_norm2 * bound_ref[hh, 2:3, :])
        skippable.append(jnp.sum(jnp.where(dead, 1.0, 0.0), axis=1, keepdims=True))

        def head_logits(n, hh=hh, q_aug=q_aug):
            return jnp.dot(kaug_ref[hh, n], q_aug, preferred_element_type=F32)

        def values(n, hh=hh):
            return jnp.concatenate([v_ref[hh, 2 * n], v_ref[hh, 2 * n + 1]], axis=1)

        logits.append(head_logits)
        fold.append(_Softmax(hh, s_ref, m_ref, l_ref, acc_ref, values))

    key = lax.broadcasted_iota(jnp.int32, (t, t), 0)
    qry = lax.broadcasted_iota(jnp.int32, (t, t), 1)
    causal = key <= qry
    j0 = functools.reduce(jnp.minimum, skippable)[0, 0].astype(jnp.int32)
    n_past = i - j0
    for hh in heads:
        s_ref[hh, 0] = logits[hh](i)
        fold[hh].reset()
    for hh in heads:
        s_ref[hh, 1] = logits[hh](j0)
        fold[hh].fold(0, i, mask=causal)

    _fold_tile_run(heads, s_ref, logits, fold, j0, n_past, n_tiles - 1)

    for hh in heads:
        o_ref[:, hh * HEAD_DIM:(hh + 1) * HEAD_DIM] = fold[hh].result().T.astype(o_ref.dtype)


def fox_attn(qv_t, k, c_rep, *, batch, n_heads, q_off, k_off, v_off, cast=()):
    mb = k.shape[1]
    nb = mb // batch
    nt = nb // 2
    m = mb * ATT_TILE
    t = TILE
    hb = HEADS_PER_STEP
    nhp = n_heads // hb
    assert n_heads % hb == 0 and q_off % hb == 0 and k_off % hb == 0 and v_off % hb == 0
    assert nt <= LANES and n_heads <= LANES
    c_in, c_out, c_shape = _slab_cast_specs(cast, batch * nhp * nt, nt,
                                            lambda b, h, i: (b * nhp + h) * nt + i)
    return pl.pallas_call(
        functools.partial(_fox_attn_kernel, n_tiles=nt, n_cast=len(cast)),
        out_shape=[jax.ShapeDtypeStruct((m, n_heads * HEAD_DIM), BF16)] + c_shape,
        grid=(batch, nhp, nt),
        in_specs=[pl.BlockSpec((hb, 2, HEAD_DIM, ATT_TILE),
                               lambda b, h, i: (q_off // hb + h, b * nt + i, 0, 0)),
                  pl.BlockSpec((hb, nb, ATT_TILE, HEAD_DIM), lambda b, h, i: (k_off // hb + h, b, 0, 0)),
                  pl.BlockSpec((hb, nb, HEAD_DIM, ATT_TILE), lambda b, h, i: (v_off // hb + h, b, 0, 0)),
                  pl.BlockSpec((nb * ATT_TILE, LANES), lambda b, h, i: (b, 0))] + c_in,
        out_specs=[pl.BlockSpec((t, hb * HEAD_DIM), lambda b, h, i: (b * nt + i, h))] + c_out,
        scratch_shapes=[pltpu.VMEM((hb, nt, t, AUG), BF16),
                        pltpu.VMEM((hb, 8, LANES), F32)] + _softmax_scratch(t),
        compiler_params=_params("parallel", "parallel", "arbitrary"),
        name="fox_attn",
    )(qv_t, k, qv_t, c_rep, *[c[0] if isinstance(c, tuple) else c for c in cast])


def _gated_merge_kernel(oa_ref, of_ref, wa_ref, wf_ref, ga_ref, gf_ref, o_ref):
    ua = jnp.dot(oa_ref[...], wa_ref[...], preferred_element_type=F32)
    uf = jnp.dot(of_ref[...], wf_ref[...], preferred_element_type=F32)
    o_ref[...] = (ga_ref[...].astype(F32) * ua + gf_ref[...].astype(F32) * uf).astype(o_ref.dtype)


def gated_merge(o_a, o_f, w_a, w_f, gates, *, tm=ROW_TILE, tn=COL_TILE):
    m, ka = o_a.shape
    kf = o_f.shape[1]
    n = w_a.shape[1]
    nj = n // tn
    return pl.pallas_call(
        _gated_merge_kernel,
        out_shape=jax.ShapeDtypeStruct((m, n), BF16),
        grid=(m // tm, nj),
        in_specs=[pl.BlockSpec((tm, ka), lambda i, j: (i, 0)),
                  pl.BlockSpec((tm, kf), lambda i, j: (i, 0)),
                  pl.BlockSpec((ka, tn), lambda i, j: (0, j)),
                  pl.BlockSpec((kf, tn), lambda i, j: (0, j)),
                  pl.BlockSpec((tm, tn), lambda i, j: (i, j)),
                  pl.BlockSpec((tm, tn), lambda i, j: (i, nj + j))],
        out_specs=pl.BlockSpec((tm, tn), lambda i, j: (i, j)),
        compiler_params=_params("parallel", "parallel"),
        name="gated_merge",
    )(o_a, o_f, w_a, w_f, gates, gates)


def _mm_res_kernel(lhs_ref, w_ref, res_ref, g_ref, o_ref, *, final_norm):
    kk = pl.program_id(1)

    @pl.when(kk == 0)
    def _():
        o_ref[...] = res_ref[...]

    o_ref[...] += jnp.dot(lhs_ref[...], w_ref[...], preferred_element_type=F32)

    if final_norm:
        @pl.when(kk == pl.num_programs(1) - 1)
        def _():
            hres = o_ref[...]
            ms = jnp.mean(hres * hres, axis=-1, keepdims=True)
            o_ref[...] = hres * lax.rsqrt(ms + RMS_EPS) * g_ref[...]


def mm_res(lhs, w, res, g=None, *, tm=ROW_TILE, tk=1024):
    m, k = lhs.shape
    n = w.shape[1]
    tk = min(tk, k)
    final_norm = g is not None
    if g is None:
        g = jnp.ones((n,), F32)
    return pl.pallas_call(
        functools.partial(_mm_res_kernel, final_norm=final_norm),
        out_shape=jax.ShapeDtypeStruct((m, n), F32),
        grid=(m // tm, k // tk),
        in_specs=[pl.BlockSpec((tm, tk), lambda i, kk: (i, kk)),
                  pl.BlockSpec((tk, n), lambda i, kk: (kk, 0)),
                  pl.BlockSpec((tm, n), lambda i, kk: (i, 0)),
                  pl.BlockSpec((1, n), lambda i, kk: (0, 0))],
        out_specs=pl.BlockSpec((tm, n), lambda i, kk: (i, 0)),
        compiler_params=_params("parallel", "arbitrary"),
        name="mm_res",
    )(lhs, w, res, g.reshape(1, n))


def _cross_attn_kernel(h_ref, g_ref, wq_ref, kv_ref, wo_ref, o_ref, *, n_heads):
    width = n_heads * HEAD_DIM
    hres = h_ref[...]
    ms = jnp.mean(hres * hres, axis=-1, keepdims=True)
    c = (hres * lax.rsqrt(ms + RMS_EPS) * g_ref[...]).astype(BF16)
    q_all = (jnp.dot(c, wq_ref[...], preferred_element_type=F32)
             * (HEAD_DIM ** -0.5 * LOG2E)).astype(BF16)
    heads = []
    for h in range(n_heads):
        q = q_all[:, h * HEAD_DIM:(h + 1) * HEAD_DIM]
        k = kv_ref[0, :, h * HEAD_DIM:(h + 1) * HEAD_DIM]
        v = kv_ref[0, :, width + h * HEAD_DIM:width + (h + 1) * HEAD_DIM]
        s = lax.dot_general(q, k, (((1,), (1,)), ((), ())), preferred_element_type=F32)
        m = jnp.max(s, axis=-1, keepdims=True)
        p = jnp.exp2(s - m)
        l = jnp.sum(p, axis=-1, keepdims=True)
        o = jnp.dot(p.astype(BF16), v, preferred_element_type=F32) * (1.0 / l)
        heads.append(o.astype(BF16))
    o_all = jnp.concatenate(heads, axis=1)
    o_ref[...] = hres + jnp.dot(o_all, wo_ref[...], preferred_element_type=F32)


def cross_attn(h, g, w_q, kv, w_o, *, batch, n_heads, tm=512):
    m, d = h.shape
    width = w_q.shape[1]
    n_mem = kv.shape[0] // batch
    tiles_per_batch = (m // batch) // tm
    kv3 = kv.reshape(batch, n_mem, 2 * width)
    return pl.pallas_call(
        functools.partial(_cross_attn_kernel, n_heads=n_heads),
        out_shape=jax.ShapeDtypeStruct((m, d), F32),
        grid=(m // tm,),
        in_specs=[pl.BlockSpec((tm, d), lambda i: (i, 0)),
                  pl.BlockSpec((1, d), lambda i: (0, 0)),
                  pl.BlockSpec((d, width), lambda i: (0, 0)),
                  pl.BlockSpec((1, n_mem, 2 * width), lambda i: (i // tiles_per_batch, 0, 0)),
                  pl.BlockSpec((width, d), lambda i: (0, 0))],
        out_specs=pl.BlockSpec((tm, d), lambda i: (i, 0)),
        compiler_params=_params("parallel"),
        name="cross_attn",
    )(h, g.reshape(1, d), w_q, kv3, w_o)


def kernel(x, mem, g_mix, w_in, b_forget, w_branch_moba, w_branch_fox, w_mix_out, rel_bias,
           g_cross, g_mem, w_cq, w_ck, w_cv, w_co, g_mlp, w_ff1, w_ff2, g_final):
    batch, seq, d = x.shape
    depth = w_in.shape[0]
    n_heads = rel_bias.shape[0]
    n_fox = b_forget.shape[1]
    wm = n_heads * HEAD_DIM
    wf = n_fox * HEAD_DIM
    m = batch * seq
    assert wm == wf and wm % COL_TILE == 0
    scale = HEAD_DIM ** -0.5
    mem2 = mem.reshape(-1, d)

    bias = moba_bias(rel_bias)
    h = x.reshape(m, d)
    for l in range(depth):
        wt = jnp.swapaxes(w_in[l], 0, 1)
        qkv_w = 3 * (wm + wf)
        w_fl = jnp.zeros((d, LANES), F32).at[:, :n_fox].set(w_in[l][:, qkv_w:qkv_w + n_fox])
        qv_t, k_hm, f_logit, a_mix = in_proj(h, g_mix[l], wt[:qkv_w].astype(BF16), w_fl,
                                             parts_w=wm, scale=scale * LOG2E)
        k_hm = k_hm.reshape(k_hm.shape[0], m // ATT_TILE, ATT_TILE, HEAD_DIM)
        c_rep = forget_cumsum(f_logit, b_forget[l], batch=batch)

        o_a, w_ff1_16, w_mix_16, w_bm16, w_bf16 = moba_attn(
            qv_t, k_hm, bias, rel_bias, batch=batch, n_heads=n_heads,
            q_off=0, k_off=0, v_off=n_heads + n_fox,
            cast=(w_ff1[l], w_mix_out[l], w_branch_moba[l], w_branch_fox[l]))
        o_f, w_ff2_16, w_g16, w_cq16, w_co16 = fox_attn(
            qv_t, k_hm, c_rep, batch=batch, n_heads=n_fox,
            q_off=n_heads, k_off=n_heads, v_off=2 * n_heads + n_fox,
            cast=(w_ff2[l], (wt, qkv_w + n_fox, 2 * d), w_cq[l], w_co[l]))
        gates = gate_proj(a_mix, w_g16)
        merged = gated_merge(o_a, o_f, w_bm16, w_bf16, gates)
        h = mm_res(merged, w_mix_16, h)

        cw = w_cq.shape[2]
        w_kv = jnp.concatenate([w_ck[l], w_cv[l]], axis=1).astype(BF16)
        kv = rms_proj(mem2, g_mem[l], w_kv, tn=2 * cw)
        h = cross_attn(h, g_cross[l], w_cq16, kv, w_co16, batch=batch, n_heads=cw // HEAD_DIM)

        u = rms_proj(h, g_mlp[l], w_ff1_16, act="relu2", tn=2 * COL_TILE)
        h = mm_res(u, w_ff2_16, h, g_final if l == depth - 1 else None)
    return h.reshape(batch, seq, d)
```

```python
import functools
import math

import jax
import jax.numpy as jnp
from jax import lax
from jax.experimental import pallas as pl
from jax.experimental.pallas import tpu as pltpu

F32 = jnp.float32
BF16 = jnp.bfloat16

HEAD_DIM = 128
MOBA_BLOCK = 256
MOBA_TOP_K = 3
NUM_BUCKETS = 32
MAX_DISTANCE = 1024
RMS_EPS = 1e-6
LOG2E = math.log2(math.e)
NEG_INF = -1e30
LANES = 128
ATT_TILE = 256
TILE = 2 * ATT_TILE
AUG = 256
HEADS_PER_STEP = 2
LOOP_UNROLL = 4
UNDERFLOW_LOG2 = 160.0
NEAR_BLOCKS = 5
VMEM_LIMIT = 56 * 1024 * 1024
ROW_TILE = 1024
COL_TILE = 1024


def _bucket_thresholds():
    max_exact = NUM_BUCKETS // 2
    thr = list(range(1, max_exact + 1))
    for k in range(max_exact + 1, NUM_BUCKETS):
        v = max_exact * (MAX_DISTANCE / max_exact) ** ((k - max_exact) / (NUM_BUCKETS - max_exact))
        n = int(math.floor(v))
        while max_exact + int(math.log(n / max_exact) / math.log(MAX_DISTANCE / max_exact)
                              * (NUM_BUCKETS - max_exact)) < k:
            n += 1
        thr.append(n)
    return tuple(thr)


BUCKET_THRESHOLDS = _bucket_thresholds()
assert (NEAR_BLOCKS - 1) * MOBA_BLOCK + 1 >= BUCKET_THRESHOLDS[-1]


def _params(*sem):
    return pltpu.CompilerParams(dimension_semantics=sem, vmem_limit_bytes=VMEM_LIMIT)


def _bf16_dot(a, b):
    return jnp.dot(a.astype(BF16), b.astype(BF16), preferred_element_type=F32)


def _rms_proj_kernel(x_ref, g_ref, w_ref, o_ref, a_ref, *, act):
    j = pl.program_id(1)

    @pl.when(j == 0)
    def _():
        x = x_ref[...]
        ms = jnp.mean(x * x, axis=-1, keepdims=True)
        a_ref[...] = (x * lax.rsqrt(ms + RMS_EPS) * g_ref[...]).astype(BF16)

    acc = jnp.dot(a_ref[...], w_ref[...], preferred_element_type=F32)
    if act == "relu2":
        acc = jnp.square(jnp.maximum(acc, 0.0))
    o_ref[...] = acc.astype(o_ref.dtype)


def rms_proj(x, g, w, *, act=None, tm=ROW_TILE, tn=COL_TILE):
    m, d = x.shape
    n = w.shape[1]
    tm, tn = min(tm, m), min(tn, n)
    return pl.pallas_call(
        functools.partial(_rms_proj_kernel, act=act),
        out_shape=jax.ShapeDtypeStruct((m, n), BF16),
        grid=(m // tm, n // tn),
        in_specs=[pl.BlockSpec((tm, d), lambda i, j: (i, 0)),
                  pl.BlockSpec((1, d), lambda i, j: (0, 0)),
                  pl.BlockSpec((d, tn), lambda i, j: (0, j))],
        out_specs=pl.BlockSpec((tm, tn), lambda i, j: (i, j)),
        scratch_shapes=[pltpu.VMEM((tm, d), BF16)],
        compiler_params=_params("parallel", "arbitrary"),
        name="rms_proj",
    )(x, g.reshape(1, d), w)


def _gate_proj_kernel(a_ref, w_ref, o_ref):
    acc = lax.dot_general(a_ref[...], w_ref[...], (((1,), (1,)), ((), ())),
                          preferred_element_type=F32)
    o_ref[...] = jax.nn.sigmoid(acc).astype(o_ref.dtype)


def gate_proj(a, w_rows, *, tm=ROW_TILE, tn=2 * COL_TILE):
    m, d = a.shape
    n = w_rows.shape[0]
    tn = min(tn, n)
    return pl.pallas_call(
        _gate_proj_kernel,
        out_shape=jax.ShapeDtypeStruct((m, n), BF16),
        grid=(m // tm, n // tn),
        in_specs=[pl.BlockSpec((tm, d), lambda i, j: (i, 0)),
                  pl.BlockSpec((tn, d), lambda i, j: (j, 0))],
        out_specs=pl.BlockSpec((tm, tn), lambda i, j: (i, j)),
        compiler_params=_params("parallel", "parallel"),
        name="gate_proj",
    )(a, w_rows)


def _in_proj_kernel(x_ref, g_ref, w_ref, wf_ref, qv_ref, k_ref, f_ref, a_ref, *, n_q, n_qv, scale):
    j = pl.program_id(1)

    @pl.when(j == 0)
    def _():
        x = x_ref[...]
        ms = jnp.mean(x * x, axis=-1, keepdims=True)
        a_ref[...] = (x * lax.rsqrt(ms + RMS_EPS) * g_ref[...]).astype(BF16)
        w_hi, w_lo, _ = _split3(wf_ref[...])
        both = _bf16_dot(a_ref[...], jnp.concatenate([w_hi, w_lo], axis=1))
        f_ref[...] = both[:, :LANES] + both[:, LANES:]

    def project():
        return lax.dot_general(a_ref[...], w_ref[...], (((1,), (1,)), ((), ())),
                               preferred_element_type=F32)

    @pl.when(j < n_qv)
    def _():
        acc = project() * jnp.where(j < n_q, F32(scale), F32(1.0))
        tm, tn = acc.shape
        for c in range(tn // LANES):
            for r in range(tm // ATT_TILE):
                blk = acc[r * ATT_TILE:(r + 1) * ATT_TILE, c * LANES:(c + 1) * LANES]
                qv_ref[c, r] = blk.T.astype(qv_ref.dtype)

    @pl.when(j >= n_qv)
    def _():
        acc = project()
        for c in range(acc.shape[1] // LANES):
            k_ref[c] = acc[:, c * LANES:(c + 1) * LANES].astype(k_ref.dtype)


def in_proj(x, g, w_rows, w_f, *, parts_w, scale, tm=ROW_TILE, tn=COL_TILE):
    m, d = x.shape
    per = parts_w // tn

    def tiles(*parts):
        return [p * per + t for p in parts for t in range(per)]

    order = tiles(0, 3, 2, 5) + tiles(1, 4)
    n_q, n_qv, n_k = 2 * per, 4 * per, 2 * per

    def w_tile(j):
        idx = order[-1]
        for t, src in reversed(list(enumerate(order[:-1]))):
            idx = jnp.where(j == t, src, idx)
        return idx

    hb = tn // LANES
    return pl.pallas_call(
        functools.partial(_in_proj_kernel, n_q=n_q, n_qv=n_qv, scale=scale),
        out_shape=(jax.ShapeDtypeStruct((n_qv * hb, m // ATT_TILE, LANES, ATT_TILE), BF16),
                   jax.ShapeDtypeStruct((n_k * hb, m, LANES), BF16),
                   jax.ShapeDtypeStruct((m, LANES), F32),
                   jax.ShapeDtypeStruct((m, d), BF16)),
        grid=(m // tm, len(order)),
        in_specs=[pl.BlockSpec((tm, d), lambda i, j: (i, 0)),
                  pl.BlockSpec((1, d), lambda i, j: (0, 0)),
                  pl.BlockSpec((tn, d), lambda i, j: (w_tile(j), 0)),
                  pl.BlockSpec((d, LANES), lambda i, j: (0, 0))],
        out_specs=(pl.BlockSpec((hb, tm // ATT_TILE, LANES, ATT_TILE),
                                lambda i, j: (jnp.minimum(j, n_qv - 1), i, 0, 0)),
                   pl.BlockSpec((hb, tm, LANES),
                                lambda i, j: (jnp.clip(j - n_qv, 0, n_k - 1), i, 0)),
                   pl.BlockSpec((tm, LANES), lambda i, j: (i, 0)),
                   pl.BlockSpec((tm, d), lambda i, j: (i, 0))),
        compiler_params=_params("parallel", "arbitrary"),
        name="in_proj",
    )(x, g.reshape(1, d), w_rows, w_f)


def _forget_cumsum_kernel(f_ref, b_ref, o_ref, carry_ref):
    t = pl.program_id(1)

    @pl.when(t == 0)
    def _():
        carry_ref[...] = jnp.zeros_like(carry_ref)

    f = f_ref[...] + b_ref[...]
    tm = f.shape[0]
    logf = jnp.minimum(f, 0.0) - jnp.log1p(jnp.exp(-jnp.abs(f)))
    logf = logf * LOG2E
    row = lax.broadcasted_iota(jnp.int32, (tm, tm), 0)
    col = lax.broadcasted_iota(jnp.int32, (tm, tm), 1)
    tri = jnp.where(col <= row, 1.0, 0.0).astype(BF16)
    hi, mid, lo = _split3(logf)
    c = (_bf16_dot(tri, hi) + (_bf16_dot(tri, mid) + _bf16_dot(tri, lo))) + carry_ref[0:1, :]
    o_ref[...] = c
    carry_ref[...] = jnp.broadcast_to(c[tm - 1:tm, :], carry_ref.shape)


def forget_cumsum(f, b_f, *, batch, tm=512):
    m = f.shape[0]
    nt = (m // batch) // tm
    b_pad = jnp.zeros((1, LANES), F32).at[0, :b_f.shape[0]].set(b_f)
    return pl.pallas_call(
        _forget_cumsum_kernel,
        out_shape=jax.ShapeDtypeStruct((m, LANES), F32),
        grid=(batch, nt),
        in_specs=[pl.BlockSpec((tm, LANES), lambda b, t: (b * nt + t, 0)),
                  pl.BlockSpec((1, LANES), lambda b, t: (0, 0))],
        out_specs=pl.BlockSpec((tm, LANES), lambda b, t: (b * nt + t, 0)),
        scratch_shapes=[pltpu.VMEM((8, LANES), F32)],
        compiler_params=_params("parallel", "arbitrary"),
        name="forget_cumsum",
    )(f, b_pad)


def _moba_bias_kernel(tab_ref, o_ref):
    h = pl.program_id(0)
    key = lax.broadcasted_iota(jnp.int32, (ATT_TILE, ATT_TILE), 0)
    qry = lax.broadcasted_iota(jnp.int32, (ATT_TILE, ATT_TILE), 1)
    far = tab_ref[h, NUM_BUCKETS - 1] * LOG2E

    def block(delta):
        dist = delta * MOBA_BLOCK + qry - key
        val = jnp.full((ATT_TILE, ATT_TILE), tab_ref[h, 0], F32)
        for k in range(1, NUM_BUCKETS):
            val = jnp.where(dist >= BUCKET_THRESHOLDS[k - 1], tab_ref[h, k], val)
        val = val * LOG2E
        if delta == 0:
            val = jnp.where(dist >= 0, val, NEG_INF)
        return val

    t = [block(delta) for delta in range(NEAR_BLOCKS + 1)]
    a = ATT_TILE
    o_ref[0, 0, 0:a, 0:a] = t[0]
    o_ref[0, 0, 0:a, a:2 * a] = t[1]
    o_ref[0, 0, a:2 * a, 0:a] = jnp.full((a, a), NEG_INF, F32)
    o_ref[0, 0, a:2 * a, a:2 * a] = t[0]
    for d in (1, 2):
        o_ref[0, d, 0:a, 0:a] = t[2 * d] - far
        o_ref[0, d, 0:a, a:2 * a] = t[2 * d + 1] - far
        o_ref[0, d, a:2 * a, 0:a] = t[2 * d - 1] - far
        o_ref[0, d, a:2 * a, a:2 * a] = t[2 * d] - far


def moba_bias(rel_bias):
    h = rel_bias.shape[0]
    return pl.pallas_call(
        _moba_bias_kernel,
        out_shape=jax.ShapeDtypeStruct((h, 3, TILE, TILE), F32),
        grid=(h,),
        in_specs=[pl.BlockSpec(memory_space=pltpu.SMEM)],
        out_specs=pl.BlockSpec((1, 3, TILE, TILE), lambda i: (i, 0, 0, 0)),
        compiler_params=_params("parallel"),
        name="moba_bias",
    )(rel_bias.astype(F32))


class _Softmax:
    def __init__(self, hh, s_ref, m_ref, l_ref, acc_ref, load_values):
        self.s_ref, self.m_ref, self.l_ref, self.acc_ref = s_ref.at[hh], m_ref.at[hh], l_ref.at[hh], acc_ref.at[hh]
        self.load_values = load_values

    def reset(self):
        self.m_ref[...] = jnp.full(self.m_ref.shape, NEG_INF, F32)
        self.l_ref[...] = jnp.zeros(self.l_ref.shape, F32)
        self.acc_ref[...] = jnp.zeros(self.acc_ref.shape, F32)

    def fold(self, slot, tile, mask=None):
        for half in range(2):
            self.fold_half(slot, tile, half, mask)

    def fold_half(self, slot, tile, half, mask=None):
        width = self.m_ref.shape[-1] // 2
        cols = slice(half * width, (half + 1) * width)
        s = self.s_ref[slot, :, cols]
        if mask is not None:
            s = jnp.where(mask[:, cols], s, NEG_INF)
        m = self.m_ref[:, cols]
        m_new = jnp.maximum(m, jnp.max(s, axis=0, keepdims=True))
        alpha = jnp.exp2(m - m_new)
        p = jnp.exp2(s - m_new)
        self.acc_ref[:, cols] = alpha * self.acc_ref[:, cols] + jnp.dot(
            self.load_values(tile), p.astype(BF16), preferred_element_type=F32)
        self.l_ref[:, cols] = alpha * self.l_ref[:, cols] + jnp.sum(p, axis=0, keepdims=True)
        self.m_ref[:, cols] = m_new

    def result(self):
        return self.acc_ref[...] * (1.0 / self.l_ref[...])


def _softmax_scratch(t):
    nh = HEADS_PER_STEP
    return [pltpu.VMEM((nh, 2, t, t), F32),
            pltpu.VMEM((nh, 1, t), F32),
            pltpu.VMEM((nh, 1, t), F32),
            pltpu.VMEM((nh, HEAD_DIM, t), F32)]


def _fold_tile_run(heads, s_ref, logits, fold, base, count, last_tile):
    def step(r, slot, look_ahead=True):
        for hh in heads:
            if look_ahead:
                s_ref[hh, 1 - slot] = logits[hh](jnp.minimum(base + r + 1, last_tile))
            fold[hh].fold(slot, base + r)

    def unrolled_body(p, carry):
        for u in range(LOOP_UNROLL):
            step(LOOP_UNROLL * p + u, (u + 1) % 2)
        return carry

    lax.fori_loop(0, count // LOOP_UNROLL, unrolled_body, 0)
    assert LOOP_UNROLL == 4
    rem = count % LOOP_UNROLL
    done = count - rem

    @pl.when(rem >= 2)
    def _():
        step(done, 1)
        step(done + 1, 0)

    @pl.when(rem % 2 == 1)
    def _():
        step(count - 1, 1, look_ahead=False)


def _top_k_bias(gate, eligible, blk):
    lowest = float(jnp.finfo(F32).min)
    blk_f = blk.astype(F32)
    g = jnp.where(eligible, gate, NEG_INF)
    bias = jnp.full(gate.shape, NEG_INF, F32)
    for _ in range(MOBA_TOP_K):
        best = jnp.max(g, axis=0, keepdims=True)
        first = jnp.min(jnp.where(g == best, blk_f, float(gate.shape[0])), axis=0, keepdims=True)
        hit = blk_f == first
        bias = jnp.where(hit, 0.0, bias)
        g = jnp.where(hit, lowest, g)
    return jnp.where(eligible, bias, NEG_INF)


def _split3(x):
    hi = x.astype(BF16).astype(F32)
    mid = (x - hi).astype(BF16).astype(F32)
    lo = (x - hi - mid).astype(BF16).astype(F32)
    return hi, mid, lo


def _slab_cast_specs(weights, n_steps, inner_steps, step_index):
    in_specs, out_specs, out_shapes = [], [], []
    for w in weights:
        if isinstance(w, tuple):
            w, first_row, n_rows = w
        else:
            first_row, n_rows = 0, w.shape[0]
        rows = max(n_rows // n_steps, 16)
        n_slabs = n_rows // rows
        assert n_slabs * rows == n_rows and rows % 16 == 0 and n_steps % n_slabs == 0
        share = n_steps // n_slabs
        assert inner_steps % share == 0

        def slab(*g, share=share):
            return step_index(*g) // share

        if first_row == 0:
            in_specs.append(pl.BlockSpec((rows, w.shape[1]), lambda *g, slab=slab: (slab(*g), 0)))
        else:
            in_specs.append(pl.BlockSpec(
                (pl.Element(rows), pl.Element(w.shape[1])),
                lambda *g, first_row=first_row, rows=rows, slab=slab: (
                    pl.multiple_of(first_row + slab(*g) * rows, 8), 0)))
        out_specs.append(pl.BlockSpec((rows, w.shape[1]), lambda *g, slab=slab: (slab(*g), 0)))
        out_shapes.append(jax.ShapeDtypeStruct((n_rows, w.shape[1]), BF16))
    return in_specs, out_specs, out_shapes


def _moba_attn_kernel(*refs, n_blocks, n_cast):
    tab_ref, q_ref, k_ref, v_ref, bias_ref = refs[:5]
    o_ref = refs[5 + n_cast]
    kaug_ref, kbar_ref, kb3_ref, s_ref, m_ref, l_ref, acc_ref = refs[6 + 2 * n_cast:]
    for src, dst in zip(refs[5:5 + n_cast], refs[6 + n_cast:6 + 2 * n_cast]):
        dst[...] = src[...].astype(dst.dtype)
    _moba_attn_body(tab_ref, q_ref, k_ref, v_ref, bias_ref, o_ref,
                    kaug_ref, kbar_ref, kb3_ref, s_ref, m_ref, l_ref, acc_ref, n_blocks=n_blocks)


def _moba_attn_body(tab_ref, q_ref, k_ref, v_ref, bias_ref, o_ref,
                    kaug_ref, kbar_ref, kb3_ref, s_ref, m_ref, l_ref, acc_ref, *, n_blocks):
    hp = pl.program_id(1)
    i = pl.program_id(2)
    n_tiles = n_blocks // 2
    heads = range(HEADS_PER_STEP)

    @pl.when(i == 0)
    def _():
        lane = lax.broadcasted_iota(jnp.int32, (TILE, AUG - HEAD_DIM), 1)
        row = lax.broadcasted_iota(jnp.int32, (TILE, AUG - HEAD_DIM), 0)
        ones_lane = jnp.where(lane == n_blocks, 1.0, jnp.where(lane == n_blocks + 1, 1.0, 0.0))
        for hh in heads:
            for n in range(n_blocks):
                kbar_ref[hh, n:n + 1, :] = jnp.sum(k_ref[hh, n].astype(F32), axis=0, keepdims=True)
            hi, mid, lo = _split3(kbar_ref[hh] * (1.0 / MOBA_BLOCK))
            kb3_ref[hh, 0:n_blocks, :] = hi.astype(BF16)
            kb3_ref[hh, n_blocks:2 * n_blocks, :] = mid.astype(BF16)
            kb3_ref[hh, 2 * n_blocks:3 * n_blocks, :] = lo.astype(BF16)
            for j in range(n_tiles):
                blk_of_row = jnp.where(row < ATT_TILE, 2 * j, 2 * j + 1)
                right = jnp.where(lane == blk_of_row, 1.0, ones_lane).astype(BF16)
                kaug_ref[hh, j] = jnp.concatenate(
                    [k_ref[hh, 2 * j:2 * j + 2].reshape(TILE, HEAD_DIM), right], axis=1)

    blk = lax.broadcasted_iota(jnp.int32, (n_blocks, TILE), 0)
    qlane = lax.broadcasted_iota(jnp.int32, (n_blocks, TILE), 1)
    own = 2 * i + jnp.where(qlane >= ATT_TILE, 1, 0)
    eligible = blk < own
    r16 = lax.broadcasted_iota(jnp.int32, (16, TILE), 0)

    j1 = jnp.where(i >= 1, i - 1, i + 1)
    j2 = jnp.where(i >= 2, i - 2, i + 1)
    n_far = jnp.maximum(i - 2, 0)

    far_logits, fold = [], []
    for hh in heads:
        qt = jnp.concatenate([q_ref[hh, 0], q_ref[hh, 1]], axis=1)
        g3 = jnp.dot(kb3_ref[hh], qt, preferred_element_type=F32)
        gate = g3[0:n_blocks] + g3[n_blocks:2 * n_blocks] + g3[2 * n_blocks:3 * n_blocks]
        selb = _top_k_bias(gate, eligible, blk)

        far = jnp.full((16, TILE), tab_ref[hp * HEADS_PER_STEP + hh, NUM_BUCKETS - 1] * LOG2E, F32)
        far_hi = far.astype(BF16).astype(F32)
        far_rows = jnp.where(r16 == 0, far_hi, jnp.where(r16 == 1, far - far_hi, 0.0))
        q_aug = jnp.concatenate(
            [qt, selb.astype(BF16), far_rows.astype(BF16),
             jnp.zeros((AUG - HEAD_DIM - n_blocks - 16, TILE), BF16)], axis=0)

        def values(j, hh=hh):
            return jnp.concatenate([v_ref[hh, 2 * j], v_ref[hh, 2 * j + 1]], axis=1)

        def head_far_logits(j, hh=hh, q_aug=q_aug):
            return jnp.dot(kaug_ref[hh, j], q_aug, preferred_element_type=F32)

        sm = _Softmax(hh, s_ref, m_ref, l_ref, acc_ref, values)
        far_logits.append(head_far_logits)
        fold.append(sm)

        sel_own = jnp.sum(jnp.where(blk == 2 * i, selb, 0.0), axis=0, keepdims=True)
        sel_own = jnp.where(qlane[0:1] >= ATT_TILE, sel_own, 0.0)
        kd = k_ref[hh, pl.ds(2 * i, 2)].reshape(TILE, HEAD_DIM)
        sd = jnp.dot(kd, qt, preferred_element_type=F32) + bias_ref[hh, 0]
        s_ref[hh, 0, 0:ATT_TILE, :] = sd[0:ATT_TILE] + sel_own
        s_ref[hh, 0, ATT_TILE:TILE, :] = sd[ATT_TILE:TILE]
        sm.reset()

    for hh in heads:
        s_ref[hh, 1] = far_logits[hh](j1) + bias_ref[hh, 1]
        fold[hh].fold(0, i)
    for hh in heads:
        s_ref[hh, 0] = far_logits[hh](j2) + bias_ref[hh, 2]
        fold[hh].fold(1, j1)
    for hh in heads:
        s_ref[hh, 1] = far_logits[hh](0)
        fold[hh].fold(0, j2)

    _fold_tile_run(heads, s_ref, far_logits, fold, 0, n_far, n_tiles - 1)

    for hh in heads:
        o_ref[:, hh * HEAD_DIM:(hh + 1) * HEAD_DIM] = fold[hh].result().T.astype(o_ref.dtype)


def moba_attn(qv_t, k, bias, rel_bias, *, batch, n_heads, q_off, k_off, v_off, cast=()):
    mb = k.shape[1]
    nb = mb // batch
    nt = nb // 2
    m = mb * ATT_TILE
    hb = HEADS_PER_STEP
    nhp = n_heads // hb
    assert nb + 16 <= AUG - HEAD_DIM and nt >= 4
    assert n_heads % hb == 0 and q_off % hb == 0 and k_off % hb == 0 and v_off % hb == 0
    c_in, c_out, c_shape = _slab_cast_specs(cast, batch * nhp * nt, nt,
                                            lambda b, h, i: (b * nhp + h) * nt + i)
    return pl.pallas_call(
        functools.partial(_moba_attn_kernel, n_blocks=nb, n_cast=len(cast)),
        out_shape=[jax.ShapeDtypeStruct((m, n_heads * HEAD_DIM), BF16)] + c_shape,
        grid=(batch, nhp, nt),
        in_specs=[pl.BlockSpec(memory_space=pltpu.SMEM),
                  pl.BlockSpec((hb, 2, HEAD_DIM, ATT_TILE),
                               lambda b, h, i: (q_off // hb + h, b * nt + i, 0, 0)),
                  pl.BlockSpec((hb, nb, ATT_TILE, HEAD_DIM), lambda b, h, i: (k_off // hb + h, b, 0, 0)),
                  pl.BlockSpec((hb, nb, HEAD_DIM, ATT_TILE), lambda b, h, i: (v_off // hb + h, b, 0, 0)),
                  pl.BlockSpec((hb, 3, TILE, TILE), lambda b, h, i: (h, 0, 0, 0))] + c_in,
        out_specs=[pl.BlockSpec((TILE, hb * HEAD_DIM), lambda b, h, i: (b * nt + i, h))] + c_out,
        scratch_shapes=[pltpu.VMEM((hb, nt, TILE, AUG), BF16),
                        pltpu.VMEM((hb, nb, HEAD_DIM), F32),
                        pltpu.VMEM((hb, 3 * nb, HEAD_DIM), BF16)] + _softmax_scratch(TILE),
        compiler_params=_params("parallel", "parallel", "arbitrary"),
        name="moba_attn",
    )(rel_bias.astype(F32), qv_t, k, qv_t, bias, *[c[0] if isinstance(c, tuple) else c for c in cast])


def _fox_attn_kernel(*refs, n_tiles, n_cast):
    q_ref, k_ref, v_ref, c_ref = refs[:4]
    o_ref = refs[4 + n_cast]
    kaug_ref, bound_ref, s_ref, m_ref, l_ref, acc_ref = refs[5 + 2 * n_cast:]
    for src, dst in zip(refs[4:4 + n_cast], refs[5 + n_cast:5 + 2 * n_cast]):
        dst[...] = src[...].astype(dst.dtype)
    _fox_attn_body(q_ref, k_ref, v_ref, c_ref, o_ref, kaug_ref, bound_ref, s_ref, m_ref, l_ref,
                   acc_ref, n_tiles=n_tiles)


def _fox_attn_body(q_ref, k_ref, v_ref, c_ref, o_ref, kaug_ref, bound_ref, s_ref, m_ref, l_ref,
                   acc_ref, *, n_tiles):
    hp = pl.program_id(1)
    i = pl.program_id(2)
    t = TILE
    heads = range(HEADS_PER_STEP)

    lane1 = lax.broadcasted_iota(jnp.int32, (1, LANES), 1)

    @pl.when(i == 0)
    def _():
        nh = len(heads)
        src = lax.broadcasted_iota(jnp.int32, (3 * LANES, nh * LANES), 0)
        dst = lax.broadcasted_iota(jnp.int32, (3 * LANES, nh * LANES), 1)
        place = jnp.where((src % LANES == hp * nh + dst // LANES) & (src // LANES == dst % LANES),
                          1.0, 0.0).astype(BF16)
        c_first = [jnp.zeros((1, LANES), F32) for _ in heads]
        c_last = [jnp.zeros((1, LANES), F32) for _ in heads]
        k_norm2 = [jnp.zeros((t, 1), F32) for _ in heads]
        for j in range(n_tiles):
            c_all = c_ref[j * t:(j + 1) * t, :]
            terms = jnp.concatenate([x.astype(BF16) for x in _split3(-c_all)], axis=1)
            right = jnp.dot(terms, place, preferred_element_type=F32).astype(BF16)
            for hh in heads:
                k = k_ref[hh, 2 * j:2 * j + 2].reshape(t, HEAD_DIM)
                kaug_ref[hh, j] = jnp.concatenate([k, right[:, hh * LANES:(hh + 1) * LANES]], axis=1)
                kf = k.astype(F32)
                k_norm2[hh] = jnp.maximum(k_norm2[hh], jnp.sum(kf * kf, axis=1, keepdims=True))
                ends = jnp.concatenate([c_all[0:1, :], c_all[t - 1:t, :]], axis=0)
                ends = jnp.sum(jnp.where(lane1 == hp * nh + hh, ends, 0.0), axis=1, keepdims=True)
                c_first[hh] = jnp.where(lane1 == j, ends[0:1, :], c_first[hh])
                c_last[hh] = jnp.where(lane1 == j, ends[1:2, :], c_last[hh])
        for hh in heads:
            bound_ref[hh, 0:1, :] = c_first[hh]
            bound_ref[hh, 1:2, :] = c_last[hh]
            bound_ref[hh, 2:3, :] = jnp.broadcast_to(
                jnp.max(k_norm2[hh], axis=0, keepdims=True), (1, LANES))

    r_aug = lax.broadcasted_iota(jnp.int32, (AUG - HEAD_DIM, t), 0)
    ones_rows = jnp.where(r_aug < 3, 1.0, 0.0).astype(BF16)

    logits, fold, skippable = [], [], []
    for hh in heads:
        qt = jnp.concatenate([q_ref[hh, 0], q_ref[hh, 1]], axis=1)
        q_aug = jnp.concatenate([qt, ones_rows], axis=0)

        qf = qt.astype(F32)
        q_norm2 = jnp.max(jnp.sum(qf * qf, axis=0, keepdims=True), axis=1, keepdims=True)
        c_here = jnp.sum(jnp.where(lane1 == i, bound_ref[hh, 0:1, :], 0.0), axis=1, keepdims=True)
        gap = (bound_ref[hh, 1:2, :] - c_here) - UNDERFLOW_LOG2
        dead = (lane1 < i) & (gap > 0.0) & (gap * gap > 4.0 * q_norm2 * bound_ref[hh, 2:3, :])
        skippable.append(jnp.sum(jnp.where(dead, 1.0, 0.0), axis=1, keepdims=True))

        def head_logits(n, hh=hh, q_aug=q_aug):
            return jnp.dot(kaug_ref[hh, n], q_aug, preferred_element_type=F32)

        def values(n, hh=hh):
            return jnp.concatenate([v_ref[hh, 2 * n], v_ref[hh, 2 * n + 1]], axis=1)

        logits.append(head_logits)
        fold.append(_Softmax(hh, s_ref, m_ref, l_ref, acc_ref, values))

    key = lax.broadcasted_iota(jnp.int32, (t, t), 0)
    qry = lax.broadcasted_iota(jnp.int32, (t, t), 1)
    causal = key <= qry
    j0 = functools.reduce(jnp.minimum, skippable)[0, 0].astype(jnp.int32)
    n_past = i - j0
    for hh in heads:
        s_ref[hh, 0] = logits[hh](i)
        fold[hh].reset()
    for hh in heads:
        s_ref[hh, 1] = logits[hh](j0)
        fold[hh].fold(0, i, mask=causal)

    _fold_tile_run(heads, s_ref, logits, fold, j0, n_past, n_tiles - 1)

    for hh in heads:
        o_ref[:, hh * HEAD_DIM:(hh + 1) * HEAD_DIM] = fold[hh].result().T.astype(o_ref.dtype)


def fox_attn(qv_t, k, c_rep, *, batch, n_heads, q_off, k_off, v_off, cast=()):
    mb = k.shape[1]
    nb = mb // batch
    nt = nb // 2
    m = mb * ATT_TILE
    t = TILE
    hb = HEADS_PER_STEP
    nhp = n_heads // hb
    assert n_heads % hb == 0 and q_off % hb == 0 and k_off % hb == 0 and v_off % hb == 0
    assert nt <= LANES and n_heads <= LANES
    c_in, c_out, c_shape = _slab_cast_specs(cast, batch * nhp * nt, nt,
                                            lambda b, h, i: (b * nhp + h) * nt + i)
    return pl.pallas_call(
        functools.partial(_fox_attn_kernel, n_tiles=nt, n_cast=len(cast)),
        out_shape=[jax.ShapeDtypeStruct((m, n_heads * HEAD_DIM), BF16)] + c_shape,
        grid=(batch, nhp, nt),
        in_specs=[pl.BlockSpec((hb, 2, HEAD_DIM, ATT_TILE),
                               lambda b, h, i: (q_off // hb + h, b * nt + i, 0, 0)),
                  pl.BlockSpec((hb, nb, ATT_TILE, HEAD_DIM), lambda b, h, i: (k_off // hb + h, b, 0, 0)),
                  pl.BlockSpec((hb, nb, HEAD_DIM, ATT_TILE), lambda b, h, i: (v_off // hb + h, b, 0, 0)),
                  pl.BlockSpec((nb * ATT_TILE, LANES), lambda b, h, i: (b, 0))] + c_in,
        out_specs=[pl.BlockSpec((t, hb * HEAD_DIM), lambda b, h, i: (b * nt + i, h))] + c_out,
        scratch_shapes=[pltpu.VMEM((hb, nt, t, AUG), BF16),
                        pltpu.VMEM((hb, 8, LANES), F32)] + _softmax_scratch(t),
        compiler_params=_params("parallel", "parallel", "arbitrary"),
        name="fox_attn",
    )(qv_t, k, qv_t, c_rep, *[c[0] if isinstance(c, tuple) else c for c in cast])


def _gated_merge_kernel(oa_ref, of_ref, wa_ref, wf_ref, ga_ref, gf_ref, o_ref):
    ua = jnp.dot(oa_ref[...], wa_ref[...], preferred_element_type=F32)
    uf = jnp.dot(of_ref[...], wf_ref[...], preferred_element_type=F32)
    o_ref[...] = (ga_ref[...].astype(F32) * ua + gf_ref[...].astype(F32) * uf).astype(o_ref.dtype)


def gated_merge(o_a, o_f, w_a, w_f, gates, *, tm=ROW_TILE, tn=COL_TILE):
    m, ka = o_a.shape
    kf = o_f.shape[1]
    n = w_a.shape[1]
    nj = n // tn
    return pl.pallas_call(
        _gated_merge_kernel,
        out_shape=jax.ShapeDtypeStruct((m, n), BF16),
        grid=(m // tm, nj),
        in_specs=[pl.BlockSpec((tm, ka), lambda i, j: (i, 0)),
                  pl.BlockSpec((tm, kf), lambda i, j: (i, 0)),
                  pl.BlockSpec((ka, tn), lambda i, j: (0, j)),
                  pl.BlockSpec((kf, tn), lambda i, j: (0, j)),
                  pl.BlockSpec((tm, tn), lambda i, j: (i, j)),
                  pl.BlockSpec((tm, tn), lambda i, j: (i, nj + j))],
        out_specs=pl.BlockSpec((tm, tn), lambda i, j: (i, j)),
        compiler_params=_params("parallel", "parallel"),
        name="gated_merge",
    )(o_a, o_f, w_a, w_f, gates, gates)


def _mm_res_kernel(lhs_ref, w_ref, res_ref, g_ref, o_ref, *, final_norm):
    kk = pl.program_id(1)

    @pl.when(kk == 0)
    def _():
        o_ref[...] = res_ref[...]

    o_ref[...] += jnp.dot(lhs_ref[...], w_ref[...], preferred_element_type=F32)

    if final_norm:
        @pl.when(kk == pl.num_programs(1) - 1)
        def _():
            hres = o_ref[...]
            ms = jnp.mean(hres * hres, axis=-1, keepdims=True)
            o_ref[...] = hres * lax.rsqrt(ms + RMS_EPS) * g_ref[...]


def mm_res(lhs, w, res, g=None, *, tm=ROW_TILE, tk=1024):
    m, k = lhs.shape
    n = w.shape[1]
    tk = min(tk, k)
    final_norm = g is not None
    if g is None:
        g = jnp.ones((n,), F32)
    return pl.pallas_call(
        functools.partial(_mm_res_kernel, final_norm=final_norm),
        out_shape=jax.ShapeDtypeStruct((m, n), F32),
        grid=(m // tm, k // tk),
        in_specs=[pl.BlockSpec((tm, tk), lambda i, kk: (i, kk)),
                  pl.BlockSpec((tk, n), lambda i, kk: (kk, 0)),
                  pl.BlockSpec((tm, n), lambda i, kk: (i, 0)),
                  pl.BlockSpec((1, n), lambda i, kk: (0, 0))],
        out_specs=pl.BlockSpec((tm, n), lambda i, kk: (i, 0)),
        compiler_params=_params("parallel", "arbitrary"),
        name="mm_res",
    )(lhs, w, res, g.reshape(1, n))


def _cross_attn_kernel(h_ref, g_ref, wq_ref, kv_ref, wo_ref, o_ref, *, n_heads):
    width = n_heads * HEAD_DIM
    hres = h_ref[...]
    ms = jnp.mean(hres * hres, axis=-1, keepdims=True)
    c = (hres * lax.rsqrt(ms + RMS_EPS) * g_ref[...]).astype(BF16)
    q_all = (jnp.dot(c, wq_ref[...], preferred_element_type=F32)
             * (HEAD_DIM ** -0.5 * LOG2E)).astype(BF16)
    heads = []
    for h in range(n_heads):
        q = q_all[:, h * HEAD_DIM:(h + 1) * HEAD_DIM]
        k = kv_ref[0, :, h * HEAD_DIM:(h + 1) * HEAD_DIM]
        v = kv_ref[0, :, width + h * HEAD_DIM:width + (h + 1) * HEAD_DIM]
        s = lax.dot_general(q, k, (((1,), (1,)), ((), ())), preferred_element_type=F32)
        m = jnp.max(s, axis=-1, keepdims=True)
        p = jnp.exp2(s - m)
        l = jnp.sum(p, axis=-1, keepdims=True)
        o = jnp.dot(p.astype(BF16), v, preferred_element_type=F32) * (1.0 / l)
        heads.append(o.astype(BF16))
    o_all = jnp.concatenate(heads, axis=1)
    o_ref[...] = hres + jnp.dot(o_all, wo_ref[...], preferred_element_type=F32)


def cross_attn(h, g, w_q, kv, w_o, *, batch, n_heads, tm=ROW_TILE):
    m, d = h.shape
    width = w_q.shape[1]
    n_mem = kv.shape[0] // batch
    tiles_per_batch = (m // batch) // tm
    kv3 = kv.reshape(batch, n_mem, 2 * width)
    return pl.pallas_call(
        functools.partial(_cross_attn_kernel, n_heads=n_heads),
        out_shape=jax.ShapeDtypeStruct((m, d), F32),
        grid=(m // tm,),
        in_specs=[pl.BlockSpec((tm, d), lambda i: (i, 0)),
                  pl.BlockSpec((1, d), lambda i: (0, 0)),
                  pl.BlockSpec((d, width), lambda i: (0, 0)),
                  pl.BlockSpec((1, n_mem, 2 * width), lambda i: (i // tiles_per_batch, 0, 0)),
                  pl.BlockSpec((width, d), lambda i: (0, 0))],
        out_specs=pl.BlockSpec((tm, d), lambda i: (i, 0)),
        compiler_params=_params("parallel"),
        name="cross_attn",
    )(h, g.reshape(1, d), w_q, kv3, w_o)


def kernel(x, mem, g_mix, w_in, b_forget, w_branch_moba, w_branch_fox, w_mix_out, rel_bias,
           g_cross, g_mem, w_cq, w_ck, w_cv, w_co, g_mlp, w_ff1, w_ff2, g_final):
    batch, seq, d = x.shape
    depth = w_in.shape[0]
    n_heads = rel_bias.shape[0]
    n_fox = b_forget.shape[1]
    wm = n_heads * HEAD_DIM
    wf = n_fox * HEAD_DIM
    m = batch * seq
    assert wm == wf and wm % COL_TILE == 0
    scale = HEAD_DIM ** -0.5
    mem2 = mem.reshape(-1, d)

    bias = moba_bias(rel_bias)
    h = x.reshape(m, d)
    for l in range(depth):
        wt = jnp.swapaxes(w_in[l], 0, 1)
        qkv_w = 3 * (wm + wf)
        w_fl = jnp.zeros((d, LANES), F32).at[:, :n_fox].set(w_in[l][:, qkv_w:qkv_w + n_fox])
        qv_t, k_hm, f_logit, a_mix = in_proj(h, g_mix[l], wt[:qkv_w].astype(BF16), w_fl,
                                             parts_w=wm, scale=scale * LOG2E)
        k_hm = k_hm.reshape(k_hm.shape[0], m // ATT_TILE, ATT_TILE, HEAD_DIM)
        c_rep = forget_cumsum(f_logit, b_forget[l], batch=batch)

        o_a, w_ff1_16, w_mix_16, w_bm16, w_bf16 = moba_attn(
            qv_t, k_hm, bias, rel_bias, batch=batch, n_heads=n_heads,
            q_off=0, k_off=0, v_off=n_heads + n_fox,
            cast=(w_ff1[l], w_mix_out[l], w_branch_moba[l], w_branch_fox[l]))
        o_f, w_ff2_16, w_g16, w_cq16, w_co16 = fox_attn(
            qv_t, k_hm, c_rep, batch=batch, n_heads=n_fox,
            q_off=n_heads, k_off=n_heads, v_off=2 * n_heads + n_fox,
            cast=(w_ff2[l], (wt, qkv_w + n_fox, 2 * d), w_cq[l], w_co[l]))
        gates = gate_proj(a_mix, w_g16)
        merged = gated_merge(o_a, o_f, w_bm16, w_bf16, gates)
        h = mm_res(merged, w_mix_16, h)

        cw = w_cq.shape[2]
        w_kv = jnp.concatenate([w_ck[l], w_cv[l]], axis=1).astype(BF16)
        kv = rms_proj(mem2, g_mem[l], w_kv, tn=2 * cw)
        h = cross_attn(h, g_cross[l], w_cq16, kv, w_co16, batch=batch, n_heads=cw // HEAD_DIM)

        u = rms_proj(h, g_mlp[l], w_ff1_16, act="relu2", tn=2 * COL_TILE)
        h = mm_res(u, w_ff2_16, h, g_final if l == depth - 1 else None)
    return h.reshape(batch, seq, d)
```

```python
import functools
import math

import jax
import jax.numpy as jnp
from jax import lax
from jax.experimental import pallas as pl
from jax.experimental.pallas import tpu as pltpu

F32 = jnp.float32
BF16 = jnp.bfloat16

HEAD_DIM = 128
MOBA_BLOCK = 256
MOBA_TOP_K = 3
NUM_BUCKETS = 32
MAX_DISTANCE = 1024
RMS_EPS = 1e-6
LOG2E = math.log2(math.e)
NEG_INF = -1e30
LANES = 128
ATT_TILE = 256
TILE = 2 * ATT_TILE
AUG = 256
HEADS_PER_STEP = 2
LOOP_UNROLL = 4
FOX_Q_TILES = 2
UNDERFLOW_LOG2 = 160.0
NEAR_BLOCKS = 5
VMEM_LIMIT = 60 * 1024 * 1024
ROW_TILE = 1024
COL_TILE = 1024


def _bucket_thresholds():
    max_exact = NUM_BUCKETS // 2
    thr = list(range(1, max_exact + 1))
    for k in range(max_exact + 1, NUM_BUCKETS):
        v = max_exact * (MAX_DISTANCE / max_exact) ** ((k - max_exact) / (NUM_BUCKETS - max_exact))
        n = int(math.floor(v))
        while max_exact + int(math.log(n / max_exact) / math.log(MAX_DISTANCE / max_exact)
                              * (NUM_BUCKETS - max_exact)) < k:
            n += 1
        thr.append(n)
    return tuple(thr)


BUCKET_THRESHOLDS = _bucket_thresholds()
assert (NEAR_BLOCKS - 1) * MOBA_BLOCK + 1 >= BUCKET_THRESHOLDS[-1]


def _params(*sem):
    return pltpu.CompilerParams(dimension_semantics=sem, vmem_limit_bytes=VMEM_LIMIT)


def _bf16_dot(a, b):
    return jnp.dot(a.astype(BF16), b.astype(BF16), preferred_element_type=F32)


def _rms_proj_kernel(x_ref, g_ref, w_ref, o_ref, a_ref, *, act):
    j = pl.program_id(1)

    @pl.when(j == 0)
    def _():
        x = x_ref[...]
        ms = jnp.mean(x * x, axis=-1, keepdims=True)
        a_ref[...] = (x * lax.rsqrt(ms + RMS_EPS) * g_ref[...]).astype(BF16)

    acc = jnp.dot(a_ref[...], w_ref[...], preferred_element_type=F32)
    if act == "relu2":
        acc = jnp.square(jnp.maximum(acc, 0.0))
    o_ref[...] = acc.astype(o_ref.dtype)


def rms_proj(x, g, w, *, act=None, tm=ROW_TILE, tn=COL_TILE):
    m, d = x.shape
    n = w.shape[1]
    tm, tn = min(tm, m), min(tn, n)
    return pl.pallas_call(
        functools.partial(_rms_proj_kernel, act=act),
        out_shape=jax.ShapeDtypeStruct((m, n), BF16),
        grid=(m // tm, n // tn),
        in_specs=[pl.BlockSpec((tm, d), lambda i, j: (i, 0)),
                  pl.BlockSpec((1, d), lambda i, j: (0, 0)),
                  pl.BlockSpec((d, tn), lambda i, j: (0, j))],
        out_specs=pl.BlockSpec((tm, tn), lambda i, j: (i, j)),
        scratch_shapes=[pltpu.VMEM((tm, d), BF16)],
        compiler_params=_params("parallel", "arbitrary"),
        name="rms_proj",
    )(x, g.reshape(1, d), w)


def _gate_proj_kernel(a_ref, w_ref, o_ref):
    acc = lax.dot_general(a_ref[...], w_ref[...], (((1,), (1,)), ((), ())),
                          preferred_element_type=F32)
    o_ref[...] = jax.nn.sigmoid(acc).astype(o_ref.dtype)


def gate_proj(a, w_rows, *, tm=ROW_TILE, tn=2 * COL_TILE):
    m, d = a.shape
    n = w_rows.shape[0]
    tn = min(tn, n)
    return pl.pallas_call(
        _gate_proj_kernel,
        out_shape=jax.ShapeDtypeStruct((m, n), BF16),
        grid=(m // tm, n // tn),
        in_specs=[pl.BlockSpec((tm, d), lambda i, j: (i, 0)),
                  pl.BlockSpec((tn, d), lambda i, j: (j, 0))],
        out_specs=pl.BlockSpec((tm, tn), lambda i, j: (i, j)),
        compiler_params=_params("parallel", "parallel"),
        name="gate_proj",
    )(a, w_rows)


def _in_proj_kernel(x_ref, g_ref, w_ref, wf_ref, qv_ref, k_ref, f_ref, a_ref, *, n_q, n_qv, scale):
    j = pl.program_id(1)

    @pl.when(j == 0)
    def _():
        x = x_ref[...]
        ms = jnp.mean(x * x, axis=-1, keepdims=True)
        a_ref[...] = (x * lax.rsqrt(ms + RMS_EPS) * g_ref[...]).astype(BF16)
        w_hi, w_lo, _ = _split3(wf_ref[...])
        both = _bf16_dot(a_ref[...], jnp.concatenate([w_hi, w_lo], axis=1))
        f_ref[...] = both[:, :LANES] + both[:, LANES:]

    def project():
        return lax.dot_general(a_ref[...], w_ref[...], (((1,), (1,)), ((), ())),
                               preferred_element_type=F32)

    @pl.when(j < n_qv)
    def _():
        acc = project() * jnp.where(j < n_q, F32(scale), F32(1.0))
        tm, tn = acc.shape
        for c in range(tn // LANES):
            for r in range(tm // ATT_TILE):
                blk = acc[r * ATT_TILE:(r + 1) * ATT_TILE, c * LANES:(c + 1) * LANES]
                qv_ref[c, r] = blk.T.astype(qv_ref.dtype)

    @pl.when(j >= n_qv)
    def _():
        acc = project()
        for c in range(acc.shape[1] // LANES):
            k_ref[c] = acc[:, c * LANES:(c + 1) * LANES].astype(k_ref.dtype)


def in_proj(x, g, w_rows, w_f, *, parts_w, scale, tm=ROW_TILE, tn=COL_TILE):
    m, d = x.shape
    per = parts_w // tn

    def tiles(*parts):
        return [p * per + t for p in parts for t in range(per)]

    order = tiles(0, 3, 2, 5) + tiles(1, 4)
    n_q, n_qv, n_k = 2 * per, 4 * per, 2 * per

    def w_tile(j):
        idx = order[-1]
        for t, src in reversed(list(enumerate(order[:-1]))):
            idx = jnp.where(j == t, src, idx)
        return idx

    hb = tn // LANES
    return pl.pallas_call(
        functools.partial(_in_proj_kernel, n_q=n_q, n_qv=n_qv, scale=scale),
        out_shape=(jax.ShapeDtypeStruct((n_qv * hb, m // ATT_TILE, LANES, ATT_TILE), BF16),
                   jax.ShapeDtypeStruct((n_k * hb, m, LANES), BF16),
                   jax.ShapeDtypeStruct((m, LANES), F32),
                   jax.ShapeDtypeStruct((m, d), BF16)),
        grid=(m // tm, len(order)),
        in_specs=[pl.BlockSpec((tm, d), lambda i, j: (i, 0)),
                  pl.BlockSpec((1, d), lambda i, j: (0, 0)),
                  pl.BlockSpec((tn, d), lambda i, j: (w_tile(j), 0)),
                  pl.BlockSpec((d, LANES), lambda i, j: (0, 0))],
        out_specs=(pl.BlockSpec((hb, tm // ATT_TILE, LANES, ATT_TILE),
                                lambda i, j: (jnp.minimum(j, n_qv - 1), i, 0, 0)),
                   pl.BlockSpec((hb, tm, LANES),
                                lambda i, j: (jnp.clip(j - n_qv, 0, n_k - 1), i, 0)),
                   pl.BlockSpec((tm, LANES), lambda i, j: (i, 0)),
                   pl.BlockSpec((tm, d), lambda i, j: (i, 0))),
        compiler_params=_params("parallel", "arbitrary"),
        name="in_proj",
    )(x, g.reshape(1, d), w_rows, w_f)


def _forget_cumsum_kernel(f_ref, b_ref, o_ref, carry_ref):
    t = pl.program_id(1)

    @pl.when(t == 0)
    def _():
        carry_ref[...] = jnp.zeros_like(carry_ref)

    f = f_ref[...] + b_ref[...]
    tm = f.shape[0]
    logf = jnp.minimum(f, 0.0) - jnp.log1p(jnp.exp(-jnp.abs(f)))
    logf = logf * LOG2E
    row = lax.broadcasted_iota(jnp.int32, (tm, tm), 0)
    col = lax.broadcasted_iota(jnp.int32, (tm, tm), 1)
    tri = jnp.where(col <= row, 1.0, 0.0).astype(BF16)
    hi, mid, lo = _split3(logf)
    c = (_bf16_dot(tri, hi) + (_bf16_dot(tri, mid) + _bf16_dot(tri, lo))) + carry_ref[0:1, :]
    o_ref[...] = c
    carry_ref[...] = jnp.broadcast_to(c[tm - 1:tm, :], carry_ref.shape)


def forget_cumsum(f, b_f, *, batch, tm=512):
    m = f.shape[0]
    nt = (m // batch) // tm
    b_pad = jnp.zeros((1, LANES), F32).at[0, :b_f.shape[0]].set(b_f)
    return pl.pallas_call(
        _forget_cumsum_kernel,
        out_shape=jax.ShapeDtypeStruct((m, LANES), F32),
        grid=(batch, nt),
        in_specs=[pl.BlockSpec((tm, LANES), lambda b, t: (b * nt + t, 0)),
                  pl.BlockSpec((1, LANES), lambda b, t: (0, 0))],
        out_specs=pl.BlockSpec((tm, LANES), lambda b, t: (b * nt + t, 0)),
        scratch_shapes=[pltpu.VMEM((8, LANES), F32)],
        compiler_params=_params("parallel", "arbitrary"),
        name="forget_cumsum",
    )(f, b_pad)


def _moba_bias_kernel(tab_ref, o_ref):
    h = pl.program_id(0)
    key = lax.broadcasted_iota(jnp.int32, (ATT_TILE, ATT_TILE), 0)
    qry = lax.broadcasted_iota(jnp.int32, (ATT_TILE, ATT_TILE), 1)
    far = tab_ref[h, NUM_BUCKETS - 1] * LOG2E

    def block(delta):
        dist = delta * MOBA_BLOCK + qry - key
        val = jnp.full((ATT_TILE, ATT_TILE), tab_ref[h, 0], F32)
        for k in range(1, NUM_BUCKETS):
            val = jnp.where(dist >= BUCKET_THRESHOLDS[k - 1], tab_ref[h, k], val)
        val = val * LOG2E
        if delta == 0:
            val = jnp.where(dist >= 0, val, NEG_INF)
        return val

    t = [block(delta) for delta in range(NEAR_BLOCKS + 1)]
    a = ATT_TILE
    o_ref[0, 0, 0:a, 0:a] = t[0]
    o_ref[0, 0, 0:a, a:2 * a] = t[1]
    o_ref[0, 0, a:2 * a, 0:a] = jnp.full((a, a), NEG_INF, F32)
    o_ref[0, 0, a:2 * a, a:2 * a] = t[0]
    for d in (1, 2):
        o_ref[0, d, 0:a, 0:a] = t[2 * d] - far
        o_ref[0, d, 0:a, a:2 * a] = t[2 * d + 1] - far
        o_ref[0, d, a:2 * a, 0:a] = t[2 * d - 1] - far
        o_ref[0, d, a:2 * a, a:2 * a] = t[2 * d] - far


def moba_bias(rel_bias):
    h = rel_bias.shape[0]
    return pl.pallas_call(
        _moba_bias_kernel,
        out_shape=jax.ShapeDtypeStruct((h, 3, TILE, TILE), F32),
        grid=(h,),
        in_specs=[pl.BlockSpec(memory_space=pltpu.SMEM)],
        out_specs=pl.BlockSpec((1, 3, TILE, TILE), lambda i: (i, 0, 0, 0)),
        compiler_params=_params("parallel"),
        name="moba_bias",
    )(rel_bias.astype(F32))


class _Softmax:
    def __init__(self, hh, s_ref, m_ref, l_ref, acc_ref, load_values):
        self.s_ref, self.m_ref, self.l_ref, self.acc_ref = s_ref.at[hh], m_ref.at[hh], l_ref.at[hh], acc_ref.at[hh]
        self.load_values = load_values

    def reset(self):
        self.m_ref[...] = jnp.full(self.m_ref.shape, NEG_INF, F32)
        self.l_ref[...] = jnp.zeros(self.l_ref.shape, F32)
        self.acc_ref[...] = jnp.zeros(self.acc_ref.shape, F32)

    def fold(self, slot, tile, mask=None):
        for half in range(2):
            self.fold_half(slot, tile, half, mask)

    def fold_half(self, slot, tile, half, mask=None):
        width = self.m_ref.shape[-1] // 2
        cols = slice(half * width, (half + 1) * width)
        s = self.s_ref[slot, :, cols]
        if mask is not None:
            s = jnp.where(mask[:, cols], s, NEG_INF)
        m = self.m_ref[:, cols]
        m_new = jnp.maximum(m, jnp.max(s, axis=0, keepdims=True))
        alpha = jnp.exp2(m - m_new)
        p = jnp.exp2(s - m_new)
        self.acc_ref[:, cols] = alpha * self.acc_ref[:, cols] + jnp.dot(
            self.load_values(tile), p.astype(BF16), preferred_element_type=F32)
        self.l_ref[:, cols] = alpha * self.l_ref[:, cols] + jnp.sum(p, axis=0, keepdims=True)
        self.m_ref[:, cols] = m_new

    def result(self):
        return self.acc_ref[...] * (1.0 / self.l_ref[...])


def _softmax_scratch(t, tq=None):
    nh = HEADS_PER_STEP
    tq = t if tq is None else tq
    return [pltpu.VMEM((nh, 2, t, tq), F32),
            pltpu.VMEM((nh, 1, tq), F32),
            pltpu.VMEM((nh, 1, tq), F32),
            pltpu.VMEM((nh, HEAD_DIM, tq), F32)]


def _fold_tile_run(heads, s_ref, logits, fold, base, count, last_tile):
    def step(r, slot, look_ahead=True):
        for hh in heads:
            if look_ahead:
                s_ref[hh, 1 - slot] = logits[hh](jnp.minimum(base + r + 1, last_tile))
            fold[hh].fold(slot, base + r)

    def unrolled_body(p, carry):
        for u in range(LOOP_UNROLL):
            step(LOOP_UNROLL * p + u, (u + 1) % 2)
        return carry

    lax.fori_loop(0, count // LOOP_UNROLL, unrolled_body, 0)
    assert LOOP_UNROLL == 4
    rem = count % LOOP_UNROLL
    done = count - rem

    @pl.when(rem >= 2)
    def _():
        step(done, 1)
        step(done + 1, 0)

    @pl.when(rem % 2 == 1)
    def _():
        step(count - 1, 1, look_ahead=False)


def _top_k_bias(gate, eligible, blk):
    lowest = float(jnp.finfo(F32).min)
    blk_f = blk.astype(F32)
    g = jnp.where(eligible, gate, NEG_INF)
    bias = jnp.full(gate.shape, NEG_INF, F32)
    for _ in range(MOBA_TOP_K):
        best = jnp.max(g, axis=0, keepdims=True)
        first = jnp.min(jnp.where(g == best, blk_f, float(gate.shape[0])), axis=0, keepdims=True)
        hit = blk_f == first
        bias = jnp.where(hit, 0.0, bias)
        g = jnp.where(hit, lowest, g)
    return jnp.where(eligible, bias, NEG_INF)


def _split3(x):
    hi = x.astype(BF16).astype(F32)
    mid = (x - hi).astype(BF16).astype(F32)
    lo = (x - hi - mid).astype(BF16).astype(F32)
    return hi, mid, lo


def _slab_cast_specs(weights, n_steps, inner_steps, step_index):
    in_specs, out_specs, out_shapes = [], [], []
    for w in weights:
        if isinstance(w, tuple):
            w, first_row, n_rows = w
        else:
            first_row, n_rows = 0, w.shape[0]
        rows = max(n_rows // n_steps, 16)
        n_slabs = n_rows // rows
        assert n_slabs * rows == n_rows and rows % 16 == 0 and n_steps % n_slabs == 0
        share = n_steps // n_slabs
        assert inner_steps % share == 0

        def slab(*g, share=share):
            return step_index(*g) // share

        if first_row == 0:
            in_specs.append(pl.BlockSpec((rows, w.shape[1]), lambda *g, slab=slab: (slab(*g), 0)))
        else:
            in_specs.append(pl.BlockSpec(
                (pl.Element(rows), pl.Element(w.shape[1])),
                lambda *g, first_row=first_row, rows=rows, slab=slab: (
                    pl.multiple_of(first_row + slab(*g) * rows, 8), 0)))
        out_specs.append(pl.BlockSpec((rows, w.shape[1]), lambda *g, slab=slab: (slab(*g), 0)))
        out_shapes.append(jax.ShapeDtypeStruct((n_rows, w.shape[1]), BF16))
    return in_specs, out_specs, out_shapes


def _moba_attn_kernel(*refs, n_blocks, n_cast):
    tab_ref, q_ref, k_ref, v_ref, bias_ref = refs[:5]
    o_ref = refs[5 + n_cast]
    kaug_ref, kbar_ref, kb3_ref, s_ref, m_ref, l_ref, acc_ref = refs[6 + 2 * n_cast:]
    for src, dst in zip(refs[5:5 + n_cast], refs[6 + n_cast:6 + 2 * n_cast]):
        dst[...] = src[...].astype(dst.dtype)
    _moba_attn_body(tab_ref, q_ref, k_ref, v_ref, bias_ref, o_ref,
                    kaug_ref, kbar_ref, kb3_ref, s_ref, m_ref, l_ref, acc_ref, n_blocks=n_blocks)


def _moba_attn_body(tab_ref, q_ref, k_ref, v_ref, bias_ref, o_ref,
                    kaug_ref, kbar_ref, kb3_ref, s_ref, m_ref, l_ref, acc_ref, *, n_blocks):
    hp = pl.program_id(1)
    i = pl.program_id(2)
    n_tiles = n_blocks // 2
    heads = range(HEADS_PER_STEP)

    @pl.when(i == 0)
    def _():
        lane = lax.broadcasted_iota(jnp.int32, (TILE, AUG - HEAD_DIM), 1)
        row = lax.broadcasted_iota(jnp.int32, (TILE, AUG - HEAD_DIM), 0)
        ones_lane = jnp.where(lane == n_blocks, 1.0, jnp.where(lane == n_blocks + 1, 1.0, 0.0))
        for hh in heads:
            for n in range(n_blocks):
                kbar_ref[hh, n:n + 1, :] = jnp.sum(k_ref[hh, n].astype(F32), axis=0, keepdims=True)
            hi, mid, lo = _split3(kbar_ref[hh] * (1.0 / MOBA_BLOCK))
            kb3_ref[hh, 0:n_blocks, :] = hi.astype(BF16)
            kb3_ref[hh, n_blocks:2 * n_blocks, :] = mid.astype(BF16)
            kb3_ref[hh, 2 * n_blocks:3 * n_blocks, :] = lo.astype(BF16)
            for j in range(n_tiles):
                blk_of_row = jnp.where(row < ATT_TILE, 2 * j, 2 * j + 1)
                right = jnp.where(lane == blk_of_row, 1.0, ones_lane).astype(BF16)
                kaug_ref[hh, j] = jnp.concatenate(
                    [k_ref[hh, 2 * j:2 * j + 2].reshape(TILE, HEAD_DIM), right], axis=1)

    blk = lax.broadcasted_iota(jnp.int32, (n_blocks, TILE), 0)
    qlane = lax.broadcasted_iota(jnp.int32, (n_blocks, TILE), 1)
    own = 2 * i + jnp.where(qlane >= ATT_TILE, 1, 0)
    eligible = blk < own
    r16 = lax.broadcasted_iota(jnp.int32, (16, TILE), 0)

    j1 = jnp.where(i >= 1, i - 1, i + 1)
    j2 = jnp.where(i >= 2, i - 2, i + 1)
    n_far = jnp.maximum(i - 2, 0)

    far_logits, fold = [], []
    for hh in heads:
        qt = jnp.concatenate([q_ref[hh, 0], q_ref[hh, 1]], axis=1)
        g3 = jnp.dot(kb3_ref[hh], qt, preferred_element_type=F32)
        gate = g3[0:n_blocks] + g3[n_blocks:2 * n_blocks] + g3[2 * n_blocks:3 * n_blocks]
        selb = _top_k_bias(gate, eligible, blk)

        far = jnp.full((16, TILE), tab_ref[hp * HEADS_PER_STEP + hh, NUM_BUCKETS - 1] * LOG2E, F32)
        far_hi = far.astype(BF16).astype(F32)
        far_rows = jnp.where(r16 == 0, far_hi, jnp.where(r16 == 1, far - far_hi, 0.0))
        q_aug = jnp.concatenate(
            [qt, selb.astype(BF16), far_rows.astype(BF16),
             jnp.zeros((AUG - HEAD_DIM - n_blocks - 16, TILE), BF16)], axis=0)

        def values(j, hh=hh):
            return jnp.concatenate([v_ref[hh, 2 * j], v_ref[hh, 2 * j + 1]], axis=1)

        def head_far_logits(j, hh=hh, q_aug=q_aug):
            return jnp.dot(kaug_ref[hh, j], q_aug, preferred_element_type=F32)

        sm = _Softmax(hh, s_ref, m_ref, l_ref, acc_ref, values)
        far_logits.append(head_far_logits)
        fold.append(sm)

        sel_own = jnp.sum(jnp.where(blk == 2 * i, selb, 0.0), axis=0, keepdims=True)
        sel_own = jnp.where(qlane[0:1] >= ATT_TILE, sel_own, 0.0)
        kd = k_ref[hh, pl.ds(2 * i, 2)].reshape(TILE, HEAD_DIM)
        sd = jnp.dot(kd, qt, preferred_element_type=F32) + bias_ref[hh, 0]
        s_ref[hh, 0, 0:ATT_TILE, :] = sd[0:ATT_TILE] + sel_own
        s_ref[hh, 0, ATT_TILE:TILE, :] = sd[ATT_TILE:TILE]
        sm.reset()

    for hh in heads:
        s_ref[hh, 1] = far_logits[hh](j1) + bias_ref[hh, 1]
        fold[hh].fold(0, i)
    for hh in heads:
        s_ref[hh, 0] = far_logits[hh](j2) + bias_ref[hh, 2]
        fold[hh].fold(1, j1)
    for hh in heads:
        s_ref[hh, 1] = far_logits[hh](0)
        fold[hh].fold(0, j2)

    _fold_tile_run(heads, s_ref, far_logits, fold, 0, n_far, n_tiles - 1)

    for hh in heads:
        o_ref[:, hh * HEAD_DIM:(hh + 1) * HEAD_DIM] = fold[hh].result().T.astype(o_ref.dtype)


def moba_attn(qv_t, k, bias, rel_bias, *, batch, n_heads, q_off, k_off, v_off, cast=()):
    mb = k.shape[1]
    nb = mb // batch
    nt = nb // 2
    m = mb * ATT_TILE
    hb = HEADS_PER_STEP
    nhp = n_heads // hb
    assert nb + 16 <= AUG - HEAD_DIM and nt >= 4
    assert n_heads % hb == 0 and q_off % hb == 0 and k_off % hb == 0 and v_off % hb == 0
    c_in, c_out, c_shape = _slab_cast_specs(cast, batch * nhp * nt, nt,
                                            lambda b, h, i: (b * nhp + h) * nt + i)
    return pl.pallas_call(
        functools.partial(_moba_attn_kernel, n_blocks=nb, n_cast=len(cast)),
        out_shape=[jax.ShapeDtypeStruct((m, n_heads * HEAD_DIM), BF16)] + c_shape,
        grid=(batch, nhp, nt),
        in_specs=[pl.BlockSpec(memory_space=pltpu.SMEM),
                  pl.BlockSpec((hb, 2, HEAD_DIM, ATT_TILE),
                               lambda b, h, i: (q_off // hb + h, b * nt + i, 0, 0)),
                  pl.BlockSpec((hb, nb, ATT_TILE, HEAD_DIM), lambda b, h, i: (k_off // hb + h, b, 0, 0)),
                  pl.BlockSpec((hb, nb, HEAD_DIM, ATT_TILE), lambda b, h, i: (v_off // hb + h, b, 0, 0)),
                  pl.BlockSpec((hb, 3, TILE, TILE), lambda b, h, i: (h, 0, 0, 0))] + c_in,
        out_specs=[pl.BlockSpec((TILE, hb * HEAD_DIM), lambda b, h, i: (b * nt + i, h))] + c_out,
        scratch_shapes=[pltpu.VMEM((hb, nt, TILE, AUG), BF16),
                        pltpu.VMEM((hb, nb, HEAD_DIM), F32),
                        pltpu.VMEM((hb, 3 * nb, HEAD_DIM), BF16)] + _softmax_scratch(TILE),
        compiler_params=_params("parallel", "parallel", "arbitrary"),
        name="moba_attn",
    )(rel_bias.astype(F32), qv_t, k, qv_t, bias, *[c[0] if isinstance(c, tuple) else c for c in cast])


def _fox_attn_kernel(*refs, n_tiles, n_cast):
    q_ref, k_ref, v_ref, c_ref = refs[:4]
    o_ref = refs[4 + n_cast]
    kaug_ref, bound_ref, s_ref, m_ref, l_ref, acc_ref = refs[5 + 2 * n_cast:]
    for src, dst in zip(refs[4:4 + n_cast], refs[5 + n_cast:5 + 2 * n_cast]):
        dst[...] = src[...].astype(dst.dtype)
    _fox_attn_body(q_ref, k_ref, v_ref, c_ref, o_ref, kaug_ref, bound_ref, s_ref, m_ref, l_ref,
                   acc_ref, n_tiles=n_tiles)


def _fox_attn_body(q_ref, k_ref, v_ref, c_ref, o_ref, kaug_ref, bound_ref, s_ref, m_ref, l_ref,
                   acc_ref, *, n_tiles):
    hp = pl.program_id(1)
    i = pl.program_id(2)
    t = TILE
    heads = range(HEADS_PER_STEP)

    lane1 = lax.broadcasted_iota(jnp.int32, (1, LANES), 1)

    @pl.when(i == 0)
    def _():
        nh = len(heads)
        src = lax.broadcasted_iota(jnp.int32, (3 * LANES, nh * LANES), 0)
        dst = lax.broadcasted_iota(jnp.int32, (3 * LANES, nh * LANES), 1)
        place = jnp.where((src % LANES == hp * nh + dst // LANES) & (src // LANES == dst % LANES),
                          1.0, 0.0).astype(BF16)
        c_first = [jnp.zeros((1, LANES), F32) for _ in heads]
        c_last = [jnp.zeros((1, LANES), F32) for _ in heads]
        k_norm2 = [jnp.zeros((t, 1), F32) for _ in heads]
        for j in range(n_tiles):
            c_all = c_ref[j * t:(j + 1) * t, :]
            terms = jnp.concatenate([x.astype(BF16) for x in _split3(-c_all)], axis=1)
            right = jnp.dot(terms, place, preferred_element_type=F32).astype(BF16)
            for hh in heads:
                k = k_ref[hh, 2 * j:2 * j + 2].reshape(t, HEAD_DIM)
                kaug_ref[hh, j] = jnp.concatenate([k, right[:, hh * LANES:(hh + 1) * LANES]], axis=1)
                kf = k.astype(F32)
                k_norm2[hh] = jnp.maximum(k_norm2[hh], jnp.sum(kf * kf, axis=1, keepdims=True))
                ends = jnp.concatenate([c_all[0:1, :], c_all[t - 1:t, :]], axis=0)
                ends = jnp.sum(jnp.where(lane1 == hp * nh + hh, ends, 0.0), axis=1, keepdims=True)
                c_first[hh] = jnp.where(lane1 == j, ends[0:1, :], c_first[hh])
                c_last[hh] = jnp.where(lane1 == j, ends[1:2, :], c_last[hh])
        for hh in heads:
            bound_ref[hh, 0:1, :] = c_first[hh]
            bound_ref[hh, 1:2, :] = c_last[hh]
            bound_ref[hh, 2:3, :] = jnp.broadcast_to(
                jnp.max(k_norm2[hh], axis=0, keepdims=True), (1, LANES))

    tq = FOX_Q_TILES * t
    d0 = FOX_Q_TILES * i
    r_aug = lax.broadcasted_iota(jnp.int32, (AUG - HEAD_DIM, tq), 0)
    ones_rows = jnp.where(r_aug < 3, 1.0, 0.0).astype(BF16)

    logits, fold, skippable = [], [], []
    for hh in heads:
        qt = jnp.concatenate([q_ref[hh, r] for r in range(tq // ATT_TILE)], axis=1)
        q_aug = jnp.concatenate([qt, ones_rows], axis=0)

        qf = qt.astype(F32)
        q_norm2 = jnp.max(jnp.sum(qf * qf, axis=0, keepdims=True), axis=1, keepdims=True)
        c_here = jnp.sum(jnp.where(lane1 == d0, bound_ref[hh, 0:1, :], 0.0), axis=1, keepdims=True)
        gap = (bound_ref[hh, 1:2, :] - c_here) - UNDERFLOW_LOG2
        dead = (lane1 < d0) & (gap > 0.0) & (gap * gap > 4.0 * q_norm2 * bound_ref[hh, 2:3, :])
        skippable.append(jnp.sum(jnp.where(dead, 1.0, 0.0), axis=1, keepdims=True))

        def head_logits(n, hh=hh, q_aug=q_aug):
            return jnp.dot(kaug_ref[hh, n], q_aug, preferred_element_type=F32)

        def values(n, hh=hh):
            return jnp.concatenate([v_ref[hh, 2 * n], v_ref[hh, 2 * n + 1]], axis=1)

        logits.append(head_logits)
        fold.append(_Softmax(hh, s_ref, m_ref, l_ref, acc_ref, values))

    assert FOX_Q_TILES == 2
    key = lax.broadcasted_iota(jnp.int32, (t, tq), 0)
    qry = lax.broadcasted_iota(jnp.int32, (t, tq), 1)
    j0 = functools.reduce(jnp.minimum, skippable)[0, 0].astype(jnp.int32)
    n_past = d0 - j0
    for hh in heads:
        s_ref[hh, 1] = logits[hh](d0)
        fold[hh].reset()
    for hh in heads:
        s_ref[hh, 0] = logits[hh](d0 + 1)
        fold[hh].fold(1, d0, mask=key <= qry)
    for hh in heads:
        s_ref[hh, 1] = logits[hh](j0)
        fold[hh].fold(0, d0 + 1, mask=key + t <= qry)

    _fold_tile_run(heads, s_ref, logits, fold, j0, n_past, n_tiles - 1)

    for hh in heads:
        o_ref[:, hh * HEAD_DIM:(hh + 1) * HEAD_DIM] = fold[hh].result().T.astype(o_ref.dtype)


def fox_attn(qv_t, k, c_rep, *, batch, n_heads, q_off, k_off, v_off, cast=()):
    mb = k.shape[1]
    nb = mb // batch
    nt = nb // 2
    m = mb * ATT_TILE
    t = TILE
    hb = HEADS_PER_STEP
    nhp = n_heads // hb
    assert n_heads % hb == 0 and q_off % hb == 0 and k_off % hb == 0 and v_off % hb == 0
    assert nt <= LANES and n_heads <= LANES and nt % FOX_Q_TILES == 0
    nq = nt // FOX_Q_TILES
    tq = FOX_Q_TILES * t
    c_in, c_out, c_shape = _slab_cast_specs(cast, batch * nhp * nq, nq,
                                            lambda b, h, i: (b * nhp + h) * nq + i)
    return pl.pallas_call(
        functools.partial(_fox_attn_kernel, n_tiles=nt, n_cast=len(cast)),
        out_shape=[jax.ShapeDtypeStruct((m, n_heads * HEAD_DIM), BF16)] + c_shape,
        grid=(batch, nhp, nq),
        in_specs=[pl.BlockSpec((hb, tq // ATT_TILE, HEAD_DIM, ATT_TILE),
                               lambda b, h, i: (q_off // hb + h, b * nq + i, 0, 0)),
                  pl.BlockSpec((hb, nb, ATT_TILE, HEAD_DIM), lambda b, h, i: (k_off // hb + h, b, 0, 0)),
                  pl.BlockSpec((hb, nb, HEAD_DIM, ATT_TILE), lambda b, h, i: (v_off // hb + h, b, 0, 0)),
                  pl.BlockSpec((nb * ATT_TILE, LANES), lambda b, h, i: (b, 0))] + c_in,
        out_specs=[pl.BlockSpec((tq, hb * HEAD_DIM), lambda b, h, i: (b * nq + i, h))] + c_out,
        scratch_shapes=[pltpu.VMEM((hb, nt, t, AUG), BF16),
                        pltpu.VMEM((hb, 8, LANES), F32)] + _softmax_scratch(t, tq),
        compiler_params=_params("parallel", "parallel", "arbitrary"),
        name="fox_attn",
    )(qv_t, k, qv_t, c_rep, *[c[0] if isinstance(c, tuple) else c for c in cast])


def _gated_merge_kernel(oa_ref, of_ref, wa_ref, wf_ref, ga_ref, gf_ref, o_ref):
    ua = jnp.dot(oa_ref[...], wa_ref[...], preferred_element_type=F32)
    uf = jnp.dot(of_ref[...], wf_ref[...], preferred_element_type=F32)
    o_ref[...] = (ga_ref[...].astype(F32) * ua + gf_ref[...].astype(F32) * uf).astype(o_ref.dtype)


def gated_merge(o_a, o_f, w_a, w_f, gates, *, tm=ROW_TILE, tn=COL_TILE):
    m, ka = o_a.shape
    kf = o_f.shape[1]
    n = w_a.shape[1]
    nj = n // tn
    return pl.pallas_call(
        _gated_merge_kernel,
        out_shape=jax.ShapeDtypeStruct((m, n), BF16),
        grid=(m // tm, nj),
        in_specs=[pl.BlockSpec((tm, ka), lambda i, j: (i, 0)),
                  pl.BlockSpec((tm, kf), lambda i, j: (i, 0)),
                  pl.BlockSpec((ka, tn), lambda i, j: (0, j)),
                  pl.BlockSpec((kf, tn), lambda i, j: (0, j)),
                  pl.BlockSpec((tm, tn), lambda i, j: (i, j)),
                  pl.BlockSpec((tm, tn), lambda i, j: (i, nj + j))],
        out_specs=pl.BlockSpec((tm, tn), lambda i, j: (i, j)),
        compiler_params=_params("parallel", "parallel"),
        name="gated_merge",
    )(o_a, o_f, w_a, w_f, gates, gates)


def _mm_res_kernel(lhs_ref, w_ref, res_ref, g_ref, o_ref, *, final_norm):
    kk = pl.program_id(1)

    @pl.when(kk == 0)
    def _():
        o_ref[...] = res_ref[...]

    o_ref[...] += jnp.dot(lhs_ref[...], w_ref[...], preferred_element_type=F32)

    if final_norm:
        @pl.when(kk == pl.num_programs(1) - 1)
        def _():
            hres = o_ref[...]
            ms = jnp.mean(hres * hres, axis=-1, keepdims=True)
            o_ref[...] = hres * lax.rsqrt(ms + RMS_EPS) * g_ref[...]


def mm_res(lhs, w, res, g=None, *, tm=ROW_TILE, tk=1024):
    m, k = lhs.shape
    n = w.shape[1]
    tk = min(tk, k)
    final_norm = g is not None
    if g is None:
        g = jnp.ones((n,), F32)
    return pl.pallas_call(
        functools.partial(_mm_res_kernel, final_norm=final_norm),
        out_shape=jax.ShapeDtypeStruct((m, n), F32),
        grid=(m // tm, k // tk),
        in_specs=[pl.BlockSpec((tm, tk), lambda i, kk: (i, kk)),
                  pl.BlockSpec((tk, n), lambda i, kk: (kk, 0)),
                  pl.BlockSpec((tm, n), lambda i, kk: (i, 0)),
                  pl.BlockSpec((1, n), lambda i, kk: (0, 0))],
        out_specs=pl.BlockSpec((tm, n), lambda i, kk: (i, 0)),
        compiler_params=_params("parallel", "arbitrary"),
        name="mm_res",
    )(lhs, w, res, g.reshape(1, n))


def _cross_attn_kernel(h_ref, g_ref, wq_ref, kv_ref, wo_ref, o_ref, *, n_heads):
    width = n_heads * HEAD_DIM
    hres = h_ref[...]
    ms = jnp.mean(hres * hres, axis=-1, keepdims=True)
    c = (hres * lax.rsqrt(ms + RMS_EPS) * g_ref[...]).astype(BF16)
    q_all = (jnp.dot(c, wq_ref[...], preferred_element_type=F32)
             * (HEAD_DIM ** -0.5 * LOG2E)).astype(BF16)
    heads = []
    for h in range(n_heads):
        q = q_all[:, h * HEAD_DIM:(h + 1) * HEAD_DIM]
        k = kv_ref[0, :, h * HEAD_DIM:(h + 1) * HEAD_DIM]
        v = kv_ref[0, :, width + h * HEAD_DIM:width + (h + 1) * HEAD_DIM]
        s = lax.dot_general(q, k, (((1,), (1,)), ((), ())), preferred_element_type=F32)
        m = jnp.max(s, axis=-1, keepdims=True)
        p = jnp.exp2(s - m)
        l = jnp.sum(p, axis=-1, keepdims=True)
        o = jnp.dot(p.astype(BF16), v, preferred_element_type=F32) * (1.0 / l)
        heads.append(o.astype(BF16))
    o_all = jnp.concatenate(heads, axis=1)
    o_ref[...] = hres + jnp.dot(o_all, wo_ref[...], preferred_element_type=F32)


def cross_attn(h, g, w_q, kv, w_o, *, batch, n_heads, tm=ROW_TILE):
    m, d = h.shape
    width = w_q.shape[1]
    n_mem = kv.shape[0] // batch
    tiles_per_batch = (m // batch) // tm
    kv3 = kv.reshape(batch, n_mem, 2 * width)
    return pl.pallas_call(
        functools.partial(_cross_attn_kernel, n_heads=n_heads),
        out_shape=jax.ShapeDtypeStruct((m, d), F32),
        grid=(m // tm,),
        in_specs=[pl.BlockSpec((tm, d), lambda i: (i, 0)),
                  pl.BlockSpec((1, d), lambda i: (0, 0)),
                  pl.BlockSpec((d, width), lambda i: (0, 0)),
                  pl.BlockSpec((1, n_mem, 2 * width), lambda i: (i // tiles_per_batch, 0, 0)),
                  pl.BlockSpec((width, d), lambda i: (0, 0))],
        out_specs=pl.BlockSpec((tm, d), lambda i: (i, 0)),
        compiler_params=_params("parallel"),
        name="cross_attn",
    )(h, g.reshape(1, d), w_q, kv3, w_o)


def kernel(x, mem, g_mix, w_in, b_forget, w_branch_moba, w_branch_fox, w_mix_out, rel_bias,
           g_cross, g_mem, w_cq, w_ck, w_cv, w_co, g_mlp, w_ff1, w_ff2, g_final):
    batch, seq, d = x.shape
    depth = w_in.shape[0]
    n_heads = rel_bias.shape[0]
    n_fox = b_forget.shape[1]
    wm = n_heads * HEAD_DIM
    wf = n_fox * HEAD_DIM
    m = batch * seq
    assert wm == wf and wm % COL_TILE == 0
    scale = HEAD_DIM ** -0.5
    mem2 = mem.reshape(-1, d)

    bias = moba_bias(rel_bias)
    h = x.reshape(m, d)
    for l in range(depth):
        wt = jnp.swapaxes(w_in[l], 0, 1)
        qkv_w = 3 * (wm + wf)
        w_fl = jnp.zeros((d, LANES), F32).at[:, :n_fox].set(w_in[l][:, qkv_w:qkv_w + n_fox])
        qv_t, k_hm, f_logit, a_mix = in_proj(h, g_mix[l], wt[:qkv_w].astype(BF16), w_fl,
                                             parts_w=wm, scale=scale * LOG2E)
        k_hm = k_hm.reshape(k_hm.shape[0], m // ATT_TILE, ATT_TILE, HEAD_DIM)
        c_rep = forget_cumsum(f_logit, b_forget[l], batch=batch)

        o_a, w_ff1_16, w_mix_16, w_bm16, w_bf16 = moba_attn(
            qv_t, k_hm, bias, rel_bias, batch=batch, n_heads=n_heads,
            q_off=0, k_off=0, v_off=n_heads + n_fox,
            cast=(w_ff1[l], w_mix_out[l], w_branch_moba[l], w_branch_fox[l]))
        o_f, w_ff2_16, w_g16, w_cq16, w_co16 = fox_attn(
            qv_t, k_hm, c_rep, batch=batch, n_heads=n_fox,
            q_off=n_heads, k_off=n_heads, v_off=2 * n_heads + n_fox,
            cast=(w_ff2[l], (wt, qkv_w + n_fox, 2 * d), w_cq[l], w_co[l]))
        gates = gate_proj(a_mix, w_g16)
        merged = gated_merge(o_a, o_f, w_bm16, w_bf16, gates)
        h = mm_res(merged, w_mix_16, h)

        cw = w_cq.shape[2]
        w_kv = jnp.concatenate([w_ck[l], w_cv[l]], axis=1).astype(BF16)
        kv = rms_proj(mem2, g_mem[l], w_kv, tn=2 * cw)
        h = cross_attn(h, g_cross[l], w_cq16, kv, w_co16, batch=batch, n_heads=cw // HEAD_DIM)

        u = rms_proj(h, g_mlp[l], w_ff1_16, act="relu2", tn=2 * COL_TILE)
        h = mm_res(u, w_ff2_16, h, g_final if l == depth - 1 else None)
    return h.reshape(batch, seq, d)
```
